```python
import math
import jax, jax.numpy as jnp
from jax import lax
import numpy as np

D_MODEL = 2048
BATCH = 8
SEQ = 8192
DEPTH = 2

N_EVEN = (DEPTH + 1) // 2
N_ODD = DEPTH // 2
NORM_EPS = 1e-6

ATTN_GROUPS = ((128, 1), (512, 4), (2048, 16))
ATTN_HEADS_PER_GROUP = 8
ATTN_HEAD_DIM = 64
ATTN_BLOCK = 128
ATTN_GROUP_WIDTH = ATTN_HEADS_PER_GROUP * ATTN_HEAD_DIM
ATTN_QKV_WIDTH = 3 * len(ATTN_GROUPS) * ATTN_GROUP_WIDTH
ATTN_OUT_WIDTH = ATTN_GROUP_WIDTH

POOL_WINDOWS = (2, 4, 8, 16)
POOL_GROUP_WIDTH = D_MODEL // 16
POOL_WIDTH = len(POOL_WINDOWS) * POOL_GROUP_WIDTH

EVEN_IN_WIDTH = ATTN_QKV_WIDTH + POOL_WIDTH
EVEN_OUT_WIDTH = ATTN_OUT_WIDTH + POOL_WIDTH

SSM_EXPAND = 2
SSM_D_INNER = SSM_EXPAND * D_MODEL
SSM_HEAD_DIM = 64
SSM_HEADS = SSM_D_INNER // SSM_HEAD_DIM
SSM_GROUPS = 8
SSM_STATE = 128
SSM_CONV = 4
SSM_CHUNK = 128
SSM_CONV_DIM = SSM_D_INNER + 2 * SSM_GROUPS * SSM_STATE
SSM_IN_WIDTH = SSM_D_INNER + SSM_CONV_DIM + SSM_HEADS

FFN_HIDDEN = 4 * D_MODEL

kernel_name = "hybrid_dilated_attn_pool_ssd_adaln"


def rmsnorm(x, g):
    xf = x.astype(jnp.float32)
    y = xf * lax.rsqrt(jnp.mean(xf * xf, axis=-1, keepdims=True) + NORM_EPS)
    return (y * g.astype(jnp.float32)).astype(x.dtype)


def modulate(h, shift, scale):
    return h * (1 + scale[:, None, :]) + shift[:, None, :]


def dilated_window_attention(q, k, v, dilation, n_back):
    b, s, h, e = q.shape
    L = s // dilation
    nb = -(-L // ATTN_BLOCK)
    Lp = nb * ATTN_BLOCK

    def to_sub(t):
        t = t.reshape(b, L, dilation, h, e).transpose(0, 2, 3, 1, 4)
        return jnp.pad(t, ((0, 0), (0, 0), (0, 0), (0, Lp - L), (0, 0)))

    def windows(t):
        t = jnp.pad(to_sub(t), ((0, 0), (0, 0), (0, 0), (ATTN_BLOCK, 0), (0, 0)))
        t = t.reshape(b, dilation, h, nb + 1, ATTN_BLOCK, e)
        return jnp.concatenate([t[:, :, :, :-1], t[:, :, :, 1:]], axis=4)

    qs = to_sub(q).reshape(b, dilation, h, nb, ATTN_BLOCK, e)
    ks, vs = windows(k), windows(v)
    scores = jnp.einsum('bdhnqe,bdhnke->bdhnqk', qs, ks).astype(jnp.float32) * (e ** -0.5)
    qi = jnp.arange(ATTN_BLOCK)[:, None]
    kj = jnp.arange(2 * ATTN_BLOCK)[None, :]
    dist = qi + ATTN_BLOCK - kj
    kpos = jnp.arange(nb)[:, None, None] * ATTN_BLOCK - ATTN_BLOCK + kj[None]
    mask = (dist >= 0) & (dist <= n_back) & (kpos >= 0)
    scores = jnp.where(mask, scores, -jnp.inf)
    m = jnp.max(scores, axis=-1, keepdims=True)
    p = jnp.exp(scores - m)
    den = jnp.sum(p, axis=-1, keepdims=True)
    o = jnp.einsum('bdhnqk,bdhnke->bdhnqe', (p / den).astype(v.dtype), vs)
    lse = (m + jnp.log(den))[..., 0]

    def from_sub(t):
        t = t.reshape(b, dilation, h, Lp, *t.shape[5:])[:, :, :, :L]
        t = jnp.moveaxis(t, 3, 1)
        return t.reshape(b, s, h, *t.shape[4:])

    return from_sub(o), from_sub(lse)


def multiscale_pool(u, pool_w, pool_scale):
    b, s, _ = u.shape
    ug = u.reshape(b, s, len(POOL_WINDOWS), POOL_GROUP_WIDTH).astype(jnp.float32)
    cs = jnp.pad(jnp.cumsum(ug, axis=1), ((0, 0), (1, 0), (0, 0), (0, 0)))
    t = jnp.arange(s)
    diffs = []
    for gi, w in enumerate(POOL_WINDOWS):
        csg = cs[:, :, gi]
        lo = jnp.maximum(t + 1 - w, 0)
        cnt = jnp.minimum(t + 1, w).astype(jnp.float32)
        mean = (csg[:, 1:] - csg[:, lo]) / cnt[None, :, None]
        diffs.append(mean - ug[:, :, gi])
    d = jnp.stack(diffs, axis=2).astype(u.dtype)
    y = jnp.einsum('bsgc,gce->bsge', d, pool_w)
    return y.reshape(b, s, POOL_WIDTH) * pool_scale


def attn_pool_mixer(h, w_in, pool_w, pool_scale, w_out):
    b, s, _ = h.shape
    proj = h @ w_in
    qkv = proj[..., :ATTN_QKV_WIDTH].reshape(b, s, 3, len(ATTN_GROUPS), ATTN_HEADS_PER_GROUP, ATTN_HEAD_DIM)
    u = proj[..., ATTN_QKV_WIDTH:]
    outs, lses = [], []
    for gi, (window, dil) in enumerate(ATTN_GROUPS):
        o, lse = dilated_window_attention(qkv[:, :, 0, gi], qkv[:, :, 1, gi], qkv[:, :, 2, gi], dil, window // dil)
        outs.append(o)
        lses.append(lse)
    wts = jax.nn.softmax(jnp.stack(lses, axis=0), axis=0)
    attn = jnp.sum(wts[..., None] * jnp.stack(outs, axis=0).astype(jnp.float32), axis=0)
    attn = attn.astype(h.dtype).reshape(b, s, ATTN_OUT_WIDTH)
    pool = multiscale_pool(u, pool_w, pool_scale)
    return jnp.concatenate([attn, pool], axis=-1) @ w_out


def causal_depthwise_conv(x, w, bias):
    y = lax.conv_general_dilated(x, w[:, None, :], window_strides=(1,), padding=[(SSM_CONV - 1, 0)],
                                 dimension_numbers=('NWC', 'WIO', 'NWC'), feature_group_count=x.shape[-1])
    return y + bias


def ssd_scan(x, dt, A, bm, cm):
    b, s, h, p = x.shape
    g, n = bm.shape[2], bm.shape[3]
    r = h // g
    q = SSM_CHUNK
    nc = s // q
    xc = x.astype(jnp.float32).reshape(b, nc, q, g, r, p)
    bc = bm.astype(jnp.float32).reshape(b, nc, q, g, n)
    cc = cm.astype(jnp.float32).reshape(b, nc, q, g, n)
    dtc = dt.reshape(b, nc, q, g, r)
    a = jnp.cumsum(dtc * A.reshape(g, r), axis=2)
    xdt = xc * dtc[..., None]
    causal = jnp.tril(jnp.ones((q, q), dtype=bool))[:, :, None, None]
    decay = jnp.exp(jnp.where(causal, a[:, :, :, None] - a[:, :, None], -jnp.inf))
    cb = jnp.einsum('bclgn,bcsgn->bclsg', cc, bc)
    y_diag = jnp.einsum('bclsgr,bcsgrp->bclgrp', cb[..., None] * decay, xdt)
    decay_to_end = jnp.exp(a[:, :, -1:] - a)
    states = jnp.einsum('bcsgn,bcsgrp->bcgrpn', bc, xdt * decay_to_end[..., None])
    chunk_decay = jnp.exp(a[:, :, -1])

    def step(carry, inp):
        st, dec = inp
        return carry * dec[..., None, None] + st, carry

    init = jnp.zeros((b, g, r, p, n), jnp.float32)
    _, h_in = lax.scan(step, init, (jnp.moveaxis(states, 1, 0), jnp.moveaxis(chunk_decay, 1, 0)))
    h_in = jnp.moveaxis(h_in, 0, 1)
    y_off = jnp.einsum('bclgn,bcgrpn->bclgrp', cc, h_in) * jnp.exp(a)[..., None]
    return (y_diag + y_off).reshape(b, s, h, p)


def gated_group_rmsnorm(y, z, g):
    yf = y.astype(jnp.float32) * jax.nn.silu(z.astype(jnp.float32))
    b, s, dim = yf.shape
    yg = yf.reshape(b, s, SSM_GROUPS, dim // SSM_GROUPS)
    yg = yg * lax.rsqrt(jnp.mean(yg * yg, axis=-1, keepdims=True) + NORM_EPS)
    return (yg.reshape(b, s, dim) * g.astype(jnp.float32)).astype(z.dtype)


def ssd_mixer(h, w_in, conv_w, conv_b, dt_bias, a_log, d_skip, norm_g, w_out):
    b, s, _ = h.shape
    proj = h @ w_in
    z = proj[..., :SSM_D_INNER]
    xbc = proj[..., SSM_D_INNER:SSM_D_INNER + SSM_CONV_DIM]
    dt = proj[..., SSM_D_INNER + SSM_CONV_DIM:]
    xbc = jax.nn.silu(causal_depthwise_conv(xbc, conv_w, conv_b))
    xs = xbc[..., :SSM_D_INNER].reshape(b, s, SSM_HEADS, SSM_HEAD_DIM)
    bm = xbc[..., SSM_D_INNER:SSM_D_INNER + SSM_GROUPS * SSM_STATE].reshape(b, s, SSM_GROUPS, SSM_STATE)
    cm = xbc[..., SSM_D_INNER + SSM_GROUPS * SSM_STATE:].reshape(b, s, SSM_GROUPS, SSM_STATE)
    dt = jax.nn.softplus((dt + dt_bias).astype(jnp.float32))
    A = -jnp.exp(a_log.astype(jnp.float32))
    y = ssd_scan(xs, dt, A, bm, cm)
    y = y + d_skip.astype(jnp.float32)[:, None] * xs.astype(jnp.float32)
    y = gated_group_rmsnorm(y.reshape(b, s, SSM_D_INNER), z, norm_g)
    return y @ w_out


def squared_relu_mlp(h, w1, w2):
    return jnp.square(jax.nn.relu(h @ w1)) @ w2


def _fwd_setup_inputs(seed: int = 0) -> dict:
    key = jax.random.key(seed)
    ks = jax.random.split(key, 24)
    f32 = jnp.float32
    nrm = lambda k, shape, scale: jax.random.normal(k, shape, f32) * scale
    D = D_MODEL
    dt0 = jnp.exp(jax.random.uniform(ks[16], (N_ODD, SSM_HEADS), f32, minval=math.log(1e-3), maxval=math.log(1e-1)))
    return {
        "x": nrm(ks[0], (BATCH, SEQ, D), 1.0),
        "c": nrm(ks[1], (BATCH, D), 1.0),
        "ada_w": nrm(ks[2], (DEPTH, D, 6 * D), 0.5 * D ** -0.5),
        "ada_b": nrm(ks[3], (DEPTH, 6 * D), 0.02),
        "norm_mix": 1.0 + nrm(ks[4], (DEPTH, D), 0.02),
        "norm_ffn": 1.0 + nrm(ks[5], (DEPTH, D), 0.02),
        "ffn_w1": nrm(ks[6], (DEPTH, D, FFN_HIDDEN), D ** -0.5),
        "ffn_w2": nrm(ks[7], (DEPTH, FFN_HIDDEN, D), FFN_HIDDEN ** -0.5),
        "even_w_in": nrm(ks[8], (N_EVEN, D, EVEN_IN_WIDTH), D ** -0.5),
        "pool_w": nrm(ks[9], (N_EVEN, len(POOL_WINDOWS), POOL_GROUP_WIDTH, POOL_GROUP_WIDTH), POOL_GROUP_WIDTH ** -0.5),
        "pool_scale": 1.0 + nrm(ks[10], (N_EVEN, POOL_WIDTH), 0.02),
        "even_w_out": nrm(ks[11], (N_EVEN, EVEN_OUT_WIDTH, D), EVEN_OUT_WIDTH ** -0.5),
        "ssm_w_in": nrm(ks[12], (N_ODD, D, SSM_IN_WIDTH), D ** -0.5),
        "ssm_conv_w": nrm(ks[13], (N_ODD, SSM_CONV, SSM_CONV_DIM), SSM_CONV ** -0.5),
        "ssm_conv_b": nrm(ks[14], (N_ODD, SSM_CONV_DIM), 0.02),
        "ssm_dt_bias": dt0 + jnp.log(-jnp.expm1(-dt0)),
        "ssm_a_log": jnp.log(jax.random.uniform(ks[15], (N_ODD, SSM_HEADS), f32, minval=1.0, maxval=16.0)),
        "ssm_d": 1.0 + nrm(ks[17], (N_ODD, SSM_HEADS), 0.02),
        "ssm_norm": 1.0 + nrm(ks[18], (N_ODD, SSM_D_INNER), 0.02),
        "ssm_w_out": nrm(ks[19], (N_ODD, SSM_D_INNER, D), SSM_D_INNER ** -0.5),
        "final_norm": 1.0 + nrm(ks[20], (D,), 0.02),
    }


def _fwd_reference(x, c, ada_w, ada_b, norm_mix, norm_ffn, ffn_w1, ffn_w2, even_w_in, pool_w, pool_scale,
              even_w_out, ssm_w_in, ssm_conv_w, ssm_conv_b, ssm_dt_bias, ssm_a_log, ssm_d, ssm_norm,
              ssm_w_out, final_norm):
    cond = jax.nn.silu(c)
    for i in range(DEPTH):
        mod = cond @ ada_w[i] + ada_b[i]
        sh1, sc1, g1, sh2, sc2, g2 = jnp.split(mod, 6, axis=-1)
        h = modulate(rmsnorm(x, norm_mix[i]), sh1, sc1)
        j = i // 2
        if i % 2 == 0:
            y = attn_pool_mixer(h, even_w_in[j], pool_w[j], pool_scale[j], even_w_out[j])
        else:
            y = ssd_mixer(h, ssm_w_in[j], ssm_conv_w[j], ssm_conv_b[j], ssm_dt_bias[j], ssm_a_log[j],
                          ssm_d[j], ssm_norm[j], ssm_w_out[j])
        x = x + g1[:, None, :] * y
        h = modulate(rmsnorm(x, norm_ffn[i]), sh2, sc2)
        x = x + g2[:, None, :] * squared_relu_mlp(h, ffn_w1[i], ffn_w2[i])
    return rmsnorm(x, final_norm)


import jax as _jax
import jax.numpy as _jnp

TWIN_FORMAT = 'train_step'
FWD_PARAMS = ['x', 'c', 'ada_w', 'ada_b', 'norm_mix', 'norm_ffn', 'ffn_w1', 'ffn_w2', 'even_w_in', 'pool_w', 'pool_scale', 'even_w_out', 'ssm_w_in', 'ssm_conv_w', 'ssm_conv_b', 'ssm_dt_bias', 'ssm_a_log', 'ssm_d', 'ssm_norm', 'ssm_w_out', 'final_norm']
TWIN_WEIGHTS = ['ada_w', 'ada_b', 'norm_mix', 'norm_ffn', 'ffn_w1', 'ffn_w2', 'even_w_in', 'pool_w', 'pool_scale', 'even_w_out', 'ssm_w_in', 'ssm_conv_w', 'ssm_conv_b', 'ssm_dt_bias', 'ssm_a_log', 'ssm_d', 'ssm_norm', 'ssm_w_out', 'final_norm']
TWIN_DIFF_INPUT = 'x'
TWIN_INPUTS = ['x', 'c', 'ada_w', 'ada_b', 'norm_mix', 'norm_ffn', 'ffn_w1', 'ffn_w2', 'even_w_in', 'pool_w', 'pool_scale', 'even_w_out', 'ssm_w_in', 'ssm_conv_w', 'ssm_conv_b', 'ssm_dt_bias', 'ssm_a_log', 'ssm_d', 'ssm_norm', 'ssm_w_out', 'final_norm', 'loss_target', 'm_ada_w', 'm_ada_b', 'm_norm_mix', 'm_norm_ffn', 'm_ffn_w1', 'm_ffn_w2', 'm_even_w_in', 'm_pool_w', 'm_pool_scale', 'm_even_w_out', 'm_ssm_w_in', 'm_ssm_conv_w', 'm_ssm_conv_b', 'm_ssm_dt_bias', 'm_ssm_a_log', 'm_ssm_d', 'm_ssm_norm', 'm_ssm_w_out', 'm_final_norm', 'v_ada_w', 'v_ada_b', 'v_norm_mix', 'v_norm_ffn', 'v_ffn_w1', 'v_ffn_w2', 'v_even_w_in', 'v_pool_w', 'v_pool_scale', 'v_even_w_out', 'v_ssm_w_in', 'v_ssm_conv_w', 'v_ssm_conv_b', 'v_ssm_dt_bias', 'v_ssm_a_log', 'v_ssm_d', 'v_ssm_norm', 'v_ssm_w_out', 'v_final_norm']
TWIN_OUTPUTS = ['loss', 'grad_x', 'grad_ada_w', 'grad_ada_b', 'grad_norm_mix', 'grad_norm_ffn', 'grad_ffn_w1', 'grad_ffn_w2', 'grad_even_w_in', 'grad_pool_w', 'grad_pool_scale', 'grad_even_w_out', 'grad_ssm_w_in', 'grad_ssm_conv_w', 'grad_ssm_conv_b', 'grad_ssm_dt_bias', 'grad_ssm_a_log', 'grad_ssm_d', 'grad_ssm_norm', 'grad_ssm_w_out', 'grad_final_norm', 'delta_ada_w', 'delta_ada_b', 'delta_norm_mix', 'delta_norm_ffn', 'delta_ffn_w1', 'delta_ffn_w2', 'delta_even_w_in', 'delta_pool_w', 'delta_pool_scale', 'delta_even_w_out', 'delta_ssm_w_in', 'delta_ssm_conv_w', 'delta_ssm_conv_b', 'delta_ssm_dt_bias', 'delta_ssm_a_log', 'delta_ssm_d', 'delta_ssm_norm', 'delta_ssm_w_out', 'delta_final_norm', 'new_m_ada_w', 'new_m_ada_b', 'new_m_norm_mix', 'new_m_norm_ffn', 'new_m_ffn_w1', 'new_m_ffn_w2', 'new_m_even_w_in', 'new_m_pool_w', 'new_m_pool_scale', 'new_m_even_w_out', 'new_m_ssm_w_in', 'new_m_ssm_conv_w', 'new_m_ssm_conv_b', 'new_m_ssm_dt_bias', 'new_m_ssm_a_log', 'new_m_ssm_d', 'new_m_ssm_norm', 'new_m_ssm_w_out', 'new_m_final_norm', 'new_v_ada_w', 'new_v_ada_b', 'new_v_norm_mix', 'new_v_norm_ffn', 'new_v_ffn_w1', 'new_v_ffn_w2', 'new_v_even_w_in', 'new_v_pool_w', 'new_v_pool_scale', 'new_v_even_w_out', 'new_v_ssm_w_in', 'new_v_ssm_conv_w', 'new_v_ssm_conv_b', 'new_v_ssm_dt_bias', 'new_v_ssm_a_log', 'new_v_ssm_d', 'new_v_ssm_norm', 'new_v_ssm_w_out', 'new_v_final_norm']
TWIN_LEAF_KINDS = {'loss': 'loss', 'grad_x': 'grad_x', 'grad_ada_w': 'grad_w', 'grad_ada_b': 'grad_w', 'grad_norm_mix': 'grad_w', 'grad_norm_ffn': 'grad_w', 'grad_ffn_w1': 'grad_w', 'grad_ffn_w2': 'grad_w', 'grad_even_w_in': 'grad_w', 'grad_pool_w': 'grad_w', 'grad_pool_scale': 'grad_w', 'grad_even_w_out': 'grad_w', 'grad_ssm_w_in': 'grad_w', 'grad_ssm_conv_w': 'grad_w', 'grad_ssm_conv_b': 'grad_w', 'grad_ssm_dt_bias': 'grad_w', 'grad_ssm_a_log': 'grad_w', 'grad_ssm_d': 'grad_w', 'grad_ssm_norm': 'grad_w', 'grad_ssm_w_out': 'grad_w', 'grad_final_norm': 'grad_w', 'delta_ada_w': 'delta_w', 'delta_ada_b': 'delta_w', 'delta_norm_mix': 'delta_w', 'delta_norm_ffn': 'delta_w', 'delta_ffn_w1': 'delta_w', 'delta_ffn_w2': 'delta_w', 'delta_even_w_in': 'delta_w', 'delta_pool_w': 'delta_w', 'delta_pool_scale': 'delta_w', 'delta_even_w_out': 'delta_w', 'delta_ssm_w_in': 'delta_w', 'delta_ssm_conv_w': 'delta_w', 'delta_ssm_conv_b': 'delta_w', 'delta_ssm_dt_bias': 'delta_w', 'delta_ssm_a_log': 'delta_w', 'delta_ssm_d': 'delta_w', 'delta_ssm_norm': 'delta_w', 'delta_ssm_w_out': 'delta_w', 'delta_final_norm': 'delta_w', 'new_m_ada_w': 'new_m', 'new_m_ada_b': 'new_m', 'new_m_norm_mix': 'new_m', 'new_m_norm_ffn': 'new_m', 'new_m_ffn_w1': 'new_m', 'new_m_ffn_w2': 'new_m', 'new_m_even_w_in': 'new_m', 'new_m_pool_w': 'new_m', 'new_m_pool_scale': 'new_m', 'new_m_even_w_out': 'new_m', 'new_m_ssm_w_in': 'new_m', 'new_m_ssm_conv_w': 'new_m', 'new_m_ssm_conv_b': 'new_m', 'new_m_ssm_dt_bias': 'new_m', 'new_m_ssm_a_log': 'new_m', 'new_m_ssm_d': 'new_m', 'new_m_ssm_norm': 'new_m', 'new_m_ssm_w_out': 'new_m', 'new_m_final_norm': 'new_m', 'new_v_ada_w': 'new_v', 'new_v_ada_b': 'new_v', 'new_v_norm_mix': 'new_v', 'new_v_norm_ffn': 'new_v', 'new_v_ffn_w1': 'new_v', 'new_v_ffn_w2': 'new_v', 'new_v_even_w_in': 'new_v', 'new_v_pool_w': 'new_v', 'new_v_pool_scale': 'new_v', 'new_v_even_w_out': 'new_v', 'new_v_ssm_w_in': 'new_v', 'new_v_ssm_conv_w': 'new_v', 'new_v_ssm_conv_b': 'new_v', 'new_v_ssm_dt_bias': 'new_v', 'new_v_ssm_a_log': 'new_v', 'new_v_ssm_d': 'new_v', 'new_v_ssm_norm': 'new_v', 'new_v_ssm_w_out': 'new_v', 'new_v_final_norm': 'new_v'}


def _forward(args):
    return _fwd_reference(*[args[k] for k in FWD_PARAMS])


def _output_shape():
    def fwd():
        inp = _fwd_setup_inputs(0)
        return _fwd_reference(*[inp[k] for k in FWD_PARAMS])
    out = _jax.eval_shape(fwd)
    return out.shape, out.dtype

N_MICROBATCH = 1
ADAM_LR = 0.001
ADAM_B1 = 0.9
ADAM_B2 = 0.999
ADAM_EPS = 1e-08
ADAM_WD = 0.01
ADAM_STEP = 10
PER_EXAMPLE_BATCH_AXIS = {'x': 0, 'c': 0, 'loss_target': 0}
SHARED_INPUTS = []
_WEIGHT_DTYPES = {'ada_w': _jnp.float32, 'ada_b': _jnp.float32, 'norm_mix': _jnp.float32, 'norm_ffn': _jnp.float32, 'ffn_w1': _jnp.float32, 'ffn_w2': _jnp.float32, 'even_w_in': _jnp.float32, 'pool_w': _jnp.float32, 'pool_scale': _jnp.float32, 'even_w_out': _jnp.float32, 'ssm_w_in': _jnp.float32, 'ssm_conv_w': _jnp.float32, 'ssm_conv_b': _jnp.float32, 'ssm_dt_bias': _jnp.float32, 'ssm_a_log': _jnp.float32, 'ssm_d': _jnp.float32, 'ssm_norm': _jnp.float32, 'ssm_w_out': _jnp.float32, 'final_norm': _jnp.float32}
MOMENT_SCALE = {'ada_w': 5.921885e-02, 'ada_b': 1.075394e-01, 'norm_mix': 3.930970e-02, 'norm_ffn': 5.364282e-02, 'ffn_w1': 2.804269e-02, 'ffn_w2': 5.215839e-02, 'even_w_in': 1.792677e-02, 'pool_w': 5.093652e-02, 'pool_scale': 5.217306e-02, 'even_w_out': 2.679997e-02, 'ssm_w_in': 2.269903e-02, 'ssm_conv_w': 2.043907e-02, 'ssm_conv_b': 2.604293e-02, 'ssm_dt_bias': 5.040896e-02, 'ssm_a_log': 9.669627e-02, 'ssm_d': 1.113081e-01, 'ssm_norm': 2.302056e-02, 'ssm_w_out': 3.397593e-02, 'final_norm': 3.225485e+01}


def _to_microbatches(a, axis):
    t = _jnp.moveaxis(a, axis, 0)
    t = t.reshape((N_MICROBATCH, t.shape[0] // N_MICROBATCH) + t.shape[1:])
    return _jnp.moveaxis(t, 1, axis + 1)


def setup_inputs(seed: int = 0) -> dict:
    inp = _fwd_setup_inputs(seed)
    key = _jax.random.fold_in(_jax.random.key(seed), 7919)
    shape, _ = _output_shape()
    out = dict(inp)
    out["loss_target"] = _jax.random.normal(_jax.random.fold_in(key, 0), shape, _jnp.float32)
    for i, name in enumerate(TWIN_WEIGHTS):
        w = inp[name].astype(_jnp.float32)
        if MOMENT_SCALE is None:
            s = _jnp.sqrt(_jnp.mean(_jnp.square(w)) + 1e-30)
        else:
            s = MOMENT_SCALE[name]
        km, kv = _jax.random.split(_jax.random.fold_in(key, i + 1))
        out[name] = w
        out["m_" + name] = s * _jax.random.normal(km, w.shape, _jnp.float32)
        out["v_" + name] = (s * s) * _jax.random.uniform(kv, w.shape, _jnp.float32, 0.5, 1.5)
    if N_MICROBATCH > 1:
        for name, axis in PER_EXAMPLE_BATCH_AXIS.items():
            out[name] = _to_microbatches(out[name], axis)
    return {'x': out['x'], 'c': out['c'], 'ada_w': out['ada_w'], 'ada_b': out['ada_b'], 'norm_mix': out['norm_mix'], 'norm_ffn': out['norm_ffn'], 'ffn_w1': out['ffn_w1'], 'ffn_w2': out['ffn_w2'], 'even_w_in': out['even_w_in'], 'pool_w': out['pool_w'], 'pool_scale': out['pool_scale'], 'even_w_out': out['even_w_out'], 'ssm_w_in': out['ssm_w_in'], 'ssm_conv_w': out['ssm_conv_w'], 'ssm_conv_b': out['ssm_conv_b'], 'ssm_dt_bias': out['ssm_dt_bias'], 'ssm_a_log': out['ssm_a_log'], 'ssm_d': out['ssm_d'], 'ssm_norm': out['ssm_norm'], 'ssm_w_out': out['ssm_w_out'], 'final_norm': out['final_norm'], 'loss_target': out['loss_target'], 'm_ada_w': out['m_ada_w'], 'm_ada_b': out['m_ada_b'], 'm_norm_mix': out['m_norm_mix'], 'm_norm_ffn': out['m_norm_ffn'], 'm_ffn_w1': out['m_ffn_w1'], 'm_ffn_w2': out['m_ffn_w2'], 'm_even_w_in': out['m_even_w_in'], 'm_pool_w': out['m_pool_w'], 'm_pool_scale': out['m_pool_scale'], 'm_even_w_out': out['m_even_w_out'], 'm_ssm_w_in': out['m_ssm_w_in'], 'm_ssm_conv_w': out['m_ssm_conv_w'], 'm_ssm_conv_b': out['m_ssm_conv_b'], 'm_ssm_dt_bias': out['m_ssm_dt_bias'], 'm_ssm_a_log': out['m_ssm_a_log'], 'm_ssm_d': out['m_ssm_d'], 'm_ssm_norm': out['m_ssm_norm'], 'm_ssm_w_out': out['m_ssm_w_out'], 'm_final_norm': out['m_final_norm'], 'v_ada_w': out['v_ada_w'], 'v_ada_b': out['v_ada_b'], 'v_norm_mix': out['v_norm_mix'], 'v_norm_ffn': out['v_norm_ffn'], 'v_ffn_w1': out['v_ffn_w1'], 'v_ffn_w2': out['v_ffn_w2'], 'v_even_w_in': out['v_even_w_in'], 'v_pool_w': out['v_pool_w'], 'v_pool_scale': out['v_pool_scale'], 'v_even_w_out': out['v_even_w_out'], 'v_ssm_w_in': out['v_ssm_w_in'], 'v_ssm_conv_w': out['v_ssm_conv_w'], 'v_ssm_conv_b': out['v_ssm_conv_b'], 'v_ssm_dt_bias': out['v_ssm_dt_bias'], 'v_ssm_a_log': out['v_ssm_a_log'], 'v_ssm_d': out['v_ssm_d'], 'v_ssm_norm': out['v_ssm_norm'], 'v_ssm_w_out': out['v_ssm_w_out'], 'v_final_norm': out['v_final_norm']}


def _loss(weights, diff, rest, loss_target):
    with _jax.named_scope("forward"):
        args = {**rest, TWIN_DIFF_INPUT: diff, **{k: w.astype(_WEIGHT_DTYPES[k]) for k, w in weights.items()}}
        y = _forward(args)
    with _jax.named_scope("loss_head"):
        err = _jnp.square(y.astype(_jnp.float32) - loss_target)
        return 0.5 * _jnp.sum(_jnp.mean(err, axis=-1)) if err.ndim else 0.5 * err


def _adamw(w, g, m, v):
    m = ADAM_B1 * m + (1.0 - ADAM_B1) * g
    v = ADAM_B2 * v + (1.0 - ADAM_B2) * _jnp.square(g)
    m_hat = m / (1.0 - ADAM_B1 ** ADAM_STEP)
    v_hat = v / (1.0 - ADAM_B2 ** ADAM_STEP)
    delta = -ADAM_LR * (m_hat / (_jnp.sqrt(v_hat) + ADAM_EPS) + ADAM_WD * w)
    return delta, m, v


def reference(x, c, ada_w, ada_b, norm_mix, norm_ffn, ffn_w1, ffn_w2, even_w_in, pool_w, pool_scale, even_w_out, ssm_w_in, ssm_conv_w, ssm_conv_b, ssm_dt_bias, ssm_a_log, ssm_d, ssm_norm, ssm_w_out, final_norm, loss_target, m_ada_w, m_ada_b, m_norm_mix, m_norm_ffn, m_ffn_w1, m_ffn_w2, m_even_w_in, m_pool_w, m_pool_scale, m_even_w_out, m_ssm_w_in, m_ssm_conv_w, m_ssm_conv_b, m_ssm_dt_bias, m_ssm_a_log, m_ssm_d, m_ssm_norm, m_ssm_w_out, m_final_norm, v_ada_w, v_ada_b, v_norm_mix, v_norm_ffn, v_ffn_w1, v_ffn_w2, v_even_w_in, v_pool_w, v_pool_scale, v_even_w_out, v_ssm_w_in, v_ssm_conv_w, v_ssm_conv_b, v_ssm_dt_bias, v_ssm_a_log, v_ssm_d, v_ssm_norm, v_ssm_w_out, v_final_norm):
    given = dict(x=x, c=c, ada_w=ada_w, ada_b=ada_b, norm_mix=norm_mix, norm_ffn=norm_ffn, ffn_w1=ffn_w1, ffn_w2=ffn_w2, even_w_in=even_w_in, pool_w=pool_w, pool_scale=pool_scale, even_w_out=even_w_out, ssm_w_in=ssm_w_in, ssm_conv_w=ssm_conv_w, ssm_conv_b=ssm_conv_b, ssm_dt_bias=ssm_dt_bias, ssm_a_log=ssm_a_log, ssm_d=ssm_d, ssm_norm=ssm_norm, ssm_w_out=ssm_w_out, final_norm=final_norm, loss_target=loss_target, m_ada_w=m_ada_w, m_ada_b=m_ada_b, m_norm_mix=m_norm_mix, m_norm_ffn=m_norm_ffn, m_ffn_w1=m_ffn_w1, m_ffn_w2=m_ffn_w2, m_even_w_in=m_even_w_in, m_pool_w=m_pool_w, m_pool_scale=m_pool_scale, m_even_w_out=m_even_w_out, m_ssm_w_in=m_ssm_w_in, m_ssm_conv_w=m_ssm_conv_w, m_ssm_conv_b=m_ssm_conv_b, m_ssm_dt_bias=m_ssm_dt_bias, m_ssm_a_log=m_ssm_a_log, m_ssm_d=m_ssm_d, m_ssm_norm=m_ssm_norm, m_ssm_w_out=m_ssm_w_out, m_final_norm=m_final_norm, v_ada_w=v_ada_w, v_ada_b=v_ada_b, v_norm_mix=v_norm_mix, v_norm_ffn=v_norm_ffn, v_ffn_w1=v_ffn_w1, v_ffn_w2=v_ffn_w2, v_even_w_in=v_even_w_in, v_pool_w=v_pool_w, v_pool_scale=v_pool_scale, v_even_w_out=v_even_w_out, v_ssm_w_in=v_ssm_w_in, v_ssm_conv_w=v_ssm_conv_w, v_ssm_conv_b=v_ssm_conv_b, v_ssm_dt_bias=v_ssm_dt_bias, v_ssm_a_log=v_ssm_a_log, v_ssm_d=v_ssm_d, v_ssm_norm=v_ssm_norm, v_ssm_w_out=v_ssm_w_out, v_final_norm=v_final_norm)
    weights = {n: given[n] for n in TWIN_WEIGHTS}
    shared = {n: given[n] for n in SHARED_INPUTS}
    per_example = {n: given[n] for n in ['x', 'c']}
    grad_fn = _jax.value_and_grad(_loss, argnums=(0, 1))

    def one_microbatch(ex, loss_target):
        ex = dict(ex)
        diff = ex.pop(TWIN_DIFF_INPUT)
        return grad_fn(weights, diff, {**shared, **ex}, loss_target)

    if N_MICROBATCH == 1:
        loss, (grad_w, grad_x) = one_microbatch(per_example, given["loss_target"])
    else:
        def body(carry, xs):
            loss_sum, grad_sum = carry
            l_k, (gw_k, gx_k) = one_microbatch(xs[0], xs[1])
            with _jax.named_scope("update"):
                return (loss_sum + l_k, _jax.tree.map(_jnp.add, grad_sum, gw_k)), gx_k

        init = (_jnp.zeros((), _jnp.float32), _jax.tree.map(_jnp.zeros_like, weights))
        (loss, grad_w), grad_x = _jax.lax.scan(body, init, (per_example, given["loss_target"]))
    with _jax.named_scope("update"):
        delta_w, new_m, new_v = {}, {}, {}
        for n in TWIN_WEIGHTS:
            delta_w[n], new_m[n], new_v[n] = _adamw(weights[n], grad_w[n], given["m_" + n], given["v_" + n])
    return (loss, grad_x, *[grad_w[n] for n in TWIN_WEIGHTS], *[delta_w[n] for n in TWIN_WEIGHTS],
            *[new_m[n] for n in TWIN_WEIGHTS], *[new_v[n] for n in TWIN_WEIGHTS])
```

```python
import functools
import math

import jax
import jax.numpy as jnp
from jax import lax
from jax.experimental import pallas as pl
from jax.experimental.pallas import tpu as pltpu

F32 = jnp.float32
BF16 = jnp.bfloat16

N_DEV = 8
D_MODEL = 2048
NORM_EPS = 1e-6
ATTN_BLOCK = 128
ATTN_GROUPS = 3
ATTN_GROUP_WIDTH = 512
ATTN_QKV_WIDTH = 3 * ATTN_GROUPS * ATTN_GROUP_WIDTH
POOL_GROUPS = 4
POOL_GROUP_WIDTH = 128
POOL_WIDTH = 512
POOL_HALO = 16
EVEN_IN_WIDTH = ATTN_QKV_WIDTH + POOL_WIDTH
EVEN_OUT_WIDTH = 1024
SSM_D_INNER = 4096
SSM_HEADS = 64
SSM_GROUPS = 8
SSM_GROUP_WIDTH = 512
SSM_STATE = 128
SSM_CHUNK = 128
SSM_CONV = 4
SSM_CONV_DIM = 6144
SSM_IN_WIDTH = 10304
SSM_IN_PAD = 10368
FFN_HIDDEN = 8192

ADAM_LR = 0.001
ADAM_B1 = 0.9
ADAM_B2 = 0.999
ADAM_EPS = 1e-08
ADAM_WD = 0.01
ADAM_STEP = 10

VMEM_LIMIT_BYTES = 56 * 1024 * 1024
NEG_BIG = -1e30

MESH_ID = pl.DeviceIdType.MESH
ANY = pl.BlockSpec(memory_space=pl.ANY)


def _cparams(*sem):
    return pltpu.CompilerParams(dimension_semantics=tuple(sem) if sem else None, vmem_limit_bytes=VMEM_LIMIT_BYTES)


def _dot(a, b):
    return lax.dot_general(a, b, (((1,), (0,)), ((), ())), preferred_element_type=F32)


def _dot_nt(a, b):
    return lax.dot_general(a, b, (((1,), (1,)), ((), ())), preferred_element_type=F32)


def _dot_tn(a, b):
    return lax.dot_general(a, b, (((0,), (0,)), ((), ())), preferred_element_type=F32)


def _split3(v):
    hi = v.astype(BF16)
    r1 = v - hi.astype(F32)
    mid = r1.astype(BF16)
    lo = (r1 - mid.astype(F32)).astype(BF16)
    return hi, mid, lo


def _dot3_left(const_bf16, v):
    hi, mid, lo = _split3(v)
    return _dot(const_bf16, hi) + _dot(const_bf16, mid) + _dot(const_bf16, lo)


def _dot3_right(v, const_bf16):
    hi, mid, lo = _split3(v)
    return _dot(hi, const_bf16) + _dot(mid, const_bf16) + _dot(lo, const_bf16)


def _iota(shape, dim):
    return lax.broadcasted_iota(jnp.int32, shape, dim)


def _sigmoid(x):
    return 1.0 / (1.0 + jnp.exp(-x))


def _peer(k):
    x, y, c = lax.axis_index("x"), lax.axis_index("y"), lax.axis_index("c")
    px = 1 - x if k & 4 else x
    py = 1 - y if k & 2 else y
    pc = 1 - c if k & 1 else c
    return (px, py, pc), 4 * px + 2 * py + pc


def _my_index():
    return 4 * lax.axis_index("x") + 2 * lax.axis_index("y") + lax.axis_index("c")


def _exchange(name, x, *, scatter):
    shard_shape = x.shape[1:] if scatter else x.shape

    def body(x_ref, out_ref, send_sems, recv_sems, local_sem):
        me = _my_index()

        def src(idx):
            return x_ref.at[idx] if scatter else x_ref

        local = pltpu.make_async_copy(src(me), out_ref.at[me], local_sem)
        local.start()
        sends = []
        for k in range(1, N_DEV):
            peer, peer_idx = _peer(k)
            cp = pltpu.make_async_remote_copy(
                src_ref=src(peer_idx), dst_ref=out_ref.at[me], send_sem=send_sems.at[k - 1], recv_sem=recv_sems.at[k - 1],
                device_id=peer, device_id_type=MESH_ID)
            cp.start()
            sends.append(cp)
        for k in range(1, N_DEV):
            peer, peer_idx = _peer(k)
            pltpu.make_async_remote_copy(
                src_ref=src(peer_idx), dst_ref=out_ref.at[peer_idx], send_sem=send_sems.at[k - 1], recv_sem=recv_sems.at[k - 1],
                device_id=peer, device_id_type=MESH_ID).wait_recv()
        for cp in sends:
            cp.wait_send()
        local.wait()

    return pl.pallas_call(
        body, name=name,
        out_shape=jax.ShapeDtypeStruct((N_DEV,) + tuple(shard_shape), x.dtype),
        in_specs=[ANY], out_specs=ANY,
        scratch_shapes=[pltpu.SemaphoreType.DMA((N_DEV - 1,)), pltpu.SemaphoreType.DMA((N_DEV - 1,)), pltpu.SemaphoreType.DMA],
    )(x)


def _all_gather(name, x):
    return _exchange(name, x, scatter=False)


def _all_to_all(name, x):
    return _exchange(name, x, scatter=True)


_DIMS = {"nn": (((1,), (0,)), ((), ())), "nt": (((1,), (1,)), ((), ())), "tn": (((0,), (0,)), ((), ()))}


def _matmul(name, a, b, *, mode, dims, tiles, outs, epilogue, a_spec=None, b_spec=None, out_specs=None,
            extras=(), extra_specs=(), a_fn=None):
    m_dim, n_dim, k_dim = dims
    tm, tn, tk = tiles
    assert m_dim % tm == 0 and n_dim % tn == 0 and k_dim % tk == 0, (name, dims, tiles)
    nk = k_dim // tk
    if a_spec is None:
        a_spec = pl.BlockSpec((tk, tm), lambda i, j, k: (k, i)) if mode == "tn" else pl.BlockSpec((tm, tk), lambda i, j, k: (i, k))
    if b_spec is None:
        b_spec = pl.BlockSpec((tn, tk), lambda i, j, k: (j, k)) if mode == "nt" else pl.BlockSpec((tk, tn), lambda i, j, k: (k, j))
    if out_specs is None:
        out_specs = [pl.BlockSpec((tm, tn), lambda i, j, k: (i, j)) for _ in outs]
    n_ex, n_out = len(extras), len(outs)
    dn = _DIMS[mode]

    def body(a_ref, b_ref, *rest):
        ex_refs, out_refs, acc = rest[:n_ex], rest[n_ex:n_ex + n_out], rest[n_ex + n_out]
        k = pl.program_id(2)

        @pl.when(k == 0)
        def _():
            acc[...] = jnp.zeros_like(acc)

        at = a_ref[...]
        if a_fn is not None:
            at = a_fn(at)
        acc[...] += lax.dot_general(at.astype(BF16), b_ref[...].astype(BF16), dn, preferred_element_type=F32)

        @pl.when(k == nk - 1)
        def _():
            res = epilogue(acc[...], *[e[...] for e in ex_refs])
            for r, o in zip(res, out_refs):
                o[...] = r.astype(o.dtype)

    return pl.pallas_call(
        body, name=name, grid=(m_dim // tm, n_dim // tn, nk),
        in_specs=[a_spec, b_spec, *extra_specs], out_specs=out_specs,
        out_shape=[jax.ShapeDtypeStruct(s, d) for s, d in outs],
        scratch_shapes=[pltpu.VMEM((tm, tn), F32)],
        compiler_params=_cparams("parallel", "parallel", "arbitrary"),
    )(a, b, *extras)


def _epi_plain(acc):
    return (acc,)


def _epi_relu2(acc):
    return jnp.square(jnp.maximum(acc, 0.0)), acc


def _epi_resgate(acc, res, gate):
    return res + gate * acc, acc


def _epi_drelu2(acc, pre):
    return (acc * (2.0 * jnp.maximum(pre.astype(F32), 0.0)),)


def _silu(v):
    return v * _sigmoid(v)


ROW_TILE = 256


def _row_spec(width, tr=ROW_TILE):
    return pl.BlockSpec((tr, width), lambda i: (i, 0))


def _vec_spec(width):
    return pl.BlockSpec((1, width), lambda i: (0, 0))


def _stat_spec(width):
    return pl.BlockSpec((8, width), lambda i: (0, 0))


def _norm_mod_fwd(name, x, gain, scale, shift):
    s, d = x.shape

    def body(x_ref, g_ref, sc_ref, sh_ref, h_ref):
        xv = x_ref[...]
        r = lax.rsqrt(jnp.mean(xv * xv, axis=-1, keepdims=True) + NORM_EPS)
        h_ref[...] = ((xv * r * g_ref[...]) * (1.0 + sc_ref[...]) + sh_ref[...]).astype(BF16)

    return pl.pallas_call(
        body, name=name, grid=(s // ROW_TILE,),
        in_specs=[_row_spec(d), _vec_spec(d), _vec_spec(d), _vec_spec(d)], out_specs=_row_spec(d),
        out_shape=jax.ShapeDtypeStruct((s, d), BF16), compiler_params=_cparams("parallel"),
    )(x, gain, scale, shift)


def _norm_mod_bwd(name, x, gain, scale, dh, dres):
    s, d = x.shape

    def body(x_ref, g_ref, sc_ref, dh_ref, dres_ref, dx_ref, st_ref):
        @pl.when(pl.program_id(0) == 0)
        def _():
            st_ref[...] = jnp.zeros_like(st_ref)

        xv = x_ref[...]
        dhv = dh_ref[...].astype(F32)
        r = lax.rsqrt(jnp.mean(xv * xv, axis=-1, keepdims=True) + NORM_EPS)
        xh = xv * r
        n = xh * g_ref[...]
        dn = dhv * (1.0 + sc_ref[...])
        dxh = dn * g_ref[...]
        dx_ref[...] = dres_ref[...] + r * (dxh - xh * jnp.mean(dxh * xh, axis=-1, keepdims=True))
        st_ref[0:1, :] += jnp.sum(dhv, axis=0, keepdims=True)
        st_ref[1:2, :] += jnp.sum(dhv * n, axis=0, keepdims=True)
        st_ref[2:3, :] += jnp.sum(dn * xh, axis=0, keepdims=True)

    return pl.pallas_call(
        body, name=name, grid=(s // ROW_TILE,),
        in_specs=[_row_spec(d), _vec_spec(d), _vec_spec(d), _row_spec(d), _row_spec(d)],
        out_specs=[_row_spec(d), _stat_spec(d)],
        out_shape=[jax.ShapeDtypeStruct((s, d), F32), jax.ShapeDtypeStruct((8, d), F32)],
        compiler_params=_cparams("arbitrary"),
    )(x, gain, scale, dh, dres)


def _gate_bwd(name, dx, y, gate):
    s, d = dx.shape

    def body(dx_ref, y_ref, g_ref, dy_ref, st_ref):
        @pl.when(pl.program_id(0) == 0)
        def _():
            st_ref[...] = jnp.zeros_like(st_ref)

        dxv = dx_ref[...]
        dy_ref[...] = (dxv * g_ref[...]).astype(BF16)
        st_ref[0:1, :] += jnp.sum(dxv * y_ref[...].astype(F32), axis=0, keepdims=True)

    return pl.pallas_call(
        body, name=name, grid=(s // ROW_TILE,),
        in_specs=[_row_spec(d), _row_spec(d), _vec_spec(d)], out_specs=[_row_spec(d), _stat_spec(d)],
        out_shape=[jax.ShapeDtypeStruct((s, d), BF16), jax.ShapeDtypeStruct((8, d), F32)],
        compiler_params=_cparams("arbitrary"),
    )(dx, y, gate)


def _loss_head(name, x, gain, target):
    s, d = x.shape

    def body(x_ref, g_ref, t_ref, dx_ref, st_ref):
        @pl.when(pl.program_id(0) == 0)
        def _():
            st_ref[...] = jnp.zeros_like(st_ref)

        xv = x_ref[...]
        r = lax.rsqrt(jnp.mean(xv * xv, axis=-1, keepdims=True) + NORM_EPS)
        xh = xv * r
        err = xh * g_ref[...] - t_ref[...]
        dy = err * (1.0 / d)
        dxh = dy * g_ref[...]
        dx_ref[...] = r * (dxh - xh * jnp.mean(dxh * xh, axis=-1, keepdims=True))
        st_ref[0:1, :] += jnp.sum(dy * xh, axis=0, keepdims=True)
        st_ref[1:2, :] += jnp.sum(err * err, axis=0, keepdims=True) * (0.5 / d)

    return pl.pallas_call(
        body, name=name, grid=(s // ROW_TILE,),
        in_specs=[_row_spec(d), _vec_spec(d), _row_spec(d)], out_specs=[_row_spec(d), _stat_spec(d)],
        out_shape=[jax.ShapeDtypeStruct((s, d), F32), jax.ShapeDtypeStruct((8, d), F32)],
        compiler_params=_cparams("arbitrary"),
    )(x, gain, target)


def _adamw(name, parts, w, m, v):
    n_parts, rows, cols = parts.shape
    tr = rows
    for cand in (512, 256, 128, 64, 32, 16, 8):
        if rows % cand == 0 and cand * cols * 4 <= 2 * 1024 * 1024:
            tr = cand
            break
    c1 = 1.0 - ADAM_B1 ** ADAM_STEP
    c2 = 1.0 - ADAM_B2 ** ADAM_STEP

    def body(p_ref, w_ref, m_ref, v_ref, g_out, d_out, m_out, v_out):
        g = p_ref[0].astype(F32)
        for i in range(1, n_parts):
            g = g + p_ref[i].astype(F32)
        m_new = ADAM_B1 * m_ref[...] + (1.0 - ADAM_B1) * g
        v_new = ADAM_B2 * v_ref[...] + (1.0 - ADAM_B2) * (g * g)
        g_out[...] = g
        m_out[...] = m_new
        v_out[...] = v_new
        d_out[...] = -ADAM_LR * ((m_new / c1) / (jnp.sqrt(v_new / c2) + ADAM_EPS) + ADAM_WD * w_ref[...])

    spec = pl.BlockSpec((tr, cols), lambda i: (i, 0))
    return pl.pallas_call(
        body, name=name, grid=(rows // tr,),
        in_specs=[pl.BlockSpec((n_parts, tr, cols), lambda i: (0, i, 0)), spec, spec, spec],
        out_specs=[spec, spec, spec, spec],
        out_shape=[jax.ShapeDtypeStruct((rows, cols), F32)] * 4, compiler_params=_cparams("parallel"),
    )(parts, w, m, v)


def _attn_fwd(qkv):
    s = qkv.shape[1]
    nblk = s // ATTN_BLOCK

    def body(q_ref, kp_ref, kc_ref, vp_ref, vc_ref, o_ref, l_ref):
        g, j = pl.program_id(0), pl.program_id(1)
        first = lax.rem(j, jnp.int32(nblk) >> (2 * g)) == 0
        qi, kj = _iota((128, 128), 0), _iota((128, 128), 1)
        mask_c = kj <= qi
        mask_p = jnp.logical_and(kj >= qi, jnp.logical_not(first))
        low = kj < 64
        for p in range(4):
            sl = slice(128 * p, 128 * p + 128)
            q, kp, kc, vp, vc = q_ref[:, sl], kp_ref[:, sl], kc_ref[:, sl], vp_ref[:, sl], vc_ref[:, sl]
            o_pair = jnp.zeros((128, 128), F32)
            l_pair = jnp.zeros((128, 128), F32)
            for half in range(2):
                hm = low if half == 0 else jnp.logical_not(low)
                qm = jnp.where(hm, q, jnp.zeros_like(q))
                sc = jnp.where(mask_c, _dot_nt(qm, kc) * 0.125, NEG_BIG)
                sp = jnp.where(mask_p, _dot_nt(qm, kp) * 0.125, NEG_BIG)
                m = jnp.maximum(jnp.max(sc, axis=-1, keepdims=True), jnp.max(sp, axis=-1, keepdims=True))
                pc, pp = jnp.exp(sc - m), jnp.exp(sp - m)
                den = jnp.sum(pc, axis=-1, keepdims=True) + jnp.sum(pp, axis=-1, keepdims=True)
                oh = _dot((pc / den).astype(BF16), vc) + _dot((pp / den).astype(BF16), vp)
                o_pair = jnp.where(hm, oh, o_pair)
                l_pair = jnp.where(hm, m + jnp.log(den), l_pair)
            o_ref[:, sl] = o_pair
            l_ref[:, sl] = l_pair

    blk = (None, ATTN_BLOCK, ATTN_GROUP_WIDTH)
    prev = lambda j: jnp.maximum(j - 1, 0)
    return pl.pallas_call(
        body, name="attn_fwd", grid=(ATTN_GROUPS, nblk),
        in_specs=[pl.BlockSpec(blk, lambda g, j: (g, j, 0)),
                  pl.BlockSpec(blk, lambda g, j: (3 + g, prev(j), 0)), pl.BlockSpec(blk, lambda g, j: (3 + g, j, 0)),
                  pl.BlockSpec(blk, lambda g, j: (6 + g, prev(j), 0)), pl.BlockSpec(blk, lambda g, j: (6 + g, j, 0))],
        out_specs=[pl.BlockSpec(blk, lambda g, j: (g, j, 0))] * 2,
        out_shape=[jax.ShapeDtypeStruct((ATTN_GROUPS, s, ATTN_GROUP_WIDTH), F32)] * 2,
        compiler_params=_cparams("parallel", "parallel"),
    )(qkv, qkv, qkv, qkv, qkv)


def _attn_bwd(qkv, do, lse, cc):
    s = qkv.shape[1]
    nblk = s // ATTN_BLOCK

    def body(q_ref, kp_ref, kc_ref, vp_ref, vc_ref, do_ref, l_ref, c_ref, dq_ref, dk_ref, dv_ref, ck, cv):
        g, j = pl.program_id(0), pl.program_id(1)
        valid = j < nblk
        jq = jnp.minimum(j, nblk - 1)
        first = lax.rem(jq, jnp.int32(nblk) >> (2 * g)) == 0

        @pl.when(j == 0)
        def _():
            ck[...] = jnp.zeros_like(ck)
            cv[...] = jnp.zeros_like(cv)

        qi, kj = _iota((128, 128), 0), _iota((128, 128), 1)
        mask_c = jnp.logical_and(kj <= qi, valid)
        mask_p = jnp.logical_and(jnp.logical_and(kj >= qi, jnp.logical_not(first)), valid)
        low = kj < 64
        for p in range(4):
            sl = slice(128 * p, 128 * p + 128)
            q, kp, kc, vp, vc, dov = q_ref[:, sl], kp_ref[:, sl], kc_ref[:, sl], vp_ref[:, sl], vc_ref[:, sl], do_ref[:, sl]
            lse_pair, c_pair = l_ref[:, sl], c_ref[:, sl]
            dq_pair = jnp.zeros((128, 128), F32)
            dkc = jnp.zeros((128, 128), F32)
            dkp = jnp.zeros((128, 128), F32)
            dvc = jnp.zeros((128, 128), F32)
            dvp = jnp.zeros((128, 128), F32)
            for half in range(2):
                hm = low if half == 0 else jnp.logical_not(low)
                col = slice(64 * half, 64 * half + 1)
                lse_h, c_h = lse_pair[:, col], c_pair[:, col]
                qm = jnp.where(hm, q, jnp.zeros_like(q))
                dom = jnp.where(hm, dov, jnp.zeros_like(dov))
                pc = jnp.exp(jnp.where(mask_c, _dot_nt(qm, kc) * 0.125, NEG_BIG) - lse_h)
                pp = jnp.exp(jnp.where(mask_p, _dot_nt(qm, kp) * 0.125, NEG_BIG) - lse_h)
                dsc = (pc * (_dot_nt(dom, vc) + c_h) * 0.125).astype(BF16)
                dsp = (pp * (_dot_nt(dom, vp) + c_h) * 0.125).astype(BF16)
                dq_pair = jnp.where(hm, _dot(dsc, kc) + _dot(dsp, kp), dq_pair)
                dkc += _dot_tn(dsc, qm)
                dkp += _dot_tn(dsp, qm)
                dvc += _dot_tn(pc.astype(BF16), dom)
                dvp += _dot_tn(pp.astype(BF16), dom)

            @pl.when(valid)
            def _():
                dq_ref[:, sl] = dq_pair.astype(BF16)

            dk_ref[:, sl] = (ck[:, sl] + dkp).astype(BF16)
            dv_ref[:, sl] = (cv[:, sl] + dvp).astype(BF16)
            ck[:, sl] = dkc
            cv[:, sl] = dvc

    blk = (None, ATTN_BLOCK, ATTN_GROUP_WIDTH)
    cur = lambda j: jnp.minimum(j, nblk - 1)
    prev = lambda j: jnp.maximum(jnp.minimum(j, nblk - 1) - 1, 0)
    out_prev = lambda j: jnp.maximum(j - 1, 0)
    return pl.pallas_call(
        body, name="attn_bwd", grid=(ATTN_GROUPS, nblk + 1),
        in_specs=[pl.BlockSpec(blk, lambda g, j: (g, cur(j), 0)),
                  pl.BlockSpec(blk, lambda g, j: (3 + g, prev(j), 0)), pl.BlockSpec(blk, lambda g, j: (3 + g, cur(j), 0)),
                  pl.BlockSpec(blk, lambda g, j: (6 + g, prev(j), 0)), pl.BlockSpec(blk, lambda g, j: (6 + g, cur(j), 0)),
                  pl.BlockSpec(blk, lambda g, j: (g, cur(j), 0)), pl.BlockSpec(blk, lambda g, j: (g, cur(j), 0)),
                  pl.BlockSpec(blk, lambda g, j: (g, cur(j), 0))],
        out_specs=[pl.BlockSpec(blk, lambda g, j: (g, cur(j), 0)),
                   pl.BlockSpec(blk, lambda g, j: (g, out_prev(j), 0)), pl.BlockSpec(blk, lambda g, j: (g, out_prev(j), 0))],
        out_shape=[jax.ShapeDtypeStruct((ATTN_GROUPS, s, ATTN_GROUP_WIDTH), BF16)] * 3,
        scratch_shapes=[pltpu.VMEM((ATTN_BLOCK, ATTN_GROUP_WIDTH), F32)] * 2,
        compiler_params=_cparams("parallel", "arbitrary"),
    )(qkv, qkv, qkv, qkv, qkv, do, lse, cc)


MP_TILE = 256


def _merge_weights(l_ref):
    l0, l1, l2 = l_ref[0], l_ref[1], l_ref[2]
    m = jnp.maximum(jnp.maximum(l0, l1), l2)
    e0, e1, e2 = jnp.exp(l0 - m), jnp.exp(l1 - m), jnp.exp(l2 - m)
    den = e0 + e1 + e2
    return e0 / den, e1 / den, e2 / den


def _pool_diff(ucat, gi, tok):
    window = 2 << gi
    ug = ucat[:, 128 * gi:128 * gi + 128]
    acc, shift = ug, 1
    while shift < window:
        acc = acc + pltpu.roll(acc, shift, 0)
        shift *= 2
    cnt = jnp.minimum(tok + 1, window).astype(F32)
    return acc[POOL_HALO:, :] / cnt - ug[POOL_HALO:, :]


def _merge_pool_fwd(o, lse, proj, pool_w, pool_scale):
    s = o.shape[1]
    tr = MP_TILE

    def body(o_ref, l_ref, u_ref, uh_ref, pw_ref, ps_ref, cat_ref):
        i = pl.program_id(0)
        w0, w1, w2 = _merge_weights(l_ref)
        cat_ref[:, 0:512] = (w0 * o_ref[0] + w1 * o_ref[1] + w2 * o_ref[2]).astype(BF16)
        halo = jnp.where(i > 0, uh_ref[...].astype(F32), 0.0)
        ucat = jnp.concatenate([halo, u_ref[...].astype(F32)], axis=0)
        tok = i * tr + _iota((tr, 1), 0)
        for gi in range(POOL_GROUPS):
            sl = slice(128 * gi, 128 * gi + 128)
            diff = _pool_diff(ucat, gi, tok)
            yg = _dot(diff.astype(BF16), pw_ref[gi].astype(BF16)) * ps_ref[:, sl]
            cat_ref[:, 512 + 128 * gi:640 + 128 * gi] = yg.astype(BF16)

    return pl.pallas_call(
        body, name="merge_pool_fwd", grid=(s // tr,),
        in_specs=[pl.BlockSpec((3, tr, 512), lambda i: (0, i, 0)), pl.BlockSpec((3, tr, 512), lambda i: (0, i, 0)),
                  pl.BlockSpec((None, tr, 512), lambda i: (9, i, 0)),
                  pl.BlockSpec((None, POOL_HALO, 512), lambda i: (9, jnp.maximum(i * (tr // POOL_HALO) - 1, 0), 0)),
                  pl.BlockSpec((4, 128, 128), lambda i: (0, 0, 0)), pl.BlockSpec((1, 512), lambda i: (0, 0))],
        out_specs=pl.BlockSpec((tr, 1024), lambda i: (i, 0)),
        out_shape=jax.ShapeDtypeStruct((s, EVEN_OUT_WIDTH), BF16), compiler_params=_cparams("parallel"),
    )(o, lse, proj, proj, pool_w, pool_scale)


def _merge_pool_bwd(dcat, o, lse, proj, pool_w, pool_scale, head_sum):
    s = o.shape[1]
    tr = MP_TILE
    n_tiles = s // tr

    def body(da_ref, dp_ref, dph_ref, o_ref, l_ref, u_ref, uh_ref, pw_ref, ps_ref, hs_ref,
             do_ref, cc_ref, du_ref, dpw_ref, st_ref):
        i = pl.program_id(0)

        @pl.when(i == 0)
        def _():
            dpw_ref[...] = jnp.zeros_like(dpw_ref)
            st_ref[...] = jnp.zeros_like(st_ref)

        ws = _merge_weights(l_ref)
        da = da_ref[...]
        attn = ws[0] * o_ref[0] + ws[1] * o_ref[1] + ws[2] * o_ref[2]
        per_head = _dot3_right(da * attn, hs_ref[...])
        for gi in range(ATTN_GROUPS):
            do_ref[gi] = (ws[gi] * da).astype(BF16)
            cc_ref[gi] = -ws[gi] * per_head

        halo = jnp.where(i > 0, uh_ref[...].astype(F32), 0.0)
        ucat = jnp.concatenate([halo, u_ref[...].astype(F32)], axis=0)
        tok = i * tr + _iota((tr, 1), 0)
        dyp = dp_ref[...]
        dnext = jnp.where(i < n_tiles - 1, dph_ref[...], 0.0)
        dyp_ext = jnp.concatenate([dyp, dnext], axis=0)
        tok_ext = i * tr + _iota((tr + POOL_HALO, 1), 0)
        for gi in range(POOL_GROUPS):
            sl = slice(128 * gi, 128 * gi + 128)
            window = 2 << gi
            pw16 = pw_ref[gi].astype(BF16)
            d16 = _pool_diff(ucat, gi, tok).astype(BF16)
            st_ref[0:1, sl] += jnp.sum(dyp[:, sl] * _dot(d16, pw16), axis=0, keepdims=True)
            dpw_ref[gi] += _dot_tn(d16, (dyp[:, sl] * ps_ref[:, sl]).astype(BF16))
            dd = _dot_nt((dyp_ext[:, sl] * ps_ref[:, sl]).astype(BF16), pw16)
            acc = dd / jnp.minimum(tok_ext + 1, window).astype(F32)
            shift = 1
            while shift < window:
                acc = acc + pltpu.roll(acc, tr + POOL_HALO - shift, 0)
                shift *= 2
            du_ref[:, sl] = (acc[:tr, :] - dd[:tr, :]).astype(BF16)

    halo_blocks = tr // POOL_HALO
    return pl.pallas_call(
        body, name="merge_pool_bwd", grid=(n_tiles,),
        in_specs=[pl.BlockSpec((tr, 512), lambda i: (i, 0)), pl.BlockSpec((tr, 512), lambda i: (i, 1)),
                  pl.BlockSpec((POOL_HALO, 512), lambda i: (jnp.minimum((i + 1) * halo_blocks, s // POOL_HALO - 1), 1)),
                  pl.BlockSpec((3, tr, 512), lambda i: (0, i, 0)), pl.BlockSpec((3, tr, 512), lambda i: (0, i, 0)),
                  pl.BlockSpec((None, tr, 512), lambda i: (9, i, 0)),
                  pl.BlockSpec((None, POOL_HALO, 512), lambda i: (9, jnp.maximum(i * halo_blocks - 1, 0), 0)),
                  pl.BlockSpec((4, 128, 128), lambda i: (0, 0, 0)), pl.BlockSpec((1, 512), lambda i: (0, 0)),
                  pl.BlockSpec((512, 512), lambda i: (0, 0))],
        out_specs=[pl.BlockSpec((3, tr, 512), lambda i: (0, i, 0)), pl.BlockSpec((3, tr, 512), lambda i: (0, i, 0)),
                   pl.BlockSpec((tr, 512), lambda i: (i, 0)), pl.BlockSpec((4, 128, 128), lambda i: (0, 0, 0)),
                   pl.BlockSpec((8, 512), lambda i: (0, 0))],
        out_shape=[jax.ShapeDtypeStruct((3, s, 512), BF16), jax.ShapeDtypeStruct((3, s, 512), F32),
                   jax.ShapeDtypeStruct((s, 512), BF16), jax.ShapeDtypeStruct((4, 128, 128), F32),
                   jax.ShapeDtypeStruct((8, 512), F32)],
        compiler_params=_cparams("arbitrary"),
    )(dcat, dcat, dcat, o, lse, proj, proj, pool_w, pool_scale, head_sum)


CONV_TILE = 512
CONV_HALO = 8
XBC_BLOCK0 = SSM_D_INNER // 512
DT_BLOCK = (SSM_D_INNER + SSM_CONV_DIM) // 128


def _conv_fwd(proj, conv_w, conv_b):
    s = proj.shape[0]
    tr = CONV_TILE

    def body(x_ref, xh_ref, w_ref, b_ref, pre_ref, act_ref):
        i = pl.program_id(0)
        xcat = jnp.concatenate([jnp.where(i > 0, xh_ref[...], 0.0), x_ref[...]], axis=0)
        w = w_ref[...]
        pre = b_ref[...] + w[3:4, :] * xcat[CONV_HALO:, :]
        for back in range(1, SSM_CONV):
            pre = pre + w[3 - back:4 - back, :] * pltpu.roll(xcat, back, 0)[CONV_HALO:, :]
        pre_ref[...] = pre
        act_ref[...] = pre * _sigmoid(pre)

    hb = tr // CONV_HALO
    return pl.pallas_call(
        body, name="conv_fwd", grid=(s // tr, SSM_CONV_DIM // 512),
        in_specs=[pl.BlockSpec((tr, 512), lambda i, j: (i, XBC_BLOCK0 + j)),
                  pl.BlockSpec((CONV_HALO, 512), lambda i, j: (jnp.maximum(i * hb - 1, 0), XBC_BLOCK0 + j)),
                  pl.BlockSpec((SSM_CONV, 512), lambda i, j: (0, j)), pl.BlockSpec((1, 512), lambda i, j: (0, j))],
        out_specs=[pl.BlockSpec((tr, 512), lambda i, j: (i, j))] * 2,
        out_shape=[jax.ShapeDtypeStruct((s, SSM_CONV_DIM), F32)] * 2, compiler_params=_cparams("parallel", "parallel"),
    )(proj, proj, conv_w, conv_b)


def _conv_bwd(name, dact, pre, proj, conv_w, dproj, *, block0):
    s, width = dact.shape
    tr = CONV_TILE
    n_tiles = s // tr
    hb = tr // CONV_HALO

    def body(da_ref, dah_ref, pre_ref, preh_ref, x_ref, xh_ref, w_ref, _, dp_ref, st_ref):
        i = pl.program_id(1)

        @pl.when(i == 0)
        def _():
            st_ref[...] = jnp.zeros_like(st_ref)

        da_ext = jnp.concatenate([da_ref[...], jnp.where(i < n_tiles - 1, dah_ref[...], 0.0)], axis=0)
        pre_ext = jnp.concatenate([pre_ref[...], preh_ref[...]], axis=0)
        sg = _sigmoid(pre_ext)
        dpre_ext = da_ext * (sg * (1.0 + pre_ext * (1.0 - sg)))
        w = w_ref[...]
        draw = w[3:4, :] * dpre_ext[:tr, :]
        for ahead in range(1, SSM_CONV):
            draw = draw + w[3 - ahead:4 - ahead, :] * pltpu.roll(dpre_ext, tr + CONV_HALO - ahead, 0)[:tr, :]
        dp_ref[...] = draw.astype(BF16)
        dpre = dpre_ext[:tr, :]
        xcat = jnp.concatenate([jnp.where(i > 0, xh_ref[...], 0.0), x_ref[...]], axis=0)
        st_ref[3:4, :] += jnp.sum(dpre * xcat[CONV_HALO:, :], axis=0, keepdims=True)
        for back in range(1, SSM_CONV):
            st_ref[3 - back:4 - back, :] += jnp.sum(dpre * pltpu.roll(xcat, back, 0)[CONV_HALO:, :], axis=0, keepdims=True)
        st_ref[4:5, :] += jnp.sum(dpre, axis=0, keepdims=True)

    nxt = lambda i: jnp.minimum((i + 1) * hb, s // CONV_HALO - 1)
    prv = lambda i: jnp.maximum(i * hb - 1, 0)
    return pl.pallas_call(
        body, name=name, grid=(width // 512, n_tiles),
        in_specs=[pl.BlockSpec((tr, 512), lambda j, i: (i, j)), pl.BlockSpec((CONV_HALO, 512), lambda j, i: (nxt(i), j)),
                  pl.BlockSpec((tr, 512), lambda j, i: (i, block0 + j)),
                  pl.BlockSpec((CONV_HALO, 512), lambda j, i: (nxt(i), block0 + j)),
                  pl.BlockSpec((tr, 512), lambda j, i: (i, XBC_BLOCK0 + block0 + j)),
                  pl.BlockSpec((CONV_HALO, 512), lambda j, i: (prv(i), XBC_BLOCK0 + block0 + j)),
                  pl.BlockSpec((SSM_CONV, 512), lambda j, i: (0, block0 + j)), ANY],
        out_specs=[pl.BlockSpec((tr, 512), lambda j, i: (i, XBC_BLOCK0 + block0 + j)),
                   pl.BlockSpec((8, 512), lambda j, i: (0, j))],
        out_shape=[jax.ShapeDtypeStruct(dproj.shape, BF16), jax.ShapeDtypeStruct((8, width), F32)],
        input_output_aliases={7: 0}, compiler_params=_cparams("parallel", "arbitrary"),
    )(dact, dact, pre, pre, proj, proj, conv_w, dproj)


def _dt_fwd(proj, dt_bias):
    s = proj.shape[0]

    def body(x_ref, b_ref, o_ref):
        v = x_ref[...] + b_ref[...]
        o_ref[...] = jnp.maximum(v, 0.0) + jnp.log(1.0 + jnp.exp(-jnp.abs(v)))

    return pl.pallas_call(
        body, name="dt_fwd", grid=(s // CONV_TILE,),
        in_specs=[pl.BlockSpec((CONV_TILE, 128), lambda i: (i, DT_BLOCK)), pl.BlockSpec((1, 128), lambda i: (0, 0))],
        out_specs=pl.BlockSpec((CONV_TILE, 128), lambda i: (i, 0)),
        out_shape=jax.ShapeDtypeStruct((s, 128), F32), compiler_params=_cparams("parallel"),
    )(proj, dt_bias)


def _dt_bwd(ddt, proj, dt_bias, dproj):
    s = proj.shape[0]

    def body(d_ref, x_ref, b_ref, _, dp_ref, st_ref):
        @pl.when(pl.program_id(0) == 0)
        def _():
            st_ref[...] = jnp.zeros_like(st_ref)

        draw = d_ref[...] * _sigmoid(x_ref[...] + b_ref[...])
        dp_ref[...] = draw.astype(BF16)
        st_ref[0:1, :] += jnp.sum(draw, axis=0, keepdims=True)

    return pl.pallas_call(
        body, name="dt_bwd", grid=(s // CONV_TILE,),
        in_specs=[pl.BlockSpec((CONV_TILE, 128), lambda i: (i, 0)), pl.BlockSpec((CONV_TILE, 128), lambda i: (i, DT_BLOCK)),
                  pl.BlockSpec((1, 128), lambda i: (0, 0)), ANY],
        out_specs=[pl.BlockSpec((CONV_TILE, 128), lambda i: (i, DT_BLOCK)), pl.BlockSpec((8, 128), lambda i: (0, 0))],
        out_shape=[jax.ShapeDtypeStruct(dproj.shape, BF16), jax.ShapeDtypeStruct((8, 128), F32)],
        input_output_aliases={3: 0}, compiler_params=_cparams("arbitrary"),
    )(ddt, proj, dt_bias, dproj)


def _ssd_common(x_ref, b_ref, c_ref, dt_ref, al_ref):
    row, col = _iota((128, 128), 0), _iota((128, 128), 1)
    tril = row >= col
    expand = jnp.where((_iota((128, 512), 1) >> 6) == _iota((128, 512), 0), 1.0, 0.0).astype(BF16)
    dt = dt_ref[...]
    a_neg = -jnp.exp(al_ref[...])
    a_col = _dot3_left(jnp.where(tril, 1.0, 0.0).astype(BF16), dt * a_neg)
    a_exp = _dot3_right(a_col, expand)
    dt_exp = _dot3_right(dt, expand)
    x = x_ref[...]
    return dict(tril=tril, col=col, expand=expand, dt=dt, a_neg=a_neg, a_col=a_col, a_row=a_col.T, a_exp=a_exp,
                dt_exp=dt_exp, a_last=a_exp[127:128, :], x=x, xd=x * dt_exp,
                b16=b_ref[...].astype(BF16), c16=c_ref[...].astype(BF16))


def _ssd_specs(nc, order):
    return [pl.BlockSpec((SSM_CHUNK, 512), lambda g, c: (order(c), g)),
            pl.BlockSpec((SSM_CHUNK, 128), lambda g, c: (order(c), SSM_D_INNER // 128 + g)),
            pl.BlockSpec((SSM_CHUNK, 128), lambda g, c: (order(c), SSM_D_INNER // 128 + SSM_GROUPS + g)),
            pl.BlockSpec((SSM_CHUNK, 512), lambda g, c: (order(c), g)),
            pl.BlockSpec((None, SSM_CHUNK, 128), lambda g, c: (g, order(c), 0)),
            pl.BlockSpec((None, 1, 128), lambda g, c: (g, 0, 0)),
            pl.BlockSpec((1, 512), lambda g, c: (0, g)), pl.BlockSpec((1, 512), lambda g, c: (0, g))]


def _ssd_fwd(act, proj, dtc, a_log, d_skip, norm_g):
    s = act.shape[0]
    nc = s // SSM_CHUNK

    def body(x_ref, b_ref, c_ref, z_ref, dt_ref, al_ref, dsk_ref, ng_ref, y_ref, yn_ref, hin_ref, h_sc):
        @pl.when(pl.program_id(1) == 0)
        def _():
            h_sc[...] = jnp.zeros_like(h_sc)

        q = _ssd_common(x_ref, b_ref, c_ref, dt_ref, al_ref)
        gmat = _dot_nt(q["c16"], q["b16"])
        h_in = h_sc[...]
        hin_ref[...] = h_in
        zmat = _dot(q["c16"], h_in.astype(BF16))
        xd16 = q["xd"].astype(BF16)
        low = q["col"] < 64
        pieces = []
        for p in range(4):
            xs = xd16[:, 128 * p:128 * p + 128]
            acc = jnp.zeros((128, 128), F32)
            for half in range(2):
                r = 2 * p + half
                decay = jnp.exp(jnp.where(q["tril"], q["a_col"][:, r:r + 1] - q["a_row"][r:r + 1, :], NEG_BIG))
                hm = low if half == 0 else jnp.logical_not(low)
                acc += _dot((gmat * decay).astype(BF16), jnp.where(hm, xs, jnp.zeros_like(xs)))
            pieces.append(acc)
        y = jnp.concatenate(pieces, axis=1) + zmat * jnp.exp(q["a_exp"]) + q["x"] * dsk_ref[...]
        y_ref[...] = y
        w16 = (q["xd"] * jnp.exp(q["a_last"] - q["a_exp"])).astype(BF16)
        h_sc[...] = h_in * jnp.exp(q["a_last"]) + _dot_tn(q["b16"], w16)
        z = z_ref[...]
        yg = y * (z * _sigmoid(z))
        rr = lax.rsqrt(jnp.mean(yg * yg, axis=-1, keepdims=True) + NORM_EPS)
        yn_ref[...] = (yg * rr * ng_ref[...]).astype(BF16)

    blk = pl.BlockSpec((SSM_CHUNK, 512), lambda g, c: (c, g))
    return pl.pallas_call(
        body, name="ssd_fwd", grid=(SSM_GROUPS, nc), in_specs=_ssd_specs(nc, lambda c: c),
        out_specs=[blk, blk, pl.BlockSpec((None, None, SSM_STATE, 512), lambda g, c: (g, c, 0, 0))],
        out_shape=[jax.ShapeDtypeStruct((s, SSM_D_INNER), F32), jax.ShapeDtypeStruct((s, SSM_D_INNER), BF16),
                   jax.ShapeDtypeStruct((SSM_GROUPS, nc, SSM_STATE, 512), F32)],
        scratch_shapes=[pltpu.VMEM((SSM_STATE, 512), F32)], compiler_params=_cparams("parallel", "arbitrary"),
    )(act, act, act, proj, dtc, a_log, d_skip, norm_g)


def _ssd_bwd(act, proj, dtc, a_log, d_skip, norm_g, y, h_in_all, dyn):
    s = act.shape[0]
    nc = s // SSM_CHUNK

    def body(x_ref, b_ref, c_ref, z_ref, dt_ref, al_ref, dsk_ref, ng_ref, y_ref, hin_ref, dyn_ref,
             dz_ref, dx_ref, db_ref, dc_ref, ddt_ref, st_ref, dal_ref, dh_sc):
        @pl.when(pl.program_id(1) == 0)
        def _():
            dh_sc[...] = jnp.zeros_like(dh_sc)
            st_ref[...] = jnp.zeros_like(st_ref)
            dal_ref[...] = jnp.zeros_like(dal_ref)

        q = _ssd_common(x_ref, b_ref, c_ref, dt_ref, al_ref)
        x, xd, b16, c16 = q["x"], q["xd"], q["b16"], q["c16"]
        z, yv, dyn_v = z_ref[...], y_ref[...], dyn_ref[...]
        sig = _sigmoid(z)
        sil = z * sig
        yg = yv * sil
        rr = lax.rsqrt(jnp.mean(yg * yg, axis=-1, keepdims=True) + NORM_EPS)
        st_ref[0:1, :] += jnp.sum(dyn_v * yg * rr, axis=0, keepdims=True)
        t1 = dyn_v * ng_ref[...]
        dyg = rr * (t1 - yg * (rr * rr) * jnp.mean(t1 * yg, axis=-1, keepdims=True))
        dy = dyg * sil
        dz_ref[...] = (dyg * yv * (sig * (1.0 + z * (1.0 - sig)))).astype(BF16)
        st_ref[1:2, :] += jnp.sum(dy * x, axis=0, keepdims=True)
        dx = dsk_ref[...] * dy
        h_in = hin_ref[...]
        h16 = h_in.astype(BF16)
        ea = jnp.exp(q["a_exp"])
        zmat = _dot(c16, h16)
        dz16 = (dy * ea).astype(BF16)
        da_ch = dy * zmat * ea
        dcm = _dot_nt(dz16, h16)
        dh_in = _dot_tn(c16, dz16)
        dh_out = dh_sc[...]
        dho16 = dh_out.astype(BF16)
        eal = jnp.exp(q["a_last"])
        dh_in += dh_out * eal
        dal_ch = jnp.sum(dh_out * h_in, axis=0, keepdims=True) * eal
        to_end = jnp.exp(q["a_last"] - q["a_exp"])
        wmat = xd * to_end
        dbm = _dot_nt(wmat.astype(BF16), dho16)
        dw = _dot(b16, dho16)
        dxd = dw * to_end
        g_end = dw * wmat
        da_ch -= g_end
        dal_ch += jnp.sum(g_end, axis=0, keepdims=True)
        gmat = _dot_nt(c16, b16)
        gmat_t = _dot_nt(b16, c16)
        triu = _iota((128, 128), 0) <= q["col"]
        xd16, dy16 = xd.astype(BF16), dy.astype(BF16)
        low = q["col"] < 64
        da_col = jnp.zeros((128, 128), F32)
        dg = jnp.zeros((128, 128), F32)
        dg_t = jnp.zeros((128, 128), F32)
        pieces = []
        for p in range(4):
            xs, dys = xd16[:, 128 * p:128 * p + 128], dy16[:, 128 * p:128 * p + 128]
            acc = jnp.zeros((128, 128), F32)
            for half in range(2):
                r = 2 * p + half
                hm = low if half == 0 else jnp.logical_not(low)
                xm = jnp.where(hm, xs, jnp.zeros_like(xs))
                dym = jnp.where(hm, dys, jnp.zeros_like(dys))
                diff = q["a_col"][:, r:r + 1] - q["a_row"][r:r + 1, :]
                decay = jnp.exp(jnp.where(q["tril"], diff, NEG_BIG))
                decay_t = jnp.exp(jnp.where(triu, -diff, NEG_BIG))
                mm, mm_t = gmat * decay, gmat_t * decay_t
                dmm, dmm_t = _dot_nt(dym, xm), _dot_nt(xm, dym)
                acc += _dot(mm_t.astype(BF16), dym)
                dg += dmm * decay
                dg_t += dmm_t * decay_t
                rs = jnp.sum(dmm * mm, axis=-1, keepdims=True) - jnp.sum(dmm_t * mm_t, axis=-1, keepdims=True)
                da_col += jnp.where(q["col"] == r, rs, 0.0)
            pieces.append(acc)
        dxd += jnp.concatenate(pieces, axis=1)
        dcm += _dot(dg.astype(BF16), b16)
        dbm += _dot(dg_t.astype(BF16), c16)
        fold = jnp.where((_iota((512, 128), 0) >> 6) == _iota((512, 128), 1), 1.0, 0.0).astype(BF16)
        da_ch += jnp.where(_iota((128, 1), 0) == 127, dal_ch, 0.0)
        da_col += _dot3_right(da_ch, fold)
        d_dta = _dot3_left(jnp.where(triu, 1.0, 0.0).astype(BF16), da_col)
        ddt_ref[...] = d_dta * q["a_neg"] + _dot3_right(dxd * x, fold)
        dal_ref[0:1, :] += jnp.sum(d_dta * q["dt"] * q["a_neg"], axis=0, keepdims=True)
        dx_ref[...] = dx + dxd * q["dt_exp"]
        db_ref[...] = dbm
        dc_ref[...] = dcm
        dh_sc[...] = dh_in

    rev = lambda c: nc - 1 - c
    blk = pl.BlockSpec((SSM_CHUNK, 512), lambda g, c: (rev(c), g))
    small = pl.BlockSpec((SSM_CHUNK, 128), lambda g, c: (rev(c), g))
    return pl.pallas_call(
        body, name="ssd_bwd", grid=(SSM_GROUPS, nc),
        in_specs=_ssd_specs(nc, rev) + [blk, pl.BlockSpec((None, None, SSM_STATE, 512), lambda g, c: (g, rev(c), 0, 0)), blk],
        out_specs=[blk, blk, small, small, pl.BlockSpec((None, SSM_CHUNK, 128), lambda g, c: (g, rev(c), 0)),
                   pl.BlockSpec((8, 512), lambda g, c: (0, g)), pl.BlockSpec((None, 8, 128), lambda g, c: (g, 0, 0))],
        out_shape=[jax.ShapeDtypeStruct((s, SSM_IN_PAD), BF16), jax.ShapeDtypeStruct((s, SSM_D_INNER), F32),
                   jax.ShapeDtypeStruct((s, SSM_GROUPS * SSM_STATE), F32), jax.ShapeDtypeStruct((s, SSM_GROUPS * SSM_STATE), F32),
                   jax.ShapeDtypeStruct((SSM_GROUPS, s, 128), F32), jax.ShapeDtypeStruct((8, SSM_D_INNER), F32),
                   jax.ShapeDtypeStruct((SSM_GROUPS, 8, 128), F32)],
        scratch_shapes=[pltpu.VMEM((SSM_STATE, 512), F32)], compiler_params=_cparams("parallel", "arbitrary"),
    )(act, act, act, proj, dtc, a_log, d_skip, norm_g, y, h_in_all, dyn)


MM_TM = 1024
MM_TK = 512
ATTN_DILATIONS = (1, 4, 16)


def _to_classes(t, d):
    s, c = t.shape
    return t if d == 1 else t.reshape(s // d, d, c).transpose(1, 0, 2).reshape(s, c)


def _from_classes(t, d):
    s, c = t.shape
    return t if d == 1 else t.reshape(d, s // d, c).transpose(1, 0, 2).reshape(s, c)


def _ij(tm, tn):
    return pl.BlockSpec((tm, tn), lambda i, j, k: (i, j))


def _mm_plain(name, a, b, mode, dims, out_dtype, tn=1024, tk=MM_TK, **kw):
    return _matmul(name, a, b, mode=mode, dims=dims, tiles=(MM_TM, tn, tk), outs=[((dims[0], dims[1]), out_dtype)],
                   epilogue=_epi_plain, **kw)[0]


def _mm_resgate(name, a, b, res, gate, k_dim):
    s, d = res.shape
    return _matmul(name, a, b, mode="nn", dims=(s, d, k_dim), tiles=(MM_TM, 1024, MM_TK),
                   outs=[((s, d), F32), ((s, d), BF16)], epilogue=_epi_resgate, extras=[res, gate],
                   extra_specs=[_ij(MM_TM, 1024), pl.BlockSpec((1, 1024), lambda i, j, k: (0, j))])


def _ffn_fwd(tag, x_in, gain, scale, shift, gate, w1, w2):
    s, d = x_in.shape
    h = _norm_mod_fwd(tag + "_norm2", x_in, gain, scale, shift)
    rr, pre = _matmul(tag + "_ffn1", h, w1, mode="nn", dims=(s, FFN_HIDDEN, d), tiles=(MM_TM, 1024, MM_TK),
                      outs=[((s, FFN_HIDDEN), BF16)] * 2, epilogue=_epi_relu2)
    x_out, f = _mm_resgate(tag + "_ffn2", rr, w2, x_in, gate, FFN_HIDDEN)
    return x_out, (h, rr, pre, f)


def _ffn_bwd(tag, dx_out, x_in, saved, gain, scale, gate, w1, w2):
    s, d = x_in.shape
    h, rr, pre, f = saved
    dy, st_gate = _gate_bwd(tag + "_gate2_bwd", dx_out, f, gate)
    da = _matmul(tag + "_ffn2_dx", dy, w2, mode="nt", dims=(s, FFN_HIDDEN, d), tiles=(MM_TM, 1024, MM_TK),
                 outs=[((s, FFN_HIDDEN), BF16)], epilogue=_epi_drelu2, extras=[pre], extra_specs=[_ij(MM_TM, 1024)])[0]
    dw2 = _mm_plain(tag + "_ffn2_dw", rr, dy, "tn", (FFN_HIDDEN, d, s), BF16)
    dh = _mm_plain(tag + "_ffn1_dx", da, w1, "nt", (s, d, FFN_HIDDEN), F32)
    dw1 = _mm_plain(tag + "_ffn1_dw", h, da, "tn", (d, FFN_HIDDEN, s), BF16)
    dx_in, st_norm = _norm_mod_bwd(tag + "_norm2_bwd", x_in, gain, scale, dh, dx_out)
    return dx_in, dw1, dw2, st_gate[0], st_norm


def _local_step(x, target, mod, wts, sm):
    s, d = x.shape
    mv = [[mod[i, k].reshape(1, d) for k in range(6)] for i in range(2)]
    nm = [sm["norm_mix"][i].reshape(1, d) for i in range(2)]
    nf = [sm["norm_ffn"][i].reshape(1, d) for i in range(2)]
    pool_w, pool_scale = sm["pool_w"].reshape(4, 128, 128), sm["pool_scale"].reshape(1, POOL_WIDTH)

    sh1, sc1, g1, sh2, sc2, g2 = mv[0]
    h1 = _norm_mod_fwd("l0_norm1", x, nm[0], sc1, sh1)
    slab = pl.BlockSpec((None, MM_TM, 512), lambda i, j, k: (j, i, 0))
    proj0 = _matmul("l0_in", h1, wts["even_w_in"], mode="nn", dims=(s, EVEN_IN_WIDTH, d), tiles=(MM_TM, 512, MM_TK),
                    outs=[((EVEN_IN_WIDTH // 512, s, 512), BF16)], out_specs=[slab], epilogue=_epi_plain)[0]
    qkv_p = jnp.stack([_to_classes(proj0[kind * 3 + g], ATTN_DILATIONS[g]) for kind in range(3) for g in range(3)])
    o_p, lse_p = _attn_fwd(qkv_p)
    o = jnp.stack([_from_classes(o_p[g], ATTN_DILATIONS[g]) for g in range(3)])
    lse = jnp.stack([_from_classes(lse_p[g], ATTN_DILATIONS[g]) for g in range(3)])
    cat = _merge_pool_fwd(o, lse, proj0, pool_w, pool_scale)
    x1, y0 = _mm_resgate("l0_out", cat, wts["even_w_out"], x, g1, EVEN_OUT_WIDTH)
    x2, ffn0 = _ffn_fwd("l0", x1, nf[0], sc2, sh2, g2, wts["ffn_w1"][0], wts["ffn_w2"][0])

    th1, tc1, tg1, th2, tc2, tg2 = mv[1]
    h3 = _norm_mod_fwd("l1_norm1", x2, nm[1], tc1, th1)
    proj1 = _mm_plain("l1_in", h3, wts["ssm_w_in"], "nn", (s, SSM_IN_PAD, d), F32, tn=1152)
    conv_w = sm["ssm_conv_w"].reshape(SSM_CONV, SSM_CONV_DIM)
    pre, act = _conv_fwd(proj1, conv_w, sm["ssm_conv_b"].reshape(1, SSM_CONV_DIM))
    dt_bias = jnp.pad(sm["ssm_dt_bias"].reshape(1, SSM_HEADS), ((0, 0), (0, 128 - SSM_HEADS)))
    dt_full = _dt_fwd(proj1, dt_bias)
    dtc = jnp.pad(dt_full[:, :SSM_HEADS].reshape(s, SSM_GROUPS, 8).transpose(1, 0, 2), ((0, 0), (0, 0), (0, 120)))
    a_log = jnp.pad(sm["ssm_a_log"].reshape(SSM_GROUPS, 1, 8), ((0, 0), (0, 0), (0, 120)))
    d_skip = jnp.repeat(sm["ssm_d"].reshape(SSM_HEADS), SSM_D_INNER // SSM_HEADS).reshape(1, SSM_D_INNER)
    norm_g = sm["ssm_norm"].reshape(1, SSM_D_INNER)
    y, yn, h_in_all = _ssd_fwd(act, proj1, dtc, a_log, d_skip, norm_g)
    x3, y1 = _mm_resgate("l1_out", yn, wts["ssm_w_out"], x2, tg1, SSM_D_INNER)
    x4, ffn1 = _ffn_fwd("l1", x3, nf[1], tc2, th2, tg2, wts["ffn_w1"][1], wts["ffn_w2"][1])

    dx4, st_loss = _loss_head("loss_head", x4, sm["final_norm"].reshape(1, d), target)
    loss = jnp.sum(st_loss[1])

    dx3, dw1_1, dw2_1, dg2_1, st_n2_1 = _ffn_bwd("l1", dx4, x3, ffn1, nf[1], tc2, tg2, wts["ffn_w1"][1], wts["ffn_w2"][1])
    dy1, st_g1_1 = _gate_bwd("l1_gate1_bwd", dx3, y1, tg1)
    dyn = _mm_plain("l1_out_dx", dy1, wts["ssm_w_out"], "nt", (s, SSM_D_INNER, d), F32)
    dw_sout = _mm_plain("l1_out_dw", yn, dy1, "tn", (SSM_D_INNER, d, s), BF16)
    dproj1, dxs, dbm, dcm, ddt, st_ssd, d_alog = _ssd_bwd(act, proj1, dtc, a_log, d_skip, norm_g, y, h_in_all, dyn)
    dproj1, st_cx = _conv_bwd("conv_bwd_x", dxs, pre, proj1, conv_w, dproj1, block0=0)
    dproj1, st_cb = _conv_bwd("conv_bwd_b", dbm, pre, proj1, conv_w, dproj1, block0=SSM_D_INNER // 512)
    dproj1, st_cc = _conv_bwd("conv_bwd_c", dcm, pre, proj1, conv_w, dproj1, block0=SSM_D_INNER // 512 + 2)
    ddt_rows = jnp.pad(ddt[:, :, :8].transpose(1, 0, 2).reshape(s, SSM_HEADS), ((0, 0), (0, 128 - SSM_HEADS)))
    dproj1, st_dt = _dt_bwd(ddt_rows, proj1, dt_bias, dproj1)
    dh3 = _mm_plain("l1_in_dx", dproj1, wts["ssm_w_in"], "nt", (s, d, SSM_IN_PAD), F32, tk=1152)
    dw_sin = _mm_plain("l1_in_dw", h3, dproj1, "tn", (d, SSM_IN_PAD, s), BF16, tn=1152)
    dx2, st_n1_1 = _norm_mod_bwd("l1_norm1_bwd", x2, nm[1], tc1, dh3, dx3)

    dx1, dw1_0, dw2_0, dg2_0, st_n2_0 = _ffn_bwd("l0", dx2, x1, ffn0, nf[0], sc2, g2, wts["ffn_w1"][0], wts["ffn_w2"][0])
    dy0, st_g1_0 = _gate_bwd("l0_gate1_bwd", dx1, y0, g1)
    dcat = _mm_plain("l0_out_dx", dy0, wts["even_w_out"], "nt", (s, EVEN_OUT_WIDTH, d), F32)
    dw_eout = _mm_plain("l0_out_dw", cat, dy0, "tn", (EVEN_OUT_WIDTH, d, s), BF16)
    lane = jnp.arange(512) // 64
    head_sum = (lane[:, None] == lane[None, :]).astype(BF16)
    do, cc, du, d_pool_w, st_pool = _merge_pool_bwd(dcat, o, lse, proj0, pool_w, pool_scale, head_sum)
    do_p = jnp.stack([_to_classes(do[g], ATTN_DILATIONS[g]) for g in range(3)])
    cc_p = jnp.stack([_to_classes(cc[g], ATTN_DILATIONS[g]) for g in range(3)])
    dq, dk, dv = _attn_bwd(qkv_p, do_p, lse_p, cc_p)
    dproj0 = jnp.stack([_from_classes(t[g], ATTN_DILATIONS[g]) for t in (dq, dk, dv) for g in range(3)] + [du])
    dh1 = _mm_plain("l0_in_dx", dproj0, wts["even_w_in"], "nt", (s, d, EVEN_IN_WIDTH), F32,
                    a_spec=pl.BlockSpec((None, MM_TM, 512), lambda i, j, k: (k, i, 0)))
    dw_ein = _mm_plain("l0_in_dw", h1, dproj0, "tn", (d, EVEN_IN_WIDTH, s), BF16, tn=512,
                       b_spec=pl.BlockSpec((None, MM_TK, 512), lambda i, j, k: (j, k, 0)))
    grad_x, st_n1_0 = _norm_mod_bwd("l0_norm1_bwd", x, nm[0], sc1, dh1, dx1)

    dmod = jnp.stack([
        jnp.stack([st_n1_0[0], st_n1_0[1], st_g1_0[0], st_n2_0[0], st_n2_0[1], dg2_0]),
        jnp.stack([st_n1_1[0], st_n1_1[1], st_g1_1[0], st_n2_1[0], st_n2_1[1], dg2_1])])
    st_conv = jnp.concatenate([st_cx, st_cb, st_cc], axis=1)
    big = dict(even_w_in=dw_ein, even_w_out=dw_eout, ffn_w1=(dw1_0, dw1_1), ffn_w2=(dw2_0, dw2_1),
               ssm_w_in=dw_sin[:, :SSM_IN_WIDTH], ssm_w_out=dw_sout)
    small = dict(
        norm_mix=jnp.stack([st_n1_0[2], st_n1_1[2]]), norm_ffn=jnp.stack([st_n2_0[2], st_n2_1[2]]),
        pool_w=d_pool_w, pool_scale=st_pool[0], ssm_conv_w=st_conv[:SSM_CONV], ssm_conv_b=st_conv[SSM_CONV],
        ssm_dt_bias=st_dt[0, :SSM_HEADS], ssm_a_log=d_alog[:, 0, :8].reshape(SSM_HEADS),
        ssm_d=jnp.sum(st_ssd[1].reshape(SSM_HEADS, SSM_D_INNER // SSM_HEADS), axis=-1), ssm_norm=st_ssd[0],
        final_norm=st_loss[0])
    return loss, grad_x, big, dmod, small


PACK_WIDTH = 1024
WEIGHT_ORDER = ("ada_w", "ada_b", "norm_mix", "norm_ffn", "ffn_w1", "ffn_w2", "even_w_in", "pool_w", "pool_scale",
                "even_w_out", "ssm_w_in", "ssm_conv_w", "ssm_conv_b", "ssm_dt_bias", "ssm_a_log", "ssm_d", "ssm_norm",
                "ssm_w_out", "final_norm")
BIG_LAYERS = (
    (("even_w_in", 0, D_MODEL, EVEN_IN_WIDTH // N_DEV, True), ("even_w_out", 0, EVEN_OUT_WIDTH, D_MODEL // N_DEV, True),
     ("ffn_w1", 0, D_MODEL, FFN_HIDDEN // N_DEV, True), ("ffn_w2", 0, FFN_HIDDEN // N_DEV, D_MODEL, False)),
    (("ssm_w_in", 0, D_MODEL, SSM_IN_WIDTH // N_DEV, True), ("ssm_w_out", 0, SSM_D_INNER // N_DEV, D_MODEL, False),
     ("ffn_w1", 1, D_MODEL, FFN_HIDDEN // N_DEV, True), ("ffn_w2", 1, FFN_HIDDEN // N_DEV, D_MODEL, False)),
)
SMALL_REPLICATED = ("norm_mix", "norm_ffn", "pool_w", "pool_scale", "ssm_dt_bias", "ssm_a_log", "ssm_d", "final_norm")
SMALL_SHARDED = ("ssm_conv_w", "ssm_conv_b", "ssm_norm")


def _pack(flat_parts, width, lead=()):
    flat = jnp.concatenate(flat_parts, axis=-1)
    n = flat.shape[-1]
    rows = -(-n // (8 * width)) * 8
    flat = jnp.pad(flat, [(0, 0)] * len(lead) + [(0, rows * width - n)])
    return flat.reshape(*lead, rows, width)


def _unpack(packed, shapes, lead=()):
    flat = packed.reshape(*lead, -1)
    out, off = [], 0
    for shp in shapes:
        n = math.prod(shp)
        out.append(flat[..., off:off + n].reshape(*lead, *shp))
        off += n
    return out


def _shards_of(full, rows, cols, by_cols):
    if by_cols:
        return full.reshape(rows, N_DEV, cols).transpose(1, 0, 2)
    return full.reshape(N_DEV, rows, cols)


def _full_of(shards, rows, cols, by_cols):
    if by_cols:
        return shards.transpose(1, 0, 2).reshape(rows, N_DEV * cols)
    return shards.reshape(N_DEV * rows, cols)


def kernel(x, c, ada_w, ada_b, norm_mix, norm_ffn, ffn_w1, ffn_w2, even_w_in, pool_w, pool_scale, even_w_out, ssm_w_in, ssm_conv_w, ssm_conv_b, ssm_dt_bias, ssm_a_log, ssm_d, ssm_norm, ssm_w_out, final_norm, loss_target, m_ada_w, m_ada_b, m_norm_mix, m_norm_ffn, m_ffn_w1, m_ffn_w2, m_even_w_in, m_pool_w, m_pool_scale, m_even_w_out, m_ssm_w_in, m_ssm_conv_w, m_ssm_conv_b, m_ssm_dt_bias, m_ssm_a_log, m_ssm_d, m_ssm_norm, m_ssm_w_out, m_final_norm, v_ada_w, v_ada_b, v_norm_mix, v_norm_ffn, v_ffn_w1, v_ffn_w2, v_even_w_in, v_pool_w, v_pool_scale, v_even_w_out, v_ssm_w_in, v_ssm_conv_w, v_ssm_conv_b, v_ssm_dt_bias, v_ssm_a_log, v_ssm_d, v_ssm_norm, v_ssm_w_out, v_final_norm):
    w = dict(ada_w=ada_w, ada_b=ada_b, norm_mix=norm_mix, norm_ffn=norm_ffn, ffn_w1=ffn_w1, ffn_w2=ffn_w2,
             even_w_in=even_w_in, pool_w=pool_w, pool_scale=pool_scale, even_w_out=even_w_out, ssm_w_in=ssm_w_in,
             ssm_conv_w=ssm_conv_w, ssm_conv_b=ssm_conv_b, ssm_dt_bias=ssm_dt_bias, ssm_a_log=ssm_a_log, ssm_d=ssm_d,
             ssm_norm=ssm_norm, ssm_w_out=ssm_w_out, final_norm=final_norm)
    m = dict(ada_w=m_ada_w, ada_b=m_ada_b, norm_mix=m_norm_mix, norm_ffn=m_norm_ffn, ffn_w1=m_ffn_w1, ffn_w2=m_ffn_w2,
             even_w_in=m_even_w_in, pool_w=m_pool_w, pool_scale=m_pool_scale, even_w_out=m_even_w_out,
             ssm_w_in=m_ssm_w_in, ssm_conv_w=m_ssm_conv_w, ssm_conv_b=m_ssm_conv_b, ssm_dt_bias=m_ssm_dt_bias,
             ssm_a_log=m_ssm_a_log, ssm_d=m_ssm_d, ssm_norm=m_ssm_norm, ssm_w_out=m_ssm_w_out, final_norm=m_final_norm)
    v = dict(ada_w=v_ada_w, ada_b=v_ada_b, norm_mix=v_norm_mix, norm_ffn=v_norm_ffn, ffn_w1=v_ffn_w1, ffn_w2=v_ffn_w2,
             even_w_in=v_even_w_in, pool_w=v_pool_w, pool_scale=v_pool_scale, even_w_out=v_even_w_out,
             ssm_w_in=v_ssm_w_in, ssm_conv_w=v_ssm_conv_w, ssm_conv_b=v_ssm_conv_b, ssm_dt_bias=v_ssm_dt_bias,
             ssm_a_log=v_ssm_a_log, ssm_d=v_ssm_d, ssm_norm=v_ssm_norm, ssm_w_out=v_ssm_w_out, final_norm=v_final_norm)
    d = D_MODEL
    me = _my_index()

    sharded_shapes = [(SSM_CONV, SSM_CONV_DIM // N_DEV), (SSM_CONV_DIM // N_DEV,), (SSM_D_INNER // N_DEV,)]
    small_in = _pack([c.reshape(-1)] + [w[k].reshape(-1) for k in SMALL_SHARDED], 128)
    got = _all_gather("gather_small_in", small_in)
    c_all, conv_w_sh, conv_b_sh, norm_sh = _unpack(got, [(d,)] + sharded_shapes, lead=(N_DEV,))
    sm = {k: w[k] for k in SMALL_REPLICATED}
    sm["ssm_conv_w"] = conv_w_sh.transpose(1, 0, 2).reshape(SSM_CONV, SSM_CONV_DIM)
    sm["ssm_conv_b"] = conv_b_sh.reshape(SSM_CONV_DIM)
    sm["ssm_norm"] = norm_sh.reshape(SSM_D_INNER)

    ada_cols = 6 * d // N_DEV
    c_pad = jnp.pad(c_all, ((0, 16 - N_DEV), (0, 0)))
    mod_cols = _matmul(
        "ada_fwd", c_pad, ada_w, mode="nn", dims=(16, 2 * ada_cols, d), tiles=(16, ada_cols // 2, d), a_fn=_silu,
        b_spec=pl.BlockSpec((None, d, ada_cols // 2), lambda i, j, k: (j // 2, k, j % 2)),
        outs=[((16, 2 * ada_cols), F32)], epilogue=_epi_plain)[0]
    mod_got = _all_to_all("ada_exchange", mod_cols[:N_DEV].reshape(N_DEV, 2, ada_cols))
    mod = (mod_got.transpose(1, 0, 2).reshape(2, 6 * d) + ada_b).reshape(2, 6, d)

    wts = {}
    for li, layer in enumerate(BIG_LAYERS):
        packed = _pack([w[name][idx].astype(BF16).reshape(-1) for name, idx, _, _, _ in layer], PACK_WIDTH)
        got = _all_gather(f"gather_weights_l{li}", packed)
        parts = _unpack(got, [(rows, cols) for _, _, rows, cols, _ in layer], lead=(N_DEV,))
        for (name, idx, rows, cols, by_cols), part in zip(layer, parts):
            full = _full_of(part, rows, cols, by_cols)
            if name in ("ffn_w1", "ffn_w2"):
                wts.setdefault(name, [None, None])[idx] = full
            else:
                wts[name] = full
    wts["ssm_w_in"] = jnp.pad(wts["ssm_w_in"], ((0, 0), (0, SSM_IN_PAD - SSM_IN_WIDTH)))

    loss, grad_x, big, dmod, small = _local_step(x[0], loss_target[0], mod, wts, sm)
    loss = lax.psum(loss, ("x", "y", "c"))

    grads, delta, new_m, new_v = {}, {}, {}, {}

    def update(name, parts, shape=None):
        rows, cols = parts.shape[1:]
        res = _adamw("adamw_" + name, parts, w[name].reshape(rows, cols), m[name].reshape(rows, cols), v[name].reshape(rows, cols))
        return [r.reshape(w[name].shape if shape is None else shape) for r in res]

    dmod_all = _all_gather("gather_dmod", dmod.reshape(2, 6 * d))
    my_cols = lax.dynamic_slice_in_dim(dmod_all, me * ada_cols, ada_cols, axis=2).reshape(N_DEV, 2 * ada_cols)
    g_ada_w = _matmul(
        "ada_dw", c_pad, jnp.pad(my_cols, ((0, 16 - N_DEV), (0, 0))), mode="tn", dims=(d, 2 * ada_cols, 16),
        tiles=(1024, ada_cols // 2, 16), a_fn=_silu, outs=[((2, d, ada_cols), F32)],
        out_specs=[pl.BlockSpec((None, 1024, ada_cols // 2), lambda i, j, k: (j // 2, i, j % 2))], epilogue=_epi_plain)[0]
    grads["ada_w"], delta["ada_w"], new_m["ada_w"], new_v["ada_w"] = update("ada_w", g_ada_w.reshape(1, 2 * d, ada_cols))
    grads["ada_b"], delta["ada_b"], new_m["ada_b"], new_v["ada_b"] = update("ada_b", dmod_all)

    stacked = {"ffn_w1": [None, None], "ffn_w2": [None, None]}
    for li, layer in reversed(list(enumerate(BIG_LAYERS))):
        pieces = []
        for name, idx, rows, cols, by_cols in layer:
            full = big[name][idx] if name in stacked else big[name]
            pieces.append(_shards_of(full, rows, cols, by_cols).reshape(N_DEV, rows * cols))
        got = _all_to_all(f"exchange_grads_l{li}", _pack(pieces, PACK_WIDTH, lead=(N_DEV,)))
        parts = _unpack(got, [(rows, cols) for _, _, rows, cols, _ in layer], lead=(N_DEV,))
        for (name, idx, rows, cols, _), part in zip(layer, parts):
            res = _adamw(f"adamw_{name}_{idx}", part, w[name][idx], m[name][idx], v[name][idx])
            if name in stacked:
                stacked[name][idx] = res
            else:
                grads[name], delta[name], new_m[name], new_v[name] = [r[None] for r in res]
    for name, per_layer in stacked.items():
        grads[name], delta[name], new_m[name], new_v[name] = [jnp.stack([per_layer[0][q], per_layer[1][q]]) for q in range(4)]

    rep_shapes = [w[k].shape for k in SMALL_REPLICATED]
    pack_rep = lambda tree: _pack([tree[k].reshape(-1) for k in SMALL_REPLICATED], 128)
    rep_got = _all_gather("gather_small_grads", pack_rep(small))
    rep_res = _adamw("adamw_small_replicated", rep_got, pack_rep(w), pack_rep(m), pack_rep(v))
    for dst, packed in zip((grads, delta, new_m, new_v), rep_res):
        for k, val in zip(SMALL_REPLICATED, _unpack(packed, rep_shapes)):
            dst[k] = val

    sh_pieces = [small["ssm_conv_w"].reshape(SSM_CONV, N_DEV, -1).transpose(1, 0, 2).reshape(N_DEV, -1),
                 small["ssm_conv_b"].reshape(N_DEV, -1), small["ssm_norm"].reshape(N_DEV, -1)]
    sh_got = _all_to_all("exchange_small_grads", _pack(sh_pieces, 128, lead=(N_DEV,)))
    pack_sh = lambda tree: _pack([tree[k].reshape(-1) for k in SMALL_SHARDED], 128)
    sh_res = _adamw("adamw_small_sharded", sh_got, pack_sh(w), pack_sh(m), pack_sh(v))
    for dst, packed in zip((grads, delta, new_m, new_v), sh_res):
        for k, val in zip(SMALL_SHARDED, _unpack(packed, [w[k].shape for k in SMALL_SHARDED])):
            dst[k] = val

    out = [loss, grad_x[None]]
    for tree in (grads, delta, new_m, new_v):
        out.extend(tree[k] for k in WEIGHT_ORDER)
    return tuple(out)
```

```python
import functools
import math

import jax
import jax.numpy as jnp
from jax import lax
from jax.experimental import pallas as pl
from jax.experimental.pallas import tpu as pltpu

F32 = jnp.float32
BF16 = jnp.bfloat16

N_DEV = 8
D_MODEL = 2048
NORM_EPS = 1e-6
ATTN_BLOCK = 128
ATTN_GROUPS = 3
ATTN_GROUP_WIDTH = 512
ATTN_QKV_WIDTH = 3 * ATTN_GROUPS * ATTN_GROUP_WIDTH
POOL_GROUPS = 4
POOL_GROUP_WIDTH = 128
POOL_WIDTH = 512
POOL_HALO = 16
EVEN_IN_WIDTH = ATTN_QKV_WIDTH + POOL_WIDTH
EVEN_OUT_WIDTH = 1024
SSM_D_INNER = 4096
SSM_HEADS = 64
SSM_GROUPS = 8
SSM_GROUP_WIDTH = 512
SSM_STATE = 128
SSM_CHUNK = 128
SSM_CONV = 4
SSM_CONV_DIM = 6144
SSM_IN_WIDTH = 10304
SSM_IN_PAD = 10368
FFN_HIDDEN = 8192

ADAM_LR = 0.001
ADAM_B1 = 0.9
ADAM_B2 = 0.999
ADAM_EPS = 1e-08
ADAM_WD = 0.01
ADAM_STEP = 10

VMEM_LIMIT_BYTES = 56 * 1024 * 1024
NEG_BIG = -1e30

MESH_ID = pl.DeviceIdType.MESH
ANY = pl.BlockSpec(memory_space=pl.ANY)


def _cparams(*sem):
    return pltpu.CompilerParams(dimension_semantics=tuple(sem) if sem else None, vmem_limit_bytes=VMEM_LIMIT_BYTES)


def _dot(a, b):
    return lax.dot_general(a, b, (((1,), (0,)), ((), ())), preferred_element_type=F32)


def _dot_nt(a, b):
    return lax.dot_general(a, b, (((1,), (1,)), ((), ())), preferred_element_type=F32)


def _dot_tn(a, b):
    return lax.dot_general(a, b, (((0,), (0,)), ((), ())), preferred_element_type=F32)


def _split3(v):
    hi = v.astype(BF16)
    r1 = v - hi.astype(F32)
    mid = r1.astype(BF16)
    lo = (r1 - mid.astype(F32)).astype(BF16)
    return hi, mid, lo


def _dot3_left(const_bf16, v):
    hi, mid, lo = _split3(v)
    return _dot(const_bf16, hi) + _dot(const_bf16, mid) + _dot(const_bf16, lo)


def _dot3_right(v, const_bf16):
    hi, mid, lo = _split3(v)
    return _dot(hi, const_bf16) + _dot(mid, const_bf16) + _dot(lo, const_bf16)


def _iota(shape, dim):
    return lax.broadcasted_iota(jnp.int32, shape, dim)


def _sigmoid(x):
    return 1.0 / (1.0 + jnp.exp(-x))


def _peer(k):
    x, y, c = lax.axis_index("x"), lax.axis_index("y"), lax.axis_index("c")
    px = 1 - x if k & 4 else x
    py = 1 - y if k & 2 else y
    pc = 1 - c if k & 1 else c
    return (px, py, pc), 4 * px + 2 * py + pc


def _my_index():
    return 4 * lax.axis_index("x") + 2 * lax.axis_index("y") + lax.axis_index("c")


def _exchange(name, arrays, *, scatter):
    n = len(arrays)
    shard_shapes = [tuple(a.shape[1:]) if scatter else tuple(a.shape) for a in arrays]

    def body(*refs):
        x_refs, out_refs = refs[:n], refs[n:2 * n]
        send_sems, recv_sems, local_sems = refs[2 * n:]
        me = _my_index()

        def src(a, idx):
            return x_refs[a].at[idx] if scatter else x_refs[a]

        local = [pltpu.make_async_copy(src(a, me), out_refs[a].at[me], local_sems.at[a]) for a in range(n)]
        for cp in local:
            cp.start()
        sends = []
        for k in range(1, N_DEV):
            peer, peer_idx = _peer(k)
            for a in range(n):
                sem = a * (N_DEV - 1) + k - 1
                cp = pltpu.make_async_remote_copy(
                    src_ref=src(a, peer_idx), dst_ref=out_refs[a].at[me], send_sem=send_sems.at[sem],
                    recv_sem=recv_sems.at[sem], device_id=peer, device_id_type=MESH_ID)
                cp.start()
                sends.append(cp)
        for k in range(1, N_DEV):
            peer, peer_idx = _peer(k)
            for a in range(n):
                sem = a * (N_DEV - 1) + k - 1
                pltpu.make_async_remote_copy(
                    src_ref=src(a, peer_idx), dst_ref=out_refs[a].at[peer_idx], send_sem=send_sems.at[sem],
                    recv_sem=recv_sems.at[sem], device_id=peer, device_id_type=MESH_ID).wait_recv()
        for cp in sends:
            cp.wait_send()
        for cp in local:
            cp.wait()

    return pl.pallas_call(
        body, name=name,
        out_shape=[jax.ShapeDtypeStruct((N_DEV,) + shp, a.dtype) for shp, a in zip(shard_shapes, arrays)],
        in_specs=[ANY] * n, out_specs=[ANY] * n,
        scratch_shapes=[pltpu.SemaphoreType.DMA((n * (N_DEV - 1),)), pltpu.SemaphoreType.DMA((n * (N_DEV - 1),)),
                        pltpu.SemaphoreType.DMA((n,))],
    )(*arrays)


def _all_gather(name, x):
    return _exchange(name, [x], scatter=False)[0]


def _all_to_all(name, x):
    return _exchange(name, [x], scatter=True)[0]


def _assemble_cols(name, shards, width):
    _, k_dim, ns = shards.shape
    tr = 256

    def body(s_ref, o_ref):
        for dev in range(N_DEV):
            o_ref[:, ns * dev:ns * (dev + 1)] = s_ref[dev]
        if width > N_DEV * ns:
            o_ref[:, N_DEV * ns:] = jnp.zeros((tr, width - N_DEV * ns), o_ref.dtype)

    return pl.pallas_call(
        body, name=name, grid=(k_dim // tr,), in_specs=[pl.BlockSpec((N_DEV, tr, ns), lambda i: (0, i, 0))],
        out_specs=pl.BlockSpec((tr, width), lambda i: (i, 0)), out_shape=jax.ShapeDtypeStruct((k_dim, width), shards.dtype),
        compiler_params=_cparams("parallel"),
    )(shards)


def _split_cols(name, full, ns):
    k_dim, width = full.shape
    tr = 256

    def body(f_ref, o_ref):
        for dev in range(N_DEV):
            o_ref[dev] = f_ref[:, ns * dev:ns * (dev + 1)]

    return pl.pallas_call(
        body, name=name, grid=(k_dim // tr,), in_specs=[pl.BlockSpec((tr, width), lambda i: (i, 0))],
        out_specs=pl.BlockSpec((N_DEV, tr, ns), lambda i: (0, i, 0)),
        out_shape=jax.ShapeDtypeStruct((N_DEV, k_dim, ns), full.dtype), compiler_params=_cparams("parallel"),
    )(full)


_DIMS = {"nn": (((1,), (0,)), ((), ())), "nt": (((1,), (1,)), ((), ())), "tn": (((0,), (0,)), ((), ()))}


def _matmul(name, a, b, *, mode, dims, tiles, outs, epilogue, a_spec=None, b_spec=None, out_specs=None,
            extras=(), extra_specs=(), a_fn=None):
    m_dim, n_dim, k_dim = dims
    tm, tn, tk = tiles
    assert m_dim % tm == 0 and n_dim % tn == 0 and k_dim % tk == 0, (name, dims, tiles)
    nk = k_dim // tk
    if a_spec is None:
        a_spec = pl.BlockSpec((tk, tm), lambda i, j, k: (k, i)) if mode == "tn" else pl.BlockSpec((tm, tk), lambda i, j, k: (i, k))
    if b_spec is None:
        b_spec = pl.BlockSpec((tn, tk), lambda i, j, k: (j, k)) if mode == "nt" else pl.BlockSpec((tk, tn), lambda i, j, k: (k, j))
    if out_specs is None:
        out_specs = [pl.BlockSpec((tm, tn), lambda i, j, k: (i, j)) for _ in outs]
    n_ex, n_out = len(extras), len(outs)
    dn = _DIMS[mode]

    def body(a_ref, b_ref, *rest):
        ex_refs, out_refs, acc = rest[:n_ex], rest[n_ex:n_ex + n_out], rest[n_ex + n_out]
        k = pl.program_id(2)

        @pl.when(k == 0)
        def _():
            acc[...] = jnp.zeros_like(acc)

        at = a_ref[...]
        if a_fn is not None:
            at = a_fn(at)
        acc[...] += lax.dot_general(at.astype(BF16), b_ref[...].astype(BF16), dn, preferred_element_type=F32)

        @pl.when(k == nk - 1)
        def _():
            res = epilogue(acc[...], *[e[...] for e in ex_refs])
            for r, o in zip(res, out_refs):
                o[...] = r.astype(o.dtype)

    return pl.pallas_call(
        body, name=name, grid=(m_dim // tm, n_dim // tn, nk),
        in_specs=[a_spec, b_spec, *extra_specs], out_specs=out_specs,
        out_shape=[jax.ShapeDtypeStruct(s, d) for s, d in outs],
        scratch_shapes=[pltpu.VMEM((tm, tn), F32)],
        compiler_params=_cparams("parallel", "parallel", "arbitrary"),
    )(a, b, *extras)


def _epi_plain(acc):
    return (acc,)


def _epi_relu2(acc):
    return jnp.square(jnp.maximum(acc, 0.0)), acc


def _epi_resgate(acc, res, gate):
    return res + gate * acc, acc


def _epi_drelu2(acc, pre):
    return (acc * (2.0 * jnp.maximum(pre.astype(F32), 0.0)),)


def _silu(v):
    return v * _sigmoid(v)


ROW_TILE = 256


def _row_spec(width, tr=ROW_TILE):
    return pl.BlockSpec((tr, width), lambda i: (i, 0))


def _vec_spec(width):
    return pl.BlockSpec((1, width), lambda i: (0, 0))


def _stat_spec(width):
    return pl.BlockSpec((8, width), lambda i: (0, 0))


def _norm_mod_fwd(name, x, gain, scale, shift):
    s, d = x.shape

    def body(x_ref, g_ref, sc_ref, sh_ref, h_ref):
        xv = x_ref[...]
        r = lax.rsqrt(jnp.mean(xv * xv, axis=-1, keepdims=True) + NORM_EPS)
        h_ref[...] = ((xv * r * g_ref[...]) * (1.0 + sc_ref[...]) + sh_ref[...]).astype(BF16)

    return pl.pallas_call(
        body, name=name, grid=(s // ROW_TILE,),
        in_specs=[_row_spec(d), _vec_spec(d), _vec_spec(d), _vec_spec(d)], out_specs=_row_spec(d),
        out_shape=jax.ShapeDtypeStruct((s, d), BF16), compiler_params=_cparams("parallel"),
    )(x, gain, scale, shift)


def _norm_mod_bwd(name, x, gain, scale, dh, dres):
    s, d = x.shape

    def body(x_ref, g_ref, sc_ref, dh_ref, dres_ref, dx_ref, st_ref):
        @pl.when(pl.program_id(0) == 0)
        def _():
            st_ref[...] = jnp.zeros_like(st_ref)

        xv = x_ref[...]
        dhv = dh_ref[...].astype(F32)
        r = lax.rsqrt(jnp.mean(xv * xv, axis=-1, keepdims=True) + NORM_EPS)
        xh = xv * r
        n = xh * g_ref[...]
        dn = dhv * (1.0 + sc_ref[...])
        dxh = dn * g_ref[...]
        dx_ref[...] = dres_ref[...] + r * (dxh - xh * jnp.mean(dxh * xh, axis=-1, keepdims=True))
        st_ref[0:1, :] += jnp.sum(dhv, axis=0, keepdims=True)
        st_ref[1:2, :] += jnp.sum(dhv * n, axis=0, keepdims=True)
        st_ref[2:3, :] += jnp.sum(dn * xh, axis=0, keepdims=True)

    return pl.pallas_call(
        body, name=name, grid=(s // ROW_TILE,),
        in_specs=[_row_spec(d), _vec_spec(d), _vec_spec(d), _row_spec(d), _row_spec(d)],
        out_specs=[_row_spec(d), _stat_spec(d)],
        out_shape=[jax.ShapeDtypeStruct((s, d), F32), jax.ShapeDtypeStruct((8, d), F32)],
        compiler_params=_cparams("arbitrary"),
    )(x, gain, scale, dh, dres)


def _gate_bwd(name, dx, y, gate):
    s, d = dx.shape

    def body(dx_ref, y_ref, g_ref, dy_ref, st_ref):
        @pl.when(pl.program_id(0) == 0)
        def _():
            st_ref[...] = jnp.zeros_like(st_ref)

        dxv = dx_ref[...]
        dy_ref[...] = (dxv * g_ref[...]).astype(BF16)
        st_ref[0:1, :] += jnp.sum(dxv * y_ref[...].astype(F32), axis=0, keepdims=True)

    return pl.pallas_call(
        body, name=name, grid=(s // ROW_TILE,),
        in_specs=[_row_spec(d), _row_spec(d), _vec_spec(d)], out_specs=[_row_spec(d), _stat_spec(d)],
        out_shape=[jax.ShapeDtypeStruct((s, d), BF16), jax.ShapeDtypeStruct((8, d), F32)],
        compiler_params=_cparams("arbitrary"),
    )(dx, y, gate)


def _loss_head(name, x, gain, target):
    s, d = x.shape

    def body(x_ref, g_ref, t_ref, dx_ref, st_ref):
        @pl.when(pl.program_id(0) == 0)
        def _():
            st_ref[...] = jnp.zeros_like(st_ref)

        xv = x_ref[...]
        r = lax.rsqrt(jnp.mean(xv * xv, axis=-1, keepdims=True) + NORM_EPS)
        xh = xv * r
        err = xh * g_ref[...] - t_ref[...]
        dy = err * (1.0 / d)
        dxh = dy * g_ref[...]
        dx_ref[...] = r * (dxh - xh * jnp.mean(dxh * xh, axis=-1, keepdims=True))
        st_ref[0:1, :] += jnp.sum(dy * xh, axis=0, keepdims=True)
        st_ref[1:2, :] += jnp.sum(err * err, axis=0, keepdims=True) * (0.5 / d)

    return pl.pallas_call(
        body, name=name, grid=(s // ROW_TILE,),
        in_specs=[_row_spec(d), _vec_spec(d), _row_spec(d)], out_specs=[_row_spec(d), _stat_spec(d)],
        out_shape=[jax.ShapeDtypeStruct((s, d), F32), jax.ShapeDtypeStruct((8, d), F32)],
        compiler_params=_cparams("arbitrary"),
    )(x, gain, target)


def _adamw(name, parts, w, m, v):
    n_parts, rows, cols = parts.shape
    tr = rows
    for cand in (512, 256, 128, 64, 32, 16, 8):
        if rows % cand == 0 and cand * cols * 4 <= 2 * 1024 * 1024:
            tr = cand
            break
    c1 = 1.0 - ADAM_B1 ** ADAM_STEP
    c2 = 1.0 - ADAM_B2 ** ADAM_STEP

    def body(p_ref, w_ref, m_ref, v_ref, g_out, d_out, m_out, v_out):
        g = p_ref[0].astype(F32)
        for i in range(1, n_parts):
            g = g + p_ref[i].astype(F32)
        m_new = ADAM_B1 * m_ref[...] + (1.0 - ADAM_B1) * g
        v_new = ADAM_B2 * v_ref[...] + (1.0 - ADAM_B2) * (g * g)
        g_out[...] = g
        m_out[...] = m_new
        v_out[...] = v_new
        d_out[...] = -ADAM_LR * ((m_new / c1) / (jnp.sqrt(v_new / c2) + ADAM_EPS) + ADAM_WD * w_ref[...])

    spec = pl.BlockSpec((tr, cols), lambda i: (i, 0))
    return pl.pallas_call(
        body, name=name, grid=(rows // tr,),
        in_specs=[pl.BlockSpec((n_parts, tr, cols), lambda i: (0, i, 0)), spec, spec, spec],
        out_specs=[spec, spec, spec, spec],
        out_shape=[jax.ShapeDtypeStruct((rows, cols), F32)] * 4, compiler_params=_cparams("parallel"),
    )(parts, w, m, v)


def _attn_fwd(qkv):
    s = qkv.shape[1]
    nblk = s // ATTN_BLOCK

    def body(q_ref, kp_ref, kc_ref, vp_ref, vc_ref, o_ref, l_ref):
        g, j = pl.program_id(0), pl.program_id(1)
        first = lax.rem(j, jnp.int32(nblk) >> (2 * g)) == 0
        qi, kj = _iota((128, 128), 0), _iota((128, 128), 1)
        mask_c = kj <= qi
        mask_p = jnp.logical_and(kj >= qi, jnp.logical_not(first))
        low = kj < 64
        for p in range(4):
            sl = slice(128 * p, 128 * p + 128)
            q, kp, kc, vp, vc = q_ref[:, sl], kp_ref[:, sl], kc_ref[:, sl], vp_ref[:, sl], vc_ref[:, sl]
            o_pair = jnp.zeros((128, 128), F32)
            l_pair = jnp.zeros((128, 128), F32)
            for half in range(2):
                hm = low if half == 0 else jnp.logical_not(low)
                qm = jnp.where(hm, q, jnp.zeros_like(q))
                sc = jnp.where(mask_c, _dot_nt(qm, kc) * 0.125, NEG_BIG)
                sp = jnp.where(mask_p, _dot_nt(qm, kp) * 0.125, NEG_BIG)
                m = jnp.maximum(jnp.max(sc, axis=-1, keepdims=True), jnp.max(sp, axis=-1, keepdims=True))
                pc, pp = jnp.exp(sc - m), jnp.exp(sp - m)
                den = jnp.sum(pc, axis=-1, keepdims=True) + jnp.sum(pp, axis=-1, keepdims=True)
                oh = _dot((pc / den).astype(BF16), vc) + _dot((pp / den).astype(BF16), vp)
                o_pair = jnp.where(hm, oh, o_pair)
                l_pair = jnp.where(hm, m + jnp.log(den), l_pair)
            o_ref[:, sl] = o_pair
            l_ref[:, sl] = l_pair

    blk = (None, ATTN_BLOCK, ATTN_GROUP_WIDTH)
    prev = lambda j: jnp.maximum(j - 1, 0)
    return pl.pallas_call(
        body, name="attn_fwd", grid=(ATTN_GROUPS, nblk),
        in_specs=[pl.BlockSpec(blk, lambda g, j: (g, j, 0)),
                  pl.BlockSpec(blk, lambda g, j: (3 + g, prev(j), 0)), pl.BlockSpec(blk, lambda g, j: (3 + g, j, 0)),
                  pl.BlockSpec(blk, lambda g, j: (6 + g, prev(j), 0)), pl.BlockSpec(blk, lambda g, j: (6 + g, j, 0))],
        out_specs=[pl.BlockSpec(blk, lambda g, j: (g, j, 0))] * 2,
        out_shape=[jax.ShapeDtypeStruct((ATTN_GROUPS, s, ATTN_GROUP_WIDTH), F32)] * 2,
        compiler_params=_cparams("parallel", "parallel"),
    )(qkv, qkv, qkv, qkv, qkv)


def _attn_bwd(qkv, do, lse, cc):
    s = qkv.shape[1]
    nblk = s // ATTN_BLOCK

    def body(q_ref, kp_ref, kc_ref, vp_ref, vc_ref, do_ref, l_ref, c_ref, dq_ref, dk_ref, dv_ref, ck, cv):
        g, j = pl.program_id(0), pl.program_id(1)
        valid = j < nblk
        jq = jnp.minimum(j, nblk - 1)
        first = lax.rem(jq, jnp.int32(nblk) >> (2 * g)) == 0

        @pl.when(j == 0)
        def _():
            ck[...] = jnp.zeros_like(ck)
            cv[...] = jnp.zeros_like(cv)

        qi, kj = _iota((128, 128), 0), _iota((128, 128), 1)
        mask_c = jnp.logical_and(kj <= qi, valid)
        mask_p = jnp.logical_and(jnp.logical_and(kj >= qi, jnp.logical_not(first)), valid)
        low = kj < 64
        for p in range(4):
            sl = slice(128 * p, 128 * p + 128)
            q, kp, kc, vp, vc, dov = q_ref[:, sl], kp_ref[:, sl], kc_ref[:, sl], vp_ref[:, sl], vc_ref[:, sl], do_ref[:, sl]
            lse_pair, c_pair = l_ref[:, sl], c_ref[:, sl]
            dq_pair = jnp.zeros((128, 128), F32)
            dkc = jnp.zeros((128, 128), F32)
            dkp = jnp.zeros((128, 128), F32)
            dvc = jnp.zeros((128, 128), F32)
            dvp = jnp.zeros((128, 128), F32)
            for half in range(2):
                hm = low if half == 0 else jnp.logical_not(low)
                col = slice(64 * half, 64 * half + 1)
                lse_h, c_h = lse_pair[:, col], c_pair[:, col]
                qm = jnp.where(hm, q, jnp.zeros_like(q))
                dom = jnp.where(hm, dov, jnp.zeros_like(dov))
                pc = jnp.exp(jnp.where(mask_c, _dot_nt(qm, kc) * 0.125, NEG_BIG) - lse_h)
                pp = jnp.exp(jnp.where(mask_p, _dot_nt(qm, kp) * 0.125, NEG_BIG) - lse_h)
                dsc = (pc * (_dot_nt(dom, vc) + c_h) * 0.125).astype(BF16)
                dsp = (pp * (_dot_nt(dom, vp) + c_h) * 0.125).astype(BF16)
                dq_pair = jnp.where(hm, _dot(dsc, kc) + _dot(dsp, kp), dq_pair)
                dkc += _dot_tn(dsc, qm)
                dkp += _dot_tn(dsp, qm)
                dvc += _dot_tn(pc.astype(BF16), dom)
                dvp += _dot_tn(pp.astype(BF16), dom)

            @pl.when(valid)
            def _():
                dq_ref[:, sl] = dq_pair.astype(BF16)

            dk_ref[:, sl] = (ck[:, sl] + dkp).astype(BF16)
            dv_ref[:, sl] = (cv[:, sl] + dvp).astype(BF16)
            ck[:, sl] = dkc
            cv[:, sl] = dvc

    blk = (None, ATTN_BLOCK, ATTN_GROUP_WIDTH)
    cur = lambda j: jnp.minimum(j, nblk - 1)
    prev = lambda j: jnp.maximum(jnp.minimum(j, nblk - 1) - 1, 0)
    out_prev = lambda j: jnp.maximum(j - 1, 0)
    return pl.pallas_call(
        body, name="attn_bwd", grid=(ATTN_GROUPS, nblk + 1),
        in_specs=[pl.BlockSpec(blk, lambda g, j: (g, cur(j), 0)),
                  pl.BlockSpec(blk, lambda g, j: (3 + g, prev(j), 0)), pl.BlockSpec(blk, lambda g, j: (3 + g, cur(j), 0)),
                  pl.BlockSpec(blk, lambda g, j: (6 + g, prev(j), 0)), pl.BlockSpec(blk, lambda g, j: (6 + g, cur(j), 0)),
                  pl.BlockSpec(blk, lambda g, j: (g, cur(j), 0)), pl.BlockSpec(blk, lambda g, j: (g, cur(j), 0)),
                  pl.BlockSpec(blk, lambda g, j: (g, cur(j), 0))],
        out_specs=[pl.BlockSpec(blk, lambda g, j: (g, cur(j), 0)),
                   pl.BlockSpec(blk, lambda g, j: (g, out_prev(j), 0)), pl.BlockSpec(blk, lambda g, j: (g, out_prev(j), 0))],
        out_shape=[jax.ShapeDtypeStruct((ATTN_GROUPS, s, ATTN_GROUP_WIDTH), BF16)] * 3,
        scratch_shapes=[pltpu.VMEM((ATTN_BLOCK, ATTN_GROUP_WIDTH), F32)] * 2,
        compiler_params=_cparams("parallel", "arbitrary"),
    )(qkv, qkv, qkv, qkv, qkv, do, lse, cc)


MP_TILE = 256


def _merge_weights(l_ref):
    l0, l1, l2 = l_ref[0], l_ref[1], l_ref[2]
    m = jnp.maximum(jnp.maximum(l0, l1), l2)
    e0, e1, e2 = jnp.exp(l0 - m), jnp.exp(l1 - m), jnp.exp(l2 - m)
    den = e0 + e1 + e2
    return e0 / den, e1 / den, e2 / den


def _pool_diff(ucat, gi, tok):
    window = 2 << gi
    ug = ucat[:, 128 * gi:128 * gi + 128]
    acc, shift = ug, 1
    while shift < window:
        acc = acc + pltpu.roll(acc, shift, 0)
        shift *= 2
    cnt = jnp.minimum(tok + 1, window).astype(F32)
    return acc[POOL_HALO:, :] / cnt - ug[POOL_HALO:, :]


def _merge_pool_fwd(o, lse, proj, pool_w, pool_scale):
    s = o.shape[1]
    tr = MP_TILE

    def body(o_ref, l_ref, u_ref, uh_ref, pw_ref, ps_ref, cat_ref):
        i = pl.program_id(0)
        w0, w1, w2 = _merge_weights(l_ref)
        cat_ref[:, 0:512] = (w0 * o_ref[0] + w1 * o_ref[1] + w2 * o_ref[2]).astype(BF16)
        halo = jnp.where(i > 0, uh_ref[...].astype(F32), 0.0)
        ucat = jnp.concatenate([halo, u_ref[...].astype(F32)], axis=0)
        tok = i * tr + _iota((tr, 1), 0)
        for gi in range(POOL_GROUPS):
            sl = slice(128 * gi, 128 * gi + 128)
            diff = _pool_diff(ucat, gi, tok)
            yg = _dot(diff.astype(BF16), pw_ref[gi].astype(BF16)) * ps_ref[:, sl]
            cat_ref[:, 512 + 128 * gi:640 + 128 * gi] = yg.astype(BF16)

    return pl.pallas_call(
        body, name="merge_pool_fwd", grid=(s // tr,),
        in_specs=[pl.BlockSpec((3, tr, 512), lambda i: (0, i, 0)), pl.BlockSpec((3, tr, 512), lambda i: (0, i, 0)),
                  pl.BlockSpec((None, tr, 512), lambda i: (9, i, 0)),
                  pl.BlockSpec((None, POOL_HALO, 512), lambda i: (9, jnp.maximum(i * (tr // POOL_HALO) - 1, 0), 0)),
                  pl.BlockSpec((4, 128, 128), lambda i: (0, 0, 0)), pl.BlockSpec((1, 512), lambda i: (0, 0))],
        out_specs=pl.BlockSpec((tr, 1024), lambda i: (i, 0)),
        out_shape=jax.ShapeDtypeStruct((s, EVEN_OUT_WIDTH), BF16), compiler_params=_cparams("parallel"),
    )(o, lse, proj, proj, pool_w, pool_scale)


def _merge_pool_bwd(dcat, o, lse, proj, pool_w, pool_scale, head_sum):
    s = o.shape[1]
    tr = MP_TILE
    n_tiles = s // tr

    def body(da_ref, dp_ref, dph_ref, o_ref, l_ref, u_ref, uh_ref, pw_ref, ps_ref, hs_ref,
             do_ref, cc_ref, du_ref, dpw_ref, st_ref):
        i = pl.program_id(0)

        @pl.when(i == 0)
        def _():
            dpw_ref[...] = jnp.zeros_like(dpw_ref)
            st_ref[...] = jnp.zeros_like(st_ref)

        ws = _merge_weights(l_ref)
        da = da_ref[...]
        attn = ws[0] * o_ref[0] + ws[1] * o_ref[1] + ws[2] * o_ref[2]
        per_head = _dot3_right(da * attn, hs_ref[...])
        for gi in range(ATTN_GROUPS):
            do_ref[gi] = (ws[gi] * da).astype(BF16)
            cc_ref[gi] = -ws[gi] * per_head

        halo = jnp.where(i > 0, uh_ref[...].astype(F32), 0.0)
        ucat = jnp.concatenate([halo, u_ref[...].astype(F32)], axis=0)
        tok = i * tr + _iota((tr, 1), 0)
        dyp = dp_ref[...]
        dnext = jnp.where(i < n_tiles - 1, dph_ref[...], 0.0)
        dyp_ext = jnp.concatenate([dyp, dnext], axis=0)
        tok_ext = i * tr + _iota((tr + POOL_HALO, 1), 0)
        for gi in range(POOL_GROUPS):
            sl = slice(128 * gi, 128 * gi + 128)
            window = 2 << gi
            pw16 = pw_ref[gi].astype(BF16)
            d16 = _pool_diff(ucat, gi, tok).astype(BF16)
            st_ref[0:1, sl] += jnp.sum(dyp[:, sl] * _dot(d16, pw16), axis=0, keepdims=True)
            dpw_ref[gi] += _dot_tn(d16, (dyp[:, sl] * ps_ref[:, sl]).astype(BF16))
            dd = _dot_nt((dyp_ext[:, sl] * ps_ref[:, sl]).astype(BF16), pw16)
            acc = dd / jnp.minimum(tok_ext + 1, window).astype(F32)
            shift = 1
            while shift < window:
                acc = acc + pltpu.roll(acc, tr + POOL_HALO - shift, 0)
                shift *= 2
            du_ref[:, sl] = (acc[:tr, :] - dd[:tr, :]).astype(BF16)

    halo_blocks = tr // POOL_HALO
    return pl.pallas_call(
        body, name="merge_pool_bwd", grid=(n_tiles,),
        in_specs=[pl.BlockSpec((tr, 512), lambda i: (i, 0)), pl.BlockSpec((tr, 512), lambda i: (i, 1)),
                  pl.BlockSpec((POOL_HALO, 512), lambda i: (jnp.minimum((i + 1) * halo_blocks, s // POOL_HALO - 1), 1)),
                  pl.BlockSpec((3, tr, 512), lambda i: (0, i, 0)), pl.BlockSpec((3, tr, 512), lambda i: (0, i, 0)),
                  pl.BlockSpec((None, tr, 512), lambda i: (9, i, 0)),
                  pl.BlockSpec((None, POOL_HALO, 512), lambda i: (9, jnp.maximum(i * halo_blocks - 1, 0), 0)),
                  pl.BlockSpec((4, 128, 128), lambda i: (0, 0, 0)), pl.BlockSpec((1, 512), lambda i: (0, 0)),
                  pl.BlockSpec((512, 512), lambda i: (0, 0))],
        out_specs=[pl.BlockSpec((3, tr, 512), lambda i: (0, i, 0)), pl.BlockSpec((3, tr, 512), lambda i: (0, i, 0)),
                   pl.BlockSpec((tr, 512), lambda i: (i, 0)), pl.BlockSpec((4, 128, 128), lambda i: (0, 0, 0)),
                   pl.BlockSpec((8, 512), lambda i: (0, 0))],
        out_shape=[jax.ShapeDtypeStruct((3, s, 512), BF16), jax.ShapeDtypeStruct((3, s, 512), F32),
                   jax.ShapeDtypeStruct((s, 512), BF16), jax.ShapeDtypeStruct((4, 128, 128), F32),
                   jax.ShapeDtypeStruct((8, 512), F32)],
        compiler_params=_cparams("arbitrary"),
    )(dcat, dcat, dcat, o, lse, proj, proj, pool_w, pool_scale, head_sum)


CONV_TILE = 512
CONV_HALO = 8
XBC_BLOCK0 = SSM_D_INNER // 512
DT_BLOCK = (SSM_D_INNER + SSM_CONV_DIM) // 128


def _conv_fwd(proj, conv_w, conv_b):
    s = proj.shape[0]
    tr = CONV_TILE

    def body(x_ref, xh_ref, w_ref, b_ref, pre_ref, act_ref):
        i = pl.program_id(0)
        xcat = jnp.concatenate([jnp.where(i > 0, xh_ref[...], 0.0), x_ref[...]], axis=0)
        w = w_ref[...]
        pre = b_ref[...] + w[3:4, :] * xcat[CONV_HALO:, :]
        for back in range(1, SSM_CONV):
            pre = pre + w[3 - back:4 - back, :] * pltpu.roll(xcat, back, 0)[CONV_HALO:, :]
        pre_ref[...] = pre
        act_ref[...] = pre * _sigmoid(pre)

    hb = tr // CONV_HALO
    return pl.pallas_call(
        body, name="conv_fwd", grid=(s // tr, SSM_CONV_DIM // 512),
        in_specs=[pl.BlockSpec((tr, 512), lambda i, j: (i, XBC_BLOCK0 + j)),
                  pl.BlockSpec((CONV_HALO, 512), lambda i, j: (jnp.maximum(i * hb - 1, 0), XBC_BLOCK0 + j)),
                  pl.BlockSpec((SSM_CONV, 512), lambda i, j: (0, j)), pl.BlockSpec((1, 512), lambda i, j: (0, j))],
        out_specs=[pl.BlockSpec((tr, 512), lambda i, j: (i, j))] * 2,
        out_shape=[jax.ShapeDtypeStruct((s, SSM_CONV_DIM), F32)] * 2, compiler_params=_cparams("parallel", "parallel"),
    )(proj, proj, conv_w, conv_b)


def _conv_bwd(name, dact, pre, proj, conv_w, dproj, *, block0):
    s, width = dact.shape
    tr = CONV_TILE
    n_tiles = s // tr
    hb = tr // CONV_HALO

    def body(da_ref, dah_ref, pre_ref, preh_ref, x_ref, xh_ref, w_ref, _, dp_ref, st_ref):
        i = pl.program_id(1)

        @pl.when(i == 0)
        def _():
            st_ref[...] = jnp.zeros_like(st_ref)

        da_ext = jnp.concatenate([da_ref[...], jnp.where(i < n_tiles - 1, dah_ref[...], 0.0)], axis=0)
        pre_ext = jnp.concatenate([pre_ref[...], preh_ref[...]], axis=0)
        sg = _sigmoid(pre_ext)
        dpre_ext = da_ext * (sg * (1.0 + pre_ext * (1.0 - sg)))
        w = w_ref[...]
        draw = w[3:4, :] * dpre_ext[:tr, :]
        for ahead in range(1, SSM_CONV):
            draw = draw + w[3 - ahead:4 - ahead, :] * pltpu.roll(dpre_ext, tr + CONV_HALO - ahead, 0)[:tr, :]
        dp_ref[...] = draw.astype(BF16)
        dpre = dpre_ext[:tr, :]
        xcat = jnp.concatenate([jnp.where(i > 0, xh_ref[...], 0.0), x_ref[...]], axis=0)
        st_ref[3:4, :] += jnp.sum(dpre * xcat[CONV_HALO:, :], axis=0, keepdims=True)
        for back in range(1, SSM_CONV):
            st_ref[3 - back:4 - back, :] += jnp.sum(dpre * pltpu.roll(xcat, back, 0)[CONV_HALO:, :], axis=0, keepdims=True)
        st_ref[4:5, :] += jnp.sum(dpre, axis=0, keepdims=True)

    nxt = lambda i: jnp.minimum((i + 1) * hb, s // CONV_HALO - 1)
    prv = lambda i: jnp.maximum(i * hb - 1, 0)
    return pl.pallas_call(
        body, name=name, grid=(width // 512, n_tiles),
        in_specs=[pl.BlockSpec((tr, 512), lambda j, i: (i, j)), pl.BlockSpec((CONV_HALO, 512), lambda j, i: (nxt(i), j)),
                  pl.BlockSpec((tr, 512), lambda j, i: (i, block0 + j)),
                  pl.BlockSpec((CONV_HALO, 512), lambda j, i: (nxt(i), block0 + j)),
                  pl.BlockSpec((tr, 512), lambda j, i: (i, XBC_BLOCK0 + block0 + j)),
                  pl.BlockSpec((CONV_HALO, 512), lambda j, i: (prv(i), XBC_BLOCK0 + block0 + j)),
                  pl.BlockSpec((SSM_CONV, 512), lambda j, i: (0, block0 + j)), ANY],
        out_specs=[pl.BlockSpec((tr, 512), lambda j, i: (i, XBC_BLOCK0 + block0 + j)),
                   pl.BlockSpec((8, 512), lambda j, i: (0, j))],
        out_shape=[jax.ShapeDtypeStruct(dproj.shape, BF16), jax.ShapeDtypeStruct((8, width), F32)],
        input_output_aliases={7: 0}, compiler_params=_cparams("parallel", "arbitrary"),
    )(dact, dact, pre, pre, proj, proj, conv_w, dproj)


def _dt_fwd(proj, dt_bias):
    s = proj.shape[0]

    def body(x_ref, b_ref, o_ref):
        v = x_ref[...] + b_ref[...]
        o_ref[...] = jnp.maximum(v, 0.0) + jnp.log(1.0 + jnp.exp(-jnp.abs(v)))

    return pl.pallas_call(
        body, name="dt_fwd", grid=(s // CONV_TILE,),
        in_specs=[pl.BlockSpec((CONV_TILE, 128), lambda i: (i, DT_BLOCK)), pl.BlockSpec((1, 128), lambda i: (0, 0))],
        out_specs=pl.BlockSpec((CONV_TILE, 128), lambda i: (i, 0)),
        out_shape=jax.ShapeDtypeStruct((s, 128), F32), compiler_params=_cparams("parallel"),
    )(proj, dt_bias)


def _dt_bwd(ddt, proj, dt_bias, dproj):
    s = proj.shape[0]

    def body(d_ref, x_ref, b_ref, _, dp_ref, st_ref):
        @pl.when(pl.program_id(0) == 0)
        def _():
            st_ref[...] = jnp.zeros_like(st_ref)

        draw = d_ref[...] * _sigmoid(x_ref[...] + b_ref[...])
        dp_ref[...] = draw.astype(BF16)
        st_ref[0:1, :] += jnp.sum(draw, axis=0, keepdims=True)

    return pl.pallas_call(
        body, name="dt_bwd", grid=(s // CONV_TILE,),
        in_specs=[pl.BlockSpec((CONV_TILE, 128), lambda i: (i, 0)), pl.BlockSpec((CONV_TILE, 128), lambda i: (i, DT_BLOCK)),
                  pl.BlockSpec((1, 128), lambda i: (0, 0)), ANY],
        out_specs=[pl.BlockSpec((CONV_TILE, 128), lambda i: (i, DT_BLOCK)), pl.BlockSpec((8, 128), lambda i: (0, 0))],
        out_shape=[jax.ShapeDtypeStruct(dproj.shape, BF16), jax.ShapeDtypeStruct((8, 128), F32)],
        input_output_aliases={3: 0}, compiler_params=_cparams("arbitrary"),
    )(ddt, proj, dt_bias, dproj)


def _ssd_common(x_ref, b_ref, c_ref, dt_ref, al_ref):
    row, col = _iota((128, 128), 0), _iota((128, 128), 1)
    tril = row >= col
    expand = jnp.where((_iota((128, 512), 1) >> 6) == _iota((128, 512), 0), 1.0, 0.0).astype(BF16)
    dt = dt_ref[...]
    a_neg = -jnp.exp(al_ref[...])
    a_col = _dot3_left(jnp.where(tril, 1.0, 0.0).astype(BF16), dt * a_neg)
    a_exp = _dot3_right(a_col, expand)
    dt_exp = _dot3_right(dt, expand)
    x = x_ref[...]
    return dict(tril=tril, col=col, expand=expand, dt=dt, a_neg=a_neg, a_col=a_col, a_row=a_col.T, a_exp=a_exp,
                dt_exp=dt_exp, a_last=a_exp[127:128, :], x=x, xd=x * dt_exp,
                b16=b_ref[...].astype(BF16), c16=c_ref[...].astype(BF16))


def _ssd_specs(nc, order):
    return [pl.BlockSpec((SSM_CHUNK, 512), lambda g, c: (order(c), g)),
            pl.BlockSpec((SSM_CHUNK, 128), lambda g, c: (order(c), SSM_D_INNER // 128 + g)),
            pl.BlockSpec((SSM_CHUNK, 128), lambda g, c: (order(c), SSM_D_INNER // 128 + SSM_GROUPS + g)),
            pl.BlockSpec((SSM_CHUNK, 512), lambda g, c: (order(c), g)),
            pl.BlockSpec((None, SSM_CHUNK, 128), lambda g, c: (g, order(c), 0)),
            pl.BlockSpec((None, 1, 128), lambda g, c: (g, 0, 0)),
            pl.BlockSpec((1, 512), lambda g, c: (0, g)), pl.BlockSpec((1, 512), lambda g, c: (0, g))]


def _ssd_fwd(act, proj, dtc, a_log, d_skip, norm_g):
    s = act.shape[0]
    nc = s // SSM_CHUNK

    def body(x_ref, b_ref, c_ref, z_ref, dt_ref, al_ref, dsk_ref, ng_ref, y_ref, yn_ref, hin_ref, h_sc):
        @pl.when(pl.program_id(1) == 0)
        def _():
            h_sc[...] = jnp.zeros_like(h_sc)

        q = _ssd_common(x_ref, b_ref, c_ref, dt_ref, al_ref)
        gmat = _dot_nt(q["c16"], q["b16"])
        h_in = h_sc[...]
        hin_ref[...] = h_in
        zmat = _dot(q["c16"], h_in.astype(BF16))
        xd16 = q["xd"].astype(BF16)
        low = q["col"] < 64
        pieces = []
        for p in range(4):
            xs = xd16[:, 128 * p:128 * p + 128]
            acc = jnp.zeros((128, 128), F32)
            for half in range(2):
                r = 2 * p + half
                decay = jnp.exp(jnp.where(q["tril"], q["a_col"][:, r:r + 1] - q["a_row"][r:r + 1, :], NEG_BIG))
                hm = low if half == 0 else jnp.logical_not(low)
                acc += _dot((gmat * decay).astype(BF16), jnp.where(hm, xs, jnp.zeros_like(xs)))
            pieces.append(acc)
        y = jnp.concatenate(pieces, axis=1) + zmat * jnp.exp(q["a_exp"]) + q["x"] * dsk_ref[...]
        y_ref[...] = y
        w16 = (q["xd"] * jnp.exp(q["a_last"] - q["a_exp"])).astype(BF16)
        h_sc[...] = h_in * jnp.exp(q["a_last"]) + _dot_tn(q["b16"], w16)
        z = z_ref[...]
        yg = y * (z * _sigmoid(z))
        rr = lax.rsqrt(jnp.mean(yg * yg, axis=-1, keepdims=True) + NORM_EPS)
        yn_ref[...] = (yg * rr * ng_ref[...]).astype(BF16)

    blk = pl.BlockSpec((SSM_CHUNK, 512), lambda g, c: (c, g))
    return pl.pallas_call(
        body, name="ssd_fwd", grid=(SSM_GROUPS, nc), in_specs=_ssd_specs(nc, lambda c: c),
        out_specs=[blk, blk, pl.BlockSpec((None, None, SSM_STATE, 512), lambda g, c: (g, c, 0, 0))],
        out_shape=[jax.ShapeDtypeStruct((s, SSM_D_INNER), F32), jax.ShapeDtypeStruct((s, SSM_D_INNER), BF16),
                   jax.ShapeDtypeStruct((SSM_GROUPS, nc, SSM_STATE, 512), F32)],
        scratch_shapes=[pltpu.VMEM((SSM_STATE, 512), F32)], compiler_params=_cparams("parallel", "arbitrary"),
    )(act, act, act, proj, dtc, a_log, d_skip, norm_g)


def _ssd_bwd(act, proj, dtc, a_log, d_skip, norm_g, y, h_in_all, dyn):
    s = act.shape[0]
    nc = s // SSM_CHUNK

    def body(x_ref, b_ref, c_ref, z_ref, dt_ref, al_ref, dsk_ref, ng_ref, y_ref, hin_ref, dyn_ref,
             dz_ref, dx_ref, db_ref, dc_ref, ddt_ref, st_ref, dal_ref, dh_sc):
        @pl.when(pl.program_id(1) == 0)
        def _():
            dh_sc[...] = jnp.zeros_like(dh_sc)
            st_ref[...] = jnp.zeros_like(st_ref)
            dal_ref[...] = jnp.zeros_like(dal_ref)

        q = _ssd_common(x_ref, b_ref, c_ref, dt_ref, al_ref)
        x, xd, b16, c16 = q["x"], q["xd"], q["b16"], q["c16"]
        z, yv, dyn_v = z_ref[...], y_ref[...], dyn_ref[...]
        sig = _sigmoid(z)
        sil = z * sig
        yg = yv * sil
        rr = lax.rsqrt(jnp.mean(yg * yg, axis=-1, keepdims=True) + NORM_EPS)
        st_ref[0:1, :] += jnp.sum(dyn_v * yg * rr, axis=0, keepdims=True)
        t1 = dyn_v * ng_ref[...]
        dyg = rr * (t1 - yg * (rr * rr) * jnp.mean(t1 * yg, axis=-1, keepdims=True))
        dy = dyg * sil
        dz_ref[...] = (dyg * yv * (sig * (1.0 + z * (1.0 - sig)))).astype(BF16)
        st_ref[1:2, :] += jnp.sum(dy * x, axis=0, keepdims=True)
        dx = dsk_ref[...] * dy
        h_in = hin_ref[...]
        h16 = h_in.astype(BF16)
        ea = jnp.exp(q["a_exp"])
        zmat = _dot(c16, h16)
        dz16 = (dy * ea).astype(BF16)
        da_ch = dy * zmat * ea
        dcm = _dot_nt(dz16, h16)
        dh_in = _dot_tn(c16, dz16)
        dh_out = dh_sc[...]
        dho16 = dh_out.astype(BF16)
        eal = jnp.exp(q["a_last"])
        dh_in += dh_out * eal
        dal_ch = jnp.sum(dh_out * h_in, axis=0, keepdims=True) * eal
        to_end = jnp.exp(q["a_last"] - q["a_exp"])
        wmat = xd * to_end
        dbm = _dot_nt(wmat.astype(BF16), dho16)
        dw = _dot(b16, dho16)
        dxd = dw * to_end
        g_end = dw * wmat
        da_ch -= g_end
        dal_ch += jnp.sum(g_end, axis=0, keepdims=True)
        gmat = _dot_nt(c16, b16)
        gmat_t = _dot_nt(b16, c16)
        triu = _iota((128, 128), 0) <= q["col"]
        xd16, dy16 = xd.astype(BF16), dy.astype(BF16)
        low = q["col"] < 64
        da_col = jnp.zeros((128, 128), F32)
        dg = jnp.zeros((128, 128), F32)
        dg_t = jnp.zeros((128, 128), F32)
        pieces = []
        for p in range(4):
            xs, dys = xd16[:, 128 * p:128 * p + 128], dy16[:, 128 * p:128 * p + 128]
            acc = jnp.zeros((128, 128), F32)
            for half in range(2):
                r = 2 * p + half
                hm = low if half == 0 else jnp.logical_not(low)
                xm = jnp.where(hm, xs, jnp.zeros_like(xs))
                dym = jnp.where(hm, dys, jnp.zeros_like(dys))
                diff = q["a_col"][:, r:r + 1] - q["a_row"][r:r + 1, :]
                decay = jnp.exp(jnp.where(q["tril"], diff, NEG_BIG))
                decay_t = jnp.exp(jnp.where(triu, -diff, NEG_BIG))
                mm, mm_t = gmat * decay, gmat_t * decay_t
                dmm, dmm_t = _dot_nt(dym, xm), _dot_nt(xm, dym)
                acc += _dot(mm_t.astype(BF16), dym)
                dg += dmm * decay
                dg_t += dmm_t * decay_t
                rs = jnp.sum(dmm * mm, axis=-1, keepdims=True) - jnp.sum(dmm_t * mm_t, axis=-1, keepdims=True)
                da_col += jnp.where(q["col"] == r, rs, 0.0)
            pieces.append(acc)
        dxd += jnp.concatenate(pieces, axis=1)
        dcm += _dot(dg.astype(BF16), b16)
        dbm += _dot(dg_t.astype(BF16), c16)
        fold = jnp.where((_iota((512, 128), 0) >> 6) == _iota((512, 128), 1), 1.0, 0.0).astype(BF16)
        da_ch += jnp.where(_iota((128, 1), 0) == 127, dal_ch, 0.0)
        da_col += _dot3_right(da_ch, fold)
        d_dta = _dot3_left(jnp.where(triu, 1.0, 0.0).astype(BF16), da_col)
        ddt_ref[...] = d_dta * q["a_neg"] + _dot3_right(dxd * x, fold)
        dal_ref[0:1, :] += jnp.sum(d_dta * q["dt"] * q["a_neg"], axis=0, keepdims=True)
        dx_ref[...] = dx + dxd * q["dt_exp"]
        db_ref[...] = dbm
        dc_ref[...] = dcm
        dh_sc[...] = dh_in

    rev = lambda c: nc - 1 - c
    blk = pl.BlockSpec((SSM_CHUNK, 512), lambda g, c: (rev(c), g))
    small = pl.BlockSpec((SSM_CHUNK, 128), lambda g, c: (rev(c), g))
    return pl.pallas_call(
        body, name="ssd_bwd", grid=(SSM_GROUPS, nc),
        in_specs=_ssd_specs(nc, rev) + [blk, pl.BlockSpec((None, None, SSM_STATE, 512), lambda g, c: (g, rev(c), 0, 0)), blk],
        out_specs=[blk, blk, small, small, pl.BlockSpec((None, SSM_CHUNK, 128), lambda g, c: (g, rev(c), 0)),
                   pl.BlockSpec((8, 512), lambda g, c: (0, g)), pl.BlockSpec((None, 8, 128), lambda g, c: (g, 0, 0))],
        out_shape=[jax.ShapeDtypeStruct((s, SSM_IN_PAD), BF16), jax.ShapeDtypeStruct((s, SSM_D_INNER), F32),
                   jax.ShapeDtypeStruct((s, SSM_GROUPS * SSM_STATE), F32), jax.ShapeDtypeStruct((s, SSM_GROUPS * SSM_STATE), F32),
                   jax.ShapeDtypeStruct((SSM_GROUPS, s, 128), F32), jax.ShapeDtypeStruct((8, SSM_D_INNER), F32),
                   jax.ShapeDtypeStruct((SSM_GROUPS, 8, 128), F32)],
        scratch_shapes=[pltpu.VMEM((SSM_STATE, 512), F32)], compiler_params=_cparams("parallel", "arbitrary"),
    )(act, act, act, proj, dtc, a_log, d_skip, norm_g, y, h_in_all, dyn)


MM_TM = 1024
MM_TK = 512
FFN_SHARD = FFN_HIDDEN // N_DEV
ATTN_DILATIONS = (1, 4, 16)


def _to_classes(t, d):
    s, c = t.shape
    return t if d == 1 else t.reshape(s // d, d, c).transpose(1, 0, 2).reshape(s, c)


def _from_classes(t, d):
    s, c = t.shape
    return t if d == 1 else t.reshape(d, s // d, c).transpose(1, 0, 2).reshape(s, c)


def _ij(tm, tn):
    return pl.BlockSpec((tm, tn), lambda i, j, k: (i, j))


def _mm_plain(name, a, b, mode, dims, out_dtype, tn=1024, tk=MM_TK, **kw):
    return _matmul(name, a, b, mode=mode, dims=dims, tiles=(MM_TM, tn, tk), outs=[((dims[0], dims[1]), out_dtype)],
                   epilogue=_epi_plain, **kw)[0]


def _mm_resgate(name, a, b, res, gate, k_dim):
    s, d = res.shape
    return _matmul(name, a, b, mode="nn", dims=(s, d, k_dim), tiles=(MM_TM, 1024, MM_TK),
                   outs=[((s, d), F32), ((s, d), BF16)], epilogue=_epi_resgate, extras=[res, gate],
                   extra_specs=[_ij(MM_TM, 1024), pl.BlockSpec((1, 1024), lambda i, j, k: (0, j))])


def _ffn_fwd(tag, x_in, gain, scale, shift, gate, w1, w2):
    s, d = x_in.shape
    h = _norm_mod_fwd(tag + "_norm2", x_in, gain, scale, shift)
    rr, pre = _matmul(tag + "_ffn1", h, w1, mode="nn", dims=(s, FFN_HIDDEN, d), tiles=(MM_TM, FFN_SHARD, MM_TK),
                      b_spec=pl.BlockSpec((None, MM_TK, FFN_SHARD), lambda i, j, k: (j, k, 0)),
                      outs=[((s, FFN_HIDDEN), BF16)] * 2, epilogue=_epi_relu2)
    x_out, f = _mm_resgate(tag + "_ffn2", rr, w2, x_in, gate, FFN_HIDDEN)
    return x_out, (h, rr, pre, f)


def _ffn_bwd(tag, dx_out, x_in, saved, gain, scale, gate, w1, w2):
    s, d = x_in.shape
    h, rr, pre, f = saved
    dy, st_gate = _gate_bwd(tag + "_gate2_bwd", dx_out, f, gate)
    da = _matmul(tag + "_ffn2_dx", dy, w2, mode="nt", dims=(s, FFN_HIDDEN, d), tiles=(MM_TM, 1024, MM_TK),
                 outs=[((s, FFN_HIDDEN), BF16)], epilogue=_epi_drelu2, extras=[pre], extra_specs=[_ij(MM_TM, 1024)])[0]
    dw2 = _mm_plain(tag + "_ffn2_dw", rr, dy, "tn", (FFN_HIDDEN, d, s), BF16)
    per_shard = FFN_SHARD // MM_TK
    dh = _mm_plain(tag + "_ffn1_dx", da, w1, "nt", (s, d, FFN_HIDDEN), F32,
                   b_spec=pl.BlockSpec((None, 1024, MM_TK), lambda i, j, k: (k // per_shard, j, k % per_shard)))
    dw1 = _matmul(tag + "_ffn1_dw", h, da, mode="tn", dims=(d, FFN_HIDDEN, s), tiles=(MM_TM, FFN_SHARD, MM_TK),
                  outs=[((N_DEV, d, FFN_SHARD), BF16)], epilogue=_epi_plain,
                  out_specs=[pl.BlockSpec((None, MM_TM, FFN_SHARD), lambda i, j, k: (j, i, 0))])[0]
    dx_in, st_norm = _norm_mod_bwd(tag + "_norm2_bwd", x_in, gain, scale, dh, dx_out)
    return dx_in, dw1, dw2, st_gate[0], st_norm


def _local_step(x, target, mod, wts, sm):
    s, d = x.shape
    mv = [[mod[i, k].reshape(1, d) for k in range(6)] for i in range(2)]
    nm = [sm["norm_mix"][i].reshape(1, d) for i in range(2)]
    nf = [sm["norm_ffn"][i].reshape(1, d) for i in range(2)]
    pool_w, pool_scale = sm["pool_w"].reshape(4, 128, 128), sm["pool_scale"].reshape(1, POOL_WIDTH)

    sh1, sc1, g1, sh2, sc2, g2 = mv[0]
    h1 = _norm_mod_fwd("l0_norm1", x, nm[0], sc1, sh1)
    slab = pl.BlockSpec((None, MM_TM, 512), lambda i, j, k: (j, i, 0))
    proj0 = _matmul("l0_in", h1, wts["even_w_in"], mode="nn", dims=(s, EVEN_IN_WIDTH, d), tiles=(MM_TM, 512, MM_TK),
                    outs=[((EVEN_IN_WIDTH // 512, s, 512), BF16)], out_specs=[slab], epilogue=_epi_plain)[0]
    qkv_p = jnp.stack([_to_classes(proj0[kind * 3 + g], ATTN_DILATIONS[g]) for kind in range(3) for g in range(3)])
    o_p, lse_p = _attn_fwd(qkv_p)
    o = jnp.stack([_from_classes(o_p[g], ATTN_DILATIONS[g]) for g in range(3)])
    lse = jnp.stack([_from_classes(lse_p[g], ATTN_DILATIONS[g]) for g in range(3)])
    cat = _merge_pool_fwd(o, lse, proj0, pool_w, pool_scale)
    x1, y0 = _mm_resgate("l0_out", cat, wts["even_w_out"], x, g1, EVEN_OUT_WIDTH)
    x2, ffn0 = _ffn_fwd("l0", x1, nf[0], sc2, sh2, g2, wts["ffn_w1"][0], wts["ffn_w2"][0])

    th1, tc1, tg1, th2, tc2, tg2 = mv[1]
    h3 = _norm_mod_fwd("l1_norm1", x2, nm[1], tc1, th1)
    proj1 = _mm_plain("l1_in", h3, wts["ssm_w_in"], "nn", (s, SSM_IN_PAD, d), F32, tn=1152)
    conv_w = sm["ssm_conv_w"].reshape(SSM_CONV, SSM_CONV_DIM)
    pre, act = _conv_fwd(proj1, conv_w, sm["ssm_conv_b"].reshape(1, SSM_CONV_DIM))
    dt_bias = jnp.pad(sm["ssm_dt_bias"].reshape(1, SSM_HEADS), ((0, 0), (0, 128 - SSM_HEADS)))
    dt_full = _dt_fwd(proj1, dt_bias)
    dtc = jnp.pad(dt_full[:, :SSM_HEADS].reshape(s, SSM_GROUPS, 8).transpose(1, 0, 2), ((0, 0), (0, 0), (0, 120)))
    a_log = jnp.pad(sm["ssm_a_log"].reshape(SSM_GROUPS, 1, 8), ((0, 0), (0, 0), (0, 120)))
    d_skip = jnp.repeat(sm["ssm_d"].reshape(SSM_HEADS), SSM_D_INNER // SSM_HEADS).reshape(1, SSM_D_INNER)
    norm_g = sm["ssm_norm"].reshape(1, SSM_D_INNER)
    y, yn, h_in_all = _ssd_fwd(act, proj1, dtc, a_log, d_skip, norm_g)
    x3, y1 = _mm_resgate("l1_out", yn, wts["ssm_w_out"], x2, tg1, SSM_D_INNER)
    x4, ffn1 = _ffn_fwd("l1", x3, nf[1], tc2, th2, tg2, wts["ffn_w1"][1], wts["ffn_w2"][1])

    dx4, st_loss = _loss_head("loss_head", x4, sm["final_norm"].reshape(1, d), target)
    loss = jnp.sum(st_loss[1])

    dx3, dw1_1, dw2_1, dg2_1, st_n2_1 = _ffn_bwd("l1", dx4, x3, ffn1, nf[1], tc2, tg2, wts["ffn_w1"][1], wts["ffn_w2"][1])
    dy1, st_g1_1 = _gate_bwd("l1_gate1_bwd", dx3, y1, tg1)
    dyn = _mm_plain("l1_out_dx", dy1, wts["ssm_w_out"], "nt", (s, SSM_D_INNER, d), F32)
    dw_sout = _mm_plain("l1_out_dw", yn, dy1, "tn", (SSM_D_INNER, d, s), BF16)
    dproj1, dxs, dbm, dcm, ddt, st_ssd, d_alog = _ssd_bwd(act, proj1, dtc, a_log, d_skip, norm_g, y, h_in_all, dyn)
    dproj1, st_cx = _conv_bwd("conv_bwd_x", dxs, pre, proj1, conv_w, dproj1, block0=0)
    dproj1, st_cb = _conv_bwd("conv_bwd_b", dbm, pre, proj1, conv_w, dproj1, block0=SSM_D_INNER // 512)
    dproj1, st_cc = _conv_bwd("conv_bwd_c", dcm, pre, proj1, conv_w, dproj1, block0=SSM_D_INNER // 512 + 2)
    ddt_rows = jnp.pad(ddt[:, :, :8].transpose(1, 0, 2).reshape(s, SSM_HEADS), ((0, 0), (0, 128 - SSM_HEADS)))
    dproj1, st_dt = _dt_bwd(ddt_rows, proj1, dt_bias, dproj1)
    dh3 = _mm_plain("l1_in_dx", dproj1, wts["ssm_w_in"], "nt", (s, d, SSM_IN_PAD), F32, tk=1152)
    dw_sin = _mm_plain("l1_in_dw", h3, dproj1, "tn", (d, SSM_IN_PAD, s), BF16, tn=1152)
    dx2, st_n1_1 = _norm_mod_bwd("l1_norm1_bwd", x2, nm[1], tc1, dh3, dx3)

    dx1, dw1_0, dw2_0, dg2_0, st_n2_0 = _ffn_bwd("l0", dx2, x1, ffn0, nf[0], sc2, g2, wts["ffn_w1"][0], wts["ffn_w2"][0])
    dy0, st_g1_0 = _gate_bwd("l0_gate1_bwd", dx1, y0, g1)
    dcat = _mm_plain("l0_out_dx", dy0, wts["even_w_out"], "nt", (s, EVEN_OUT_WIDTH, d), F32)
    dw_eout = _mm_plain("l0_out_dw", cat, dy0, "tn", (EVEN_OUT_WIDTH, d, s), BF16)
    lane = jnp.arange(512) // 64
    head_sum = (lane[:, None] == lane[None, :]).astype(BF16)
    do, cc, du, d_pool_w, st_pool = _merge_pool_bwd(dcat, o, lse, proj0, pool_w, pool_scale, head_sum)
    do_p = jnp.stack([_to_classes(do[g], ATTN_DILATIONS[g]) for g in range(3)])
    cc_p = jnp.stack([_to_classes(cc[g], ATTN_DILATIONS[g]) for g in range(3)])
    dq, dk, dv = _attn_bwd(qkv_p, do_p, lse_p, cc_p)
    dproj0 = jnp.stack([_from_classes(t[g], ATTN_DILATIONS[g]) for t in (dq, dk, dv) for g in range(3)] + [du])
    dh1 = _mm_plain("l0_in_dx", dproj0, wts["even_w_in"], "nt", (s, d, EVEN_IN_WIDTH), F32,
                    a_spec=pl.BlockSpec((None, MM_TM, 512), lambda i, j, k: (k, i, 0)))
    dw_ein = _mm_plain("l0_in_dw", h1, dproj0, "tn", (d, EVEN_IN_WIDTH, s), BF16, tn=512,
                       b_spec=pl.BlockSpec((None, MM_TK, 512), lambda i, j, k: (j, k, 0)))
    grad_x, st_n1_0 = _norm_mod_bwd("l0_norm1_bwd", x, nm[0], sc1, dh1, dx1)

    dmod = jnp.stack([
        jnp.stack([st_n1_0[0], st_n1_0[1], st_g1_0[0], st_n2_0[0], st_n2_0[1], dg2_0]),
        jnp.stack([st_n1_1[0], st_n1_1[1], st_g1_1[0], st_n2_1[0], st_n2_1[1], dg2_1])])
    st_conv = jnp.concatenate([st_cx, st_cb, st_cc], axis=1)
    big = dict(even_w_in=dw_ein, even_w_out=dw_eout, ffn_w1=(dw1_0, dw1_1), ffn_w2=(dw2_0, dw2_1),
               ssm_w_in=dw_sin, ssm_w_out=dw_sout)
    small = dict(
        norm_mix=jnp.stack([st_n1_0[2], st_n1_1[2]]), norm_ffn=jnp.stack([st_n2_0[2], st_n2_1[2]]),
        pool_w=d_pool_w, pool_scale=st_pool[0], ssm_conv_w=st_conv[:SSM_CONV], ssm_conv_b=st_conv[SSM_CONV],
        ssm_dt_bias=st_dt[0, :SSM_HEADS], ssm_a_log=d_alog[:, 0, :8].reshape(SSM_HEADS),
        ssm_d=jnp.sum(st_ssd[1].reshape(SSM_HEADS, SSM_D_INNER // SSM_HEADS), axis=-1), ssm_norm=st_ssd[0],
        final_norm=st_loss[0])
    return loss, grad_x, big, dmod, small


WEIGHT_ORDER = ("ada_w", "ada_b", "norm_mix", "norm_ffn", "ffn_w1", "ffn_w2", "even_w_in", "pool_w", "pool_scale",
                "even_w_out", "ssm_w_in", "ssm_conv_w", "ssm_conv_b", "ssm_dt_bias", "ssm_a_log", "ssm_d", "ssm_norm",
                "ssm_w_out", "final_norm")
BIG_LAYERS = ((("even_w_in", 0), ("even_w_out", 0), ("ffn_w1", 0), ("ffn_w2", 0)),
              (("ssm_w_in", 0), ("ssm_w_out", 0), ("ffn_w1", 1), ("ffn_w2", 1)))
STACKED = ("ffn_w1", "ffn_w2")
ASSEMBLED = {"even_w_in": EVEN_IN_WIDTH, "even_w_out": D_MODEL, "ssm_w_in": SSM_IN_PAD}
SMALL_REPLICATED = ("norm_mix", "norm_ffn", "pool_w", "pool_scale", "ssm_dt_bias", "ssm_a_log", "ssm_d", "final_norm")
SMALL_SHARDED = ("ssm_conv_w", "ssm_conv_b", "ssm_norm")


def _pack(flat_parts, width, lead=()):
    flat = jnp.concatenate(flat_parts, axis=-1)
    n = flat.shape[-1]
    rows = -(-n // (8 * width)) * 8
    flat = jnp.pad(flat, [(0, 0)] * len(lead) + [(0, rows * width - n)])
    return flat.reshape(*lead, rows, width)


def _unpack(packed, shapes, lead=()):
    flat = packed.reshape(*lead, -1)
    out, off = [], 0
    for shp in shapes:
        n = math.prod(shp)
        out.append(flat[..., off:off + n].reshape(*lead, *shp))
        off += n
    return out


def kernel(x, c, ada_w, ada_b, norm_mix, norm_ffn, ffn_w1, ffn_w2, even_w_in, pool_w, pool_scale, even_w_out, ssm_w_in, ssm_conv_w, ssm_conv_b, ssm_dt_bias, ssm_a_log, ssm_d, ssm_norm, ssm_w_out, final_norm, loss_target, m_ada_w, m_ada_b, m_norm_mix, m_norm_ffn, m_ffn_w1, m_ffn_w2, m_even_w_in, m_pool_w, m_pool_scale, m_even_w_out, m_ssm_w_in, m_ssm_conv_w, m_ssm_conv_b, m_ssm_dt_bias, m_ssm_a_log, m_ssm_d, m_ssm_norm, m_ssm_w_out, m_final_norm, v_ada_w, v_ada_b, v_norm_mix, v_norm_ffn, v_ffn_w1, v_ffn_w2, v_even_w_in, v_pool_w, v_pool_scale, v_even_w_out, v_ssm_w_in, v_ssm_conv_w, v_ssm_conv_b, v_ssm_dt_bias, v_ssm_a_log, v_ssm_d, v_ssm_norm, v_ssm_w_out, v_final_norm):
    w = dict(ada_w=ada_w, ada_b=ada_b, norm_mix=norm_mix, norm_ffn=norm_ffn, ffn_w1=ffn_w1, ffn_w2=ffn_w2,
             even_w_in=even_w_in, pool_w=pool_w, pool_scale=pool_scale, even_w_out=even_w_out, ssm_w_in=ssm_w_in,
             ssm_conv_w=ssm_conv_w, ssm_conv_b=ssm_conv_b, ssm_dt_bias=ssm_dt_bias, ssm_a_log=ssm_a_log, ssm_d=ssm_d,
             ssm_norm=ssm_norm, ssm_w_out=ssm_w_out, final_norm=final_norm)
    m = dict(ada_w=m_ada_w, ada_b=m_ada_b, norm_mix=m_norm_mix, norm_ffn=m_norm_ffn, ffn_w1=m_ffn_w1, ffn_w2=m_ffn_w2,
             even_w_in=m_even_w_in, pool_w=m_pool_w, pool_scale=m_pool_scale, even_w_out=m_even_w_out,
             ssm_w_in=m_ssm_w_in, ssm_conv_w=m_ssm_conv_w, ssm_conv_b=m_ssm_conv_b, ssm_dt_bias=m_ssm_dt_bias,
             ssm_a_log=m_ssm_a_log, ssm_d=m_ssm_d, ssm_norm=m_ssm_norm, ssm_w_out=m_ssm_w_out, final_norm=m_final_norm)
    v = dict(ada_w=v_ada_w, ada_b=v_ada_b, norm_mix=v_norm_mix, norm_ffn=v_norm_ffn, ffn_w1=v_ffn_w1, ffn_w2=v_ffn_w2,
             even_w_in=v_even_w_in, pool_w=v_pool_w, pool_scale=v_pool_scale, even_w_out=v_even_w_out,
             ssm_w_in=v_ssm_w_in, ssm_conv_w=v_ssm_conv_w, ssm_conv_b=v_ssm_conv_b, ssm_dt_bias=v_ssm_dt_bias,
             ssm_a_log=v_ssm_a_log, ssm_d=v_ssm_d, ssm_norm=v_ssm_norm, ssm_w_out=v_ssm_w_out, final_norm=v_final_norm)
    d = D_MODEL
    me = _my_index()

    sharded_shapes = [(SSM_CONV, SSM_CONV_DIM // N_DEV), (SSM_CONV_DIM // N_DEV,), (SSM_D_INNER // N_DEV,)]
    small_in = _pack([c.reshape(-1)] + [w[k].reshape(-1) for k in SMALL_SHARDED], 128)
    got = _all_gather("gather_small_in", small_in)
    c_all, conv_w_sh, conv_b_sh, norm_sh = _unpack(got, [(d,)] + sharded_shapes, lead=(N_DEV,))
    sm = {k: w[k] for k in SMALL_REPLICATED}
    sm["ssm_conv_w"] = conv_w_sh.transpose(1, 0, 2).reshape(SSM_CONV, SSM_CONV_DIM)
    sm["ssm_conv_b"] = conv_b_sh.reshape(SSM_CONV_DIM)
    sm["ssm_norm"] = norm_sh.reshape(SSM_D_INNER)

    ada_cols = 6 * d // N_DEV
    c_pad = jnp.pad(c_all, ((0, 16 - N_DEV), (0, 0)))
    mod_cols = _matmul(
        "ada_fwd", c_pad, ada_w, mode="nn", dims=(16, 2 * ada_cols, d), tiles=(16, ada_cols // 2, d), a_fn=_silu,
        b_spec=pl.BlockSpec((None, d, ada_cols // 2), lambda i, j, k: (j // 2, k, j % 2)),
        outs=[((16, 2 * ada_cols), F32)], epilogue=_epi_plain)[0]
    mod_got = _all_to_all("ada_exchange", mod_cols[:N_DEV].reshape(N_DEV, 2, ada_cols))
    mod = (mod_got.transpose(1, 0, 2).reshape(2, 6 * d) + ada_b).reshape(2, 6, d)

    wts = {}
    for li, layer in enumerate(BIG_LAYERS):
        got = _exchange(f"gather_weights_l{li}", [w[name][idx].astype(BF16) for name, idx in layer], scatter=False)
        for (name, idx), part in zip(layer, got):
            if name in ASSEMBLED:
                full = _assemble_cols("assemble_" + name, part, ASSEMBLED[name])
            elif name == "ffn_w1":
                full = part
            else:
                full = part.reshape(-1, d)
            if name in STACKED:
                wts.setdefault(name, [None, None])[idx] = full
            else:
                wts[name] = full

    loss, grad_x, big, dmod, small = _local_step(x[0], loss_target[0], mod, wts, sm)
    loss = lax.psum(loss, ("x", "y", "c"))

    grads, delta, new_m, new_v = {}, {}, {}, {}

    def update(name, parts, shape=None):
        rows, cols = parts.shape[1:]
        res = _adamw("adamw_" + name, parts, w[name].reshape(rows, cols), m[name].reshape(rows, cols), v[name].reshape(rows, cols))
        return [r.reshape(w[name].shape if shape is None else shape) for r in res]

    dmod_all = _all_gather("gather_dmod", dmod.reshape(2, 6 * d))
    my_cols = lax.dynamic_slice_in_dim(dmod_all, me * ada_cols, ada_cols, axis=2).reshape(N_DEV, 2 * ada_cols)
    g_ada_w = _matmul(
        "ada_dw", c_pad, jnp.pad(my_cols, ((0, 16 - N_DEV), (0, 0))), mode="tn", dims=(d, 2 * ada_cols, 16),
        tiles=(1024, ada_cols // 2, 16), a_fn=_silu, outs=[((2, d, ada_cols), F32)],
        out_specs=[pl.BlockSpec((None, 1024, ada_cols // 2), lambda i, j, k: (j // 2, i, j % 2))], epilogue=_epi_plain)[0]
    grads["ada_w"], delta["ada_w"], new_m["ada_w"], new_v["ada_w"] = update("ada_w", g_ada_w.reshape(1, 2 * d, ada_cols))
    grads["ada_b"], delta["ada_b"], new_m["ada_b"], new_v["ada_b"] = update("ada_b", dmod_all)

    stacked = {name: [None, None] for name in STACKED}
    for li, layer in reversed(list(enumerate(BIG_LAYERS))):
        pieces = []
        for name, idx in layer:
            full = big[name][idx] if name in stacked else big[name]
            shard_shape = w[name].shape[1:]
            if name in ASSEMBLED:
                pieces.append(_split_cols("split_" + name, full, shard_shape[1]))
            else:
                pieces.append(full.reshape(N_DEV, *shard_shape))
        got = _exchange(f"exchange_grads_l{li}", pieces, scatter=True)
        for (name, idx), part in zip(layer, got):
            res = _adamw(f"adamw_{name}_{idx}", part, w[name][idx], m[name][idx], v[name][idx])
            if name in stacked:
                stacked[name][idx] = res
            else:
                grads[name], delta[name], new_m[name], new_v[name] = [r[None] for r in res]
    for name, per_layer in stacked.items():
        grads[name], delta[name], new_m[name], new_v[name] = [jnp.stack([per_layer[0][q], per_layer[1][q]]) for q in range(4)]

    rep_shapes = [w[k].shape for k in SMALL_REPLICATED]
    pack_rep = lambda tree: _pack([tree[k].reshape(-1) for k in SMALL_REPLICATED], 128)
    rep_got = _all_gather("gather_small_grads", pack_rep(small))
    rep_res = _adamw("adamw_small_replicated", rep_got, pack_rep(w), pack_rep(m), pack_rep(v))
    for dst, packed in zip((grads, delta, new_m, new_v), rep_res):
        for k, val in zip(SMALL_REPLICATED, _unpack(packed, rep_shapes)):
            dst[k] = val

    sh_pieces = [small["ssm_conv_w"].reshape(SSM_CONV, N_DEV, -1).transpose(1, 0, 2).reshape(N_DEV, -1),
                 small["ssm_conv_b"].reshape(N_DEV, -1), small["ssm_norm"].reshape(N_DEV, -1)]
    sh_got = _all_to_all("exchange_small_grads", _pack(sh_pieces, 128, lead=(N_DEV,)))
    pack_sh = lambda tree: _pack([tree[k].reshape(-1) for k in SMALL_SHARDED], 128)
    sh_res = _adamw("adamw_small_sharded", sh_got, pack_sh(w), pack_sh(m), pack_sh(v))
    for dst, packed in zip((grads, delta, new_m, new_v), sh_res):
        for k, val in zip(SMALL_SHARDED, _unpack(packed, [w[k].shape for k in SMALL_SHARDED])):
            dst[k] = val

    out = [loss, grad_x[None]]
    for tree in (grads, delta, new_m, new_v):
        out.extend(tree[k] for k in WEIGHT_ORDER)
    return tuple(out)
```

```python
import functools
import math

import jax
import jax.numpy as jnp
from jax import lax
from jax.experimental import pallas as pl
from jax.experimental.pallas import tpu as pltpu

F32 = jnp.float32
BF16 = jnp.bfloat16

N_DEV = 8
D_MODEL = 2048
NORM_EPS = 1e-6
ATTN_BLOCK = 128
ATTN_GROUPS = 3
ATTN_GROUP_WIDTH = 512
ATTN_QKV_WIDTH = 3 * ATTN_GROUPS * ATTN_GROUP_WIDTH
POOL_GROUPS = 4
POOL_GROUP_WIDTH = 128
POOL_WIDTH = 512
POOL_HALO = 16
EVEN_IN_WIDTH = ATTN_QKV_WIDTH + POOL_WIDTH
EVEN_OUT_WIDTH = 1024
SSM_D_INNER = 4096
SSM_HEADS = 64
SSM_GROUPS = 8
SSM_GROUP_WIDTH = 512
SSM_STATE = 128
SSM_CHUNK = 128
SSM_CONV = 4
SSM_CONV_DIM = 6144
SSM_IN_WIDTH = 10304
SSM_IN_PAD = 10368
FFN_HIDDEN = 8192

ADAM_LR = 0.001
ADAM_B1 = 0.9
ADAM_B2 = 0.999
ADAM_EPS = 1e-08
ADAM_WD = 0.01
ADAM_STEP = 10

VMEM_LIMIT_BYTES = 56 * 1024 * 1024
NEG_BIG = -1e30

MESH_ID = pl.DeviceIdType.MESH
ANY = pl.BlockSpec(memory_space=pl.ANY)


def _cparams(*sem):
    return pltpu.CompilerParams(dimension_semantics=tuple(sem) if sem else None, vmem_limit_bytes=VMEM_LIMIT_BYTES)


def _dot(a, b):
    return lax.dot_general(a, b, (((1,), (0,)), ((), ())), preferred_element_type=F32)


def _dot_nt(a, b):
    return lax.dot_general(a, b, (((1,), (1,)), ((), ())), preferred_element_type=F32)


def _dot_tn(a, b):
    return lax.dot_general(a, b, (((0,), (0,)), ((), ())), preferred_element_type=F32)


def _split3(v):
    hi = v.astype(BF16)
    r1 = v - hi.astype(F32)
    mid = r1.astype(BF16)
    lo = (r1 - mid.astype(F32)).astype(BF16)
    return hi, mid, lo


def _dot3_left(const_bf16, v):
    hi, mid, lo = _split3(v)
    return _dot(const_bf16, hi) + _dot(const_bf16, mid) + _dot(const_bf16, lo)


def _dot3_right(v, const_bf16):
    hi, mid, lo = _split3(v)
    return _dot(hi, const_bf16) + _dot(mid, const_bf16) + _dot(lo, const_bf16)


def _iota(shape, dim):
    return lax.broadcasted_iota(jnp.int32, shape, dim)


def _sigmoid(x):
    return 1.0 / (1.0 + jnp.exp(-x))


def _peer(k):
    x, y, c = lax.axis_index("x"), lax.axis_index("y"), lax.axis_index("c")
    px = 1 - x if k & 4 else x
    py = 1 - y if k & 2 else y
    pc = 1 - c if k & 1 else c
    return (px, py, pc), 4 * px + 2 * py + pc


def _my_index():
    return 4 * lax.axis_index("x") + 2 * lax.axis_index("y") + lax.axis_index("c")


def _exchange(name, arrays, *, scatter):
    n = len(arrays)

    def body(*refs):
        ex = _Exchange(refs[:n], refs[n:2 * n], *refs[2 * n:], scatter)
        ex.start()
        ex.wait()

    return pl.pallas_call(
        body, name=name, out_shape=_exchange_out_shapes(arrays, scatter), in_specs=[ANY] * n, out_specs=[ANY] * n,
        scratch_shapes=_exchange_sems(n),
    )(*arrays)


def _exchange_out_shapes(arrays, scatter):
    return [jax.ShapeDtypeStruct((N_DEV,) + (tuple(a.shape[1:]) if scatter else tuple(a.shape)), a.dtype) for a in arrays]


def _exchange_sems(n):
    return [pltpu.SemaphoreType.DMA((n * (N_DEV - 1),)), pltpu.SemaphoreType.DMA((n * (N_DEV - 1),)),
            pltpu.SemaphoreType.DMA((n,))]


class _Exchange:
    def __init__(self, x_refs, out_refs, send_sems, recv_sems, local_sems, scatter):
        self.x_refs, self.out_refs, self.scatter = x_refs, out_refs, scatter
        self.send_sems, self.recv_sems, self.local_sems = send_sems, recv_sems, local_sems

    def _src(self, a, idx):
        return self.x_refs[a].at[idx] if self.scatter else self.x_refs[a]

    def _local(self, a):
        me = _my_index()
        return pltpu.make_async_copy(self._src(a, me), self.out_refs[a].at[me], self.local_sems.at[a])

    def _remote(self, a, k, landing):
        peer, peer_idx = _peer(k)
        sem = a * (N_DEV - 1) + k - 1
        slot = peer_idx if landing else _my_index()
        return pltpu.make_async_remote_copy(
            src_ref=self._src(a, peer_idx), dst_ref=self.out_refs[a].at[slot], send_sem=self.send_sems.at[sem],
            recv_sem=self.recv_sems.at[sem], device_id=peer, device_id_type=MESH_ID)

    def start(self):
        n = len(self.x_refs)
        for a in range(n):
            self._local(a).start()
        for k in range(1, N_DEV):
            for a in range(n):
                self._remote(a, k, False).start()

    def wait(self):
        n = len(self.x_refs)
        for k in range(1, N_DEV):
            for a in range(n):
                self._remote(a, k, True).wait_recv()
        for k in range(1, N_DEV):
            for a in range(n):
                self._remote(a, k, False).wait_send()
        for a in range(n):
            self._local(a).wait()


def _all_gather(name, x):
    return _exchange(name, [x], scatter=False)[0]


def _all_to_all(name, x):
    return _exchange(name, [x], scatter=True)[0]


def _assemble_cols(name, shards, width):
    _, k_dim, ns = shards.shape
    tr = 256

    def body(s_ref, o_ref):
        for dev in range(N_DEV):
            o_ref[:, ns * dev:ns * (dev + 1)] = s_ref[dev]
        if width > N_DEV * ns:
            o_ref[:, N_DEV * ns:] = jnp.zeros((tr, width - N_DEV * ns), o_ref.dtype)

    return pl.pallas_call(
        body, name=name, grid=(k_dim // tr,), in_specs=[pl.BlockSpec((N_DEV, tr, ns), lambda i: (0, i, 0))],
        out_specs=pl.BlockSpec((tr, width), lambda i: (i, 0)), out_shape=jax.ShapeDtypeStruct((k_dim, width), shards.dtype),
        compiler_params=_cparams("parallel"),
    )(shards)


def _split_cols(name, full, ns):
    k_dim, width = full.shape
    tr = 256

    def body(f_ref, o_ref):
        for dev in range(N_DEV):
            o_ref[dev] = f_ref[:, ns * dev:ns * (dev + 1)]

    return pl.pallas_call(
        body, name=name, grid=(k_dim // tr,), in_specs=[pl.BlockSpec((tr, width), lambda i: (i, 0))],
        out_specs=pl.BlockSpec((N_DEV, tr, ns), lambda i: (0, i, 0)),
        out_shape=jax.ShapeDtypeStruct((N_DEV, k_dim, ns), full.dtype), compiler_params=_cparams("parallel"),
    )(full)


_DIMS = {"nn": (((1,), (0,)), ((), ())), "nt": (((1,), (1,)), ((), ())), "tn": (((0,), (0,)), ((), ()))}


def _matmul(name, a, b, *, mode, dims, tiles, outs, epilogue, a_spec=None, b_spec=None, out_specs=None,
            extras=(), extra_specs=(), a_fn=None, ride=None):
    m_dim, n_dim, k_dim = dims
    tm, tn, tk = tiles
    assert m_dim % tm == 0 and n_dim % tn == 0 and k_dim % tk == 0, (name, dims, tiles)
    grid = (m_dim // tm, n_dim // tn, k_dim // tk)
    nk = grid[2]
    if a_spec is None:
        a_spec = pl.BlockSpec((tk, tm), lambda i, j, k: (k, i)) if mode == "tn" else pl.BlockSpec((tm, tk), lambda i, j, k: (i, k))
    if b_spec is None:
        b_spec = pl.BlockSpec((tn, tk), lambda i, j, k: (j, k)) if mode == "nt" else pl.BlockSpec((tk, tn), lambda i, j, k: (k, j))
    if out_specs is None:
        out_specs = [pl.BlockSpec((tm, tn), lambda i, j, k: (i, j)) for _ in outs]
    n_ex, n_out = len(extras), len(outs)
    ride_arrays, scatter = ride if ride is not None else ((), False)
    n_ride = len(ride_arrays)
    dn = _DIMS[mode]

    def body(a_ref, b_ref, *rest):
        ex_refs, rest = rest[:n_ex], rest[n_ex:]
        ride_in, rest = rest[:n_ride], rest[n_ride:]
        out_refs, rest = rest[:n_out], rest[n_out:]
        ride_out, rest = rest[:n_ride], rest[n_ride:]
        i, j, k = pl.program_id(0), pl.program_id(1), pl.program_id(2)
        if n_ride:
            exchange = _Exchange(ride_in, ride_out, *rest[-3:], scatter)

            @pl.when((i == 0) & (j == 0) & (k == 0))
            def _():
                exchange.start()

        at = a_ref[...]
        if a_fn is not None:
            at = a_fn(at)
        part = lax.dot_general(at.astype(BF16), b_ref[...].astype(BF16), dn, preferred_element_type=F32)

        def finish(total):
            res = epilogue(total, *[e[...] for e in ex_refs])
            for r, o in zip(res, out_refs):
                o[...] = r.astype(o.dtype)

        if nk == 1:
            finish(part)
        else:
            acc = rest[0]

            @pl.when(k == 0)
            def _():
                acc[...] = part

            @pl.when(k > 0)
            def _():
                acc[...] += part

            @pl.when(k == nk - 1)
            def _():
                finish(acc[...])

        if n_ride:
            @pl.when((i == grid[0] - 1) & (j == grid[1] - 1) & (k == nk - 1))
            def _():
                exchange.wait()

    scratch = ([pltpu.VMEM((tm, tn), F32)] if nk > 1 else []) + (_exchange_sems(n_ride) if n_ride else [])
    sem = ("arbitrary",) * 3 if n_ride else ("parallel", "parallel", "arbitrary")
    return pl.pallas_call(
        body, name=name, grid=grid,
        in_specs=[a_spec, b_spec, *extra_specs] + [ANY] * n_ride, out_specs=list(out_specs) + [ANY] * n_ride,
        out_shape=[jax.ShapeDtypeStruct(s, d) for s, d in outs] + (_exchange_out_shapes(ride_arrays, scatter) if n_ride else []),
        scratch_shapes=scratch, compiler_params=_cparams(*sem),
    )(a, b, *extras, *ride_arrays)


def _epi_plain(acc):
    return (acc,)


def _epi_relu2(acc):
    return jnp.square(jnp.maximum(acc, 0.0)), acc


def _epi_resgate(acc, res, gate):
    return res + gate * acc, acc


def _epi_drelu2(acc, pre):
    return (acc * (2.0 * jnp.maximum(pre.astype(F32), 0.0)),)


def _silu(v):
    return v * _sigmoid(v)


ROW_TILE = 256


def _row_spec(width, tr=ROW_TILE):
    return pl.BlockSpec((tr, width), lambda i: (i, 0))


def _vec_spec(width):
    return pl.BlockSpec((1, width), lambda i: (0, 0))


def _stat_spec(width):
    return pl.BlockSpec((8, width), lambda i: (0, 0))


def _norm_mod_fwd(name, x, gain, scale, shift):
    s, d = x.shape

    def body(x_ref, g_ref, sc_ref, sh_ref, h_ref):
        xv = x_ref[...]
        r = lax.rsqrt(jnp.mean(xv * xv, axis=-1, keepdims=True) + NORM_EPS)
        h_ref[...] = ((xv * r * g_ref[...]) * (1.0 + sc_ref[...]) + sh_ref[...]).astype(BF16)

    return pl.pallas_call(
        body, name=name, grid=(s // ROW_TILE,),
        in_specs=[_row_spec(d), _vec_spec(d), _vec_spec(d), _vec_spec(d)], out_specs=_row_spec(d),
        out_shape=jax.ShapeDtypeStruct((s, d), BF16), compiler_params=_cparams("parallel"),
    )(x, gain, scale, shift)


def _norm_mod_bwd(name, x, gain, scale, dh, dres):
    s, d = x.shape

    def body(x_ref, g_ref, sc_ref, dh_ref, dres_ref, dx_ref, st_ref):
        @pl.when(pl.program_id(0) == 0)
        def _():
            st_ref[...] = jnp.zeros_like(st_ref)

        xv = x_ref[...]
        dhv = dh_ref[...].astype(F32)
        r = lax.rsqrt(jnp.mean(xv * xv, axis=-1, keepdims=True) + NORM_EPS)
        xh = xv * r
        n = xh * g_ref[...]
        dn = dhv * (1.0 + sc_ref[...])
        dxh = dn * g_ref[...]
        dx_ref[...] = dres_ref[...] + r * (dxh - xh * jnp.mean(dxh * xh, axis=-1, keepdims=True))
        st_ref[0:1, :] += jnp.sum(dhv, axis=0, keepdims=True)
        st_ref[1:2, :] += jnp.sum(dhv * n, axis=0, keepdims=True)
        st_ref[2:3, :] += jnp.sum(dn * xh, axis=0, keepdims=True)

    return pl.pallas_call(
        body, name=name, grid=(s // ROW_TILE,),
        in_specs=[_row_spec(d), _vec_spec(d), _vec_spec(d), _row_spec(d), _row_spec(d)],
        out_specs=[_row_spec(d), _stat_spec(d)],
        out_shape=[jax.ShapeDtypeStruct((s, d), F32), jax.ShapeDtypeStruct((8, d), F32)],
        compiler_params=_cparams("arbitrary"),
    )(x, gain, scale, dh, dres)


def _gate_bwd(name, dx, y, gate):
    s, d = dx.shape

    def body(dx_ref, y_ref, g_ref, dy_ref, st_ref):
        @pl.when(pl.program_id(0) == 0)
        def _():
            st_ref[...] = jnp.zeros_like(st_ref)

        dxv = dx_ref[...]
        dy_ref[...] = (dxv * g_ref[...]).astype(BF16)
        st_ref[0:1, :] += jnp.sum(dxv * y_ref[...].astype(F32), axis=0, keepdims=True)

    return pl.pallas_call(
        body, name=name, grid=(s // ROW_TILE,),
        in_specs=[_row_spec(d), _row_spec(d), _vec_spec(d)], out_specs=[_row_spec(d), _stat_spec(d)],
        out_shape=[jax.ShapeDtypeStruct((s, d), BF16), jax.ShapeDtypeStruct((8, d), F32)],
        compiler_params=_cparams("arbitrary"),
    )(dx, y, gate)


def _loss_head(name, x, gain, target):
    s, d = x.shape

    def body(x_ref, g_ref, t_ref, dx_ref, st_ref):
        @pl.when(pl.program_id(0) == 0)
        def _():
            st_ref[...] = jnp.zeros_like(st_ref)

        xv = x_ref[...]
        r = lax.rsqrt(jnp.mean(xv * xv, axis=-1, keepdims=True) + NORM_EPS)
        xh = xv * r
        err = xh * g_ref[...] - t_ref[...]
        dy = err * (1.0 / d)
        dxh = dy * g_ref[...]
        dx_ref[...] = r * (dxh - xh * jnp.mean(dxh * xh, axis=-1, keepdims=True))
        st_ref[0:1, :] += jnp.sum(dy * xh, axis=0, keepdims=True)
        st_ref[1:2, :] += jnp.sum(err * err, axis=0, keepdims=True) * (0.5 / d)

    return pl.pallas_call(
        body, name=name, grid=(s // ROW_TILE,),
        in_specs=[_row_spec(d), _vec_spec(d), _row_spec(d)], out_specs=[_row_spec(d), _stat_spec(d)],
        out_shape=[jax.ShapeDtypeStruct((s, d), F32), jax.ShapeDtypeStruct((8, d), F32)],
        compiler_params=_cparams("arbitrary"),
    )(x, gain, target)


def _adamw(name, parts, w, m, v):
    n_parts, rows, cols = parts.shape
    tr = rows
    for cand in (512, 256, 128, 64, 32, 16, 8):
        if rows % cand == 0 and cand * cols * 4 <= 2 * 1024 * 1024:
            tr = cand
            break
    c1 = 1.0 - ADAM_B1 ** ADAM_STEP
    c2 = 1.0 - ADAM_B2 ** ADAM_STEP

    def body(p_ref, w_ref, m_ref, v_ref, g_out, d_out, m_out, v_out):
        g = p_ref[0].astype(F32)
        for i in range(1, n_parts):
            g = g + p_ref[i].astype(F32)
        m_new = ADAM_B1 * m_ref[...] + (1.0 - ADAM_B1) * g
        v_new = ADAM_B2 * v_ref[...] + (1.0 - ADAM_B2) * (g * g)
        g_out[...] = g
        m_out[...] = m_new
        v_out[...] = v_new
        d_out[...] = -ADAM_LR * ((m_new / c1) / (jnp.sqrt(v_new / c2) + ADAM_EPS) + ADAM_WD * w_ref[...])

    spec = pl.BlockSpec((tr, cols), lambda i: (i, 0))
    return pl.pallas_call(
        body, name=name, grid=(rows // tr,),
        in_specs=[pl.BlockSpec((n_parts, tr, cols), lambda i: (0, i, 0)), spec, spec, spec],
        out_specs=[spec, spec, spec, spec],
        out_shape=[jax.ShapeDtypeStruct((rows, cols), F32)] * 4, compiler_params=_cparams("parallel"),
    )(parts, w, m, v)


def _attn_fwd(qkv):
    s = qkv.shape[1]
    nblk = s // ATTN_BLOCK

    def body(q_ref, kp_ref, kc_ref, vp_ref, vc_ref, o_ref, l_ref):
        g, j = pl.program_id(0), pl.program_id(1)
        first = lax.rem(j, jnp.int32(nblk) >> (2 * g)) == 0
        qi, kj = _iota((128, 128), 0), _iota((128, 128), 1)
        mask_c = kj <= qi
        mask_p = jnp.logical_and(kj >= qi, jnp.logical_not(first))
        low = kj < 64
        for p in range(4):
            sl = slice(128 * p, 128 * p + 128)
            q, kp, kc, vp, vc = q_ref[:, sl], kp_ref[:, sl], kc_ref[:, sl], vp_ref[:, sl], vc_ref[:, sl]
            o_pair = jnp.zeros((128, 128), F32)
            l_pair = jnp.zeros((128, 128), F32)
            for half in range(2):
                hm = low if half == 0 else jnp.logical_not(low)
                qm = jnp.where(hm, q, jnp.zeros_like(q))
                sc = jnp.where(mask_c, _dot_nt(qm, kc) * 0.125, NEG_BIG)
                sp = jnp.where(mask_p, _dot_nt(qm, kp) * 0.125, NEG_BIG)
                m = jnp.maximum(jnp.max(sc, axis=-1, keepdims=True), jnp.max(sp, axis=-1, keepdims=True))
                pc, pp = jnp.exp(sc - m), jnp.exp(sp - m)
                den = jnp.sum(pc, axis=-1, keepdims=True) + jnp.sum(pp, axis=-1, keepdims=True)
                oh = _dot((pc / den).astype(BF16), vc) + _dot((pp / den).astype(BF16), vp)
                o_pair = jnp.where(hm, oh, o_pair)
                l_pair = jnp.where(hm, m + jnp.log(den), l_pair)
            o_ref[:, sl] = o_pair
            l_ref[:, sl] = l_pair

    blk = (None, ATTN_BLOCK, ATTN_GROUP_WIDTH)
    prev = lambda j: jnp.maximum(j - 1, 0)
    return pl.pallas_call(
        body, name="attn_fwd", grid=(ATTN_GROUPS, nblk),
        in_specs=[pl.BlockSpec(blk, lambda g, j: (g, j, 0)),
                  pl.BlockSpec(blk, lambda g, j: (3 + g, prev(j), 0)), pl.BlockSpec(blk, lambda g, j: (3 + g, j, 0)),
                  pl.BlockSpec(blk, lambda g, j: (6 + g, prev(j), 0)), pl.BlockSpec(blk, lambda g, j: (6 + g, j, 0))],
        out_specs=[pl.BlockSpec(blk, lambda g, j: (g, j, 0))] * 2,
        out_shape=[jax.ShapeDtypeStruct((ATTN_GROUPS, s, ATTN_GROUP_WIDTH), F32)] * 2,
        compiler_params=_cparams("parallel", "parallel"),
    )(qkv, qkv, qkv, qkv, qkv)


def _attn_bwd(qkv, do, lse, cc):
    s = qkv.shape[1]
    nblk = s // ATTN_BLOCK

    def body(q_ref, kp_ref, kc_ref, vp_ref, vc_ref, do_ref, l_ref, c_ref, dq_ref, dk_ref, dv_ref, ck, cv):
        g, j = pl.program_id(0), pl.program_id(1)
        valid = j < nblk
        jq = jnp.minimum(j, nblk - 1)
        first = lax.rem(jq, jnp.int32(nblk) >> (2 * g)) == 0

        @pl.when(j == 0)
        def _():
            ck[...] = jnp.zeros_like(ck)
            cv[...] = jnp.zeros_like(cv)

        qi, kj = _iota((128, 128), 0), _iota((128, 128), 1)
        mask_c = jnp.logical_and(kj <= qi, valid)
        mask_p = jnp.logical_and(jnp.logical_and(kj >= qi, jnp.logical_not(first)), valid)
        low = kj < 64
        for p in range(4):
            sl = slice(128 * p, 128 * p + 128)
            q, kp, kc, vp, vc, dov = q_ref[:, sl], kp_ref[:, sl], kc_ref[:, sl], vp_ref[:, sl], vc_ref[:, sl], do_ref[:, sl]
            lse_pair, c_pair = l_ref[:, sl], c_ref[:, sl]
            dq_pair = jnp.zeros((128, 128), F32)
            dkc = jnp.zeros((128, 128), F32)
            dkp = jnp.zeros((128, 128), F32)
            dvc = jnp.zeros((128, 128), F32)
            dvp = jnp.zeros((128, 128), F32)
            for half in range(2):
                hm = low if half == 0 else jnp.logical_not(low)
                col = slice(64 * half, 64 * half + 1)
                lse_h, c_h = lse_pair[:, col], c_pair[:, col]
                qm = jnp.where(hm, q, jnp.zeros_like(q))
                dom = jnp.where(hm, dov, jnp.zeros_like(dov))
                pc = jnp.exp(jnp.where(mask_c, _dot_nt(qm, kc) * 0.125, NEG_BIG) - lse_h)
                pp = jnp.exp(jnp.where(mask_p, _dot_nt(qm, kp) * 0.125, NEG_BIG) - lse_h)
                dsc = (pc * (_dot_nt(dom, vc) + c_h) * 0.125).astype(BF16)
                dsp = (pp * (_dot_nt(dom, vp) + c_h) * 0.125).astype(BF16)
                dq_pair = jnp.where(hm, _dot(dsc, kc) + _dot(dsp, kp), dq_pair)
                dkc += _dot_tn(dsc, qm)
                dkp += _dot_tn(dsp, qm)
                dvc += _dot_tn(pc.astype(BF16), dom)
                dvp += _dot_tn(pp.astype(BF16), dom)

            @pl.when(valid)
            def _():
                dq_ref[:, sl] = dq_pair.astype(BF16)

            dk_ref[:, sl] = (ck[:, sl] + dkp).astype(BF16)
            dv_ref[:, sl] = (cv[:, sl] + dvp).astype(BF16)
            ck[:, sl] = dkc
            cv[:, sl] = dvc

    blk = (None, ATTN_BLOCK, ATTN_GROUP_WIDTH)
    cur = lambda j: jnp.minimum(j, nblk - 1)
    prev = lambda j: jnp.maximum(jnp.minimum(j, nblk - 1) - 1, 0)
    out_prev = lambda j: jnp.maximum(j - 1, 0)
    return pl.pallas_call(
        body, name="attn_bwd", grid=(ATTN_GROUPS, nblk + 1),
        in_specs=[pl.BlockSpec(blk, lambda g, j: (g, cur(j), 0)),
                  pl.BlockSpec(blk, lambda g, j: (3 + g, prev(j), 0)), pl.BlockSpec(blk, lambda g, j: (3 + g, cur(j), 0)),
                  pl.BlockSpec(blk, lambda g, j: (6 + g, prev(j), 0)), pl.BlockSpec(blk, lambda g, j: (6 + g, cur(j), 0)),
                  pl.BlockSpec(blk, lambda g, j: (g, cur(j), 0)), pl.BlockSpec(blk, lambda g, j: (g, cur(j), 0)),
                  pl.BlockSpec(blk, lambda g, j: (g, cur(j), 0))],
        out_specs=[pl.BlockSpec(blk, lambda g, j: (g, cur(j), 0)),
                   pl.BlockSpec(blk, lambda g, j: (g, out_prev(j), 0)), pl.BlockSpec(blk, lambda g, j: (g, out_prev(j), 0))],
        out_shape=[jax.ShapeDtypeStruct((ATTN_GROUPS, s, ATTN_GROUP_WIDTH), BF16)] * 3,
        scratch_shapes=[pltpu.VMEM((ATTN_BLOCK, ATTN_GROUP_WIDTH), F32)] * 2,
        compiler_params=_cparams("parallel", "arbitrary"),
    )(qkv, qkv, qkv, qkv, qkv, do, lse, cc)


MP_TILE = 256


def _merge_weights(l_ref):
    l0, l1, l2 = l_ref[0], l_ref[1], l_ref[2]
    m = jnp.maximum(jnp.maximum(l0, l1), l2)
    e0, e1, e2 = jnp.exp(l0 - m), jnp.exp(l1 - m), jnp.exp(l2 - m)
    den = e0 + e1 + e2
    return e0 / den, e1 / den, e2 / den


def _pool_diff(ucat, gi, tok):
    window = 2 << gi
    ug = ucat[:, 128 * gi:128 * gi + 128]
    acc, shift = ug, 1
    while shift < window:
        acc = acc + pltpu.roll(acc, shift, 0)
        shift *= 2
    cnt = jnp.minimum(tok + 1, window).astype(F32)
    return acc[POOL_HALO:, :] / cnt - ug[POOL_HALO:, :]


def _merge_pool_fwd(o, lse, proj, pool_w, pool_scale):
    s = o.shape[1]
    tr = MP_TILE

    def body(o_ref, l_ref, u_ref, uh_ref, pw_ref, ps_ref, cat_ref):
        i = pl.program_id(0)
        w0, w1, w2 = _merge_weights(l_ref)
        cat_ref[:, 0:512] = (w0 * o_ref[0] + w1 * o_ref[1] + w2 * o_ref[2]).astype(BF16)
        halo = jnp.where(i > 0, uh_ref[...].astype(F32), 0.0)
        ucat = jnp.concatenate([halo, u_ref[...].astype(F32)], axis=0)
        tok = i * tr + _iota((tr, 1), 0)
        for gi in range(POOL_GROUPS):
            sl = slice(128 * gi, 128 * gi + 128)
            diff = _pool_diff(ucat, gi, tok)
            yg = _dot(diff.astype(BF16), pw_ref[gi].astype(BF16)) * ps_ref[:, sl]
            cat_ref[:, 512 + 128 * gi:640 + 128 * gi] = yg.astype(BF16)

    return pl.pallas_call(
        body, name="merge_pool_fwd", grid=(s // tr,),
        in_specs=[pl.BlockSpec((3, tr, 512), lambda i: (0, i, 0)), pl.BlockSpec((3, tr, 512), lambda i: (0, i, 0)),
                  pl.BlockSpec((None, tr, 512), lambda i: (9, i, 0)),
                  pl.BlockSpec((None, POOL_HALO, 512), lambda i: (9, jnp.maximum(i * (tr // POOL_HALO) - 1, 0), 0)),
                  pl.BlockSpec((4, 128, 128), lambda i: (0, 0, 0)), pl.BlockSpec((1, 512), lambda i: (0, 0))],
        out_specs=pl.BlockSpec((tr, 1024), lambda i: (i, 0)),
        out_shape=jax.ShapeDtypeStruct((s, EVEN_OUT_WIDTH), BF16), compiler_params=_cparams("parallel"),
    )(o, lse, proj, proj, pool_w, pool_scale)


def _merge_pool_bwd(dcat, o, lse, proj, pool_w, pool_scale, head_sum):
    s = o.shape[1]
    tr = MP_TILE
    n_tiles = s // tr

    def body(da_ref, dp_ref, dph_ref, o_ref, l_ref, u_ref, uh_ref, pw_ref, ps_ref, hs_ref,
             do_ref, cc_ref, du_ref, dpw_ref, st_ref):
        i = pl.program_id(0)

        @pl.when(i == 0)
        def _():
            dpw_ref[...] = jnp.zeros_like(dpw_ref)
            st_ref[...] = jnp.zeros_like(st_ref)

        ws = _merge_weights(l_ref)
        da = da_ref[...]
        attn = ws[0] * o_ref[0] + ws[1] * o_ref[1] + ws[2] * o_ref[2]
        per_head = _dot3_right(da * attn, hs_ref[...])
        for gi in range(ATTN_GROUPS):
            do_ref[gi] = (ws[gi] * da).astype(BF16)
            cc_ref[gi] = -ws[gi] * per_head

        halo = jnp.where(i > 0, uh_ref[...].astype(F32), 0.0)
        ucat = jnp.concatenate([halo, u_ref[...].astype(F32)], axis=0)
        tok = i * tr + _iota((tr, 1), 0)
        dyp = dp_ref[...]
        dnext = jnp.where(i < n_tiles - 1, dph_ref[...], 0.0)
        dyp_ext = jnp.concatenate([dyp, dnext], axis=0)
        tok_ext = i * tr + _iota((tr + POOL_HALO, 1), 0)
        for gi in range(POOL_GROUPS):
            sl = slice(128 * gi, 128 * gi + 128)
            window = 2 << gi
            pw16 = pw_ref[gi].astype(BF16)
            d16 = _pool_diff(ucat, gi, tok).astype(BF16)
            st_ref[0:1, sl] += jnp.sum(dyp[:, sl] * _dot(d16, pw16), axis=0, keepdims=True)
            dpw_ref[gi] += _dot_tn(d16, (dyp[:, sl] * ps_ref[:, sl]).astype(BF16))
            dd = _dot_nt((dyp_ext[:, sl] * ps_ref[:, sl]).astype(BF16), pw16)
            acc = dd / jnp.minimum(tok_ext + 1, window).astype(F32)
            shift = 1
            while shift < window:
                acc = acc + pltpu.roll(acc, tr + POOL_HALO - shift, 0)
                shift *= 2
            du_ref[:, sl] = (acc[:tr, :] - dd[:tr, :]).astype(BF16)

    halo_blocks = tr // POOL_HALO
    return pl.pallas_call(
        body, name="merge_pool_bwd", grid=(n_tiles,),
        in_specs=[pl.BlockSpec((tr, 512), lambda i: (i, 0)), pl.BlockSpec((tr, 512), lambda i: (i, 1)),
                  pl.BlockSpec((POOL_HALO, 512), lambda i: (jnp.minimum((i + 1) * halo_blocks, s // POOL_HALO - 1), 1)),
                  pl.BlockSpec((3, tr, 512), lambda i: (0, i, 0)), pl.BlockSpec((3, tr, 512), lambda i: (0, i, 0)),
                  pl.BlockSpec((None, tr, 512), lambda i: (9, i, 0)),
                  pl.BlockSpec((None, POOL_HALO, 512), lambda i: (9, jnp.maximum(i * halo_blocks - 1, 0), 0)),
                  pl.BlockSpec((4, 128, 128), lambda i: (0, 0, 0)), pl.BlockSpec((1, 512), lambda i: (0, 0)),
                  pl.BlockSpec((512, 512), lambda i: (0, 0))],
        out_specs=[pl.BlockSpec((3, tr, 512), lambda i: (0, i, 0)), pl.BlockSpec((3, tr, 512), lambda i: (0, i, 0)),
                   pl.BlockSpec((tr, 512), lambda i: (i, 0)), pl.BlockSpec((4, 128, 128), lambda i: (0, 0, 0)),
                   pl.BlockSpec((8, 512), lambda i: (0, 0))],
        out_shape=[jax.ShapeDtypeStruct((3, s, 512), BF16), jax.ShapeDtypeStruct((3, s, 512), F32),
                   jax.ShapeDtypeStruct((s, 512), BF16), jax.ShapeDtypeStruct((4, 128, 128), F32),
                   jax.ShapeDtypeStruct((8, 512), F32)],
        compiler_params=_cparams("arbitrary"),
    )(dcat, dcat, dcat, o, lse, proj, proj, pool_w, pool_scale, head_sum)


CONV_TILE = 512
CONV_HALO = 8
XBC_BLOCK0 = SSM_D_INNER // 512
DT_BLOCK = (SSM_D_INNER + SSM_CONV_DIM) // 128


def _conv_fwd(proj, conv_w, conv_b):
    s = proj.shape[0]
    tr = CONV_TILE

    def body(x_ref, xh_ref, w_ref, b_ref, pre_ref, act_ref):
        i = pl.program_id(0)
        xcat = jnp.concatenate([jnp.where(i > 0, xh_ref[...], 0.0), x_ref[...]], axis=0)
        w = w_ref[...]
        pre = b_ref[...] + w[3:4, :] * xcat[CONV_HALO:, :]
        for back in range(1, SSM_CONV):
            pre = pre + w[3 - back:4 - back, :] * pltpu.roll(xcat, back, 0)[CONV_HALO:, :]
        pre_ref[...] = pre
        act_ref[...] = pre * _sigmoid(pre)

    hb = tr // CONV_HALO
    return pl.pallas_call(
        body, name="conv_fwd", grid=(s // tr, SSM_CONV_DIM // 512),
        in_specs=[pl.BlockSpec((tr, 512), lambda i, j: (i, XBC_BLOCK0 + j)),
                  pl.BlockSpec((CONV_HALO, 512), lambda i, j: (jnp.maximum(i * hb - 1, 0), XBC_BLOCK0 + j)),
                  pl.BlockSpec((SSM_CONV, 512), lambda i, j: (0, j)), pl.BlockSpec((1, 512), lambda i, j: (0, j))],
        out_specs=[pl.BlockSpec((tr, 512), lambda i, j: (i, j))] * 2,
        out_shape=[jax.ShapeDtypeStruct((s, SSM_CONV_DIM), F32)] * 2, compiler_params=_cparams("parallel", "parallel"),
    )(proj, proj, conv_w, conv_b)


def _conv_bwd(name, dact, pre, proj, conv_w, dproj, *, block0):
    s, width = dact.shape
    tr = CONV_TILE
    n_tiles = s // tr
    hb = tr // CONV_HALO

    def body(da_ref, dah_ref, pre_ref, preh_ref, x_ref, xh_ref, w_ref, _, dp_ref, st_ref):
        i = pl.program_id(1)

        @pl.when(i == 0)
        def _():
            st_ref[...] = jnp.zeros_like(st_ref)

        da_ext = jnp.concatenate([da_ref[...], jnp.where(i < n_tiles - 1, dah_ref[...], 0.0)], axis=0)
        pre_ext = jnp.concatenate([pre_ref[...], preh_ref[...]], axis=0)
        sg = _sigmoid(pre_ext)
        dpre_ext = da_ext * (sg * (1.0 + pre_ext * (1.0 - sg)))
        w = w_ref[...]
        draw = w[3:4, :] * dpre_ext[:tr, :]
        for ahead in range(1, SSM_CONV):
            draw = draw + w[3 - ahead:4 - ahead, :] * pltpu.roll(dpre_ext, tr + CONV_HALO - ahead, 0)[:tr, :]
        dp_ref[...] = draw.astype(BF16)
        dpre = dpre_ext[:tr, :]
        xcat = jnp.concatenate([jnp.where(i > 0, xh_ref[...], 0.0), x_ref[...]], axis=0)
        st_ref[3:4, :] += jnp.sum(dpre * xcat[CONV_HALO:, :], axis=0, keepdims=True)
        for back in range(1, SSM_CONV):
            st_ref[3 - back:4 - back, :] += jnp.sum(dpre * pltpu.roll(xcat, back, 0)[CONV_HALO:, :], axis=0, keepdims=True)
        st_ref[4:5, :] += jnp.sum(dpre, axis=0, keepdims=True)

    nxt = lambda i: jnp.minimum((i + 1) * hb, s // CONV_HALO - 1)
    prv = lambda i: jnp.maximum(i * hb - 1, 0)
    return pl.pallas_call(
        body, name=name, grid=(width // 512, n_tiles),
        in_specs=[pl.BlockSpec((tr, 512), lambda j, i: (i, j)), pl.BlockSpec((CONV_HALO, 512), lambda j, i: (nxt(i), j)),
                  pl.BlockSpec((tr, 512), lambda j, i: (i, block0 + j)),
                  pl.BlockSpec((CONV_HALO, 512), lambda j, i: (nxt(i), block0 + j)),
                  pl.BlockSpec((tr, 512), lambda j, i: (i, XBC_BLOCK0 + block0 + j)),
                  pl.BlockSpec((CONV_HALO, 512), lambda j, i: (prv(i), XBC_BLOCK0 + block0 + j)),
                  pl.BlockSpec((SSM_CONV, 512), lambda j, i: (0, block0 + j)), ANY],
        out_specs=[pl.BlockSpec((tr, 512), lambda j, i: (i, XBC_BLOCK0 + block0 + j)),
                   pl.BlockSpec((8, 512), lambda j, i: (0, j))],
        out_shape=[jax.ShapeDtypeStruct(dproj.shape, BF16), jax.ShapeDtypeStruct((8, width), F32)],
        input_output_aliases={7: 0}, compiler_params=_cparams("parallel", "arbitrary"),
    )(dact, dact, pre, pre, proj, proj, conv_w, dproj)


def _dt_fwd(proj, dt_bias):
    s = proj.shape[0]

    def body(x_ref, b_ref, o_ref):
        v = x_ref[...] + b_ref[...]
        o_ref[...] = jnp.maximum(v, 0.0) + jnp.log(1.0 + jnp.exp(-jnp.abs(v)))

    return pl.pallas_call(
        body, name="dt_fwd", grid=(s // CONV_TILE,),
        in_specs=[pl.BlockSpec((CONV_TILE, 128), lambda i: (i, DT_BLOCK)), pl.BlockSpec((1, 128), lambda i: (0, 0))],
        out_specs=pl.BlockSpec((CONV_TILE, 128), lambda i: (i, 0)),
        out_shape=jax.ShapeDtypeStruct((s, 128), F32), compiler_params=_cparams("parallel"),
    )(proj, dt_bias)


def _dt_bwd(ddt, proj, dt_bias, dproj):
    s = proj.shape[0]

    def body(d_ref, x_ref, b_ref, _, dp_ref, st_ref):
        @pl.when(pl.program_id(0) == 0)
        def _():
            st_ref[...] = jnp.zeros_like(st_ref)

        draw = d_ref[...] * _sigmoid(x_ref[...] + b_ref[...])
        dp_ref[...] = draw.astype(BF16)
        st_ref[0:1, :] += jnp.sum(draw, axis=0, keepdims=True)

    return pl.pallas_call(
        body, name="dt_bwd", grid=(s // CONV_TILE,),
        in_specs=[pl.BlockSpec((CONV_TILE, 128), lambda i: (i, 0)), pl.BlockSpec((CONV_TILE, 128), lambda i: (i, DT_BLOCK)),
                  pl.BlockSpec((1, 128), lambda i: (0, 0)), ANY],
        out_specs=[pl.BlockSpec((CONV_TILE, 128), lambda i: (i, DT_BLOCK)), pl.BlockSpec((8, 128), lambda i: (0, 0))],
        out_shape=[jax.ShapeDtypeStruct(dproj.shape, BF16), jax.ShapeDtypeStruct((8, 128), F32)],
        input_output_aliases={3: 0}, compiler_params=_cparams("arbitrary"),
    )(ddt, proj, dt_bias, dproj)


def _ssd_common(x_ref, b_ref, c_ref, dt_ref, al_ref):
    row, col = _iota((128, 128), 0), _iota((128, 128), 1)
    tril = row >= col
    expand = jnp.where((_iota((128, 512), 1) >> 6) == _iota((128, 512), 0), 1.0, 0.0).astype(BF16)
    dt = dt_ref[...]
    a_neg = -jnp.exp(al_ref[...])
    a_col = _dot3_left(jnp.where(tril, 1.0, 0.0).astype(BF16), dt * a_neg)
    a_exp = _dot3_right(a_col, expand)
    dt_exp = _dot3_right(dt, expand)
    x = x_ref[...]
    return dict(tril=tril, col=col, expand=expand, dt=dt, a_neg=a_neg, a_col=a_col, a_row=a_col.T, a_exp=a_exp,
                dt_exp=dt_exp, a_last=a_exp[127:128, :], x=x, xd=x * dt_exp,
                b16=b_ref[...].astype(BF16), c16=c_ref[...].astype(BF16))


def _ssd_specs(nc, order):
    return [pl.BlockSpec((SSM_CHUNK, 512), lambda g, c: (order(c), g)),
            pl.BlockSpec((SSM_CHUNK, 128), lambda g, c: (order(c), SSM_D_INNER // 128 + g)),
            pl.BlockSpec((SSM_CHUNK, 128), lambda g, c: (order(c), SSM_D_INNER // 128 + SSM_GROUPS + g)),
            pl.BlockSpec((SSM_CHUNK, 512), lambda g, c: (order(c), g)),
            pl.BlockSpec((None, SSM_CHUNK, 128), lambda g, c: (g, order(c), 0)),
            pl.BlockSpec((None, 1, 128), lambda g, c: (g, 0, 0)),
            pl.BlockSpec((1, 512), lambda g, c: (0, g)), pl.BlockSpec((1, 512), lambda g, c: (0, g))]


def _ssd_fwd(act, proj, dtc, a_log, d_skip, norm_g):
    s = act.shape[0]
    nc = s // SSM_CHUNK

    def body(x_ref, b_ref, c_ref, z_ref, dt_ref, al_ref, dsk_ref, ng_ref, y_ref, yn_ref, hin_ref, h_sc):
        @pl.when(pl.program_id(1) == 0)
        def _():
            h_sc[...] = jnp.zeros_like(h_sc)

        q = _ssd_common(x_ref, b_ref, c_ref, dt_ref, al_ref)
        gmat = _dot_nt(q["c16"], q["b16"])
        h_in = h_sc[...]
        hin_ref[...] = h_in
        zmat = _dot(q["c16"], h_in.astype(BF16))
        xd16 = q["xd"].astype(BF16)
        low = q["col"] < 64
        pieces = []
        for p in range(4):
            xs = xd16[:, 128 * p:128 * p + 128]
            acc = jnp.zeros((128, 128), F32)
            for half in range(2):
                r = 2 * p + half
                decay = jnp.exp(jnp.where(q["tril"], q["a_col"][:, r:r + 1] - q["a_row"][r:r + 1, :], NEG_BIG))
                hm = low if half == 0 else jnp.logical_not(low)
                acc += _dot((gmat * decay).astype(BF16), jnp.where(hm, xs, jnp.zeros_like(xs)))
            pieces.append(acc)
        y = jnp.concatenate(pieces, axis=1) + zmat * jnp.exp(q["a_exp"]) + q["x"] * dsk_ref[...]
        y_ref[...] = y
        w16 = (q["xd"] * jnp.exp(q["a_last"] - q["a_exp"])).astype(BF16)
        h_sc[...] = h_in * jnp.exp(q["a_last"]) + _dot_tn(q["b16"], w16)
        z = z_ref[...]
        yg = y * (z * _sigmoid(z))
        rr = lax.rsqrt(jnp.mean(yg * yg, axis=-1, keepdims=True) + NORM_EPS)
        yn_ref[...] = (yg * rr * ng_ref[...]).astype(BF16)

    blk = pl.BlockSpec((SSM_CHUNK, 512), lambda g, c: (c, g))
    return pl.pallas_call(
        body, name="ssd_fwd", grid=(SSM_GROUPS, nc), in_specs=_ssd_specs(nc, lambda c: c),
        out_specs=[blk, blk, pl.BlockSpec((None, None, SSM_STATE, 512), lambda g, c: (g, c, 0, 0))],
        out_shape=[jax.ShapeDtypeStruct((s, SSM_D_INNER), F32), jax.ShapeDtypeStruct((s, SSM_D_INNER), BF16),
                   jax.ShapeDtypeStruct((SSM_GROUPS, nc, SSM_STATE, 512), F32)],
        scratch_shapes=[pltpu.VMEM((SSM_STATE, 512), F32)], compiler_params=_cparams("parallel", "arbitrary"),
    )(act, act, act, proj, dtc, a_log, d_skip, norm_g)


def _ssd_bwd(act, proj, dtc, a_log, d_skip, norm_g, y, h_in_all, dyn):
    s = act.shape[0]
    nc = s // SSM_CHUNK

    def body(x_ref, b_ref, c_ref, z_ref, dt_ref, al_ref, dsk_ref, ng_ref, y_ref, hin_ref, dyn_ref,
             dz_ref, dx_ref, db_ref, dc_ref, ddt_ref, st_ref, dal_ref, dh_sc):
        @pl.when(pl.program_id(1) == 0)
        def _():
            dh_sc[...] = jnp.zeros_like(dh_sc)
            st_ref[...] = jnp.zeros_like(st_ref)
            dal_ref[...] = jnp.zeros_like(dal_ref)

        q = _ssd_common(x_ref, b_ref, c_ref, dt_ref, al_ref)
        x, xd, b16, c16 = q["x"], q["xd"], q["b16"], q["c16"]
        z, yv, dyn_v = z_ref[...], y_ref[...], dyn_ref[...]
        sig = _sigmoid(z)
        sil = z * sig
        yg = yv * sil
        rr = lax.rsqrt(jnp.mean(yg * yg, axis=-1, keepdims=True) + NORM_EPS)
        st_ref[0:1, :] += jnp.sum(dyn_v * yg * rr, axis=0, keepdims=True)
        t1 = dyn_v * ng_ref[...]
        dyg = rr * (t1 - yg * (rr * rr) * jnp.mean(t1 * yg, axis=-1, keepdims=True))
        dy = dyg * sil
        dz_ref[...] = (dyg * yv * (sig * (1.0 + z * (1.0 - sig)))).astype(BF16)
        st_ref[1:2, :] += jnp.sum(dy * x, axis=0, keepdims=True)
        dx = dsk_ref[...] * dy
        h_in = hin_ref[...]
        h16 = h_in.astype(BF16)
        ea = jnp.exp(q["a_exp"])
        zmat = _dot(c16, h16)
        dz16 = (dy * ea).astype(BF16)
        da_ch = dy * zmat * ea
        dcm = _dot_nt(dz16, h16)
        dh_in = _dot_tn(c16, dz16)
        dh_out = dh_sc[...]
        dho16 = dh_out.astype(BF16)
        eal = jnp.exp(q["a_last"])
        dh_in += dh_out * eal
        dal_ch = jnp.sum(dh_out * h_in, axis=0, keepdims=True) * eal
        to_end = jnp.exp(q["a_last"] - q["a_exp"])
        wmat = xd * to_end
        dbm = _dot_nt(wmat.astype(BF16), dho16)
        dw = _dot(b16, dho16)
        dxd = dw * to_end
        g_end = dw * wmat
        da_ch -= g_end
        dal_ch += jnp.sum(g_end, axis=0, keepdims=True)
        gmat = _dot_nt(c16, b16)
        gmat_t = _dot_nt(b16, c16)
        triu = _iota((128, 128), 0) <= q["col"]
        xd16, dy16 = xd.astype(BF16), dy.astype(BF16)
        low = q["col"] < 64
        da_col = jnp.zeros((128, 128), F32)
        dg = jnp.zeros((128, 128), F32)
        dg_t = jnp.zeros((128, 128), F32)
        pieces = []
        for p in range(4):
            xs, dys = xd16[:, 128 * p:128 * p + 128], dy16[:, 128 * p:128 * p + 128]
            acc = jnp.zeros((128, 128), F32)
            for half in range(2):
                r = 2 * p + half
                hm = low if half == 0 else jnp.logical_not(low)
                xm = jnp.where(hm, xs, jnp.zeros_like(xs))
                dym = jnp.where(hm, dys, jnp.zeros_like(dys))
                diff = q["a_col"][:, r:r + 1] - q["a_row"][r:r + 1, :]
                decay = jnp.exp(jnp.where(q["tril"], diff, NEG_BIG))
                decay_t = jnp.exp(jnp.where(triu, -diff, NEG_BIG))
                mm, mm_t = gmat * decay, gmat_t * decay_t
                dmm, dmm_t = _dot_nt(dym, xm), _dot_nt(xm, dym)
                acc += _dot(mm_t.astype(BF16), dym)
                dg += dmm * decay
                dg_t += dmm_t * decay_t
                rs = jnp.sum(dmm * mm, axis=-1, keepdims=True) - jnp.sum(dmm_t * mm_t, axis=-1, keepdims=True)
                da_col += jnp.where(q["col"] == r, rs, 0.0)
            pieces.append(acc)
        dxd += jnp.concatenate(pieces, axis=1)
        dcm += _dot(dg.astype(BF16), b16)
        dbm += _dot(dg_t.astype(BF16), c16)
        fold = jnp.where((_iota((512, 128), 0) >> 6) == _iota((512, 128), 1), 1.0, 0.0).astype(BF16)
        da_ch += jnp.where(_iota((128, 1), 0) == 127, dal_ch, 0.0)
        da_col += _dot3_right(da_ch, fold)
        d_dta = _dot3_left(jnp.where(triu, 1.0, 0.0).astype(BF16), da_col)
        ddt_ref[...] = d_dta * q["a_neg"] + _dot3_right(dxd * x, fold)
        dal_ref[0:1, :] += jnp.sum(d_dta * q["dt"] * q["a_neg"], axis=0, keepdims=True)
        dx_ref[...] = dx + dxd * q["dt_exp"]
        db_ref[...] = dbm
        dc_ref[...] = dcm
        dh_sc[...] = dh_in

    rev = lambda c: nc - 1 - c
    blk = pl.BlockSpec((SSM_CHUNK, 512), lambda g, c: (rev(c), g))
    small = pl.BlockSpec((SSM_CHUNK, 128), lambda g, c: (rev(c), g))
    return pl.pallas_call(
        body, name="ssd_bwd", grid=(SSM_GROUPS, nc),
        in_specs=_ssd_specs(nc, rev) + [blk, pl.BlockSpec((None, None, SSM_STATE, 512), lambda g, c: (g, rev(c), 0, 0)), blk],
        out_specs=[blk, blk, small, small, pl.BlockSpec((None, SSM_CHUNK, 128), lambda g, c: (g, rev(c), 0)),
                   pl.BlockSpec((8, 512), lambda g, c: (0, g)), pl.BlockSpec((None, 8, 128), lambda g, c: (g, 0, 0))],
        out_shape=[jax.ShapeDtypeStruct((s, SSM_IN_PAD), BF16), jax.ShapeDtypeStruct((s, SSM_D_INNER), F32),
                   jax.ShapeDtypeStruct((s, SSM_GROUPS * SSM_STATE), F32), jax.ShapeDtypeStruct((s, SSM_GROUPS * SSM_STATE), F32),
                   jax.ShapeDtypeStruct((SSM_GROUPS, s, 128), F32), jax.ShapeDtypeStruct((8, SSM_D_INNER), F32),
                   jax.ShapeDtypeStruct((SSM_GROUPS, 8, 128), F32)],
        scratch_shapes=[pltpu.VMEM((SSM_STATE, 512), F32)], compiler_params=_cparams("parallel", "arbitrary"),
    )(act, act, act, proj, dtc, a_log, d_skip, norm_g, y, h_in_all, dyn)


MM_TM = 1024
MM_TK = 512
MM_TK_BIG = 2048
FFN_SHARD = FFN_HIDDEN // N_DEV
ATTN_DILATIONS = (1, 4, 16)


def _to_classes(t, d):
    s, c = t.shape
    return t if d == 1 else t.reshape(s // d, d, c).transpose(1, 0, 2).reshape(s, c)


def _from_classes(t, d):
    s, c = t.shape
    return t if d == 1 else t.reshape(d, s // d, c).transpose(1, 0, 2).reshape(s, c)


def _ij(tm, tn):
    return pl.BlockSpec((tm, tn), lambda i, j, k: (i, j))


def _mm_plain(name, a, b, mode, dims, out_dtype, tn=1024, tk=MM_TK, **kw):
    res = _matmul(name, a, b, mode=mode, dims=dims, tiles=(MM_TM, tn, tk), outs=[((dims[0], dims[1]), out_dtype)],
                  epilogue=_epi_plain, **kw)
    return res[0], res[1:]


def _mm_resgate(name, a, b, res, gate, k_dim, tk, ride=None):
    s, d = res.shape
    got = _matmul(name, a, b, mode="nn", dims=(s, d, k_dim), tiles=(MM_TM, 1024, tk),
                  outs=[((s, d), F32), ((s, d), BF16)], epilogue=_epi_resgate, extras=[res, gate],
                  extra_specs=[_ij(MM_TM, 1024), pl.BlockSpec((1, 1024), lambda i, j, k: (0, j))], ride=ride)
    return got[0], got[1], got[2:]


def _ffn_fwd(tag, x_in, gain, scale, shift, gate, w1, w2_of, ride1=None, ride2=None):
    s, d = x_in.shape
    h = _norm_mod_fwd(tag + "_norm2", x_in, gain, scale, shift)
    rr, pre, *got1 = _matmul(tag + "_ffn1", h, w1, mode="nn", dims=(s, FFN_HIDDEN, d), tiles=(MM_TM, FFN_SHARD, MM_TK_BIG),
                             b_spec=pl.BlockSpec((None, MM_TK_BIG, FFN_SHARD), lambda i, j, k: (j, k, 0)),
                             outs=[((s, FFN_HIDDEN), BF16)] * 2, epilogue=_epi_relu2, ride=ride1)
    x_out, f, got2 = _mm_resgate(tag + "_ffn2", rr, w2_of(got1), x_in, gate, FFN_HIDDEN, MM_TK_BIG, ride=ride2)
    return x_out, (h, rr, pre, f), got1, got2


def _ffn_bwd(tag, dx_out, x_in, saved, gain, scale, gate, w1, w2, ride_dx2=None, ride_dx1_of=None):
    s, d = x_in.shape
    h, rr, pre, f = saved
    dy, st_gate = _gate_bwd(tag + "_gate2_bwd", dx_out, f, gate)
    da, *got2 = _matmul(tag + "_ffn2_dx", dy, w2, mode="nt", dims=(s, FFN_HIDDEN, d), tiles=(MM_TM, 1024, MM_TK_BIG),
                        outs=[((s, FFN_HIDDEN), BF16)], epilogue=_epi_drelu2, extras=[pre],
                        extra_specs=[_ij(MM_TM, 1024)], ride=ride_dx2)
    dw2, _ = _mm_plain(tag + "_ffn2_dw", rr, dy, "tn", (FFN_HIDDEN, d, s), BF16, tk=MM_TK_BIG)
    dh, got1 = _mm_plain(tag + "_ffn1_dx", da, w1, "nt", (s, d, FFN_HIDDEN), F32, tk=FFN_SHARD,
                         b_spec=pl.BlockSpec((None, 1024, FFN_SHARD), lambda i, j, k: (k, j, 0)),
                         ride=None if ride_dx1_of is None else ride_dx1_of(dw2))
    dw1 = _matmul(tag + "_ffn1_dw", h, da, mode="tn", dims=(d, FFN_HIDDEN, s), tiles=(MM_TM, FFN_SHARD, MM_TK_BIG),
                  outs=[((N_DEV, d, FFN_SHARD), BF16)], epilogue=_epi_plain,
                  out_specs=[pl.BlockSpec((None, MM_TM, FFN_SHARD), lambda i, j, k: (j, i, 0))])[0]
    dx_in, st_norm = _norm_mod_bwd(tag + "_norm2_bwd", x_in, gain, scale, dh, dx_out)
    return dx_in, dw1, dw2, st_gate[0], st_norm, got2, got1


def _full_weight(name, gathered):
    if name in ASSEMBLED:
        return _assemble_cols("assemble_" + name, gathered, ASSEMBLED[name])
    if name == "ffn_w1":
        return gathered
    return gathered.reshape(-1, gathered.shape[2])


def _grad_pieces(name, full, shard_shape):
    if name in ASSEMBLED:
        return _split_cols("split_" + name, full, shard_shape[1])
    return full.reshape(N_DEV, *shard_shape)


def _device_step(x, target, mod, sm, shards, w_even_in):
    s, d = x.shape
    mv = [[mod[i, k].reshape(1, d) for k in range(6)] for i in range(2)]
    nm = [sm["norm_mix"][i].reshape(1, d) for i in range(2)]
    nf = [sm["norm_ffn"][i].reshape(1, d) for i in range(2)]
    pool_w, pool_scale = sm["pool_w"].reshape(4, 128, 128), sm["pool_scale"].reshape(1, POOL_WIDTH)
    got = {}

    def gather(*items):
        return [shards[it] for it in items], False

    def scatter(*pieces):
        return list(pieces), True

    def pieces(item, full):
        return _grad_pieces(item[0], full, shards[item].shape)

    sh1, sc1, g1, sh2, sc2, g2 = mv[0]
    h1 = _norm_mod_fwd("l0_norm1", x, nm[0], sc1, sh1)
    slab = pl.BlockSpec((None, MM_TM, 512), lambda i, j, k: (j, i, 0))
    proj0, g_eout, g_w1_0 = _matmul(
        "l0_in", h1, w_even_in, mode="nn", dims=(s, EVEN_IN_WIDTH, d), tiles=(MM_TM, 512, MM_TK_BIG),
        outs=[((EVEN_IN_WIDTH // 512, s, 512), BF16)], out_specs=[slab], epilogue=_epi_plain,
        ride=gather(("even_w_out", 0), ("ffn_w1", 0)))
    w_even_out, w1_0 = _full_weight("even_w_out", g_eout), _full_weight("ffn_w1", g_w1_0)
    qkv_p = jnp.stack([_to_classes(proj0[kind * 3 + g], ATTN_DILATIONS[g]) for kind in range(3) for g in range(3)])
    o_p, lse_p = _attn_fwd(qkv_p)
    o = jnp.stack([_from_classes(o_p[g], ATTN_DILATIONS[g]) for g in range(3)])
    lse = jnp.stack([_from_classes(lse_p[g], ATTN_DILATIONS[g]) for g in range(3)])
    cat = _merge_pool_fwd(o, lse, proj0, pool_w, pool_scale)
    x1, y0, _ = _mm_resgate("l0_out", cat, w_even_out, x, g1, EVEN_OUT_WIDTH, EVEN_OUT_WIDTH)
    x2, ffn0, g_w2_0, g_sin = _ffn_fwd(
        "l0", x1, nf[0], sc2, sh2, g2, w1_0, lambda arrived: _full_weight("ffn_w2", arrived[0]),
        ride1=gather(("ffn_w2", 0)), ride2=gather(("ssm_w_in", 0)))
    w2_0, w_ssm_in = _full_weight("ffn_w2", g_w2_0[0]), _full_weight("ssm_w_in", g_sin[0])

    th1, tc1, tg1, th2, tc2, tg2 = mv[1]
    h3 = _norm_mod_fwd("l1_norm1", x2, nm[1], tc1, th1)
    proj1, (g_sout, g_w1_1) = _mm_plain("l1_in", h3, w_ssm_in, "nn", (s, SSM_IN_PAD, d), F32, tn=1152, tk=MM_TK_BIG,
                                        ride=gather(("ssm_w_out", 0), ("ffn_w1", 1)))
    w_ssm_out, w1_1 = _full_weight("ssm_w_out", g_sout), _full_weight("ffn_w1", g_w1_1)
    conv_w = sm["ssm_conv_w"].reshape(SSM_CONV, SSM_CONV_DIM)
    pre, act = _conv_fwd(proj1, conv_w, sm["ssm_conv_b"].reshape(1, SSM_CONV_DIM))
    dt_bias = jnp.pad(sm["ssm_dt_bias"].reshape(1, SSM_HEADS), ((0, 0), (0, 128 - SSM_HEADS)))
    dt_full = _dt_fwd(proj1, dt_bias)
    dtc = jnp.pad(dt_full[:, :SSM_HEADS].reshape(s, SSM_GROUPS, 8).transpose(1, 0, 2), ((0, 0), (0, 0), (0, 120)))
    a_log = jnp.pad(sm["ssm_a_log"].reshape(SSM_GROUPS, 1, 8), ((0, 0), (0, 0), (0, 120)))
    d_skip = jnp.repeat(sm["ssm_d"].reshape(SSM_HEADS), SSM_D_INNER // SSM_HEADS).reshape(1, SSM_D_INNER)
    norm_g = sm["ssm_norm"].reshape(1, SSM_D_INNER)
    y, yn, h_in_all = _ssd_fwd(act, proj1, dtc, a_log, d_skip, norm_g)
    x3, y1, _ = _mm_resgate("l1_out", yn, w_ssm_out, x2, tg1, SSM_D_INNER, MM_TK_BIG)
    x4, ffn1, g_w2_1, _ = _ffn_fwd(
        "l1", x3, nf[1], tc2, th2, tg2, w1_1, lambda arrived: _full_weight("ffn_w2", arrived[0]),
        ride1=gather(("ffn_w2", 1)))
    w2_1 = _full_weight("ffn_w2", g_w2_1[0])

    dx4, st_loss = _loss_head("loss_head", x4, sm["final_norm"].reshape(1, d), target)
    loss = jnp.sum(st_loss[1])

    dx3, dw1_1, dw2_1, dg2_1, st_n2_1, _, (got[("ffn_w2", 1)],) = _ffn_bwd(
        "l1", dx4, x3, ffn1, nf[1], tc2, tg2, w1_1, w2_1,
        ride_dx1_of=lambda dw2: scatter(pieces(("ffn_w2", 1), dw2)))
    dy1, st_g1_1 = _gate_bwd("l1_gate1_bwd", dx3, y1, tg1)
    dyn, _ = _mm_plain("l1_out_dx", dy1, w_ssm_out, "nt", (s, SSM_D_INNER, d), F32, tk=MM_TK_BIG)
    dw_sout, _ = _mm_plain("l1_out_dw", yn, dy1, "tn", (SSM_D_INNER, d, s), BF16, tk=MM_TK_BIG)
    dproj1, dxs, dbm, dcm, ddt, st_ssd, d_alog = _ssd_bwd(act, proj1, dtc, a_log, d_skip, norm_g, y, h_in_all, dyn)
    dproj1, st_cx = _conv_bwd("conv_bwd_x", dxs, pre, proj1, conv_w, dproj1, block0=0)
    dproj1, st_cb = _conv_bwd("conv_bwd_b", dbm, pre, proj1, conv_w, dproj1, block0=SSM_D_INNER // 512)
    dproj1, st_cc = _conv_bwd("conv_bwd_c", dcm, pre, proj1, conv_w, dproj1, block0=SSM_D_INNER // 512 + 2)
    ddt_rows = jnp.pad(ddt[:, :, :8].transpose(1, 0, 2).reshape(s, SSM_HEADS), ((0, 0), (0, 128 - SSM_HEADS)))
    dproj1, st_dt = _dt_bwd(ddt_rows, proj1, dt_bias, dproj1)
    dh3, (got[("ffn_w1", 1)],) = _mm_plain("l1_in_dx", dproj1, w_ssm_in, "nt", (s, d, SSM_IN_PAD), F32, tk=SSM_IN_PAD // 3,
                                           ride=scatter(dw1_1))
    dw_sin, (got[("ssm_w_out", 0)],) = _mm_plain("l1_in_dw", h3, dproj1, "tn", (d, SSM_IN_PAD, s), BF16, tn=1152,
                                                 tk=MM_TK_BIG, ride=scatter(pieces(("ssm_w_out", 0), dw_sout)))
    dx2, st_n1_1 = _norm_mod_bwd("l1_norm1_bwd", x2, nm[1], tc1, dh3, dx3)

    dx1, dw1_0, dw2_0, dg2_0, st_n2_0, (got[("ssm_w_in", 0)],), (got[("ffn_w2", 0)],) = _ffn_bwd(
        "l0", dx2, x1, ffn0, nf[0], sc2, g2, w1_0, w2_0, ride_dx2=scatter(pieces(("ssm_w_in", 0), dw_sin)),
        ride_dx1_of=lambda dw2: scatter(pieces(("ffn_w2", 0), dw2)))
    dy0, st_g1_0 = _gate_bwd("l0_gate1_bwd", dx1, y0, g1)
    dcat, _ = _mm_plain("l0_out_dx", dy0, w_even_out, "nt", (s, EVEN_OUT_WIDTH, d), F32, tk=MM_TK_BIG)
    dw_eout, _ = _mm_plain("l0_out_dw", cat, dy0, "tn", (EVEN_OUT_WIDTH, d, s), BF16, tk=MM_TK_BIG)
    lane = jnp.arange(512) // 64
    head_sum = (lane[:, None] == lane[None, :]).astype(BF16)
    do, cc, du, d_pool_w, st_pool = _merge_pool_bwd(dcat, o, lse, proj0, pool_w, pool_scale, head_sum)
    do_p = jnp.stack([_to_classes(do[g], ATTN_DILATIONS[g]) for g in range(3)])
    cc_p = jnp.stack([_to_classes(cc[g], ATTN_DILATIONS[g]) for g in range(3)])
    dq, dk, dv = _attn_bwd(qkv_p, do_p, lse_p, cc_p)
    dproj0 = jnp.stack([_from_classes(t[g], ATTN_DILATIONS[g]) for t in (dq, dk, dv) for g in range(3)] + [du])
    dh1, (got[("ffn_w1", 0)],) = _mm_plain("l0_in_dx", dproj0, w_even_in, "nt", (s, d, EVEN_IN_WIDTH), F32,
                                           a_spec=pl.BlockSpec((None, MM_TM, 512), lambda i, j, k: (k, i, 0)),
                                           ride=scatter(dw1_0))
    dw_ein, (got[("even_w_out", 0)],) = _mm_plain(
        "l0_in_dw", h1, dproj0, "tn", (d, EVEN_IN_WIDTH, s), BF16, tn=512, tk=MM_TK_BIG,
        b_spec=pl.BlockSpec((None, MM_TK_BIG, 512), lambda i, j, k: (j, k, 0)),
        ride=scatter(pieces(("even_w_out", 0), dw_eout)))
    grad_x, st_n1_0 = _norm_mod_bwd("l0_norm1_bwd", x, nm[0], sc1, dh1, dx1)

    dmod = jnp.stack([
        jnp.stack([st_n1_0[0], st_n1_0[1], st_g1_0[0], st_n2_0[0], st_n2_0[1], dg2_0]),
        jnp.stack([st_n1_1[0], st_n1_1[1], st_g1_1[0], st_n2_1[0], st_n2_1[1], dg2_1])])
    st_conv = jnp.concatenate([st_cx, st_cb, st_cc], axis=1)
    small = dict(
        norm_mix=jnp.stack([st_n1_0[2], st_n1_1[2]]), norm_ffn=jnp.stack([st_n2_0[2], st_n2_1[2]]),
        pool_w=d_pool_w, pool_scale=st_pool[0], ssm_conv_w=st_conv[:SSM_CONV], ssm_conv_b=st_conv[SSM_CONV],
        ssm_dt_bias=st_dt[0, :SSM_HEADS], ssm_a_log=d_alog[:, 0, :8].reshape(SSM_HEADS),
        ssm_d=jnp.sum(st_ssd[1].reshape(SSM_HEADS, SSM_D_INNER // SSM_HEADS), axis=-1), ssm_norm=st_ssd[0],
        final_norm=st_loss[0])
    return loss, grad_x, got, pieces(("even_w_in", 0), dw_ein), dmod, small


WEIGHT_ORDER = ("ada_w", "ada_b", "norm_mix", "norm_ffn", "ffn_w1", "ffn_w2", "even_w_in", "pool_w", "pool_scale",
                "even_w_out", "ssm_w_in", "ssm_conv_w", "ssm_conv_b", "ssm_dt_bias", "ssm_a_log", "ssm_d", "ssm_norm",
                "ssm_w_out", "final_norm")
BIG_LAYERS = ((("even_w_in", 0), ("even_w_out", 0), ("ffn_w1", 0), ("ffn_w2", 0)),
              (("ssm_w_in", 0), ("ssm_w_out", 0), ("ffn_w1", 1), ("ffn_w2", 1)))
STACKED = ("ffn_w1", "ffn_w2")
ASSEMBLED = {"even_w_in": EVEN_IN_WIDTH, "even_w_out": D_MODEL, "ssm_w_in": SSM_IN_PAD}
SMALL_REPLICATED = ("norm_mix", "norm_ffn", "pool_w", "pool_scale", "ssm_dt_bias", "ssm_a_log", "ssm_d", "final_norm")
SMALL_SHARDED = ("ssm_conv_w", "ssm_conv_b", "ssm_norm")


def _pack(flat_parts, width, lead=()):
    flat = jnp.concatenate(flat_parts, axis=-1)
    n = flat.shape[-1]
    rows = -(-n // (8 * width)) * 8
    flat = jnp.pad(flat, [(0, 0)] * len(lead) + [(0, rows * width - n)])
    return flat.reshape(*lead, rows, width)


def _unpack(packed, shapes, lead=()):
    flat = packed.reshape(*lead, -1)
    out, off = [], 0
    for shp in shapes:
        n = math.prod(shp)
        out.append(flat[..., off:off + n].reshape(*lead, *shp))
        off += n
    return out


def kernel(x, c, ada_w, ada_b, norm_mix, norm_ffn, ffn_w1, ffn_w2, even_w_in, pool_w, pool_scale, even_w_out, ssm_w_in, ssm_conv_w, ssm_conv_b, ssm_dt_bias, ssm_a_log, ssm_d, ssm_norm, ssm_w_out, final_norm, loss_target, m_ada_w, m_ada_b, m_norm_mix, m_norm_ffn, m_ffn_w1, m_ffn_w2, m_even_w_in, m_pool_w, m_pool_scale, m_even_w_out, m_ssm_w_in, m_ssm_conv_w, m_ssm_conv_b, m_ssm_dt_bias, m_ssm_a_log, m_ssm_d, m_ssm_norm, m_ssm_w_out, m_final_norm, v_ada_w, v_ada_b, v_norm_mix, v_norm_ffn, v_ffn_w1, v_ffn_w2, v_even_w_in, v_pool_w, v_pool_scale, v_even_w_out, v_ssm_w_in, v_ssm_conv_w, v_ssm_conv_b, v_ssm_dt_bias, v_ssm_a_log, v_ssm_d, v_ssm_norm, v_ssm_w_out, v_final_norm):
    w = dict(ada_w=ada_w, ada_b=ada_b, norm_mix=norm_mix, norm_ffn=norm_ffn, ffn_w1=ffn_w1, ffn_w2=ffn_w2,
             even_w_in=even_w_in, pool_w=pool_w, pool_scale=pool_scale, even_w_out=even_w_out, ssm_w_in=ssm_w_in,
             ssm_conv_w=ssm_conv_w, ssm_conv_b=ssm_conv_b, ssm_dt_bias=ssm_dt_bias, ssm_a_log=ssm_a_log, ssm_d=ssm_d,
             ssm_norm=ssm_norm, ssm_w_out=ssm_w_out, final_norm=final_norm)
    m = dict(ada_w=m_ada_w, ada_b=m_ada_b, norm_mix=m_norm_mix, norm_ffn=m_norm_ffn, ffn_w1=m_ffn_w1, ffn_w2=m_ffn_w2,
             even_w_in=m_even_w_in, pool_w=m_pool_w, pool_scale=m_pool_scale, even_w_out=m_even_w_out,
             ssm_w_in=m_ssm_w_in, ssm_conv_w=m_ssm_conv_w, ssm_conv_b=m_ssm_conv_b, ssm_dt_bias=m_ssm_dt_bias,
             ssm_a_log=m_ssm_a_log, ssm_d=m_ssm_d, ssm_norm=m_ssm_norm, ssm_w_out=m_ssm_w_out, final_norm=m_final_norm)
    v = dict(ada_w=v_ada_w, ada_b=v_ada_b, norm_mix=v_norm_mix, norm_ffn=v_norm_ffn, ffn_w1=v_ffn_w1, ffn_w2=v_ffn_w2,
             even_w_in=v_even_w_in, pool_w=v_pool_w, pool_scale=v_pool_scale, even_w_out=v_even_w_out,
             ssm_w_in=v_ssm_w_in, ssm_conv_w=v_ssm_conv_w, ssm_conv_b=v_ssm_conv_b, ssm_dt_bias=v_ssm_dt_bias,
             ssm_a_log=v_ssm_a_log, ssm_d=v_ssm_d, ssm_norm=v_ssm_norm, ssm_w_out=v_ssm_w_out, final_norm=v_final_norm)
    d = D_MODEL
    me = _my_index()

    sharded_shapes = [(SSM_CONV, SSM_CONV_DIM // N_DEV), (SSM_CONV_DIM // N_DEV,), (SSM_D_INNER // N_DEV,)]
    shards = {(name, idx): w[name][idx].astype(BF16) for layer in BIG_LAYERS for name, idx in layer}
    small_in = _pack([c.reshape(-1)] + [w[k].reshape(-1) for k in SMALL_SHARDED], 128)
    got, got_even_in = _exchange("gather_first", [small_in, shards[("even_w_in", 0)]], scatter=False)
    w_even_in = _full_weight("even_w_in", got_even_in)
    c_all, conv_w_sh, conv_b_sh, norm_sh = _unpack(got, [(d,)] + sharded_shapes, lead=(N_DEV,))
    sm = {k: w[k] for k in SMALL_REPLICATED}
    sm["ssm_conv_w"] = conv_w_sh.transpose(1, 0, 2).reshape(SSM_CONV, SSM_CONV_DIM)
    sm["ssm_conv_b"] = conv_b_sh.reshape(SSM_CONV_DIM)
    sm["ssm_norm"] = norm_sh.reshape(SSM_D_INNER)

    ada_cols = 6 * d // N_DEV
    c_pad = jnp.pad(c_all, ((0, 16 - N_DEV), (0, 0)))
    mod_cols = _matmul(
        "ada_fwd", c_pad, ada_w, mode="nn", dims=(16, 2 * ada_cols, d), tiles=(16, ada_cols // 2, d), a_fn=_silu,
        b_spec=pl.BlockSpec((None, d, ada_cols // 2), lambda i, j, k: (j // 2, k, j % 2)),
        outs=[((16, 2 * ada_cols), F32)], epilogue=_epi_plain)[0]
    mod_got = _all_to_all("ada_exchange", mod_cols[:N_DEV].reshape(N_DEV, 2, ada_cols))
    mod = (mod_got.transpose(1, 0, 2).reshape(2, 6 * d) + ada_b).reshape(2, 6, d)

    loss, grad_x, grad_got, even_in_pieces, dmod, small = _device_step(x[0], loss_target[0], mod, sm, shards, w_even_in)
    loss = lax.psum(loss, ("x", "y", "c"))

    grads, delta, new_m, new_v = {}, {}, {}, {}

    def update(name, parts, shape=None):
        rows, cols = parts.shape[1:]
        res = _adamw("adamw_" + name, parts, w[name].reshape(rows, cols), m[name].reshape(rows, cols), v[name].reshape(rows, cols))
        return [r.reshape(w[name].shape if shape is None else shape) for r in res]

    rep_shapes = [w[k].shape for k in SMALL_REPLICATED]
    pack_rep = lambda tree: _pack([tree[k].reshape(-1) for k in SMALL_REPLICATED], 128)
    dmod_all, rep_got = _exchange("gather_small_grads", [dmod.reshape(2, 6 * d), pack_rep(small)], scatter=False)

    my_cols = lax.dynamic_slice_in_dim(dmod_all, me * ada_cols, ada_cols, axis=2).reshape(N_DEV, 2 * ada_cols)
    g_ada_w = _matmul(
        "ada_dw", c_pad, jnp.pad(my_cols, ((0, 16 - N_DEV), (0, 0))), mode="tn", dims=(d, 2 * ada_cols, 16),
        tiles=(1024, ada_cols // 2, 16), a_fn=_silu, outs=[((2, d, ada_cols), F32)],
        out_specs=[pl.BlockSpec((None, 1024, ada_cols // 2), lambda i, j, k: (j // 2, i, j % 2))], epilogue=_epi_plain)[0]
    grads["ada_w"], delta["ada_w"], new_m["ada_w"], new_v["ada_w"] = update("ada_w", g_ada_w.reshape(1, 2 * d, ada_cols))
    grads["ada_b"], delta["ada_b"], new_m["ada_b"], new_v["ada_b"] = update("ada_b", dmod_all)

    sh_pieces = [small["ssm_conv_w"].reshape(SSM_CONV, N_DEV, -1).transpose(1, 0, 2).reshape(N_DEV, -1),
                 small["ssm_conv_b"].reshape(N_DEV, -1), small["ssm_norm"].reshape(N_DEV, -1)]
    grad_got[("even_w_in", 0)], sh_got = _exchange(
        "exchange_last", [even_in_pieces, _pack(sh_pieces, 128, lead=(N_DEV,))], scatter=True)

    stacked = {name: [None, None] for name in STACKED}
    for layer in BIG_LAYERS:
        for name, idx in layer:
            res = _adamw(f"adamw_{name}_{idx}", grad_got[(name, idx)], w[name][idx], m[name][idx], v[name][idx])
            if name in stacked:
                stacked[name][idx] = res
            else:
                grads[name], delta[name], new_m[name], new_v[name] = [r[None] for r in res]
    for name, per_layer in stacked.items():
        grads[name], delta[name], new_m[name], new_v[name] = [jnp.stack([per_layer[0][q], per_layer[1][q]]) for q in range(4)]

    rep_res = _adamw("adamw_small_replicated", rep_got, pack_rep(w), pack_rep(m), pack_rep(v))
    for dst, packed in zip((grads, delta, new_m, new_v), rep_res):
        for k, val in zip(SMALL_REPLICATED, _unpack(packed, rep_shapes)):
            dst[k] = val
    pack_sh = lambda tree: _pack([tree[k].reshape(-1) for k in SMALL_SHARDED], 128)
    sh_res = _adamw("adamw_small_sharded", sh_got, pack_sh(w), pack_sh(m), pack_sh(v))
    for dst, packed in zip((grads, delta, new_m, new_v), sh_res):
        for k, val in zip(SMALL_SHARDED, _unpack(packed, [w[k].shape for k in SMALL_SHARDED])):
            dst[k] = val

    out = [loss, grad_x[None]]
    for tree in (grads, delta, new_m, new_v):
        out.extend(tree[k] for k in WEIGHT_ORDER)
    return tuple(out)
```

```python
import functools
import math

import jax
import jax.numpy as jnp
from jax import lax
from jax.experimental import pallas as pl
from jax.experimental.pallas import tpu as pltpu

F32 = jnp.float32
BF16 = jnp.bfloat16

N_DEV = 8
D_MODEL = 2048
NORM_EPS = 1e-6
ATTN_BLOCK = 128
ATTN_GROUPS = 3
ATTN_GROUP_WIDTH = 512
ATTN_QKV_WIDTH = 3 * ATTN_GROUPS * ATTN_GROUP_WIDTH
POOL_GROUPS = 4
POOL_GROUP_WIDTH = 128
POOL_WIDTH = 512
POOL_HALO = 16
EVEN_IN_WIDTH = ATTN_QKV_WIDTH + POOL_WIDTH
EVEN_OUT_WIDTH = 1024
SSM_D_INNER = 4096
SSM_HEADS = 64
SSM_GROUPS = 8
SSM_GROUP_WIDTH = 512
SSM_STATE = 128
SSM_CHUNK = 128
SSM_CONV = 4
SSM_CONV_DIM = 6144
SSM_IN_WIDTH = 10304
SSM_IN_PAD = 10368
FFN_HIDDEN = 8192

ADAM_LR = 0.001
ADAM_B1 = 0.9
ADAM_B2 = 0.999
ADAM_EPS = 1e-08
ADAM_WD = 0.01
ADAM_STEP = 10

VMEM_LIMIT_BYTES = 56 * 1024 * 1024
NEG_BIG = -1e30

MESH_ID = pl.DeviceIdType.MESH
ANY = pl.BlockSpec(memory_space=pl.ANY)


def _cparams(*sem):
    return pltpu.CompilerParams(dimension_semantics=tuple(sem) if sem else None, vmem_limit_bytes=VMEM_LIMIT_BYTES)


def _dot(a, b):
    return lax.dot_general(a, b, (((1,), (0,)), ((), ())), preferred_element_type=F32)


def _dot_nt(a, b):
    return lax.dot_general(a, b, (((1,), (1,)), ((), ())), preferred_element_type=F32)


def _dot_tn(a, b):
    return lax.dot_general(a, b, (((0,), (0,)), ((), ())), preferred_element_type=F32)


def _split3(v):
    hi = v.astype(BF16)
    r1 = v - hi.astype(F32)
    mid = r1.astype(BF16)
    lo = (r1 - mid.astype(F32)).astype(BF16)
    return hi, mid, lo


def _dot3_left(const_bf16, v):
    hi, mid, lo = _split3(v)
    return _dot(const_bf16, hi) + _dot(const_bf16, mid) + _dot(const_bf16, lo)


def _dot3_right(v, const_bf16):
    hi, mid, lo = _split3(v)
    return _dot(hi, const_bf16) + _dot(mid, const_bf16) + _dot(lo, const_bf16)


def _iota(shape, dim):
    return lax.broadcasted_iota(jnp.int32, shape, dim)


def _sigmoid(x):
    return 1.0 / (1.0 + jnp.exp(-x))


def _peer(k):
    x, y, c = lax.axis_index("x"), lax.axis_index("y"), lax.axis_index("c")
    px = 1 - x if k & 4 else x
    py = 1 - y if k & 2 else y
    pc = 1 - c if k & 1 else c
    return (px, py, pc), 4 * px + 2 * py + pc


def _my_index():
    return 4 * lax.axis_index("x") + 2 * lax.axis_index("y") + lax.axis_index("c")


def _exchange(name, arrays, *, scatter):
    n = len(arrays)

    def body(*refs):
        ex = _Exchange(refs[:n], refs[n:2 * n], *refs[2 * n:], scatter)
        ex.start()
        ex.wait()

    return pl.pallas_call(
        body, name=name, out_shape=_exchange_out_shapes(arrays, scatter), in_specs=[ANY] * n, out_specs=[ANY] * n,
        scratch_shapes=_exchange_sems(n),
    )(*arrays)


def _exchange_out_shapes(arrays, scatter):
    return [jax.ShapeDtypeStruct((N_DEV,) + (tuple(a.shape[1:]) if scatter else tuple(a.shape)), a.dtype) for a in arrays]


def _exchange_sems(n):
    return [pltpu.SemaphoreType.DMA((n * (N_DEV - 1),)), pltpu.SemaphoreType.DMA((n * (N_DEV - 1),)),
            pltpu.SemaphoreType.DMA((n,))]


class _Exchange:
    def __init__(self, x_refs, out_refs, send_sems, recv_sems, local_sems, scatter):
        self.x_refs, self.out_refs, self.scatter = x_refs, out_refs, scatter
        self.send_sems, self.recv_sems, self.local_sems = send_sems, recv_sems, local_sems

    def _src(self, a, idx):
        return self.x_refs[a].at[idx] if self.scatter else self.x_refs[a]

    def _local(self, a):
        me = _my_index()
        return pltpu.make_async_copy(self._src(a, me), self.out_refs[a].at[me], self.local_sems.at[a])

    def _remote(self, a, k, landing):
        peer, peer_idx = _peer(k)
        sem = a * (N_DEV - 1) + k - 1
        slot = peer_idx if landing else _my_index()
        return pltpu.make_async_remote_copy(
            src_ref=self._src(a, peer_idx), dst_ref=self.out_refs[a].at[slot], send_sem=self.send_sems.at[sem],
            recv_sem=self.recv_sems.at[sem], device_id=peer, device_id_type=MESH_ID)

    def start(self):
        n = len(self.x_refs)
        for a in range(n):
            self._local(a).start()
        for k in range(1, N_DEV):
            for a in range(n):
                self._remote(a, k, False).start()

    def wait(self):
        n = len(self.x_refs)
        for k in range(1, N_DEV):
            for a in range(n):
                self._remote(a, k, True).wait_recv()
        for k in range(1, N_DEV):
            for a in range(n):
                self._remote(a, k, False).wait_send()
        for a in range(n):
            self._local(a).wait()


def _pcall(name, body, *, grid, in_specs, out_specs, out_shape, scratch_shapes, sem, args, ride=None, aliases=None):
    aliases = aliases or {}
    if ride is None:
        res = pl.pallas_call(body, name=name, grid=grid, in_specs=in_specs, out_specs=out_specs, out_shape=out_shape,
                             scratch_shapes=scratch_shapes, input_output_aliases=aliases, compiler_params=_cparams(*sem))(*args)
        return list(res), []
    arrays, scatter = ride
    n_in, n_out, n_scr, n_ride = len(in_specs), len(out_specs), len(scratch_shapes), len(arrays)

    def wrapped(*refs):
        ins, refs = refs[:n_in], refs[n_in:]
        ride_in, refs = refs[:n_ride], refs[n_ride:]
        outs, refs = refs[:n_out], refs[n_out:]
        ride_out, refs = refs[:n_ride], refs[n_ride:]
        scr, sems = refs[:n_scr], refs[n_scr:]
        exchange = _Exchange(ride_in, ride_out, *sems, scatter)
        ids = [pl.program_id(axis) for axis in range(len(grid))]
        first, last = ids[0] == 0, ids[0] == grid[0] - 1
        for axis in range(1, len(grid)):
            first, last = first & (ids[axis] == 0), last & (ids[axis] == grid[axis] - 1)

        @pl.when(first)
        def _():
            exchange.start()

        body(*ins, *outs, *scr)

        @pl.when(last)
        def _():
            exchange.wait()

    res = pl.pallas_call(
        wrapped, name=name, grid=grid, in_specs=list(in_specs) + [ANY] * n_ride, out_specs=list(out_specs) + [ANY] * n_ride,
        out_shape=list(out_shape) + _exchange_out_shapes(arrays, scatter),
        scratch_shapes=list(scratch_shapes) + _exchange_sems(n_ride), input_output_aliases=aliases,
        compiler_params=_cparams(*(("arbitrary",) * len(grid))))(*args, *arrays)
    return list(res[:n_out]), list(res[n_out:])


def _all_gather(name, x):
    return _exchange(name, [x], scatter=False)[0]


def _all_to_all(name, x):
    return _exchange(name, [x], scatter=True)[0]


def _assemble_cols(name, shards, width):
    _, k_dim, ns = shards.shape
    tr = 256

    def body(s_ref, o_ref):
        for dev in range(N_DEV):
            o_ref[:, ns * dev:ns * (dev + 1)] = s_ref[dev]
        if width > N_DEV * ns:
            o_ref[:, N_DEV * ns:] = jnp.zeros((tr, width - N_DEV * ns), o_ref.dtype)

    return pl.pallas_call(
        body, name=name, grid=(k_dim // tr,), in_specs=[pl.BlockSpec((N_DEV, tr, ns), lambda i: (0, i, 0))],
        out_specs=pl.BlockSpec((tr, width), lambda i: (i, 0)), out_shape=jax.ShapeDtypeStruct((k_dim, width), shards.dtype),
        compiler_params=_cparams("parallel"),
    )(shards)


def _split_cols(name, full, ns):
    k_dim, width = full.shape
    tr = 256

    def body(f_ref, o_ref):
        for dev in range(N_DEV):
            o_ref[dev] = f_ref[:, ns * dev:ns * (dev + 1)]

    return pl.pallas_call(
        body, name=name, grid=(k_dim // tr,), in_specs=[pl.BlockSpec((tr, width), lambda i: (i, 0))],
        out_specs=pl.BlockSpec((N_DEV, tr, ns), lambda i: (0, i, 0)),
        out_shape=jax.ShapeDtypeStruct((N_DEV, k_dim, ns), full.dtype), compiler_params=_cparams("parallel"),
    )(full)


_DIMS = {"nn": (((1,), (0,)), ((), ())), "nt": (((1,), (1,)), ((), ())), "tn": (((0,), (0,)), ((), ()))}


def _matmul(name, a, b, *, mode, dims, tiles, outs, epilogue, a_spec=None, b_spec=None, out_specs=None,
            extras=(), extra_specs=(), a_fn=None, ride=None):
    m_dim, n_dim, k_dim = dims
    tm, tn, tk = tiles
    assert m_dim % tm == 0 and n_dim % tn == 0 and k_dim % tk == 0, (name, dims, tiles)
    grid = (m_dim // tm, n_dim // tn, k_dim // tk)
    nk = grid[2]
    if a_spec is None:
        a_spec = pl.BlockSpec((tk, tm), lambda i, j, k: (k, i)) if mode == "tn" else pl.BlockSpec((tm, tk), lambda i, j, k: (i, k))
    if b_spec is None:
        b_spec = pl.BlockSpec((tn, tk), lambda i, j, k: (j, k)) if mode == "nt" else pl.BlockSpec((tk, tn), lambda i, j, k: (k, j))
    if out_specs is None:
        out_specs = [pl.BlockSpec((tm, tn), lambda i, j, k: (i, j)) for _ in outs]
    n_ex, n_out = len(extras), len(outs)
    ride_arrays, scatter = ride if ride is not None else ((), False)
    n_ride = len(ride_arrays)
    dn = _DIMS[mode]

    def body(a_ref, b_ref, *rest):
        ex_refs, rest = rest[:n_ex], rest[n_ex:]
        ride_in, rest = rest[:n_ride], rest[n_ride:]
        out_refs, rest = rest[:n_out], rest[n_out:]
        ride_out, rest = rest[:n_ride], rest[n_ride:]
        i, j, k = pl.program_id(0), pl.program_id(1), pl.program_id(2)
        if n_ride:
            exchange = _Exchange(ride_in, ride_out, *rest[-3:], scatter)

            @pl.when((i == 0) & (j == 0) & (k == 0))
            def _():
                exchange.start()

        at = a_ref[...]
        if a_fn is not None:
            at = a_fn(at)
        part = lax.dot_general(at.astype(BF16), b_ref[...].astype(BF16), dn, preferred_element_type=F32)

        def finish(total):
            res = epilogue(total, *[e[...] for e in ex_refs])
            for r, o in zip(res, out_refs):
                o[...] = r.astype(o.dtype)

        if nk == 1:
            finish(part)
        else:
            acc = rest[0]

            @pl.when(k == 0)
            def _():
                acc[...] = part

            @pl.when(k > 0)
            def _():
                acc[...] += part

            @pl.when(k == nk - 1)
            def _():
                finish(acc[...])

        if n_ride:
            @pl.when((i == grid[0] - 1) & (j == grid[1] - 1) & (k == nk - 1))
            def _():
                exchange.wait()

    scratch = ([pltpu.VMEM((tm, tn), F32)] if nk > 1 else []) + (_exchange_sems(n_ride) if n_ride else [])
    sem = ("arbitrary",) * 3 if n_ride else ("parallel", "parallel", "arbitrary")
    return pl.pallas_call(
        body, name=name, grid=grid,
        in_specs=[a_spec, b_spec, *extra_specs] + [ANY] * n_ride, out_specs=list(out_specs) + [ANY] * n_ride,
        out_shape=[jax.ShapeDtypeStruct(s, d) for s, d in outs] + (_exchange_out_shapes(ride_arrays, scatter) if n_ride else []),
        scratch_shapes=scratch, compiler_params=_cparams(*sem),
    )(a, b, *extras, *ride_arrays)


def _epi_plain(acc):
    return (acc,)


def _epi_relu2(acc):
    return jnp.square(jnp.maximum(acc, 0.0)), acc


def _epi_resgate(acc, res, gate):
    return res + gate * acc, acc


def _epi_drelu2(acc, pre):
    return (acc * (2.0 * jnp.maximum(pre.astype(F32), 0.0)),)


def _silu(v):
    return v * _sigmoid(v)


ROW_TILE = 256


def _row_spec(width, tr=ROW_TILE):
    return pl.BlockSpec((tr, width), lambda i: (i, 0))


def _vec_spec(width):
    return pl.BlockSpec((1, width), lambda i: (0, 0))


def _stat_spec(width):
    return pl.BlockSpec((8, width), lambda i: (0, 0))


def _norm_mod_fwd(name, x, gain, scale, shift):
    s, d = x.shape

    def body(x_ref, g_ref, sc_ref, sh_ref, h_ref):
        xv = x_ref[...]
        r = lax.rsqrt(jnp.mean(xv * xv, axis=-1, keepdims=True) + NORM_EPS)
        h_ref[...] = ((xv * r * g_ref[...]) * (1.0 + sc_ref[...]) + sh_ref[...]).astype(BF16)

    return pl.pallas_call(
        body, name=name, grid=(s // ROW_TILE,),
        in_specs=[_row_spec(d), _vec_spec(d), _vec_spec(d), _vec_spec(d)], out_specs=_row_spec(d),
        out_shape=jax.ShapeDtypeStruct((s, d), BF16), compiler_params=_cparams("parallel"),
    )(x, gain, scale, shift)


def _norm_mod_bwd(name, x, gain, scale, dh, dres):
    s, d = x.shape

    def body(x_ref, g_ref, sc_ref, dh_ref, dres_ref, dx_ref, st_ref):
        @pl.when(pl.program_id(0) == 0)
        def _():
            st_ref[...] = jnp.zeros_like(st_ref)

        xv = x_ref[...]
        dhv = dh_ref[...].astype(F32)
        r = lax.rsqrt(jnp.mean(xv * xv, axis=-1, keepdims=True) + NORM_EPS)
        xh = xv * r
        n = xh * g_ref[...]
        dn = dhv * (1.0 + sc_ref[...])
        dxh = dn * g_ref[...]
        dx_ref[...] = dres_ref[...] + r * (dxh - xh * jnp.mean(dxh * xh, axis=-1, keepdims=True))
        st_ref[0:1, :] += jnp.sum(dhv, axis=0, keepdims=True)
        st_ref[1:2, :] += jnp.sum(dhv * n, axis=0, keepdims=True)
        st_ref[2:3, :] += jnp.sum(dn * xh, axis=0, keepdims=True)

    return pl.pallas_call(
        body, name=name, grid=(s // ROW_TILE,),
        in_specs=[_row_spec(d), _vec_spec(d), _vec_spec(d), _row_spec(d), _row_spec(d)],
        out_specs=[_row_spec(d), _stat_spec(d)],
        out_shape=[jax.ShapeDtypeStruct((s, d), F32), jax.ShapeDtypeStruct((8, d), F32)],
        compiler_params=_cparams("arbitrary"),
    )(x, gain, scale, dh, dres)


def _gate_bwd(name, dx, y, gate):
    s, d = dx.shape

    def body(dx_ref, y_ref, g_ref, dy_ref, st_ref):
        @pl.when(pl.program_id(0) == 0)
        def _():
            st_ref[...] = jnp.zeros_like(st_ref)

        dxv = dx_ref[...]
        dy_ref[...] = (dxv * g_ref[...]).astype(BF16)
        st_ref[0:1, :] += jnp.sum(dxv * y_ref[...].astype(F32), axis=0, keepdims=True)

    return pl.pallas_call(
        body, name=name, grid=(s // ROW_TILE,),
        in_specs=[_row_spec(d), _row_spec(d), _vec_spec(d)], out_specs=[_row_spec(d), _stat_spec(d)],
        out_shape=[jax.ShapeDtypeStruct((s, d), BF16), jax.ShapeDtypeStruct((8, d), F32)],
        compiler_params=_cparams("arbitrary"),
    )(dx, y, gate)


def _loss_head(name, x, gain, target):
    s, d = x.shape

    def body(x_ref, g_ref, t_ref, dx_ref, st_ref):
        @pl.when(pl.program_id(0) == 0)
        def _():
            st_ref[...] = jnp.zeros_like(st_ref)

        xv = x_ref[...]
        r = lax.rsqrt(jnp.mean(xv * xv, axis=-1, keepdims=True) + NORM_EPS)
        xh = xv * r
        err = xh * g_ref[...] - t_ref[...]
        dy = err * (1.0 / d)
        dxh = dy * g_ref[...]
        dx_ref[...] = r * (dxh - xh * jnp.mean(dxh * xh, axis=-1, keepdims=True))
        st_ref[0:1, :] += jnp.sum(dy * xh, axis=0, keepdims=True)
        st_ref[1:2, :] += jnp.sum(err * err, axis=0, keepdims=True) * (0.5 / d)

    return pl.pallas_call(
        body, name=name, grid=(s // ROW_TILE,),
        in_specs=[_row_spec(d), _vec_spec(d), _row_spec(d)], out_specs=[_row_spec(d), _stat_spec(d)],
        out_shape=[jax.ShapeDtypeStruct((s, d), F32), jax.ShapeDtypeStruct((8, d), F32)],
        compiler_params=_cparams("arbitrary"),
    )(x, gain, target)


def _adamw(name, parts, w, m, v):
    n_parts, rows, cols = parts.shape
    tr = rows
    for cand in (512, 256, 128, 64, 32, 16, 8):
        if rows % cand == 0 and cand * cols * 4 <= 2 * 1024 * 1024:
            tr = cand
            break
    c1 = 1.0 - ADAM_B1 ** ADAM_STEP
    c2 = 1.0 - ADAM_B2 ** ADAM_STEP

    def body(p_ref, w_ref, m_ref, v_ref, g_out, d_out, m_out, v_out):
        g = p_ref[0].astype(F32)
        for i in range(1, n_parts):
            g = g + p_ref[i].astype(F32)
        m_new = ADAM_B1 * m_ref[...] + (1.0 - ADAM_B1) * g
        v_new = ADAM_B2 * v_ref[...] + (1.0 - ADAM_B2) * (g * g)
        g_out[...] = g
        m_out[...] = m_new
        v_out[...] = v_new
        d_out[...] = -ADAM_LR * ((m_new / c1) / (jnp.sqrt(v_new / c2) + ADAM_EPS) + ADAM_WD * w_ref[...])

    spec = pl.BlockSpec((tr, cols), lambda i: (i, 0))
    return pl.pallas_call(
        body, name=name, grid=(rows // tr,),
        in_specs=[pl.BlockSpec((n_parts, tr, cols), lambda i: (0, i, 0)), spec, spec, spec],
        out_specs=[spec, spec, spec, spec],
        out_shape=[jax.ShapeDtypeStruct((rows, cols), F32)] * 4, compiler_params=_cparams("parallel"),
    )(parts, w, m, v)


ATTN_DILATIONS = (1, 4, 16)


def _attn_fwd(proj, g, ride=None):
    s = proj.shape[1]
    dil = ATTN_DILATIONS[g]
    rows, nb = s // dil, s // dil // ATTN_BLOCK

    def body(q_ref, kp_ref, kc_ref, vp_ref, vc_ref, o_ref, l_ref):
        first = pl.program_id(1) == 0
        qi, kj = _iota((128, 128), 0), _iota((128, 128), 1)
        mask_c = kj <= qi
        mask_p = jnp.logical_and(kj >= qi, jnp.logical_not(first))
        low = kj < 64
        for p in range(4):
            sl = slice(128 * p, 128 * p + 128)
            q, kp, kc, vp, vc = q_ref[:, sl], kp_ref[:, sl], kc_ref[:, sl], vp_ref[:, sl], vc_ref[:, sl]
            o_pair = jnp.zeros((128, 128), F32)
            l_pair = jnp.zeros((128, 128), F32)
            for half in range(2):
                hm = low if half == 0 else jnp.logical_not(low)
                qm = jnp.where(hm, q, jnp.zeros_like(q))
                sc = jnp.where(mask_c, _dot_nt(qm, kc) * 0.125, NEG_BIG)
                sp = jnp.where(mask_p, _dot_nt(qm, kp) * 0.125, NEG_BIG)
                m = jnp.maximum(jnp.max(sc, axis=-1, keepdims=True), jnp.max(sp, axis=-1, keepdims=True))
                pc, pp = jnp.exp(sc - m), jnp.exp(sp - m)
                den = jnp.sum(pc, axis=-1, keepdims=True) + jnp.sum(pp, axis=-1, keepdims=True)
                oh = _dot((pc / den).astype(BF16), vc) + _dot((pp / den).astype(BF16), vp)
                o_pair = jnp.where(hm, oh, o_pair)
                l_pair = jnp.where(hm, m + jnp.log(den), l_pair)
            o_ref[:, sl] = o_pair
            l_ref[:, sl] = l_pair

    view = proj.reshape(proj.shape[0], rows, dil * ATTN_GROUP_WIDTH)
    blk = (None, ATTN_BLOCK, ATTN_GROUP_WIDTH)
    prev = lambda j: jnp.maximum(j - 1, 0)
    out_blk = pl.BlockSpec((ATTN_BLOCK, ATTN_GROUP_WIDTH), lambda r, j: (j, r))
    (o, lse), got = _pcall(
        f"attn_fwd_g{g}", body, grid=(dil, nb),
        in_specs=[pl.BlockSpec(blk, lambda r, j: (g, j, r)),
                  pl.BlockSpec(blk, lambda r, j: (3 + g, prev(j), r)), pl.BlockSpec(blk, lambda r, j: (3 + g, j, r)),
                  pl.BlockSpec(blk, lambda r, j: (6 + g, prev(j), r)), pl.BlockSpec(blk, lambda r, j: (6 + g, j, r))],
        out_specs=[out_blk] * 2, out_shape=[jax.ShapeDtypeStruct((rows, dil * ATTN_GROUP_WIDTH), F32)] * 2,
        scratch_shapes=[], sem=("parallel", "parallel"), args=(view,) * 5, ride=ride)
    return o.reshape(s, ATTN_GROUP_WIDTH), lse.reshape(s, ATTN_GROUP_WIDTH), got


def _attn_bwd(proj, g, do, lse, cc, ride=None):
    s = proj.shape[1]
    dil = ATTN_DILATIONS[g]
    rows, nblk = s // dil, s // dil // ATTN_BLOCK

    def body(q_ref, kp_ref, kc_ref, vp_ref, vc_ref, do_ref, l_ref, c_ref, dq_ref, dk_ref, dv_ref, ck, cv):
        j = pl.program_id(1)
        valid = j < nblk
        first = jnp.minimum(j, nblk - 1) == 0

        @pl.when(j == 0)
        def _():
            ck[...] = jnp.zeros_like(ck)
            cv[...] = jnp.zeros_like(cv)

        qi, kj = _iota((128, 128), 0), _iota((128, 128), 1)
        mask_c = jnp.logical_and(kj <= qi, valid)
        mask_p = jnp.logical_and(jnp.logical_and(kj >= qi, jnp.logical_not(first)), valid)
        low = kj < 64
        for p in range(4):
            sl = slice(128 * p, 128 * p + 128)
            q, kp, kc, vp, vc, dov = q_ref[:, sl], kp_ref[:, sl], kc_ref[:, sl], vp_ref[:, sl], vc_ref[:, sl], do_ref[:, sl]
            lse_pair, c_pair = l_ref[:, sl], c_ref[:, sl]
            dq_pair = jnp.zeros((128, 128), F32)
            dkc = jnp.zeros((128, 128), F32)
            dkp = jnp.zeros((128, 128), F32)
            dvc = jnp.zeros((128, 128), F32)
            dvp = jnp.zeros((128, 128), F32)
            for half in range(2):
                hm = low if half == 0 else jnp.logical_not(low)
                col = slice(64 * half, 64 * half + 1)
                lse_h, c_h = lse_pair[:, col], c_pair[:, col]
                qm = jnp.where(hm, q, jnp.zeros_like(q))
                dom = jnp.where(hm, dov, jnp.zeros_like(dov))
                pc = jnp.exp(jnp.where(mask_c, _dot_nt(qm, kc) * 0.125, NEG_BIG) - lse_h)
                pp = jnp.exp(jnp.where(mask_p, _dot_nt(qm, kp) * 0.125, NEG_BIG) - lse_h)
                dsc = (pc * (_dot_nt(dom, vc) + c_h) * 0.125).astype(BF16)
                dsp = (pp * (_dot_nt(dom, vp) + c_h) * 0.125).astype(BF16)
                dq_pair = jnp.where(hm, _dot(dsc, kc) + _dot(dsp, kp), dq_pair)
                dkc += _dot_tn(dsc, qm)
                dkp += _dot_tn(dsp, qm)
                dvc += _dot_tn(pc.astype(BF16), dom)
                dvp += _dot_tn(pp.astype(BF16), dom)

            @pl.when(valid)
            def _():
                dq_ref[:, sl] = dq_pair.astype(BF16)

            dk_ref[:, sl] = (ck[:, sl] + dkp).astype(BF16)
            dv_ref[:, sl] = (cv[:, sl] + dvp).astype(BF16)
            ck[:, sl] = dkc
            cv[:, sl] = dvc

    wide = dil * ATTN_GROUP_WIDTH
    view = proj.reshape(proj.shape[0], rows, wide)
    blk = (None, ATTN_BLOCK, ATTN_GROUP_WIDTH)
    flat = (ATTN_BLOCK, ATTN_GROUP_WIDTH)
    cur = lambda j: jnp.minimum(j, nblk - 1)
    prev = lambda j: jnp.maximum(jnp.minimum(j, nblk - 1) - 1, 0)
    out_prev = lambda j: jnp.maximum(j - 1, 0)
    (dq, dk, dv), got = _pcall(
        f"attn_bwd_g{g}", body, grid=(dil, nblk + 1),
        in_specs=[pl.BlockSpec(blk, lambda r, j: (g, cur(j), r)),
                  pl.BlockSpec(blk, lambda r, j: (3 + g, prev(j), r)), pl.BlockSpec(blk, lambda r, j: (3 + g, cur(j), r)),
                  pl.BlockSpec(blk, lambda r, j: (6 + g, prev(j), r)), pl.BlockSpec(blk, lambda r, j: (6 + g, cur(j), r)),
                  pl.BlockSpec(flat, lambda r, j: (cur(j), r)), pl.BlockSpec(flat, lambda r, j: (cur(j), r)),
                  pl.BlockSpec(flat, lambda r, j: (cur(j), r))],
        out_specs=[pl.BlockSpec(flat, lambda r, j: (cur(j), r)),
                   pl.BlockSpec(flat, lambda r, j: (out_prev(j), r)), pl.BlockSpec(flat, lambda r, j: (out_prev(j), r))],
        out_shape=[jax.ShapeDtypeStruct((rows, wide), BF16)] * 3,
        scratch_shapes=[pltpu.VMEM((ATTN_BLOCK, ATTN_GROUP_WIDTH), F32)] * 2, sem=("parallel", "arbitrary"),
        args=(view,) * 5 + (do.reshape(rows, wide), lse.reshape(rows, wide), cc.reshape(rows, wide)), ride=ride)
    return [t.reshape(s, ATTN_GROUP_WIDTH) for t in (dq, dk, dv)], got


MP_TILE = 256


def _merge_weights(l_refs):
    l0, l1, l2 = l_refs[0][...], l_refs[1][...], l_refs[2][...]
    m = jnp.maximum(jnp.maximum(l0, l1), l2)
    e0, e1, e2 = jnp.exp(l0 - m), jnp.exp(l1 - m), jnp.exp(l2 - m)
    den = e0 + e1 + e2
    return e0 / den, e1 / den, e2 / den


def _pool_diff(ucat, gi, tok):
    window = 2 << gi
    ug = ucat[:, 128 * gi:128 * gi + 128]
    acc, shift = ug, 1
    while shift < window:
        acc = acc + pltpu.roll(acc, shift, 0)
        shift *= 2
    cnt = jnp.minimum(tok + 1, window).astype(F32)
    return acc[POOL_HALO:, :] / cnt - ug[POOL_HALO:, :]


def _merge_pool_fwd(o, lse, proj, pool_w, pool_scale):
    s = o[0].shape[0]
    tr = MP_TILE

    def body(o0_ref, o1_ref, o2_ref, l0_ref, l1_ref, l2_ref, u_ref, uh_ref, pw_ref, ps_ref, cat_ref):
        i = pl.program_id(0)
        w0, w1, w2 = _merge_weights((l0_ref, l1_ref, l2_ref))
        cat_ref[:, 0:512] = (w0 * o0_ref[...] + w1 * o1_ref[...] + w2 * o2_ref[...]).astype(BF16)
        halo = jnp.where(i > 0, uh_ref[...].astype(F32), 0.0)
        ucat = jnp.concatenate([halo, u_ref[...].astype(F32)], axis=0)
        tok = i * tr + _iota((tr, 1), 0)
        for gi in range(POOL_GROUPS):
            sl = slice(128 * gi, 128 * gi + 128)
            diff = _pool_diff(ucat, gi, tok)
            yg = _dot(diff.astype(BF16), pw_ref[gi].astype(BF16)) * ps_ref[:, sl]
            cat_ref[:, 512 + 128 * gi:640 + 128 * gi] = yg.astype(BF16)

    return pl.pallas_call(
        body, name="merge_pool_fwd", grid=(s // tr,),
        in_specs=[pl.BlockSpec((tr, 512), lambda i: (i, 0))] * 6 + [
                  pl.BlockSpec((None, tr, 512), lambda i: (9, i, 0)),
                  pl.BlockSpec((None, POOL_HALO, 512), lambda i: (9, jnp.maximum(i * (tr // POOL_HALO) - 1, 0), 0)),
                  pl.BlockSpec((4, 128, 128), lambda i: (0, 0, 0)), pl.BlockSpec((1, 512), lambda i: (0, 0))],
        out_specs=pl.BlockSpec((tr, 1024), lambda i: (i, 0)),
        out_shape=jax.ShapeDtypeStruct((s, EVEN_OUT_WIDTH), BF16), compiler_params=_cparams("parallel"),
    )(*o, *lse, proj, proj, pool_w, pool_scale)


def _merge_pool_bwd(dcat, o, lse, proj, pool_w, pool_scale, head_sum):
    s = o[0].shape[0]
    tr = MP_TILE
    n_tiles = s // tr

    def body(da_ref, dp_ref, dph_ref, o0_ref, o1_ref, o2_ref, l0_ref, l1_ref, l2_ref, u_ref, uh_ref, pw_ref, ps_ref,
             hs_ref, do0_ref, do1_ref, do2_ref, cc0_ref, cc1_ref, cc2_ref, du_ref, dpw_ref, st_ref):
        i = pl.program_id(0)

        @pl.when(i == 0)
        def _():
            dpw_ref[...] = jnp.zeros_like(dpw_ref)
            st_ref[...] = jnp.zeros_like(st_ref)

        ws = _merge_weights((l0_ref, l1_ref, l2_ref))
        da = da_ref[...]
        attn = ws[0] * o0_ref[...] + ws[1] * o1_ref[...] + ws[2] * o2_ref[...]
        per_head = _dot3_right(da * attn, hs_ref[...])
        for wg, do_ref, cc_ref in zip(ws, (do0_ref, do1_ref, do2_ref), (cc0_ref, cc1_ref, cc2_ref)):
            do_ref[...] = (wg * da).astype(BF16)
            cc_ref[...] = -wg * per_head

        halo = jnp.where(i > 0, uh_ref[...].astype(F32), 0.0)
        ucat = jnp.concatenate([halo, u_ref[...].astype(F32)], axis=0)
        tok = i * tr + _iota((tr, 1), 0)
        dyp = dp_ref[...]
        dnext = jnp.where(i < n_tiles - 1, dph_ref[...], 0.0)
        dyp_ext = jnp.concatenate([dyp, dnext], axis=0)
        tok_ext = i * tr + _iota((tr + POOL_HALO, 1), 0)
        for gi in range(POOL_GROUPS):
            sl = slice(128 * gi, 128 * gi + 128)
            window = 2 << gi
            pw16 = pw_ref[gi].astype(BF16)
            d16 = _pool_diff(ucat, gi, tok).astype(BF16)
            st_ref[0:1, sl] += jnp.sum(dyp[:, sl] * _dot(d16, pw16), axis=0, keepdims=True)
            dpw_ref[gi] += _dot_tn(d16, (dyp[:, sl] * ps_ref[:, sl]).astype(BF16))
            dd = _dot_nt((dyp_ext[:, sl] * ps_ref[:, sl]).astype(BF16), pw16)
            acc = dd / jnp.minimum(tok_ext + 1, window).astype(F32)
            shift = 1
            while shift < window:
                acc = acc + pltpu.roll(acc, tr + POOL_HALO - shift, 0)
                shift *= 2
            du_ref[:, sl] = (acc[:tr, :] - dd[:tr, :]).astype(BF16)

    halo_blocks = tr // POOL_HALO
    return pl.pallas_call(
        body, name="merge_pool_bwd", grid=(n_tiles,),
        in_specs=[pl.BlockSpec((tr, 512), lambda i: (i, 0)), pl.BlockSpec((tr, 512), lambda i: (i, 1)),
                  pl.BlockSpec((POOL_HALO, 512), lambda i: (jnp.minimum((i + 1) * halo_blocks, s // POOL_HALO - 1), 1))]
                 + [pl.BlockSpec((tr, 512), lambda i: (i, 0))] * 6 + [
                  pl.BlockSpec((None, tr, 512), lambda i: (9, i, 0)),
                  pl.BlockSpec((None, POOL_HALO, 512), lambda i: (9, jnp.maximum(i * halo_blocks - 1, 0), 0)),
                  pl.BlockSpec((4, 128, 128), lambda i: (0, 0, 0)), pl.BlockSpec((1, 512), lambda i: (0, 0)),
                  pl.BlockSpec((512, 512), lambda i: (0, 0))],
        out_specs=[pl.BlockSpec((tr, 512), lambda i: (i, 0))] * 7 + [
                   pl.BlockSpec((4, 128, 128), lambda i: (0, 0, 0)), pl.BlockSpec((8, 512), lambda i: (0, 0))],
        out_shape=[jax.ShapeDtypeStruct((s, 512), BF16)] * 3 + [jax.ShapeDtypeStruct((s, 512), F32)] * 3 + [
                   jax.ShapeDtypeStruct((s, 512), BF16), jax.ShapeDtypeStruct((4, 128, 128), F32),
                   jax.ShapeDtypeStruct((8, 512), F32)],
        compiler_params=_cparams("arbitrary"),
    )(dcat, dcat, dcat, *o, *lse, proj, proj, pool_w, pool_scale, head_sum)


CONV_TILE = 512
CONV_HALO = 8
XBC_BLOCK0 = SSM_D_INNER // 512
DT_BLOCK = (SSM_D_INNER + SSM_CONV_DIM) // 128


def _conv_fwd(proj, conv_w, conv_b):
    s = proj.shape[0]
    tr = CONV_TILE

    def body(x_ref, xh_ref, w_ref, b_ref, pre_ref, act_ref):
        i = pl.program_id(0)
        xcat = jnp.concatenate([jnp.where(i > 0, xh_ref[...], 0.0), x_ref[...]], axis=0)
        w = w_ref[...]
        pre = b_ref[...] + w[3:4, :] * xcat[CONV_HALO:, :]
        for back in range(1, SSM_CONV):
            pre = pre + w[3 - back:4 - back, :] * pltpu.roll(xcat, back, 0)[CONV_HALO:, :]
        pre_ref[...] = pre
        act_ref[...] = pre * _sigmoid(pre)

    hb = tr // CONV_HALO
    return pl.pallas_call(
        body, name="conv_fwd", grid=(s // tr, SSM_CONV_DIM // 512),
        in_specs=[pl.BlockSpec((tr, 512), lambda i, j: (i, XBC_BLOCK0 + j)),
                  pl.BlockSpec((CONV_HALO, 512), lambda i, j: (jnp.maximum(i * hb - 1, 0), XBC_BLOCK0 + j)),
                  pl.BlockSpec((SSM_CONV, 512), lambda i, j: (0, j)), pl.BlockSpec((1, 512), lambda i, j: (0, j))],
        out_specs=[pl.BlockSpec((tr, 512), lambda i, j: (i, j))] * 2,
        out_shape=[jax.ShapeDtypeStruct((s, SSM_CONV_DIM), F32)] * 2, compiler_params=_cparams("parallel", "parallel"),
    )(proj, proj, conv_w, conv_b)


def _conv_bwd(name, dact, pre, proj, conv_w, dproj, *, block0):
    s, width = dact.shape
    tr = CONV_TILE
    n_tiles = s // tr
    hb = tr // CONV_HALO

    def body(da_ref, dah_ref, pre_ref, preh_ref, x_ref, xh_ref, w_ref, _, dp_ref, st_ref):
        i = pl.program_id(1)

        @pl.when(i == 0)
        def _():
            st_ref[...] = jnp.zeros_like(st_ref)

        da_ext = jnp.concatenate([da_ref[...], jnp.where(i < n_tiles - 1, dah_ref[...], 0.0)], axis=0)
        pre_ext = jnp.concatenate([pre_ref[...], preh_ref[...]], axis=0)
        sg = _sigmoid(pre_ext)
        dpre_ext = da_ext * (sg * (1.0 + pre_ext * (1.0 - sg)))
        w = w_ref[...]
        draw = w[3:4, :] * dpre_ext[:tr, :]
        for ahead in range(1, SSM_CONV):
            draw = draw + w[3 - ahead:4 - ahead, :] * pltpu.roll(dpre_ext, tr + CONV_HALO - ahead, 0)[:tr, :]
        dp_ref[...] = draw.astype(BF16)
        dpre = dpre_ext[:tr, :]
        xcat = jnp.concatenate([jnp.where(i > 0, xh_ref[...], 0.0), x_ref[...]], axis=0)
        st_ref[3:4, :] += jnp.sum(dpre * xcat[CONV_HALO:, :], axis=0, keepdims=True)
        for back in range(1, SSM_CONV):
            st_ref[3 - back:4 - back, :] += jnp.sum(dpre * pltpu.roll(xcat, back, 0)[CONV_HALO:, :], axis=0, keepdims=True)
        st_ref[4:5, :] += jnp.sum(dpre, axis=0, keepdims=True)

    nxt = lambda i: jnp.minimum((i + 1) * hb, s // CONV_HALO - 1)
    prv = lambda i: jnp.maximum(i * hb - 1, 0)
    return pl.pallas_call(
        body, name=name, grid=(width // 512, n_tiles),
        in_specs=[pl.BlockSpec((tr, 512), lambda j, i: (i, j)), pl.BlockSpec((CONV_HALO, 512), lambda j, i: (nxt(i), j)),
                  pl.BlockSpec((tr, 512), lambda j, i: (i, block0 + j)),
                  pl.BlockSpec((CONV_HALO, 512), lambda j, i: (nxt(i), block0 + j)),
                  pl.BlockSpec((tr, 512), lambda j, i: (i, XBC_BLOCK0 + block0 + j)),
                  pl.BlockSpec((CONV_HALO, 512), lambda j, i: (prv(i), XBC_BLOCK0 + block0 + j)),
                  pl.BlockSpec((SSM_CONV, 512), lambda j, i: (0, block0 + j)), ANY],
        out_specs=[pl.BlockSpec((tr, 512), lambda j, i: (i, XBC_BLOCK0 + block0 + j)),
                   pl.BlockSpec((8, 512), lambda j, i: (0, j))],
        out_shape=[jax.ShapeDtypeStruct(dproj.shape, BF16), jax.ShapeDtypeStruct((8, width), F32)],
        input_output_aliases={7: 0}, compiler_params=_cparams("parallel", "arbitrary"),
    )(dact, dact, pre, pre, proj, proj, conv_w, dproj)


def _dt_fwd(proj, dt_bias):
    s = proj.shape[0]

    def body(x_ref, b_ref, o_ref):
        v = x_ref[...] + b_ref[...]
        o_ref[...] = jnp.maximum(v, 0.0) + jnp.log(1.0 + jnp.exp(-jnp.abs(v)))

    return pl.pallas_call(
        body, name="dt_fwd", grid=(s // CONV_TILE,),
        in_specs=[pl.BlockSpec((CONV_TILE, 128), lambda i: (i, DT_BLOCK)), pl.BlockSpec((1, 128), lambda i: (0, 0))],
        out_specs=pl.BlockSpec((CONV_TILE, 128), lambda i: (i, 0)),
        out_shape=jax.ShapeDtypeStruct((s, 128), F32), compiler_params=_cparams("parallel"),
    )(proj, dt_bias)


def _dt_bwd(ddt, proj, dt_bias, dproj):
    s = proj.shape[0]

    def body(d_ref, x_ref, b_ref, _, dp_ref, st_ref):
        @pl.when(pl.program_id(0) == 0)
        def _():
            st_ref[...] = jnp.zeros_like(st_ref)

        draw = d_ref[...] * _sigmoid(x_ref[...] + b_ref[...])
        dp_ref[...] = draw.astype(BF16)
        st_ref[0:1, :] += jnp.sum(draw, axis=0, keepdims=True)

    return pl.pallas_call(
        body, name="dt_bwd", grid=(s // CONV_TILE,),
        in_specs=[pl.BlockSpec((CONV_TILE, 128), lambda i: (i, 0)), pl.BlockSpec((CONV_TILE, 128), lambda i: (i, DT_BLOCK)),
                  pl.BlockSpec((1, 128), lambda i: (0, 0)), ANY],
        out_specs=[pl.BlockSpec((CONV_TILE, 128), lambda i: (i, DT_BLOCK)), pl.BlockSpec((8, 128), lambda i: (0, 0))],
        out_shape=[jax.ShapeDtypeStruct(dproj.shape, BF16), jax.ShapeDtypeStruct((8, 128), F32)],
        input_output_aliases={3: 0}, compiler_params=_cparams("arbitrary"),
    )(ddt, proj, dt_bias, dproj)


def _ssd_common(x_ref, b_ref, c_ref, dt_ref, al_ref):
    row, col = _iota((128, 128), 0), _iota((128, 128), 1)
    tril = row >= col
    expand = jnp.where((_iota((128, 512), 1) >> 6) == _iota((128, 512), 0), 1.0, 0.0).astype(BF16)
    dt = dt_ref[...]
    a_neg = -jnp.exp(al_ref[...])
    a_col = _dot3_left(jnp.where(tril, 1.0, 0.0).astype(BF16), dt * a_neg)
    a_exp = _dot3_right(a_col, expand)
    dt_exp = _dot3_right(dt, expand)
    x = x_ref[...]
    return dict(tril=tril, col=col, expand=expand, dt=dt, a_neg=a_neg, a_col=a_col, a_row=a_col.T, a_exp=a_exp,
                dt_exp=dt_exp, a_last=a_exp[127:128, :], x=x, xd=x * dt_exp,
                b16=b_ref[...].astype(BF16), c16=c_ref[...].astype(BF16))


def _ssd_specs(nc, order):
    return [pl.BlockSpec((SSM_CHUNK, 512), lambda g, c: (order(c), g)),
            pl.BlockSpec((SSM_CHUNK, 128), lambda g, c: (order(c), SSM_D_INNER // 128 + g)),
            pl.BlockSpec((SSM_CHUNK, 128), lambda g, c: (order(c), SSM_D_INNER // 128 + SSM_GROUPS + g)),
            pl.BlockSpec((SSM_CHUNK, 512), lambda g, c: (order(c), g)),
            pl.BlockSpec((None, SSM_CHUNK, 128), lambda g, c: (g, order(c), 0)),
            pl.BlockSpec((None, 1, 128), lambda g, c: (g, 0, 0)),
            pl.BlockSpec((1, 512), lambda g, c: (0, g)), pl.BlockSpec((1, 512), lambda g, c: (0, g))]


def _ssd_fwd(act, proj, dtc, a_log, d_skip, norm_g, ride=None):
    s = act.shape[0]
    nc = s // SSM_CHUNK

    def body(x_ref, b_ref, c_ref, z_ref, dt_ref, al_ref, dsk_ref, ng_ref, y_ref, yn_ref, hin_ref, h_sc):
        @pl.when(pl.program_id(1) == 0)
        def _():
            h_sc[...] = jnp.zeros_like(h_sc)

        q = _ssd_common(x_ref, b_ref, c_ref, dt_ref, al_ref)
        gmat = _dot_nt(q["c16"], q["b16"])
        h_in = h_sc[...]
        hin_ref[...] = h_in
        zmat = _dot(q["c16"], h_in.astype(BF16))
        xd16 = q["xd"].astype(BF16)
        low = q["col"] < 64
        pieces = []
        for p in range(4):
            xs = xd16[:, 128 * p:128 * p + 128]
            acc = jnp.zeros((128, 128), F32)
            for half in range(2):
                r = 2 * p + half
                decay = jnp.exp(jnp.where(q["tril"], q["a_col"][:, r:r + 1] - q["a_row"][r:r + 1, :], NEG_BIG))
                hm = low if half == 0 else jnp.logical_not(low)
                acc += _dot((gmat * decay).astype(BF16), jnp.where(hm, xs, jnp.zeros_like(xs)))
            pieces.append(acc)
        y = jnp.concatenate(pieces, axis=1) + zmat * jnp.exp(q["a_exp"]) + q["x"] * dsk_ref[...]
        y_ref[...] = y
        w16 = (q["xd"] * jnp.exp(q["a_last"] - q["a_exp"])).astype(BF16)
        h_sc[...] = h_in * jnp.exp(q["a_last"]) + _dot_tn(q["b16"], w16)
        z = z_ref[...]
        yg = y * (z * _sigmoid(z))
        rr = lax.rsqrt(jnp.mean(yg * yg, axis=-1, keepdims=True) + NORM_EPS)
        yn_ref[...] = (yg * rr * ng_ref[...]).astype(BF16)

    blk = pl.BlockSpec((SSM_CHUNK, 512), lambda g, c: (c, g))
    return _pcall(
        "ssd_fwd", body, grid=(SSM_GROUPS, nc), in_specs=_ssd_specs(nc, lambda c: c),
        out_specs=[blk, blk, pl.BlockSpec((None, None, SSM_STATE, 512), lambda g, c: (g, c, 0, 0))],
        out_shape=[jax.ShapeDtypeStruct((s, SSM_D_INNER), F32), jax.ShapeDtypeStruct((s, SSM_D_INNER), BF16),
                   jax.ShapeDtypeStruct((SSM_GROUPS, nc, SSM_STATE, 512), F32)],
        scratch_shapes=[pltpu.VMEM((SSM_STATE, 512), F32)], sem=("parallel", "arbitrary"),
        args=(act, act, act, proj, dtc, a_log, d_skip, norm_g), ride=ride)


def _ssd_bwd(act, proj, dtc, a_log, d_skip, norm_g, y, h_in_all, dyn, ride=None):
    s = act.shape[0]
    nc = s // SSM_CHUNK

    def body(x_ref, b_ref, c_ref, z_ref, dt_ref, al_ref, dsk_ref, ng_ref, y_ref, hin_ref, dyn_ref,
             dz_ref, dx_ref, db_ref, dc_ref, ddt_ref, st_ref, dal_ref, dh_sc):
        @pl.when(pl.program_id(1) == 0)
        def _():
            dh_sc[...] = jnp.zeros_like(dh_sc)
            st_ref[...] = jnp.zeros_like(st_ref)
            dal_ref[...] = jnp.zeros_like(dal_ref)

        q = _ssd_common(x_ref, b_ref, c_ref, dt_ref, al_ref)
        x, xd, b16, c16 = q["x"], q["xd"], q["b16"], q["c16"]
        z, yv, dyn_v = z_ref[...], y_ref[...], dyn_ref[...]
        sig = _sigmoid(z)
        sil = z * sig
        yg = yv * sil
        rr = lax.rsqrt(jnp.mean(yg * yg, axis=-1, keepdims=True) + NORM_EPS)
        st_ref[0:1, :] += jnp.sum(dyn_v * yg * rr, axis=0, keepdims=True)
        t1 = dyn_v * ng_ref[...]
        dyg = rr * (t1 - yg * (rr * rr) * jnp.mean(t1 * yg, axis=-1, keepdims=True))
        dy = dyg * sil
        dz_ref[...] = (dyg * yv * (sig * (1.0 + z * (1.0 - sig)))).astype(BF16)
        st_ref[1:2, :] += jnp.sum(dy * x, axis=0, keepdims=True)
        dx = dsk_ref[...] * dy
        h_in = hin_ref[...]
        h16 = h_in.astype(BF16)
        ea = jnp.exp(q["a_exp"])
        zmat = _dot(c16, h16)
        dz16 = (dy * ea).astype(BF16)
        da_ch = dy * zmat * ea
        dcm = _dot_nt(dz16, h16)
        dh_in = _dot_tn(c16, dz16)
        dh_out = dh_sc[...]
        dho16 = dh_out.astype(BF16)
        eal = jnp.exp(q["a_last"])
        dh_in += dh_out * eal
        dal_ch = jnp.sum(dh_out * h_in, axis=0, keepdims=True) * eal
        to_end = jnp.exp(q["a_last"] - q["a_exp"])
        wmat = xd * to_end
        dbm = _dot_nt(wmat.astype(BF16), dho16)
        dw = _dot(b16, dho16)
        dxd = dw * to_end
        g_end = dw * wmat
        da_ch -= g_end
        dal_ch += jnp.sum(g_end, axis=0, keepdims=True)
        gmat = _dot_nt(c16, b16)
        gmat_t = _dot_nt(b16, c16)
        triu = _iota((128, 128), 0) <= q["col"]
        xd16, dy16 = xd.astype(BF16), dy.astype(BF16)
        low = q["col"] < 64
        da_col = jnp.zeros((128, 128), F32)
        dg = jnp.zeros((128, 128), F32)
        dg_t = jnp.zeros((128, 128), F32)
        pieces = []
        for p in range(4):
            xs, dys = xd16[:, 128 * p:128 * p + 128], dy16[:, 128 * p:128 * p + 128]
            acc = jnp.zeros((128, 128), F32)
            for half in range(2):
                r = 2 * p + half
                hm = low if half == 0 else jnp.logical_not(low)
                xm = jnp.where(hm, xs, jnp.zeros_like(xs))
                dym = jnp.where(hm, dys, jnp.zeros_like(dys))
                diff = q["a_col"][:, r:r + 1] - q["a_row"][r:r + 1, :]
                decay = jnp.exp(jnp.where(q["tril"], diff, NEG_BIG))
                decay_t = jnp.exp(jnp.where(triu, -diff, NEG_BIG))
                mm, mm_t = gmat * decay, gmat_t * decay_t
                dmm, dmm_t = _dot_nt(dym, xm), _dot_nt(xm, dym)
                acc += _dot(mm_t.astype(BF16), dym)
                dg += dmm * decay
                dg_t += dmm_t * decay_t
                rs = jnp.sum(dmm * mm, axis=-1, keepdims=True) - jnp.sum(dmm_t * mm_t, axis=-1, keepdims=True)
                da_col += jnp.where(q["col"] == r, rs, 0.0)
            pieces.append(acc)
        dxd += jnp.concatenate(pieces, axis=1)
        dcm += _dot(dg.astype(BF16), b16)
        dbm += _dot(dg_t.astype(BF16), c16)
        fold = jnp.where((_iota((512, 128), 0) >> 6) == _iota((512, 128), 1), 1.0, 0.0).astype(BF16)
        da_ch += jnp.where(_iota((128, 1), 0) == 127, dal_ch, 0.0)
        da_col += _dot3_right(da_ch, fold)
        d_dta = _dot3_left(jnp.where(triu, 1.0, 0.0).astype(BF16), da_col)
        ddt_ref[...] = d_dta * q["a_neg"] + _dot3_right(dxd * x, fold)
        dal_ref[0:1, :] += jnp.sum(d_dta * q["dt"] * q["a_neg"], axis=0, keepdims=True)
        dx_ref[...] = dx + dxd * q["dt_exp"]
        db_ref[...] = dbm
        dc_ref[...] = dcm
        dh_sc[...] = dh_in

    rev = lambda c: nc - 1 - c
    blk = pl.BlockSpec((SSM_CHUNK, 512), lambda g, c: (rev(c), g))
    small = pl.BlockSpec((SSM_CHUNK, 128), lambda g, c: (rev(c), g))
    return _pcall(
        "ssd_bwd", body, grid=(SSM_GROUPS, nc), ride=ride,
        args=(act, act, act, proj, dtc, a_log, d_skip, norm_g, y, h_in_all, dyn), sem=("parallel", "arbitrary"),
        in_specs=_ssd_specs(nc, rev) + [blk, pl.BlockSpec((None, None, SSM_STATE, 512), lambda g, c: (g, rev(c), 0, 0)), blk],
        out_specs=[blk, blk, small, small, pl.BlockSpec((None, SSM_CHUNK, 128), lambda g, c: (g, rev(c), 0)),
                   pl.BlockSpec((8, 512), lambda g, c: (0, g)), pl.BlockSpec((None, 8, 128), lambda g, c: (g, 0, 0))],
        out_shape=[jax.ShapeDtypeStruct((s, SSM_IN_PAD), BF16), jax.ShapeDtypeStruct((s, SSM_D_INNER), F32),
                   jax.ShapeDtypeStruct((s, SSM_GROUPS * SSM_STATE), F32), jax.ShapeDtypeStruct((s, SSM_GROUPS * SSM_STATE), F32),
                   jax.ShapeDtypeStruct((SSM_GROUPS, s, 128), F32), jax.ShapeDtypeStruct((8, SSM_D_INNER), F32),
                   jax.ShapeDtypeStruct((SSM_GROUPS, 8, 128), F32)],
        scratch_shapes=[pltpu.VMEM((SSM_STATE, 512), F32)])


MM_TM = 1024
MM_TK = 512
MM_TK_BIG = 2048
FFN_SHARD = FFN_HIDDEN // N_DEV


def _ij(tm, tn):
    return pl.BlockSpec((tm, tn), lambda i, j, k: (i, j))


def _mm_plain(name, a, b, mode, dims, out_dtype, tn=1024, tk=MM_TK, **kw):
    res = _matmul(name, a, b, mode=mode, dims=dims, tiles=(MM_TM, tn, tk), outs=[((dims[0], dims[1]), out_dtype)],
                  epilogue=_epi_plain, **kw)
    return res[0], res[1:]


def _mm_resgate(name, a, b, res, gate, k_dim, tk, ride=None):
    s, d = res.shape
    got = _matmul(name, a, b, mode="nn", dims=(s, d, k_dim), tiles=(MM_TM, 1024, tk),
                  outs=[((s, d), F32), ((s, d), BF16)], epilogue=_epi_resgate, extras=[res, gate],
                  extra_specs=[_ij(MM_TM, 1024), pl.BlockSpec((1, 1024), lambda i, j, k: (0, j))], ride=ride)
    return got[0], got[1], got[2:]


def _ffn_fwd(tag, x_in, gain, scale, shift, gate, w1, w2_of, ride1=None, ride2=None):
    s, d = x_in.shape
    h = _norm_mod_fwd(tag + "_norm2", x_in, gain, scale, shift)
    rr, pre, *got1 = _matmul(tag + "_ffn1", h, w1, mode="nn", dims=(s, FFN_HIDDEN, d), tiles=(MM_TM, FFN_SHARD, MM_TK_BIG),
                             b_spec=pl.BlockSpec((None, MM_TK_BIG, FFN_SHARD), lambda i, j, k: (j, k, 0)),
                             outs=[((s, FFN_HIDDEN), BF16)] * 2, epilogue=_epi_relu2, ride=ride1)
    x_out, f, got2 = _mm_resgate(tag + "_ffn2", rr, w2_of(got1), x_in, gate, FFN_HIDDEN, MM_TK_BIG, ride=ride2)
    return x_out, (h, rr, pre, f), got1, got2


def _ffn_bwd(tag, dx_out, x_in, saved, gain, scale, gate, w1, w2, ride_dx2=None, ride_dx1_of=None):
    s, d = x_in.shape
    h, rr, pre, f = saved
    dy, st_gate = _gate_bwd(tag + "_gate2_bwd", dx_out, f, gate)
    da, *got2 = _matmul(tag + "_ffn2_dx", dy, w2, mode="nt", dims=(s, FFN_HIDDEN, d), tiles=(MM_TM, 1024, MM_TK_BIG),
                        outs=[((s, FFN_HIDDEN), BF16)], epilogue=_epi_drelu2, extras=[pre],
                        extra_specs=[_ij(MM_TM, 1024)], ride=ride_dx2)
    dw2, _ = _mm_plain(tag + "_ffn2_dw", rr, dy, "tn", (FFN_HIDDEN, d, s), BF16, tk=MM_TK_BIG)
    dh, got1 = _mm_plain(tag + "_ffn1_dx", da, w1, "nt", (s, d, FFN_HIDDEN), F32, tk=FFN_SHARD,
                         b_spec=pl.BlockSpec((None, 1024, FFN_SHARD), lambda i, j, k: (k, j, 0)),
                         ride=None if ride_dx1_of is None else ride_dx1_of(dw2))
    dw1 = _matmul(tag + "_ffn1_dw", h, da, mode="tn", dims=(d, FFN_HIDDEN, s), tiles=(MM_TM, FFN_SHARD, MM_TK_BIG),
                  outs=[((N_DEV, d, FFN_SHARD), BF16)], epilogue=_epi_plain,
                  out_specs=[pl.BlockSpec((None, MM_TM, FFN_SHARD), lambda i, j, k: (j, i, 0))])[0]
    dx_in, st_norm = _norm_mod_bwd(tag + "_norm2_bwd", x_in, gain, scale, dh, dx_out)
    return dx_in, dw1, dw2, st_gate[0], st_norm, got2, got1


def _full_weight(name, gathered):
    if name in ASSEMBLED:
        return _assemble_cols("assemble_" + name, gathered, ASSEMBLED[name])
    if name == "ffn_w1":
        return gathered
    return gathered.reshape(-1, gathered.shape[2])


def _grad_pieces(name, full, shard_shape):
    if name in ASSEMBLED:
        return _split_cols("split_" + name, full, shard_shape[1])
    return full.reshape(N_DEV, *shard_shape)


def _device_step(x, target, mod, sm, shards, w_even_in):
    s, d = x.shape
    mv = [[mod[i, k].reshape(1, d) for k in range(6)] for i in range(2)]
    nm = [sm["norm_mix"][i].reshape(1, d) for i in range(2)]
    nf = [sm["norm_ffn"][i].reshape(1, d) for i in range(2)]
    pool_w, pool_scale = sm["pool_w"].reshape(4, 128, 128), sm["pool_scale"].reshape(1, POOL_WIDTH)
    got = {}

    def gather(*items):
        return [shards[it] for it in items], False

    def scatter(*pieces):
        return list(pieces), True

    def pieces(item, full):
        return _grad_pieces(item[0], full, shards[item].shape)

    sh1, sc1, g1, sh2, sc2, g2 = mv[0]
    h1 = _norm_mod_fwd("l0_norm1", x, nm[0], sc1, sh1)
    slab = pl.BlockSpec((None, MM_TM, 512), lambda i, j, k: (j, i, 0))
    proj0, g_eout, g_sout = _matmul(
        "l0_in", h1, w_even_in, mode="nn", dims=(s, EVEN_IN_WIDTH, d), tiles=(MM_TM, 512, MM_TK_BIG),
        outs=[((EVEN_IN_WIDTH // 512, s, 512), BF16)], out_specs=[slab], epilogue=_epi_plain,
        ride=gather(("even_w_out", 0), ("ssm_w_out", 0)))
    w_even_out, w_ssm_out = _full_weight("even_w_out", g_eout), _full_weight("ssm_w_out", g_sout)
    o, lse = [None] * 3, [None] * 3
    o[0], lse[0], (g_w1_0,) = _attn_fwd(proj0, 0, ride=gather(("ffn_w1", 0)))
    o[1], lse[1], _ = _attn_fwd(proj0, 1)
    o[2], lse[2], _ = _attn_fwd(proj0, 2)
    w1_0 = _full_weight("ffn_w1", g_w1_0)
    cat = _merge_pool_fwd(o, lse, proj0, pool_w, pool_scale)
    x1, y0, _ = _mm_resgate("l0_out", cat, w_even_out, x, g1, EVEN_OUT_WIDTH, EVEN_OUT_WIDTH)
    x2, ffn0, g_w2_0, (g_sin,) = _ffn_fwd(
        "l0", x1, nf[0], sc2, sh2, g2, w1_0, lambda arrived: _full_weight("ffn_w2", arrived[0]),
        ride1=gather(("ffn_w2", 0)), ride2=gather(("ssm_w_in", 0)))
    w2_0, w_ssm_in = _full_weight("ffn_w2", g_w2_0[0]), _full_weight("ssm_w_in", g_sin)

    th1, tc1, tg1, th2, tc2, tg2 = mv[1]
    h3 = _norm_mod_fwd("l1_norm1", x2, nm[1], tc1, th1)
    proj1, (g_w1_1,) = _mm_plain("l1_in", h3, w_ssm_in, "nn", (s, SSM_IN_PAD, d), F32, tn=1152, tk=MM_TK_BIG,
                                 ride=gather(("ffn_w1", 1)))
    w1_1 = _full_weight("ffn_w1", g_w1_1)
    conv_w = sm["ssm_conv_w"].reshape(SSM_CONV, SSM_CONV_DIM)
    pre, act = _conv_fwd(proj1, conv_w, sm["ssm_conv_b"].reshape(1, SSM_CONV_DIM))
    dt_bias = jnp.pad(sm["ssm_dt_bias"].reshape(1, SSM_HEADS), ((0, 0), (0, 128 - SSM_HEADS)))
    dt_full = _dt_fwd(proj1, dt_bias)
    dtc = jnp.pad(dt_full[:, :SSM_HEADS].reshape(s, SSM_GROUPS, 8).transpose(1, 0, 2), ((0, 0), (0, 0), (0, 120)))
    a_log = jnp.pad(sm["ssm_a_log"].reshape(SSM_GROUPS, 1, 8), ((0, 0), (0, 0), (0, 120)))
    d_skip = jnp.repeat(sm["ssm_d"].reshape(SSM_HEADS), SSM_D_INNER // SSM_HEADS).reshape(1, SSM_D_INNER)
    norm_g = sm["ssm_norm"].reshape(1, SSM_D_INNER)
    (y, yn, h_in_all), (g_w2_1,) = _ssd_fwd(act, proj1, dtc, a_log, d_skip, norm_g, ride=gather(("ffn_w2", 1)))
    w2_1 = _full_weight("ffn_w2", g_w2_1)
    x3, y1, _ = _mm_resgate("l1_out", yn, w_ssm_out, x2, tg1, SSM_D_INNER, MM_TK_BIG)
    x4, ffn1, _, _ = _ffn_fwd("l1", x3, nf[1], tc2, th2, tg2, w1_1, lambda arrived: w2_1)

    dx4, st_loss = _loss_head("loss_head", x4, sm["final_norm"].reshape(1, d), target)
    loss = jnp.sum(st_loss[1])

    dx3, dw1_1, dw2_1, dg2_1, st_n2_1, _, (got[("ffn_w2", 1)],) = _ffn_bwd(
        "l1", dx4, x3, ffn1, nf[1], tc2, tg2, w1_1, w2_1,
        ride_dx1_of=lambda dw2: scatter(pieces(("ffn_w2", 1), dw2)))
    dy1, st_g1_1 = _gate_bwd("l1_gate1_bwd", dx3, y1, tg1)
    dyn, _ = _mm_plain("l1_out_dx", dy1, w_ssm_out, "nt", (s, SSM_D_INNER, d), F32, tk=MM_TK_BIG)
    dw_sout, _ = _mm_plain("l1_out_dw", yn, dy1, "tn", (SSM_D_INNER, d, s), BF16, tk=MM_TK_BIG)
    (dproj1, dxs, dbm, dcm, ddt, st_ssd, d_alog), (got[("ffn_w1", 1)], got[("ssm_w_out", 0)]) = _ssd_bwd(
        act, proj1, dtc, a_log, d_skip, norm_g, y, h_in_all, dyn,
        ride=scatter(dw1_1, pieces(("ssm_w_out", 0), dw_sout)))
    dproj1, st_cx = _conv_bwd("conv_bwd_x", dxs, pre, proj1, conv_w, dproj1, block0=0)
    dproj1, st_cb = _conv_bwd("conv_bwd_b", dbm, pre, proj1, conv_w, dproj1, block0=SSM_D_INNER // 512)
    dproj1, st_cc = _conv_bwd("conv_bwd_c", dcm, pre, proj1, conv_w, dproj1, block0=SSM_D_INNER // 512 + 2)
    ddt_rows = jnp.pad(ddt[:, :, :8].transpose(1, 0, 2).reshape(s, SSM_HEADS), ((0, 0), (0, 128 - SSM_HEADS)))
    dproj1, st_dt = _dt_bwd(ddt_rows, proj1, dt_bias, dproj1)
    dh3, _ = _mm_plain("l1_in_dx", dproj1, w_ssm_in, "nt", (s, d, SSM_IN_PAD), F32, tk=SSM_IN_PAD // 3)
    dw_sin, _ = _mm_plain("l1_in_dw", h3, dproj1, "tn", (d, SSM_IN_PAD, s), BF16, tn=1152, tk=MM_TK_BIG)
    dx2, st_n1_1 = _norm_mod_bwd("l1_norm1_bwd", x2, nm[1], tc1, dh3, dx3)

    dx1, dw1_0, dw2_0, dg2_0, st_n2_0, (got[("ssm_w_in", 0)],), (got[("ffn_w2", 0)],) = _ffn_bwd(
        "l0", dx2, x1, ffn0, nf[0], sc2, g2, w1_0, w2_0, ride_dx2=scatter(pieces(("ssm_w_in", 0), dw_sin)),
        ride_dx1_of=lambda dw2: scatter(pieces(("ffn_w2", 0), dw2)))
    dy0, st_g1_0 = _gate_bwd("l0_gate1_bwd", dx1, y0, g1)
    dcat, _ = _mm_plain("l0_out_dx", dy0, w_even_out, "nt", (s, EVEN_OUT_WIDTH, d), F32, tk=MM_TK_BIG)
    dw_eout, _ = _mm_plain("l0_out_dw", cat, dy0, "tn", (EVEN_OUT_WIDTH, d, s), BF16, tk=MM_TK_BIG)
    lane = jnp.arange(512) // 64
    head_sum = (lane[:, None] == lane[None, :]).astype(BF16)
    *do_cc, du, d_pool_w, st_pool = _merge_pool_bwd(dcat, o, lse, proj0, pool_w, pool_scale, head_sum)
    do, cc = do_cc[:3], do_cc[3:]
    dqkv = [None] * 3
    dqkv[0], (got[("ffn_w1", 0)],) = _attn_bwd(proj0, 0, do[0], lse[0], cc[0], ride=scatter(dw1_0))
    dqkv[1], (got[("even_w_out", 0)],) = _attn_bwd(proj0, 1, do[1], lse[1], cc[1],
                                                   ride=scatter(pieces(("even_w_out", 0), dw_eout)))
    dqkv[2], _ = _attn_bwd(proj0, 2, do[2], lse[2], cc[2])
    dproj0 = jnp.stack([dqkv[g][kind] for kind in range(3) for g in range(3)] + [du])
    dh1, _ = _mm_plain("l0_in_dx", dproj0, w_even_in, "nt", (s, d, EVEN_IN_WIDTH), F32,
                       a_spec=pl.BlockSpec((None, MM_TM, 512), lambda i, j, k: (k, i, 0)))
    dw_ein, _ = _mm_plain("l0_in_dw", h1, dproj0, "tn", (d, EVEN_IN_WIDTH, s), BF16, tn=512, tk=MM_TK_BIG,
                          b_spec=pl.BlockSpec((None, MM_TK_BIG, 512), lambda i, j, k: (j, k, 0)))
    grad_x, st_n1_0 = _norm_mod_bwd("l0_norm1_bwd", x, nm[0], sc1, dh1, dx1)

    dmod = jnp.stack([
        jnp.stack([st_n1_0[0], st_n1_0[1], st_g1_0[0], st_n2_0[0], st_n2_0[1], dg2_0]),
        jnp.stack([st_n1_1[0], st_n1_1[1], st_g1_1[0], st_n2_1[0], st_n2_1[1], dg2_1])])
    st_conv = jnp.concatenate([st_cx, st_cb, st_cc], axis=1)
    small = dict(
        norm_mix=jnp.stack([st_n1_0[2], st_n1_1[2]]), norm_ffn=jnp.stack([st_n2_0[2], st_n2_1[2]]),
        pool_w=d_pool_w, pool_scale=st_pool[0], ssm_conv_w=st_conv[:SSM_CONV], ssm_conv_b=st_conv[SSM_CONV],
        ssm_dt_bias=st_dt[0, :SSM_HEADS], ssm_a_log=d_alog[:, 0, :8].reshape(SSM_HEADS),
        ssm_d=jnp.sum(st_ssd[1].reshape(SSM_HEADS, SSM_D_INNER // SSM_HEADS), axis=-1), ssm_norm=st_ssd[0],
        final_norm=st_loss[0])
    return loss, grad_x, got, pieces(("even_w_in", 0), dw_ein), dmod, small


WEIGHT_ORDER = ("ada_w", "ada_b", "norm_mix", "norm_ffn", "ffn_w1", "ffn_w2", "even_w_in", "pool_w", "pool_scale",
                "even_w_out", "ssm_w_in", "ssm_conv_w", "ssm_conv_b", "ssm_dt_bias", "ssm_a_log", "ssm_d", "ssm_norm",
                "ssm_w_out", "final_norm")
BIG_LAYERS = ((("even_w_in", 0), ("even_w_out", 0), ("ffn_w1", 0), ("ffn_w2", 0)),
              (("ssm_w_in", 0), ("ssm_w_out", 0), ("ffn_w1", 1), ("ffn_w2", 1)))
STACKED = ("ffn_w1", "ffn_w2")
ASSEMBLED = {"even_w_in": EVEN_IN_WIDTH, "even_w_out": D_MODEL, "ssm_w_in": SSM_IN_PAD}
SMALL_REPLICATED = ("norm_mix", "norm_ffn", "pool_w", "pool_scale", "ssm_dt_bias", "ssm_a_log", "ssm_d", "final_norm")
SMALL_SHARDED = ("ssm_conv_w", "ssm_conv_b", "ssm_norm")


def _pack(flat_parts, width, lead=()):
    flat = jnp.concatenate(flat_parts, axis=-1)
    n = flat.shape[-1]
    rows = -(-n // (8 * width)) * 8
    flat = jnp.pad(flat, [(0, 0)] * len(lead) + [(0, rows * width - n)])
    return flat.reshape(*lead, rows, width)


def _unpack(packed, shapes, lead=()):
    flat = packed.reshape(*lead, -1)
    out, off = [], 0
    for shp in shapes:
        n = math.prod(shp)
        out.append(flat[..., off:off + n].reshape(*lead, *shp))
        off += n
    return out


def kernel(x, c, ada_w, ada_b, norm_mix, norm_ffn, ffn_w1, ffn_w2, even_w_in, pool_w, pool_scale, even_w_out, ssm_w_in, ssm_conv_w, ssm_conv_b, ssm_dt_bias, ssm_a_log, ssm_d, ssm_norm, ssm_w_out, final_norm, loss_target, m_ada_w, m_ada_b, m_norm_mix, m_norm_ffn, m_ffn_w1, m_ffn_w2, m_even_w_in, m_pool_w, m_pool_scale, m_even_w_out, m_ssm_w_in, m_ssm_conv_w, m_ssm_conv_b, m_ssm_dt_bias, m_ssm_a_log, m_ssm_d, m_ssm_norm, m_ssm_w_out, m_final_norm, v_ada_w, v_ada_b, v_norm_mix, v_norm_ffn, v_ffn_w1, v_ffn_w2, v_even_w_in, v_pool_w, v_pool_scale, v_even_w_out, v_ssm_w_in, v_ssm_conv_w, v_ssm_conv_b, v_ssm_dt_bias, v_ssm_a_log, v_ssm_d, v_ssm_norm, v_ssm_w_out, v_final_norm):
    w = dict(ada_w=ada_w, ada_b=ada_b, norm_mix=norm_mix, norm_ffn=norm_ffn, ffn_w1=ffn_w1, ffn_w2=ffn_w2,
             even_w_in=even_w_in, pool_w=pool_w, pool_scale=pool_scale, even_w_out=even_w_out, ssm_w_in=ssm_w_in,
             ssm_conv_w=ssm_conv_w, ssm_conv_b=ssm_conv_b, ssm_dt_bias=ssm_dt_bias, ssm_a_log=ssm_a_log, ssm_d=ssm_d,
             ssm_norm=ssm_norm, ssm_w_out=ssm_w_out, final_norm=final_norm)
    m = dict(ada_w=m_ada_w, ada_b=m_ada_b, norm_mix=m_norm_mix, norm_ffn=m_norm_ffn, ffn_w1=m_ffn_w1, ffn_w2=m_ffn_w2,
             even_w_in=m_even_w_in, pool_w=m_pool_w, pool_scale=m_pool_scale, even_w_out=m_even_w_out,
             ssm_w_in=m_ssm_w_in, ssm_conv_w=m_ssm_conv_w, ssm_conv_b=m_ssm_conv_b, ssm_dt_bias=m_ssm_dt_bias,
             ssm_a_log=m_ssm_a_log, ssm_d=m_ssm_d, ssm_norm=m_ssm_norm, ssm_w_out=m_ssm_w_out, final_norm=m_final_norm)
    v = dict(ada_w=v_ada_w, ada_b=v_ada_b, norm_mix=v_norm_mix, norm_ffn=v_norm_ffn, ffn_w1=v_ffn_w1, ffn_w2=v_ffn_w2,
             even_w_in=v_even_w_in, pool_w=v_pool_w, pool_scale=v_pool_scale, even_w_out=v_even_w_out,
             ssm_w_in=v_ssm_w_in, ssm_conv_w=v_ssm_conv_w, ssm_conv_b=v_ssm_conv_b, ssm_dt_bias=v_ssm_dt_bias,
             ssm_a_log=v_ssm_a_log, ssm_d=v_ssm_d, ssm_norm=v_ssm_norm, ssm_w_out=v_ssm_w_out, final_norm=v_final_norm)
    d = D_MODEL
    me = _my_index()

    sharded_shapes = [(SSM_CONV, SSM_CONV_DIM // N_DEV), (SSM_CONV_DIM // N_DEV,), (SSM_D_INNER // N_DEV,)]
    shards = {(name, idx): w[name][idx].astype(BF16) for layer in BIG_LAYERS for name, idx in layer}
    small_in = _pack([c.reshape(-1)] + [w[k].reshape(-1) for k in SMALL_SHARDED], 128)
    got, got_even_in = _exchange("gather_first", [small_in, shards[("even_w_in", 0)]], scatter=False)
    w_even_in = _full_weight("even_w_in", got_even_in)
    c_all, conv_w_sh, conv_b_sh, norm_sh = _unpack(got, [(d,)] + sharded_shapes, lead=(N_DEV,))
    sm = {k: w[k] for k in SMALL_REPLICATED}
    sm["ssm_conv_w"] = conv_w_sh.transpose(1, 0, 2).reshape(SSM_CONV, SSM_CONV_DIM)
    sm["ssm_conv_b"] = conv_b_sh.reshape(SSM_CONV_DIM)
    sm["ssm_norm"] = norm_sh.reshape(SSM_D_INNER)

    ada_cols = 6 * d // N_DEV
    c_pad = jnp.pad(c_all, ((0, 16 - N_DEV), (0, 0)))
    mod_cols = _matmul(
        "ada_fwd", c_pad, ada_w, mode="nn", dims=(16, 2 * ada_cols, d), tiles=(16, ada_cols // 2, d), a_fn=_silu,
        b_spec=pl.BlockSpec((None, d, ada_cols // 2), lambda i, j, k: (j // 2, k, j % 2)),
        outs=[((16, 2 * ada_cols), F32)], epilogue=_epi_plain)[0]
    mod_got = _all_to_all("ada_exchange", mod_cols[:N_DEV].reshape(N_DEV, 2, ada_cols))
    mod = (mod_got.transpose(1, 0, 2).reshape(2, 6 * d) + ada_b).reshape(2, 6, d)

    loss, grad_x, grad_got, even_in_pieces, dmod, small = _device_step(x[0], loss_target[0], mod, sm, shards, w_even_in)
    loss = lax.psum(loss, ("x", "y", "c"))

    grads, delta, new_m, new_v = {}, {}, {}, {}

    def update(name, parts, shape=None):
        rows, cols = parts.shape[1:]
        res = _adamw("adamw_" + name, parts, w[name].reshape(rows, cols), m[name].reshape(rows, cols), v[name].reshape(rows, cols))
        return [r.reshape(w[name].shape if shape is None else shape) for r in res]

    rep_shapes = [w[k].shape for k in SMALL_REPLICATED]
    pack_rep = lambda tree: _pack([tree[k].reshape(-1) for k in SMALL_REPLICATED], 128)
    dmod_all, rep_got = _exchange("gather_small_grads", [dmod.reshape(2, 6 * d), pack_rep(small)], scatter=False)

    my_cols = lax.dynamic_slice_in_dim(dmod_all, me * ada_cols, ada_cols, axis=2).reshape(N_DEV, 2 * ada_cols)
    g_ada_w = _matmul(
        "ada_dw", c_pad, jnp.pad(my_cols, ((0, 16 - N_DEV), (0, 0))), mode="tn", dims=(d, 2 * ada_cols, 16),
        tiles=(1024, ada_cols // 2, 16), a_fn=_silu, outs=[((2, d, ada_cols), F32)],
        out_specs=[pl.BlockSpec((None, 1024, ada_cols // 2), lambda i, j, k: (j // 2, i, j % 2))], epilogue=_epi_plain)[0]
    grads["ada_w"], delta["ada_w"], new_m["ada_w"], new_v["ada_w"] = update("ada_w", g_ada_w.reshape(1, 2 * d, ada_cols))
    grads["ada_b"], delta["ada_b"], new_m["ada_b"], new_v["ada_b"] = update("ada_b", dmod_all)

    sh_pieces = [small["ssm_conv_w"].reshape(SSM_CONV, N_DEV, -1).transpose(1, 0, 2).reshape(N_DEV, -1),
                 small["ssm_conv_b"].reshape(N_DEV, -1), small["ssm_norm"].reshape(N_DEV, -1)]
    grad_got[("even_w_in", 0)], sh_got = _exchange(
        "exchange_last", [even_in_pieces, _pack(sh_pieces, 128, lead=(N_DEV,))], scatter=True)

    stacked = {name: [None, None] for name in STACKED}
    for layer in BIG_LAYERS:
        for name, idx in layer:
            res = _adamw(f"adamw_{name}_{idx}", grad_got[(name, idx)], w[name][idx], m[name][idx], v[name][idx])
            if name in stacked:
                stacked[name][idx] = res
            else:
                grads[name], delta[name], new_m[name], new_v[name] = [r[None] for r in res]
    for name, per_layer in stacked.items():
        grads[name], delta[name], new_m[name], new_v[name] = [jnp.stack([per_layer[0][q], per_layer[1][q]]) for q in range(4)]

    rep_res = _adamw("adamw_small_replicated", rep_got, pack_rep(w), pack_rep(m), pack_rep(v))
    for dst, packed in zip((grads, delta, new_m, new_v), rep_res):
        for k, val in zip(SMALL_REPLICATED, _unpack(packed, rep_shapes)):
            dst[k] = val
    pack_sh = lambda tree: _pack([tree[k].reshape(-1) for k in SMALL_SHARDED], 128)
    sh_res = _adamw("adamw_small_sharded", sh_got, pack_sh(w), pack_sh(m), pack_sh(v))
    for dst, packed in zip((grads, delta, new_m, new_v), sh_res):
        for k, val in zip(SMALL_SHARDED, _unpack(packed, [w[k].shape for k in SMALL_SHARDED])):
            dst[k] = val

    out = [loss, grad_x[None]]
    for tree in (grads, delta, new_m, new_v):
        out.extend(tree[k] for k in WEIGHT_ORDER)
    return tuple(out)
```

```python
import functools
import math

import jax
import jax.numpy as jnp
from jax import lax
from jax.experimental import pallas as pl
from jax.experimental.pallas import tpu as pltpu

F32 = jnp.float32
BF16 = jnp.bfloat16

N_DEV = 8
D_MODEL = 2048
NORM_EPS = 1e-6
ATTN_BLOCK = 128
ATTN_GROUPS = 3
ATTN_GROUP_WIDTH = 512
ATTN_QKV_WIDTH = 3 * ATTN_GROUPS * ATTN_GROUP_WIDTH
POOL_GROUPS = 4
POOL_GROUP_WIDTH = 128
POOL_WIDTH = 512
POOL_HALO = 16
EVEN_IN_WIDTH = ATTN_QKV_WIDTH + POOL_WIDTH
EVEN_OUT_WIDTH = 1024
SSM_D_INNER = 4096
SSM_HEADS = 64
SSM_GROUPS = 8
SSM_GROUP_WIDTH = 512
SSM_STATE = 128
SSM_CHUNK = 128
SSM_CONV = 4
SSM_CONV_DIM = 6144
SSM_IN_WIDTH = 10304
SSM_IN_PAD = 10368
FFN_HIDDEN = 8192

ADAM_LR = 0.001
ADAM_B1 = 0.9
ADAM_B2 = 0.999
ADAM_EPS = 1e-08
ADAM_WD = 0.01
ADAM_STEP = 10

VMEM_LIMIT_BYTES = 56 * 1024 * 1024
NEG_BIG = -1e30

MESH_ID = pl.DeviceIdType.MESH
ANY = pl.BlockSpec(memory_space=pl.ANY)


def _cparams(*sem):
    return pltpu.CompilerParams(dimension_semantics=tuple(sem) if sem else None, vmem_limit_bytes=VMEM_LIMIT_BYTES)


def _dot(a, b):
    return lax.dot_general(a, b, (((1,), (0,)), ((), ())), preferred_element_type=F32)


def _dot_nt(a, b):
    return lax.dot_general(a, b, (((1,), (1,)), ((), ())), preferred_element_type=F32)


def _dot_tn(a, b):
    return lax.dot_general(a, b, (((0,), (0,)), ((), ())), preferred_element_type=F32)


def _split3(v):
    hi = v.astype(BF16)
    r1 = v - hi.astype(F32)
    mid = r1.astype(BF16)
    lo = (r1 - mid.astype(F32)).astype(BF16)
    return hi, mid, lo


def _dot3_left(const_bf16, v):
    hi, mid, lo = _split3(v)
    return _dot(const_bf16, hi) + _dot(const_bf16, mid) + _dot(const_bf16, lo)


def _dot3_right(v, const_bf16):
    hi, mid, lo = _split3(v)
    return _dot(hi, const_bf16) + _dot(mid, const_bf16) + _dot(lo, const_bf16)


def _iota(shape, dim):
    return lax.broadcasted_iota(jnp.int32, shape, dim)


def _sigmoid(x):
    return 1.0 / (1.0 + jnp.exp(-x))


def _peer(k):
    x, y, c = lax.axis_index("x"), lax.axis_index("y"), lax.axis_index("c")
    px = 1 - x if k & 4 else x
    py = 1 - y if k & 2 else y
    pc = 1 - c if k & 1 else c
    return (px, py, pc), 4 * px + 2 * py + pc


def _my_index():
    return 4 * lax.axis_index("x") + 2 * lax.axis_index("y") + lax.axis_index("c")


def _exchange(name, arrays, *, scatter):
    n = len(arrays)

    def body(*refs):
        ex = _Exchange(refs[:n], refs[n:2 * n], *refs[2 * n:], scatter)
        ex.start()
        ex.wait()

    return pl.pallas_call(
        body, name=name, out_shape=_exchange_out_shapes(arrays, scatter), in_specs=[ANY] * n, out_specs=[ANY] * n,
        scratch_shapes=_exchange_sems(n),
    )(*arrays)


def _exchange_out_shapes(arrays, scatter):
    return [jax.ShapeDtypeStruct((N_DEV,) + (tuple(a.shape[1:]) if scatter else tuple(a.shape)), a.dtype) for a in arrays]


def _exchange_sems(n):
    return [pltpu.SemaphoreType.DMA((n * (N_DEV - 1),)), pltpu.SemaphoreType.DMA((n * (N_DEV - 1),)),
            pltpu.SemaphoreType.DMA((n,))]


class _Exchange:
    def __init__(self, x_refs, out_refs, send_sems, recv_sems, local_sems, scatter):
        self.x_refs, self.out_refs, self.scatter = x_refs, out_refs, scatter
        self.send_sems, self.recv_sems, self.local_sems = send_sems, recv_sems, local_sems

    def _src(self, a, idx):
        return self.x_refs[a].at[idx] if self.scatter else self.x_refs[a]

    def _local(self, a):
        me = _my_index()
        return pltpu.make_async_copy(self._src(a, me), self.out_refs[a].at[me], self.local_sems.at[a])

    def _remote(self, a, k, landing):
        peer, peer_idx = _peer(k)
        sem = a * (N_DEV - 1) + k - 1
        slot = peer_idx if landing else _my_index()
        return pltpu.make_async_remote_copy(
            src_ref=self._src(a, peer_idx), dst_ref=self.out_refs[a].at[slot], send_sem=self.send_sems.at[sem],
            recv_sem=self.recv_sems.at[sem], device_id=peer, device_id_type=MESH_ID)

    def start(self):
        n = len(self.x_refs)
        for a in range(n):
            self._local(a).start()
        for k in range(1, N_DEV):
            for a in range(n):
                self._remote(a, k, False).start()

    def wait(self):
        n = len(self.x_refs)
        for k in range(1, N_DEV):
            for a in range(n):
                self._remote(a, k, True).wait_recv()
        for k in range(1, N_DEV):
            for a in range(n):
                self._remote(a, k, False).wait_send()
        for a in range(n):
            self._local(a).wait()


def _pcall(name, body, *, grid, in_specs, out_specs, out_shape, scratch_shapes, sem, args, ride=None, aliases=None):
    aliases = aliases or {}
    if ride is None:
        res = pl.pallas_call(body, name=name, grid=grid, in_specs=in_specs, out_specs=out_specs, out_shape=out_shape,
                             scratch_shapes=scratch_shapes, input_output_aliases=aliases, compiler_params=_cparams(*sem))(*args)
        return list(res), []
    arrays, scatter = ride
    n_in, n_out, n_scr, n_ride = len(in_specs), len(out_specs), len(scratch_shapes), len(arrays)

    def wrapped(*refs):
        ins, refs = refs[:n_in], refs[n_in:]
        ride_in, refs = refs[:n_ride], refs[n_ride:]
        outs, refs = refs[:n_out], refs[n_out:]
        ride_out, refs = refs[:n_ride], refs[n_ride:]
        scr, sems = refs[:n_scr], refs[n_scr:]
        exchange = _Exchange(ride_in, ride_out, *sems, scatter)
        ids = [pl.program_id(axis) for axis in range(len(grid))]
        first, last = ids[0] == 0, ids[0] == grid[0] - 1
        for axis in range(1, len(grid)):
            first, last = first & (ids[axis] == 0), last & (ids[axis] == grid[axis] - 1)

        @pl.when(first)
        def _():
            exchange.start()

        body(*ins, *outs, *scr)

        @pl.when(last)
        def _():
            exchange.wait()

    res = pl.pallas_call(
        wrapped, name=name, grid=grid, in_specs=list(in_specs) + [ANY] * n_ride, out_specs=list(out_specs) + [ANY] * n_ride,
        out_shape=list(out_shape) + _exchange_out_shapes(arrays, scatter),
        scratch_shapes=list(scratch_shapes) + _exchange_sems(n_ride), input_output_aliases=aliases,
        compiler_params=_cparams(*(("arbitrary",) * len(grid))))(*args, *arrays)
    return list(res[:n_out]), list(res[n_out:])


def _all_gather(name, x):
    return _exchange(name, [x], scatter=False)[0]


def _all_to_all(name, x):
    return _exchange(name, [x], scatter=True)[0]


def _assemble_cols(name, shards, width):
    _, k_dim, ns = shards.shape
    tr = 256

    def body(s_ref, o_ref):
        for dev in range(N_DEV):
            o_ref[:, ns * dev:ns * (dev + 1)] = s_ref[dev]
        if width > N_DEV * ns:
            o_ref[:, N_DEV * ns:] = jnp.zeros((tr, width - N_DEV * ns), o_ref.dtype)

    return pl.pallas_call(
        body, name=name, grid=(k_dim // tr,), in_specs=[pl.BlockSpec((N_DEV, tr, ns), lambda i: (0, i, 0))],
        out_specs=pl.BlockSpec((tr, width), lambda i: (i, 0)), out_shape=jax.ShapeDtypeStruct((k_dim, width), shards.dtype),
        compiler_params=_cparams("parallel"),
    )(shards)


def _split_cols(name, full, ns, row0=0, n_rows=None):
    k_dim, width = full.shape
    n_rows = k_dim if n_rows is None else n_rows
    tr = 256
    first = row0 // tr

    def body(f_ref, o_ref):
        for dev in range(N_DEV):
            o_ref[dev] = f_ref[:, ns * dev:ns * (dev + 1)]

    return pl.pallas_call(
        body, name=name, grid=(n_rows // tr,), in_specs=[pl.BlockSpec((tr, width), lambda i: (first + i, 0))],
        out_specs=pl.BlockSpec((N_DEV, tr, ns), lambda i: (0, i, 0)),
        out_shape=jax.ShapeDtypeStruct((N_DEV, n_rows, ns), full.dtype), compiler_params=_cparams("parallel"),
    )(full)


_DIMS = {"nn": (((1,), (0,)), ((), ())), "nt": (((1,), (1,)), ((), ())), "tn": (((0,), (0,)), ((), ()))}


def _matmul(name, a, b, *, mode, dims, tiles, outs, epilogue, a_spec=None, b_spec=None, out_specs=None,
            extras=(), extra_specs=(), a_fn=None, ride=None):
    m_dim, n_dim, k_dim = dims
    tm, tn, tk = tiles
    assert m_dim % tm == 0 and n_dim % tn == 0 and k_dim % tk == 0, (name, dims, tiles)
    grid = (m_dim // tm, n_dim // tn, k_dim // tk)
    nk = grid[2]
    if a_spec is None:
        a_spec = pl.BlockSpec((tk, tm), lambda i, j, k: (k, i)) if mode == "tn" else pl.BlockSpec((tm, tk), lambda i, j, k: (i, k))
    if b_spec is None:
        b_spec = pl.BlockSpec((tn, tk), lambda i, j, k: (j, k)) if mode == "nt" else pl.BlockSpec((tk, tn), lambda i, j, k: (k, j))
    if out_specs is None:
        out_specs = [pl.BlockSpec((tm, tn), lambda i, j, k: (i, j)) for _ in outs]
    n_ex, n_out = len(extras), len(outs)
    ride_arrays, scatter = ride if ride is not None else ((), False)
    n_ride = len(ride_arrays)
    dn = _DIMS[mode]

    def body(a_ref, b_ref, *rest):
        ex_refs, rest = rest[:n_ex], rest[n_ex:]
        ride_in, rest = rest[:n_ride], rest[n_ride:]
        out_refs, rest = rest[:n_out], rest[n_out:]
        ride_out, rest = rest[:n_ride], rest[n_ride:]
        i, j, k = pl.program_id(0), pl.program_id(1), pl.program_id(2)
        if n_ride:
            exchange = _Exchange(ride_in, ride_out, *rest[-3:], scatter)

            @pl.when((i == 0) & (j == 0) & (k == 0))
            def _():
                exchange.start()

        at = a_ref[...]
        if a_fn is not None:
            at = a_fn(at)
        part = lax.dot_general(at.astype(BF16), b_ref[...].astype(BF16), dn, preferred_element_type=F32)

        def finish(total):
            res = epilogue(total, *[e[...] for e in ex_refs])
            for r, o in zip(res, out_refs):
                o[...] = r.astype(o.dtype)

        if nk == 1:
            finish(part)
        else:
            acc = rest[0]

            @pl.when(k == 0)
            def _():
                acc[...] = part

            @pl.when(k > 0)
            def _():
                acc[...] += part

            @pl.when(k == nk - 1)
            def _():
                finish(acc[...])

        if n_ride:
            @pl.when((i == grid[0] - 1) & (j == grid[1] - 1) & (k == nk - 1))
            def _():
                exchange.wait()

    scratch = ([pltpu.VMEM((tm, tn), F32)] if nk > 1 else []) + (_exchange_sems(n_ride) if n_ride else [])
    sem = ("arbitrary",) * 3 if n_ride else ("parallel", "parallel", "arbitrary")
    return pl.pallas_call(
        body, name=name, grid=grid,
        in_specs=[a_spec, b_spec, *extra_specs] + [ANY] * n_ride, out_specs=list(out_specs) + [ANY] * n_ride,
        out_shape=[jax.ShapeDtypeStruct(s, d) for s, d in outs] + (_exchange_out_shapes(ride_arrays, scatter) if n_ride else []),
        scratch_shapes=scratch, compiler_params=_cparams(*sem),
    )(a, b, *extras, *ride_arrays)


def _epi_plain(acc):
    return (acc,)


def _epi_relu2(acc):
    return jnp.square(jnp.maximum(acc, 0.0)), acc


def _epi_resgate(acc, res, gate):
    return res + gate * acc, acc


def _epi_drelu2(acc, pre):
    return (acc * (2.0 * jnp.maximum(pre.astype(F32), 0.0)),)


def _silu(v):
    return v * _sigmoid(v)


ROW_TILE = 256


def _row_spec(width, tr=ROW_TILE):
    return pl.BlockSpec((tr, width), lambda i: (i, 0))


def _vec_spec(width):
    return pl.BlockSpec((1, width), lambda i: (0, 0))


def _stat_spec(width):
    return pl.BlockSpec((8, width), lambda i: (0, 0))


def _norm_mod_fwd(name, x, gain, scale, shift):
    s, d = x.shape

    def body(x_ref, g_ref, sc_ref, sh_ref, h_ref):
        xv = x_ref[...]
        r = lax.rsqrt(jnp.mean(xv * xv, axis=-1, keepdims=True) + NORM_EPS)
        h_ref[...] = ((xv * r * g_ref[...]) * (1.0 + sc_ref[...]) + sh_ref[...]).astype(BF16)

    return pl.pallas_call(
        body, name=name, grid=(s // ROW_TILE,),
        in_specs=[_row_spec(d), _vec_spec(d), _vec_spec(d), _vec_spec(d)], out_specs=_row_spec(d),
        out_shape=jax.ShapeDtypeStruct((s, d), BF16), compiler_params=_cparams("parallel"),
    )(x, gain, scale, shift)


def _norm_mod_bwd(name, x, gain, scale, dh, dres):
    s, d = x.shape

    def body(x_ref, g_ref, sc_ref, dh_ref, dres_ref, dx_ref, st_ref):
        @pl.when(pl.program_id(0) == 0)
        def _():
            st_ref[...] = jnp.zeros_like(st_ref)

        xv = x_ref[...]
        dhv = dh_ref[...].astype(F32)
        r = lax.rsqrt(jnp.mean(xv * xv, axis=-1, keepdims=True) + NORM_EPS)
        xh = xv * r
        n = xh * g_ref[...]
        dn = dhv * (1.0 + sc_ref[...])
        dxh = dn * g_ref[...]
        dx_ref[...] = dres_ref[...] + r * (dxh - xh * jnp.mean(dxh * xh, axis=-1, keepdims=True))
        st_ref[0:1, :] += jnp.sum(dhv, axis=0, keepdims=True)
        st_ref[1:2, :] += jnp.sum(dhv * n, axis=0, keepdims=True)
        st_ref[2:3, :] += jnp.sum(dn * xh, axis=0, keepdims=True)

    return pl.pallas_call(
        body, name=name, grid=(s // ROW_TILE,),
        in_specs=[_row_spec(d), _vec_spec(d), _vec_spec(d), _row_spec(d), _row_spec(d)],
        out_specs=[_row_spec(d), _stat_spec(d)],
        out_shape=[jax.ShapeDtypeStruct((s, d), F32), jax.ShapeDtypeStruct((8, d), F32)],
        compiler_params=_cparams("arbitrary"),
    )(x, gain, scale, dh, dres)


def _gate_bwd(name, dx, y, gate):
    s, d = dx.shape

    def body(dx_ref, y_ref, g_ref, dy_ref, st_ref):
        @pl.when(pl.program_id(0) == 0)
        def _():
            st_ref[...] = jnp.zeros_like(st_ref)

        dxv = dx_ref[...]
        dy_ref[...] = (dxv * g_ref[...]).astype(BF16)
        st_ref[0:1, :] += jnp.sum(dxv * y_ref[...].astype(F32), axis=0, keepdims=True)

    return pl.pallas_call(
        body, name=name, grid=(s // ROW_TILE,),
        in_specs=[_row_spec(d), _row_spec(d), _vec_spec(d)], out_specs=[_row_spec(d), _stat_spec(d)],
        out_shape=[jax.ShapeDtypeStruct((s, d), BF16), jax.ShapeDtypeStruct((8, d), F32)],
        compiler_params=_cparams("arbitrary"),
    )(dx, y, gate)


def _loss_head(name, x, gain, target):
    s, d = x.shape

    def body(x_ref, g_ref, t_ref, dx_ref, st_ref):
        @pl.when(pl.program_id(0) == 0)
        def _():
            st_ref[...] = jnp.zeros_like(st_ref)

        xv = x_ref[...]
        r = lax.rsqrt(jnp.mean(xv * xv, axis=-1, keepdims=True) + NORM_EPS)
        xh = xv * r
        err = xh * g_ref[...] - t_ref[...]
        dy = err * (1.0 / d)
        dxh = dy * g_ref[...]
        dx_ref[...] = r * (dxh - xh * jnp.mean(dxh * xh, axis=-1, keepdims=True))
        st_ref[0:1, :] += jnp.sum(dy * xh, axis=0, keepdims=True)
        st_ref[1:2, :] += jnp.sum(err * err, axis=0, keepdims=True) * (0.5 / d)

    return pl.pallas_call(
        body, name=name, grid=(s // ROW_TILE,),
        in_specs=[_row_spec(d), _vec_spec(d), _row_spec(d)], out_specs=[_row_spec(d), _stat_spec(d)],
        out_shape=[jax.ShapeDtypeStruct((s, d), F32), jax.ShapeDtypeStruct((8, d), F32)],
        compiler_params=_cparams("arbitrary"),
    )(x, gain, target)


def _adamw(name, parts, w, m, v):
    n_parts, rows, cols = parts.shape
    tr = rows
    for cand in (512, 256, 128, 64, 32, 16, 8):
        if rows % cand == 0 and cand * cols * 4 <= 2 * 1024 * 1024:
            tr = cand
            break
    c1 = 1.0 - ADAM_B1 ** ADAM_STEP
    c2 = 1.0 - ADAM_B2 ** ADAM_STEP

    def body(p_ref, w_ref, m_ref, v_ref, g_out, d_out, m_out, v_out):
        g = p_ref[0].astype(F32)
        for i in range(1, n_parts):
            g = g + p_ref[i].astype(F32)
        m_new = ADAM_B1 * m_ref[...] + (1.0 - ADAM_B1) * g
        v_new = ADAM_B2 * v_ref[...] + (1.0 - ADAM_B2) * (g * g)
        g_out[...] = g
        m_out[...] = m_new
        v_out[...] = v_new
        d_out[...] = -ADAM_LR * ((m_new / c1) / (jnp.sqrt(v_new / c2) + ADAM_EPS) + ADAM_WD * w_ref[...])

    spec = pl.BlockSpec((tr, cols), lambda i: (i, 0))
    return pl.pallas_call(
        body, name=name, grid=(rows // tr,),
        in_specs=[pl.BlockSpec((n_parts, tr, cols), lambda i: (0, i, 0)), spec, spec, spec],
        out_specs=[spec, spec, spec, spec],
        out_shape=[jax.ShapeDtypeStruct((rows, cols), F32)] * 4, compiler_params=_cparams("parallel"),
    )(parts, w, m, v)


ATTN_DILATIONS = (1, 4, 16)


def _attn_fwd(proj, g, ride=None):
    s = proj.shape[1]
    dil = ATTN_DILATIONS[g]
    rows, nb = s // dil, s // dil // ATTN_BLOCK

    def body(q_ref, kp_ref, kc_ref, vp_ref, vc_ref, o_ref, l_ref):
        first = pl.program_id(1) == 0
        qi, kj = _iota((128, 128), 0), _iota((128, 128), 1)
        mask_c = kj <= qi
        mask_p = jnp.logical_and(kj >= qi, jnp.logical_not(first))
        low = kj < 64
        for p in range(4):
            sl = slice(128 * p, 128 * p + 128)
            q, kp, kc, vp, vc = q_ref[:, sl], kp_ref[:, sl], kc_ref[:, sl], vp_ref[:, sl], vc_ref[:, sl]
            o_pair = jnp.zeros((128, 128), F32)
            l_pair = jnp.zeros((128, 128), F32)
            for half in range(2):
                hm = low if half == 0 else jnp.logical_not(low)
                qm = jnp.where(hm, q, jnp.zeros_like(q))
                sc = jnp.where(mask_c, _dot_nt(qm, kc) * 0.125, NEG_BIG)
                sp = jnp.where(mask_p, _dot_nt(qm, kp) * 0.125, NEG_BIG)
                m = jnp.maximum(jnp.max(sc, axis=-1, keepdims=True), jnp.max(sp, axis=-1, keepdims=True))
                pc, pp = jnp.exp(sc - m), jnp.exp(sp - m)
                den = jnp.sum(pc, axis=-1, keepdims=True) + jnp.sum(pp, axis=-1, keepdims=True)
                oh = _dot((pc / den).astype(BF16), vc) + _dot((pp / den).astype(BF16), vp)
                o_pair = jnp.where(hm, oh, o_pair)
                l_pair = jnp.where(hm, m + jnp.log(den), l_pair)
            o_ref[:, sl] = o_pair
            l_ref[:, sl] = l_pair

    view = proj.reshape(proj.shape[0], rows, dil * ATTN_GROUP_WIDTH)
    blk = (None, ATTN_BLOCK, ATTN_GROUP_WIDTH)
    prev = lambda j: jnp.maximum(j - 1, 0)
    out_blk = pl.BlockSpec((ATTN_BLOCK, ATTN_GROUP_WIDTH), lambda r, j: (j, r))
    (o, lse), got = _pcall(
        f"attn_fwd_g{g}", body, grid=(dil, nb),
        in_specs=[pl.BlockSpec(blk, lambda r, j: (g, j, r)),
                  pl.BlockSpec(blk, lambda r, j: (3 + g, prev(j), r)), pl.BlockSpec(blk, lambda r, j: (3 + g, j, r)),
                  pl.BlockSpec(blk, lambda r, j: (6 + g, prev(j), r)), pl.BlockSpec(blk, lambda r, j: (6 + g, j, r))],
        out_specs=[out_blk] * 2, out_shape=[jax.ShapeDtypeStruct((rows, dil * ATTN_GROUP_WIDTH), F32)] * 2,
        scratch_shapes=[], sem=("parallel", "parallel"), args=(view,) * 5, ride=ride)
    return o.reshape(s, ATTN_GROUP_WIDTH), lse.reshape(s, ATTN_GROUP_WIDTH), got


def _attn_bwd(proj, g, do, lse, cc, ride=None):
    s = proj.shape[1]
    dil = ATTN_DILATIONS[g]
    rows, nblk = s // dil, s // dil // ATTN_BLOCK

    def body(q_ref, kp_ref, kc_ref, vp_ref, vc_ref, do_ref, l_ref, c_ref, dq_ref, dk_ref, dv_ref, ck, cv):
        j = pl.program_id(1)
        valid = j < nblk
        first = jnp.minimum(j, nblk - 1) == 0

        @pl.when(j == 0)
        def _():
            ck[...] = jnp.zeros_like(ck)
            cv[...] = jnp.zeros_like(cv)

        qi, kj = _iota((128, 128), 0), _iota((128, 128), 1)
        mask_c = jnp.logical_and(kj <= qi, valid)
        mask_p = jnp.logical_and(jnp.logical_and(kj >= qi, jnp.logical_not(first)), valid)
        low = kj < 64
        for p in range(4):
            sl = slice(128 * p, 128 * p + 128)
            q, kp, kc, vp, vc, dov = q_ref[:, sl], kp_ref[:, sl], kc_ref[:, sl], vp_ref[:, sl], vc_ref[:, sl], do_ref[:, sl]
            lse_pair, c_pair = l_ref[:, sl], c_ref[:, sl]
            dq_pair = jnp.zeros((128, 128), F32)
            dkc = jnp.zeros((128, 128), F32)
            dkp = jnp.zeros((128, 128), F32)
            dvc = jnp.zeros((128, 128), F32)
            dvp = jnp.zeros((128, 128), F32)
            for half in range(2):
                hm = low if half == 0 else jnp.logical_not(low)
                col = slice(64 * half, 64 * half + 1)
                lse_h, c_h = lse_pair[:, col], c_pair[:, col]
                qm = jnp.where(hm, q, jnp.zeros_like(q))
                dom = jnp.where(hm, dov, jnp.zeros_like(dov))
                pc = jnp.exp(jnp.where(mask_c, _dot_nt(qm, kc) * 0.125, NEG_BIG) - lse_h)
                pp = jnp.exp(jnp.where(mask_p, _dot_nt(qm, kp) * 0.125, NEG_BIG) - lse_h)
                dsc = (pc * (_dot_nt(dom, vc) + c_h) * 0.125).astype(BF16)
                dsp = (pp * (_dot_nt(dom, vp) + c_h) * 0.125).astype(BF16)
                dq_pair = jnp.where(hm, _dot(dsc, kc) + _dot(dsp, kp), dq_pair)
                dkc += _dot_tn(dsc, qm)
                dkp += _dot_tn(dsp, qm)
                dvc += _dot_tn(pc.astype(BF16), dom)
                dvp += _dot_tn(pp.astype(BF16), dom)

            @pl.when(valid)
            def _():
                dq_ref[:, sl] = dq_pair.astype(BF16)

            dk_ref[:, sl] = (ck[:, sl] + dkp).astype(BF16)
            dv_ref[:, sl] = (cv[:, sl] + dvp).astype(BF16)
            ck[:, sl] = dkc
            cv[:, sl] = dvc

    wide = dil * ATTN_GROUP_WIDTH
    view = proj.reshape(proj.shape[0], rows, wide)
    blk = (None, ATTN_BLOCK, ATTN_GROUP_WIDTH)
    flat = (ATTN_BLOCK, ATTN_GROUP_WIDTH)
    cur = lambda j: jnp.minimum(j, nblk - 1)
    prev = lambda j: jnp.maximum(jnp.minimum(j, nblk - 1) - 1, 0)
    out_prev = lambda j: jnp.maximum(j - 1, 0)
    (dq, dk, dv), got = _pcall(
        f"attn_bwd_g{g}", body, grid=(dil, nblk + 1),
        in_specs=[pl.BlockSpec(blk, lambda r, j: (g, cur(j), r)),
                  pl.BlockSpec(blk, lambda r, j: (3 + g, prev(j), r)), pl.BlockSpec(blk, lambda r, j: (3 + g, cur(j), r)),
                  pl.BlockSpec(blk, lambda r, j: (6 + g, prev(j), r)), pl.BlockSpec(blk, lambda r, j: (6 + g, cur(j), r)),
                  pl.BlockSpec(flat, lambda r, j: (cur(j), r)), pl.BlockSpec(flat, lambda r, j: (cur(j), r)),
                  pl.BlockSpec(flat, lambda r, j: (cur(j), r))],
        out_specs=[pl.BlockSpec(flat, lambda r, j: (cur(j), r)),
                   pl.BlockSpec(flat, lambda r, j: (out_prev(j), r)), pl.BlockSpec(flat, lambda r, j: (out_prev(j), r))],
        out_shape=[jax.ShapeDtypeStruct((rows, wide), BF16)] * 3,
        scratch_shapes=[pltpu.VMEM((ATTN_BLOCK, ATTN_GROUP_WIDTH), F32)] * 2, sem=("parallel", "arbitrary"),
        args=(view,) * 5 + (do.reshape(rows, wide), lse.reshape(rows, wide), cc.reshape(rows, wide)), ride=ride)
    return [t.reshape(s, ATTN_GROUP_WIDTH) for t in (dq, dk, dv)], got


MP_TILE = 256


def _merge_weights(l_refs):
    l0, l1, l2 = l_refs[0][...], l_refs[1][...], l_refs[2][...]
    m = jnp.maximum(jnp.maximum(l0, l1), l2)
    e0, e1, e2 = jnp.exp(l0 - m), jnp.exp(l1 - m), jnp.exp(l2 - m)
    den = e0 + e1 + e2
    return e0 / den, e1 / den, e2 / den


def _pool_diff(ucat, gi, tok):
    window = 2 << gi
    ug = ucat[:, 128 * gi:128 * gi + 128]
    acc, shift = ug, 1
    while shift < window:
        acc = acc + pltpu.roll(acc, shift, 0)
        shift *= 2
    cnt = jnp.minimum(tok + 1, window).astype(F32)
    return acc[POOL_HALO:, :] / cnt - ug[POOL_HALO:, :]


def _merge_pool_fwd(o, lse, proj, pool_w, pool_scale):
    s = o[0].shape[0]
    tr = MP_TILE

    def body(o0_ref, o1_ref, o2_ref, l0_ref, l1_ref, l2_ref, u_ref, uh_ref, pw_ref, ps_ref, cat_ref):
        i = pl.program_id(0)
        w0, w1, w2 = _merge_weights((l0_ref, l1_ref, l2_ref))
        cat_ref[:, 0:512] = (w0 * o0_ref[...] + w1 * o1_ref[...] + w2 * o2_ref[...]).astype(BF16)
        halo = jnp.where(i > 0, uh_ref[...].astype(F32), 0.0)
        ucat = jnp.concatenate([halo, u_ref[...].astype(F32)], axis=0)
        tok = i * tr + _iota((tr, 1), 0)
        for gi in range(POOL_GROUPS):
            sl = slice(128 * gi, 128 * gi + 128)
            diff = _pool_diff(ucat, gi, tok)
            yg = _dot(diff.astype(BF16), pw_ref[gi].astype(BF16)) * ps_ref[:, sl]
            cat_ref[:, 512 + 128 * gi:640 + 128 * gi] = yg.astype(BF16)

    return pl.pallas_call(
        body, name="merge_pool_fwd", grid=(s // tr,),
        in_specs=[pl.BlockSpec((tr, 512), lambda i: (i, 0))] * 6 + [
                  pl.BlockSpec((None, tr, 512), lambda i: (9, i, 0)),
                  pl.BlockSpec((None, POOL_HALO, 512), lambda i: (9, jnp.maximum(i * (tr // POOL_HALO) - 1, 0), 0)),
                  pl.BlockSpec((4, 128, 128), lambda i: (0, 0, 0)), pl.BlockSpec((1, 512), lambda i: (0, 0))],
        out_specs=pl.BlockSpec((tr, 1024), lambda i: (i, 0)),
        out_shape=jax.ShapeDtypeStruct((s, EVEN_OUT_WIDTH), BF16), compiler_params=_cparams("parallel"),
    )(*o, *lse, proj, proj, pool_w, pool_scale)


def _merge_pool_bwd(dcat, o, lse, proj, pool_w, pool_scale, head_sum):
    s = o[0].shape[0]
    tr = MP_TILE
    n_tiles = s // tr

    def body(da_ref, dp_ref, dph_ref, o0_ref, o1_ref, o2_ref, l0_ref, l1_ref, l2_ref, u_ref, uh_ref, pw_ref, ps_ref,
             hs_ref, do0_ref, do1_ref, do2_ref, cc0_ref, cc1_ref, cc2_ref, du_ref, dpw_ref, st_ref):
        i = pl.program_id(0)

        @pl.when(i == 0)
        def _():
            dpw_ref[...] = jnp.zeros_like(dpw_ref)
            st_ref[...] = jnp.zeros_like(st_ref)

        ws = _merge_weights((l0_ref, l1_ref, l2_ref))
        da = da_ref[...]
        attn = ws[0] * o0_ref[...] + ws[1] * o1_ref[...] + ws[2] * o2_ref[...]
        per_head = _dot3_right(da * attn, hs_ref[...])
        for wg, do_ref, cc_ref in zip(ws, (do0_ref, do1_ref, do2_ref), (cc0_ref, cc1_ref, cc2_ref)):
            do_ref[...] = (wg * da).astype(BF16)
            cc_ref[...] = -wg * per_head

        halo = jnp.where(i > 0, uh_ref[...].astype(F32), 0.0)
        ucat = jnp.concatenate([halo, u_ref[...].astype(F32)], axis=0)
        tok = i * tr + _iota((tr, 1), 0)
        dyp = dp_ref[...]
        dnext = jnp.where(i < n_tiles - 1, dph_ref[...], 0.0)
        dyp_ext = jnp.concatenate([dyp, dnext], axis=0)
        tok_ext = i * tr + _iota((tr + POOL_HALO, 1), 0)
        for gi in range(POOL_GROUPS):
            sl = slice(128 * gi, 128 * gi + 128)
            window = 2 << gi
            pw16 = pw_ref[gi].astype(BF16)
            d16 = _pool_diff(ucat, gi, tok).astype(BF16)
            st_ref[0:1, sl] += jnp.sum(dyp[:, sl] * _dot(d16, pw16), axis=0, keepdims=True)
            dpw_ref[gi] += _dot_tn(d16, (dyp[:, sl] * ps_ref[:, sl]).astype(BF16))
            dd = _dot_nt((dyp_ext[:, sl] * ps_ref[:, sl]).astype(BF16), pw16)
            acc = dd / jnp.minimum(tok_ext + 1, window).astype(F32)
            shift = 1
            while shift < window:
                acc = acc + pltpu.roll(acc, tr + POOL_HALO - shift, 0)
                shift *= 2
            du_ref[:, sl] = (acc[:tr, :] - dd[:tr, :]).astype(BF16)

    halo_blocks = tr // POOL_HALO
    return pl.pallas_call(
        body, name="merge_pool_bwd", grid=(n_tiles,),
        in_specs=[pl.BlockSpec((tr, 512), lambda i: (i, 0)), pl.BlockSpec((tr, 512), lambda i: (i, 1)),
                  pl.BlockSpec((POOL_HALO, 512), lambda i: (jnp.minimum((i + 1) * halo_blocks, s // POOL_HALO - 1), 1))]
                 + [pl.BlockSpec((tr, 512), lambda i: (i, 0))] * 6 + [
                  pl.BlockSpec((None, tr, 512), lambda i: (9, i, 0)),
                  pl.BlockSpec((None, POOL_HALO, 512), lambda i: (9, jnp.maximum(i * halo_blocks - 1, 0), 0)),
                  pl.BlockSpec((4, 128, 128), lambda i: (0, 0, 0)), pl.BlockSpec((1, 512), lambda i: (0, 0)),
                  pl.BlockSpec((512, 512), lambda i: (0, 0))],
        out_specs=[pl.BlockSpec((tr, 512), lambda i: (i, 0))] * 7 + [
                   pl.BlockSpec((4, 128, 128), lambda i: (0, 0, 0)), pl.BlockSpec((8, 512), lambda i: (0, 0))],
        out_shape=[jax.ShapeDtypeStruct((s, 512), BF16)] * 3 + [jax.ShapeDtypeStruct((s, 512), F32)] * 3 + [
                   jax.ShapeDtypeStruct((s, 512), BF16), jax.ShapeDtypeStruct((4, 128, 128), F32),
                   jax.ShapeDtypeStruct((8, 512), F32)],
        compiler_params=_cparams("arbitrary"),
    )(dcat, dcat, dcat, *o, *lse, proj, proj, pool_w, pool_scale, head_sum)


CONV_TILE = 512
CONV_HALO = 8
XBC_BLOCK0 = SSM_D_INNER // 512
DT_BLOCK = (SSM_D_INNER + SSM_CONV_DIM) // 128


def _conv_fwd(proj, conv_w, conv_b):
    s = proj.shape[0]
    tr = CONV_TILE

    def body(x_ref, xh_ref, w_ref, b_ref, pre_ref, act_ref):
        i = pl.program_id(0)
        xcat = jnp.concatenate([jnp.where(i > 0, xh_ref[...], 0.0), x_ref[...]], axis=0)
        w = w_ref[...]
        pre = b_ref[...] + w[3:4, :] * xcat[CONV_HALO:, :]
        for back in range(1, SSM_CONV):
            pre = pre + w[3 - back:4 - back, :] * pltpu.roll(xcat, back, 0)[CONV_HALO:, :]
        pre_ref[...] = pre
        act_ref[...] = pre * _sigmoid(pre)

    hb = tr // CONV_HALO
    return pl.pallas_call(
        body, name="conv_fwd", grid=(s // tr, SSM_CONV_DIM // 512),
        in_specs=[pl.BlockSpec((tr, 512), lambda i, j: (i, XBC_BLOCK0 + j)),
                  pl.BlockSpec((CONV_HALO, 512), lambda i, j: (jnp.maximum(i * hb - 1, 0), XBC_BLOCK0 + j)),
                  pl.BlockSpec((SSM_CONV, 512), lambda i, j: (0, j)), pl.BlockSpec((1, 512), lambda i, j: (0, j))],
        out_specs=[pl.BlockSpec((tr, 512), lambda i, j: (i, j))] * 2,
        out_shape=[jax.ShapeDtypeStruct((s, SSM_CONV_DIM), F32)] * 2, compiler_params=_cparams("parallel", "parallel"),
    )(proj, proj, conv_w, conv_b)


def _conv_bwd(name, dact, pre, proj, conv_w, dproj, *, block0):
    s, width = dact.shape
    tr = CONV_TILE
    n_tiles = s // tr
    hb = tr // CONV_HALO

    def body(da_ref, dah_ref, pre_ref, preh_ref, x_ref, xh_ref, w_ref, _, dp_ref, st_ref):
        i = pl.program_id(1)

        @pl.when(i == 0)
        def _():
            st_ref[...] = jnp.zeros_like(st_ref)

        da_ext = jnp.concatenate([da_ref[...], jnp.where(i < n_tiles - 1, dah_ref[...], 0.0)], axis=0)
        pre_ext = jnp.concatenate([pre_ref[...], preh_ref[...]], axis=0)
        sg = _sigmoid(pre_ext)
        dpre_ext = da_ext * (sg * (1.0 + pre_ext * (1.0 - sg)))
        w = w_ref[...]
        draw = w[3:4, :] * dpre_ext[:tr, :]
        for ahead in range(1, SSM_CONV):
            draw = draw + w[3 - ahead:4 - ahead, :] * pltpu.roll(dpre_ext, tr + CONV_HALO - ahead, 0)[:tr, :]
        dp_ref[...] = draw.astype(BF16)
        dpre = dpre_ext[:tr, :]
        xcat = jnp.concatenate([jnp.where(i > 0, xh_ref[...], 0.0), x_ref[...]], axis=0)
        st_ref[3:4, :] += jnp.sum(dpre * xcat[CONV_HALO:, :], axis=0, keepdims=True)
        for back in range(1, SSM_CONV):
            st_ref[3 - back:4 - back, :] += jnp.sum(dpre * pltpu.roll(xcat, back, 0)[CONV_HALO:, :], axis=0, keepdims=True)
        st_ref[4:5, :] += jnp.sum(dpre, axis=0, keepdims=True)

    nxt = lambda i: jnp.minimum((i + 1) * hb, s // CONV_HALO - 1)
    prv = lambda i: jnp.maximum(i * hb - 1, 0)
    return pl.pallas_call(
        body, name=name, grid=(width // 512, n_tiles),
        in_specs=[pl.BlockSpec((tr, 512), lambda j, i: (i, j)), pl.BlockSpec((CONV_HALO, 512), lambda j, i: (nxt(i), j)),
                  pl.BlockSpec((tr, 512), lambda j, i: (i, block0 + j)),
                  pl.BlockSpec((CONV_HALO, 512), lambda j, i: (nxt(i), block0 + j)),
                  pl.BlockSpec((tr, 512), lambda j, i: (i, XBC_BLOCK0 + block0 + j)),
                  pl.BlockSpec((CONV_HALO, 512), lambda j, i: (prv(i), XBC_BLOCK0 + block0 + j)),
                  pl.BlockSpec((SSM_CONV, 512), lambda j, i: (0, block0 + j)), ANY],
        out_specs=[pl.BlockSpec((tr, 512), lambda j, i: (i, XBC_BLOCK0 + block0 + j)),
                   pl.BlockSpec((8, 512), lambda j, i: (0, j))],
        out_shape=[jax.ShapeDtypeStruct(dproj.shape, BF16), jax.ShapeDtypeStruct((8, width), F32)],
        input_output_aliases={7: 0}, compiler_params=_cparams("parallel", "arbitrary"),
    )(dact, dact, pre, pre, proj, proj, conv_w, dproj)


def _dt_fwd(proj, dt_bias):
    s = proj.shape[0]

    def body(x_ref, b_ref, o_ref):
        v = x_ref[...] + b_ref[...]
        o_ref[...] = jnp.maximum(v, 0.0) + jnp.log(1.0 + jnp.exp(-jnp.abs(v)))

    return pl.pallas_call(
        body, name="dt_fwd", grid=(s // CONV_TILE,),
        in_specs=[pl.BlockSpec((CONV_TILE, 128), lambda i: (i, DT_BLOCK)), pl.BlockSpec((1, 128), lambda i: (0, 0))],
        out_specs=pl.BlockSpec((CONV_TILE, 128), lambda i: (i, 0)),
        out_shape=jax.ShapeDtypeStruct((s, 128), F32), compiler_params=_cparams("parallel"),
    )(proj, dt_bias)


def _dt_bwd(ddt, proj, dt_bias, dproj):
    s = proj.shape[0]

    def body(d_ref, x_ref, b_ref, _, dp_ref, st_ref):
        @pl.when(pl.program_id(0) == 0)
        def _():
            st_ref[...] = jnp.zeros_like(st_ref)

        draw = d_ref[...] * _sigmoid(x_ref[...] + b_ref[...])
        dp_ref[...] = draw.astype(BF16)
        st_ref[0:1, :] += jnp.sum(draw, axis=0, keepdims=True)

    return pl.pallas_call(
        body, name="dt_bwd", grid=(s // CONV_TILE,),
        in_specs=[pl.BlockSpec((CONV_TILE, 128), lambda i: (i, 0)), pl.BlockSpec((CONV_TILE, 128), lambda i: (i, DT_BLOCK)),
                  pl.BlockSpec((1, 128), lambda i: (0, 0)), ANY],
        out_specs=[pl.BlockSpec((CONV_TILE, 128), lambda i: (i, DT_BLOCK)), pl.BlockSpec((8, 128), lambda i: (0, 0))],
        out_shape=[jax.ShapeDtypeStruct(dproj.shape, BF16), jax.ShapeDtypeStruct((8, 128), F32)],
        input_output_aliases={3: 0}, compiler_params=_cparams("arbitrary"),
    )(ddt, proj, dt_bias, dproj)


def _ssd_common(x_ref, b_ref, c_ref, dt_ref, al_ref):
    row, col = _iota((128, 128), 0), _iota((128, 128), 1)
    tril = row >= col
    expand = jnp.where((_iota((128, 512), 1) >> 6) == _iota((128, 512), 0), 1.0, 0.0).astype(BF16)
    dt = dt_ref[...]
    a_neg = -jnp.exp(al_ref[...])
    a_col = _dot3_left(jnp.where(tril, 1.0, 0.0).astype(BF16), dt * a_neg)
    a_exp = _dot3_right(a_col, expand)
    dt_exp = _dot3_right(dt, expand)
    x = x_ref[...]
    return dict(tril=tril, col=col, expand=expand, dt=dt, a_neg=a_neg, a_col=a_col, a_row=a_col.T, a_exp=a_exp,
                dt_exp=dt_exp, a_last=a_exp[127:128, :], x=x, xd=x * dt_exp,
                b16=b_ref[...].astype(BF16), c16=c_ref[...].astype(BF16))


def _ssd_specs(nc, order):
    return [pl.BlockSpec((SSM_CHUNK, 512), lambda g, c: (order(c), g)),
            pl.BlockSpec((SSM_CHUNK, 128), lambda g, c: (order(c), SSM_D_INNER // 128 + g)),
            pl.BlockSpec((SSM_CHUNK, 128), lambda g, c: (order(c), SSM_D_INNER // 128 + SSM_GROUPS + g)),
            pl.BlockSpec((SSM_CHUNK, 512), lambda g, c: (order(c), g)),
            pl.BlockSpec((None, SSM_CHUNK, 128), lambda g, c: (g, order(c), 0)),
            pl.BlockSpec((None, 1, 128), lambda g, c: (g, 0, 0)),
            pl.BlockSpec((1, 512), lambda g, c: (0, g)), pl.BlockSpec((1, 512), lambda g, c: (0, g))]


def _ssd_fwd(act, proj, dtc, a_log, d_skip, norm_g, ride=None):
    s = act.shape[0]
    nc = s // SSM_CHUNK

    def body(x_ref, b_ref, c_ref, z_ref, dt_ref, al_ref, dsk_ref, ng_ref, y_ref, yn_ref, hin_ref, h_sc):
        @pl.when(pl.program_id(1) == 0)
        def _():
            h_sc[...] = jnp.zeros_like(h_sc)

        q = _ssd_common(x_ref, b_ref, c_ref, dt_ref, al_ref)
        gmat = _dot_nt(q["c16"], q["b16"])
        h_in = h_sc[...]
        hin_ref[...] = h_in
        zmat = _dot(q["c16"], h_in.astype(BF16))
        xd16 = q["xd"].astype(BF16)
        low = q["col"] < 64
        pieces = []
        for p in range(4):
            xs = xd16[:, 128 * p:128 * p + 128]
            acc = jnp.zeros((128, 128), F32)
            for half in range(2):
                r = 2 * p + half
                decay = jnp.exp(jnp.where(q["tril"], q["a_col"][:, r:r + 1] - q["a_row"][r:r + 1, :], NEG_BIG))
                hm = low if half == 0 else jnp.logical_not(low)
                acc += _dot((gmat * decay).astype(BF16), jnp.where(hm, xs, jnp.zeros_like(xs)))
            pieces.append(acc)
        y = jnp.concatenate(pieces, axis=1) + zmat * jnp.exp(q["a_exp"]) + q["x"] * dsk_ref[...]
        y_ref[...] = y
        w16 = (q["xd"] * jnp.exp(q["a_last"] - q["a_exp"])).astype(BF16)
        h_sc[...] = h_in * jnp.exp(q["a_last"]) + _dot_tn(q["b16"], w16)
        z = z_ref[...]
        yg = y * (z * _sigmoid(z))
        rr = lax.rsqrt(jnp.mean(yg * yg, axis=-1, keepdims=True) + NORM_EPS)
        yn_ref[...] = (yg * rr * ng_ref[...]).astype(BF16)

    blk = pl.BlockSpec((SSM_CHUNK, 512), lambda g, c: (c, g))
    return _pcall(
        "ssd_fwd", body, grid=(SSM_GROUPS, nc), in_specs=_ssd_specs(nc, lambda c: c),
        out_specs=[blk, blk, pl.BlockSpec((None, None, SSM_STATE, 512), lambda g, c: (g, c, 0, 0))],
        out_shape=[jax.ShapeDtypeStruct((s, SSM_D_INNER), F32), jax.ShapeDtypeStruct((s, SSM_D_INNER), BF16),
                   jax.ShapeDtypeStruct((SSM_GROUPS, nc, SSM_STATE, 512), F32)],
        scratch_shapes=[pltpu.VMEM((SSM_STATE, 512), F32)], sem=("parallel", "arbitrary"),
        args=(act, act, act, proj, dtc, a_log, d_skip, norm_g), ride=ride)


def _ssd_bwd(act, proj, dtc, a_log, d_skip, norm_g, y, h_in_all, dyn, ride=None):
    s = act.shape[0]
    nc = s // SSM_CHUNK

    def body(x_ref, b_ref, c_ref, z_ref, dt_ref, al_ref, dsk_ref, ng_ref, y_ref, hin_ref, dyn_ref,
             dz_ref, dx_ref, db_ref, dc_ref, ddt_ref, st_ref, dal_ref, dh_sc):
        @pl.when(pl.program_id(1) == 0)
        def _():
            dh_sc[...] = jnp.zeros_like(dh_sc)
            st_ref[...] = jnp.zeros_like(st_ref)
            dal_ref[...] = jnp.zeros_like(dal_ref)

        q = _ssd_common(x_ref, b_ref, c_ref, dt_ref, al_ref)
        x, xd, b16, c16 = q["x"], q["xd"], q["b16"], q["c16"]
        z, yv, dyn_v = z_ref[...], y_ref[...], dyn_ref[...]
        sig = _sigmoid(z)
        sil = z * sig
        yg = yv * sil
        rr = lax.rsqrt(jnp.mean(yg * yg, axis=-1, keepdims=True) + NORM_EPS)
        st_ref[0:1, :] += jnp.sum(dyn_v * yg * rr, axis=0, keepdims=True)
        t1 = dyn_v * ng_ref[...]
        dyg = rr * (t1 - yg * (rr * rr) * jnp.mean(t1 * yg, axis=-1, keepdims=True))
        dy = dyg * sil
        dz_ref[...] = (dyg * yv * (sig * (1.0 + z * (1.0 - sig)))).astype(BF16)
        st_ref[1:2, :] += jnp.sum(dy * x, axis=0, keepdims=True)
        dx = dsk_ref[...] * dy
        h_in = hin_ref[...]
        h16 = h_in.astype(BF16)
        ea = jnp.exp(q["a_exp"])
        zmat = _dot(c16, h16)
        dz16 = (dy * ea).astype(BF16)
        da_ch = dy * zmat * ea
        dcm = _dot_nt(dz16, h16)
        dh_in = _dot_tn(c16, dz16)
        dh_out = dh_sc[...]
        dho16 = dh_out.astype(BF16)
        eal = jnp.exp(q["a_last"])
        dh_in += dh_out * eal
        dal_ch = jnp.sum(dh_out * h_in, axis=0, keepdims=True) * eal
        to_end = jnp.exp(q["a_last"] - q["a_exp"])
        wmat = xd * to_end
        dbm = _dot_nt(wmat.astype(BF16), dho16)
        dw = _dot(b16, dho16)
        dxd = dw * to_end
        g_end = dw * wmat
        da_ch -= g_end
        dal_ch += jnp.sum(g_end, axis=0, keepdims=True)
        gmat = _dot_nt(c16, b16)
        row = _iota((128, 128), 0)
        triu = row <= q["col"]
        xd16, dy16 = xd.astype(BF16), dy.astype(BF16)
        low = q["col"] < 64
        da_col = jnp.zeros((128, 128), F32)
        da_key = jnp.zeros((128, 128), F32)
        dg = jnp.zeros((128, 128), F32)
        pieces = []
        for p in range(4):
            xs, dys = xd16[:, 128 * p:128 * p + 128], dy16[:, 128 * p:128 * p + 128]
            acc = jnp.zeros((128, 128), F32)
            for half in range(2):
                r = 2 * p + half
                hm = low if half == 0 else jnp.logical_not(low)
                xm = jnp.where(hm, xs, jnp.zeros_like(xs))
                dym = jnp.where(hm, dys, jnp.zeros_like(dys))
                decay = jnp.exp(jnp.where(q["tril"], q["a_col"][:, r:r + 1] - q["a_row"][r:r + 1, :], NEG_BIG))
                acc += _dot_tn((gmat * decay).astype(BF16), dym)
                dgl = _dot_nt(dym, xm) * decay
                dg += dgl
                n_ls = dgl * gmat
                da_col += jnp.where(q["col"] == r, jnp.sum(n_ls, axis=-1, keepdims=True), 0.0)
                da_key += jnp.where(row == r, jnp.sum(n_ls, axis=0, keepdims=True), 0.0)
            pieces.append(acc)
        da_col -= da_key.T
        dxd += jnp.concatenate(pieces, axis=1)
        dg16 = dg.astype(BF16)
        dcm += _dot(dg16, b16)
        dbm += _dot_tn(dg16, c16)
        fold = jnp.where((_iota((512, 128), 0) >> 6) == _iota((512, 128), 1), 1.0, 0.0).astype(BF16)
        da_ch += jnp.where(_iota((128, 1), 0) == 127, dal_ch, 0.0)
        da_col += _dot3_right(da_ch, fold)
        d_dta = _dot3_left(jnp.where(triu, 1.0, 0.0).astype(BF16), da_col)
        ddt_ref[...] = d_dta * q["a_neg"] + _dot3_right(dxd * x, fold)
        dal_ref[0:1, :] += jnp.sum(d_dta * q["dt"] * q["a_neg"], axis=0, keepdims=True)
        dx_ref[...] = dx + dxd * q["dt_exp"]
        db_ref[...] = dbm
        dc_ref[...] = dcm
        dh_sc[...] = dh_in

    rev = lambda c: nc - 1 - c
    blk = pl.BlockSpec((SSM_CHUNK, 512), lambda g, c: (rev(c), g))
    small = pl.BlockSpec((SSM_CHUNK, 128), lambda g, c: (rev(c), g))
    return _pcall(
        "ssd_bwd", body, grid=(SSM_GROUPS, nc), ride=ride,
        args=(act, act, act, proj, dtc, a_log, d_skip, norm_g, y, h_in_all, dyn), sem=("parallel", "arbitrary"),
        in_specs=_ssd_specs(nc, rev) + [blk, pl.BlockSpec((None, None, SSM_STATE, 512), lambda g, c: (g, rev(c), 0, 0)), blk],
        out_specs=[blk, blk, small, small, pl.BlockSpec((None, SSM_CHUNK, 128), lambda g, c: (g, rev(c), 0)),
                   pl.BlockSpec((8, 512), lambda g, c: (0, g)), pl.BlockSpec((None, 8, 128), lambda g, c: (g, 0, 0))],
        out_shape=[jax.ShapeDtypeStruct((s, SSM_IN_PAD), BF16), jax.ShapeDtypeStruct((s, SSM_D_INNER), F32),
                   jax.ShapeDtypeStruct((s, SSM_GROUPS * SSM_STATE), F32), jax.ShapeDtypeStruct((s, SSM_GROUPS * SSM_STATE), F32),
                   jax.ShapeDtypeStruct((SSM_GROUPS, s, 128), F32), jax.ShapeDtypeStruct((8, SSM_D_INNER), F32),
                   jax.ShapeDtypeStruct((SSM_GROUPS, 8, 128), F32)],
        scratch_shapes=[pltpu.VMEM((SSM_STATE, 512), F32)])


MM_TM = 1024
MM_TK = 512
MM_TK_BIG = 2048
FFN_SHARD = FFN_HIDDEN // N_DEV


def _ij(tm, tn):
    return pl.BlockSpec((tm, tn), lambda i, j, k: (i, j))


def _mm_plain(name, a, b, mode, dims, out_dtype, tn=1024, tk=MM_TK, **kw):
    res = _matmul(name, a, b, mode=mode, dims=dims, tiles=(MM_TM, tn, tk), outs=[((dims[0], dims[1]), out_dtype)],
                  epilogue=_epi_plain, **kw)
    return res[0], res[1:]


def _mm_resgate(name, a, b, res, gate, k_dim, tk, ride=None):
    s, d = res.shape
    got = _matmul(name, a, b, mode="nn", dims=(s, d, k_dim), tiles=(MM_TM, 1024, tk),
                  outs=[((s, d), F32), ((s, d), BF16)], epilogue=_epi_resgate, extras=[res, gate],
                  extra_specs=[_ij(MM_TM, 1024), pl.BlockSpec((1, 1024), lambda i, j, k: (0, j))], ride=ride)
    return got[0], got[1], got[2:]


def _ffn_fwd(tag, x_in, gain, scale, shift, gate, w1, w2_of, ride1=None, ride2=None):
    s, d = x_in.shape
    h = _norm_mod_fwd(tag + "_norm2", x_in, gain, scale, shift)
    rr, pre, *got1 = _matmul(tag + "_ffn1", h, w1, mode="nn", dims=(s, FFN_HIDDEN, d), tiles=(MM_TM, FFN_SHARD, MM_TK_BIG),
                             b_spec=pl.BlockSpec((None, MM_TK_BIG, FFN_SHARD), lambda i, j, k: (j, k, 0)),
                             outs=[((s, FFN_HIDDEN), BF16)] * 2, epilogue=_epi_relu2, ride=ride1)
    x_out, f, got2 = _mm_resgate(tag + "_ffn2", rr, w2_of(got1), x_in, gate, FFN_HIDDEN, MM_TK_BIG, ride=ride2)
    return x_out, (h, rr, pre, f), got1, got2


def _ffn_bwd(tag, dx_out, x_in, saved, gain, scale, gate, w1, w2, ride_dx2=None, ride_dw2=None, ride_dx1_of=None):
    s, d = x_in.shape
    h, rr, pre, f = saved
    dy, st_gate = _gate_bwd(tag + "_gate2_bwd", dx_out, f, gate)
    da, *got2 = _matmul(tag + "_ffn2_dx", dy, w2, mode="nt", dims=(s, FFN_HIDDEN, d), tiles=(MM_TM, 1024, MM_TK_BIG),
                        outs=[((s, FFN_HIDDEN), BF16)], epilogue=_epi_drelu2, extras=[pre],
                        extra_specs=[_ij(MM_TM, 1024)], ride=ride_dx2)
    dw2, got_dw2 = _mm_plain(tag + "_ffn2_dw", rr, dy, "tn", (FFN_HIDDEN, d, s), BF16, tk=MM_TK_BIG, ride=ride_dw2)
    dh, got1 = _mm_plain(tag + "_ffn1_dx", da, w1, "nt", (s, d, FFN_HIDDEN), F32, tk=FFN_SHARD,
                         b_spec=pl.BlockSpec((None, 1024, FFN_SHARD), lambda i, j, k: (k, j, 0)),
                         ride=None if ride_dx1_of is None else ride_dx1_of(dw2))
    dw1 = _matmul(tag + "_ffn1_dw", h, da, mode="tn", dims=(d, FFN_HIDDEN, s), tiles=(MM_TM, FFN_SHARD, MM_TK_BIG),
                  outs=[((N_DEV, d, FFN_SHARD), BF16)], epilogue=_epi_plain,
                  out_specs=[pl.BlockSpec((None, MM_TM, FFN_SHARD), lambda i, j, k: (j, i, 0))])[0]
    dx_in, st_norm = _norm_mod_bwd(tag + "_norm2_bwd", x_in, gain, scale, dh, dx_out)
    return dx_in, dw1, dw2, st_gate[0], st_norm, got2, got_dw2, got1


def _full_weight(name, gathered):
    if name in ASSEMBLED:
        return _assemble_cols("assemble_" + name, gathered, ASSEMBLED[name])
    if name == "ffn_w1":
        return gathered
    return gathered.reshape(-1, gathered.shape[2])


def _grad_pieces(name, full, shard_shape, half=None):
    rows = shard_shape[0]
    lo, n = (0, rows) if half is None else (half * (rows // 2), rows // 2)
    if name in ASSEMBLED:
        return _split_cols(f"split_{name}_{lo}", full, shard_shape[1], lo, n)
    return full.reshape(N_DEV, *shard_shape)[:, lo:lo + n]


def _device_step(x, target, mod, sm, shards, w_even_in):
    s, d = x.shape
    mv = [[mod[i, k].reshape(1, d) for k in range(6)] for i in range(2)]
    nm = [sm["norm_mix"][i].reshape(1, d) for i in range(2)]
    nf = [sm["norm_ffn"][i].reshape(1, d) for i in range(2)]
    pool_w, pool_scale = sm["pool_w"].reshape(4, 128, 128), sm["pool_scale"].reshape(1, POOL_WIDTH)
    got = {}

    def gather(*items):
        return [shards[it] for it in items], False

    def gather_half(item, half):
        rows = shards[item].shape[0] // 2
        return [shards[item][half * rows:(half + 1) * rows]], False

    def scatter(*pieces):
        return list(pieces), True

    def pieces(item, full, half=None):
        return _grad_pieces(item[0], full, shards[item].shape, half)

    def joined(top, bottom):
        return jnp.concatenate([top, bottom], axis=1)

    sh1, sc1, g1, sh2, sc2, g2 = mv[0]
    h1 = _norm_mod_fwd("l0_norm1", x, nm[0], sc1, sh1)
    slab = pl.BlockSpec((None, MM_TM, 512), lambda i, j, k: (j, i, 0))
    proj0, g_eout, g_sout = _matmul(
        "l0_in", h1, w_even_in, mode="nn", dims=(s, EVEN_IN_WIDTH, d), tiles=(MM_TM, 512, MM_TK_BIG),
        outs=[((EVEN_IN_WIDTH // 512, s, 512), BF16)], out_specs=[slab], epilogue=_epi_plain,
        ride=gather(("even_w_out", 0), ("ssm_w_out", 0)))
    w_even_out, w_ssm_out = _full_weight("even_w_out", g_eout), _full_weight("ssm_w_out", g_sout)
    o, lse = [None] * 3, [None] * 3
    o[0], lse[0], (g_w1_top,) = _attn_fwd(proj0, 0, ride=gather_half(("ffn_w1", 0), 0))
    o[1], lse[1], (g_w1_bottom,) = _attn_fwd(proj0, 1, ride=gather_half(("ffn_w1", 0), 1))
    o[2], lse[2], (g_sin_top,) = _attn_fwd(proj0, 2, ride=gather_half(("ssm_w_in", 0), 0))
    w1_0 = _full_weight("ffn_w1", joined(g_w1_top, g_w1_bottom))
    cat = _merge_pool_fwd(o, lse, proj0, pool_w, pool_scale)
    x1, y0, _ = _mm_resgate("l0_out", cat, w_even_out, x, g1, EVEN_OUT_WIDTH, EVEN_OUT_WIDTH)
    x2, ffn0, g_w2_0, (g_sin_bottom,) = _ffn_fwd(
        "l0", x1, nf[0], sc2, sh2, g2, w1_0, lambda arrived: _full_weight("ffn_w2", arrived[0]),
        ride1=gather(("ffn_w2", 0)), ride2=gather_half(("ssm_w_in", 0), 1))
    w2_0, w_ssm_in = _full_weight("ffn_w2", g_w2_0[0]), _full_weight("ssm_w_in", joined(g_sin_top, g_sin_bottom))

    th1, tc1, tg1, th2, tc2, tg2 = mv[1]
    h3 = _norm_mod_fwd("l1_norm1", x2, nm[1], tc1, th1)
    proj1, (g_w1_1,) = _mm_plain("l1_in", h3, w_ssm_in, "nn", (s, SSM_IN_PAD, d), F32, tn=1152, tk=MM_TK_BIG,
                                 ride=gather(("ffn_w1", 1)))
    w1_1 = _full_weight("ffn_w1", g_w1_1)
    conv_w = sm["ssm_conv_w"].reshape(SSM_CONV, SSM_CONV_DIM)
    pre, act = _conv_fwd(proj1, conv_w, sm["ssm_conv_b"].reshape(1, SSM_CONV_DIM))
    dt_bias = jnp.pad(sm["ssm_dt_bias"].reshape(1, SSM_HEADS), ((0, 0), (0, 128 - SSM_HEADS)))
    dt_full = _dt_fwd(proj1, dt_bias)
    dtc = jnp.pad(dt_full[:, :SSM_HEADS].reshape(s, SSM_GROUPS, 8).transpose(1, 0, 2), ((0, 0), (0, 0), (0, 120)))
    a_log = jnp.pad(sm["ssm_a_log"].reshape(SSM_GROUPS, 1, 8), ((0, 0), (0, 0), (0, 120)))
    d_skip = jnp.repeat(sm["ssm_d"].reshape(SSM_HEADS), SSM_D_INNER // SSM_HEADS).reshape(1, SSM_D_INNER)
    norm_g = sm["ssm_norm"].reshape(1, SSM_D_INNER)
    (y, yn, h_in_all), (g_w2_1,) = _ssd_fwd(act, proj1, dtc, a_log, d_skip, norm_g, ride=gather(("ffn_w2", 1)))
    w2_1 = _full_weight("ffn_w2", g_w2_1)
    x3, y1, _ = _mm_resgate("l1_out", yn, w_ssm_out, x2, tg1, SSM_D_INNER, MM_TK_BIG)
    x4, ffn1, _, _ = _ffn_fwd("l1", x3, nf[1], tc2, th2, tg2, w1_1, lambda arrived: w2_1)

    dx4, st_loss = _loss_head("loss_head", x4, sm["final_norm"].reshape(1, d), target)
    loss = jnp.sum(st_loss[1])

    dx3, dw1_1, dw2_1, dg2_1, st_n2_1, _, _, (got[("ffn_w2", 1)],) = _ffn_bwd(
        "l1", dx4, x3, ffn1, nf[1], tc2, tg2, w1_1, w2_1,
        ride_dx1_of=lambda dw2: scatter(pieces(("ffn_w2", 1), dw2)))
    dy1, st_g1_1 = _gate_bwd("l1_gate1_bwd", dx3, y1, tg1)
    dyn, _ = _mm_plain("l1_out_dx", dy1, w_ssm_out, "nt", (s, SSM_D_INNER, d), F32, tk=MM_TK_BIG)
    dw_sout, _ = _mm_plain("l1_out_dw", yn, dy1, "tn", (SSM_D_INNER, d, s), BF16, tk=MM_TK_BIG)
    (dproj1, dxs, dbm, dcm, ddt, st_ssd, d_alog), (got[("ffn_w1", 1)], got[("ssm_w_out", 0)]) = _ssd_bwd(
        act, proj1, dtc, a_log, d_skip, norm_g, y, h_in_all, dyn,
        ride=scatter(dw1_1, pieces(("ssm_w_out", 0), dw_sout)))
    dproj1, st_cx = _conv_bwd("conv_bwd_x", dxs, pre, proj1, conv_w, dproj1, block0=0)
    dproj1, st_cb = _conv_bwd("conv_bwd_b", dbm, pre, proj1, conv_w, dproj1, block0=SSM_D_INNER // 512)
    dproj1, st_cc = _conv_bwd("conv_bwd_c", dcm, pre, proj1, conv_w, dproj1, block0=SSM_D_INNER // 512 + 2)
    ddt_rows = jnp.pad(ddt[:, :, :8].transpose(1, 0, 2).reshape(s, SSM_HEADS), ((0, 0), (0, 128 - SSM_HEADS)))
    dproj1, st_dt = _dt_bwd(ddt_rows, proj1, dt_bias, dproj1)
    dh3, _ = _mm_plain("l1_in_dx", dproj1, w_ssm_in, "nt", (s, d, SSM_IN_PAD), F32, tk=SSM_IN_PAD // 3)
    dw_sin, _ = _mm_plain("l1_in_dw", h3, dproj1, "tn", (d, SSM_IN_PAD, s), BF16, tn=1152, tk=MM_TK_BIG)
    dx2, st_n1_1 = _norm_mod_bwd("l1_norm1_bwd", x2, nm[1], tc1, dh3, dx3)

    dx1, dw1_0, dw2_0, dg2_0, st_n2_0, (sin_top,), (sin_bottom,), (got[("ffn_w2", 0)],) = _ffn_bwd(
        "l0", dx2, x1, ffn0, nf[0], sc2, g2, w1_0, w2_0, ride_dx2=scatter(pieces(("ssm_w_in", 0), dw_sin, 0)),
        ride_dw2=scatter(pieces(("ssm_w_in", 0), dw_sin, 1)),
        ride_dx1_of=lambda dw2: scatter(pieces(("ffn_w2", 0), dw2)))
    got[("ssm_w_in", 0)] = joined(sin_top, sin_bottom)
    dy0, st_g1_0 = _gate_bwd("l0_gate1_bwd", dx1, y0, g1)
    dcat, _ = _mm_plain("l0_out_dx", dy0, w_even_out, "nt", (s, EVEN_OUT_WIDTH, d), F32, tk=MM_TK_BIG)
    dw_eout, _ = _mm_plain("l0_out_dw", cat, dy0, "tn", (EVEN_OUT_WIDTH, d, s), BF16, tk=MM_TK_BIG)
    lane = jnp.arange(512) // 64
    head_sum = (lane[:, None] == lane[None, :]).astype(BF16)
    *do_cc, du, d_pool_w, st_pool = _merge_pool_bwd(dcat, o, lse, proj0, pool_w, pool_scale, head_sum)
    do, cc = do_cc[:3], do_cc[3:]
    dqkv = [None] * 3
    dqkv[0], (w1_top,) = _attn_bwd(proj0, 0, do[0], lse[0], cc[0], ride=scatter(pieces(("ffn_w1", 0), dw1_0, 0)))
    dqkv[1], (w1_bottom,) = _attn_bwd(proj0, 1, do[1], lse[1], cc[1], ride=scatter(pieces(("ffn_w1", 0), dw1_0, 1)))
    dqkv[2], (got[("even_w_out", 0)],) = _attn_bwd(proj0, 2, do[2], lse[2], cc[2],
                                                   ride=scatter(pieces(("even_w_out", 0), dw_eout)))
    got[("ffn_w1", 0)] = joined(w1_top, w1_bottom)
    dproj0 = jnp.stack([dqkv[g][kind] for kind in range(3) for g in range(3)] + [du])
    dw_ein, _ = _mm_plain("l0_in_dw", h1, dproj0, "tn", (d, EVEN_IN_WIDTH, s), BF16, tn=512, tk=MM_TK_BIG,
                          b_spec=pl.BlockSpec((None, MM_TK_BIG, 512), lambda i, j, k: (j, k, 0)))
    dh1, (got[("even_w_in", 0)],) = _mm_plain("l0_in_dx", dproj0, w_even_in, "nt", (s, d, EVEN_IN_WIDTH), F32,
                                              a_spec=pl.BlockSpec((None, MM_TM, 512), lambda i, j, k: (k, i, 0)),
                                              ride=scatter(pieces(("even_w_in", 0), dw_ein)))
    grad_x, st_n1_0 = _norm_mod_bwd("l0_norm1_bwd", x, nm[0], sc1, dh1, dx1)

    dmod = jnp.stack([
        jnp.stack([st_n1_0[0], st_n1_0[1], st_g1_0[0], st_n2_0[0], st_n2_0[1], dg2_0]),
        jnp.stack([st_n1_1[0], st_n1_1[1], st_g1_1[0], st_n2_1[0], st_n2_1[1], dg2_1])])
    st_conv = jnp.concatenate([st_cx, st_cb, st_cc], axis=1)
    small = dict(
        norm_mix=jnp.stack([st_n1_0[2], st_n1_1[2]]), norm_ffn=jnp.stack([st_n2_0[2], st_n2_1[2]]),
        pool_w=d_pool_w, pool_scale=st_pool[0], ssm_conv_w=st_conv[:SSM_CONV], ssm_conv_b=st_conv[SSM_CONV],
        ssm_dt_bias=st_dt[0, :SSM_HEADS], ssm_a_log=d_alog[:, 0, :8].reshape(SSM_HEADS),
        ssm_d=jnp.sum(st_ssd[1].reshape(SSM_HEADS, SSM_D_INNER // SSM_HEADS), axis=-1), ssm_norm=st_ssd[0],
        final_norm=st_loss[0])
    return loss, grad_x, got, dmod, small


WEIGHT_ORDER = ("ada_w", "ada_b", "norm_mix", "norm_ffn", "ffn_w1", "ffn_w2", "even_w_in", "pool_w", "pool_scale",
                "even_w_out", "ssm_w_in", "ssm_conv_w", "ssm_conv_b", "ssm_dt_bias", "ssm_a_log", "ssm_d", "ssm_norm",
                "ssm_w_out", "final_norm")
BIG_LAYERS = ((("even_w_in", 0), ("even_w_out", 0), ("ffn_w1", 0), ("ffn_w2", 0)),
              (("ssm_w_in", 0), ("ssm_w_out", 0), ("ffn_w1", 1), ("ffn_w2", 1)))
STACKED = ("ffn_w1", "ffn_w2")
ASSEMBLED = {"even_w_in": EVEN_IN_WIDTH, "even_w_out": D_MODEL, "ssm_w_in": SSM_IN_PAD}
SMALL_REPLICATED = ("norm_mix", "norm_ffn", "pool_w", "pool_scale", "ssm_dt_bias", "ssm_a_log", "ssm_d", "final_norm")
SMALL_SHARDED = ("ssm_conv_w", "ssm_conv_b", "ssm_norm")


def _pack(flat_parts, width, lead=()):
    flat = jnp.concatenate(flat_parts, axis=-1)
    n = flat.shape[-1]
    rows = -(-n // (8 * width)) * 8
    flat = jnp.pad(flat, [(0, 0)] * len(lead) + [(0, rows * width - n)])
    return flat.reshape(*lead, rows, width)


def _unpack(packed, shapes, lead=()):
    flat = packed.reshape(*lead, -1)
    out, off = [], 0
    for shp in shapes:
        n = math.prod(shp)
        out.append(flat[..., off:off + n].reshape(*lead, *shp))
        off += n
    return out


def kernel(x, c, ada_w, ada_b, norm_mix, norm_ffn, ffn_w1, ffn_w2, even_w_in, pool_w, pool_scale, even_w_out, ssm_w_in, ssm_conv_w, ssm_conv_b, ssm_dt_bias, ssm_a_log, ssm_d, ssm_norm, ssm_w_out, final_norm, loss_target, m_ada_w, m_ada_b, m_norm_mix, m_norm_ffn, m_ffn_w1, m_ffn_w2, m_even_w_in, m_pool_w, m_pool_scale, m_even_w_out, m_ssm_w_in, m_ssm_conv_w, m_ssm_conv_b, m_ssm_dt_bias, m_ssm_a_log, m_ssm_d, m_ssm_norm, m_ssm_w_out, m_final_norm, v_ada_w, v_ada_b, v_norm_mix, v_norm_ffn, v_ffn_w1, v_ffn_w2, v_even_w_in, v_pool_w, v_pool_scale, v_even_w_out, v_ssm_w_in, v_ssm_conv_w, v_ssm_conv_b, v_ssm_dt_bias, v_ssm_a_log, v_ssm_d, v_ssm_norm, v_ssm_w_out, v_final_norm):
    w = dict(ada_w=ada_w, ada_b=ada_b, norm_mix=norm_mix, norm_ffn=norm_ffn, ffn_w1=ffn_w1, ffn_w2=ffn_w2,
             even_w_in=even_w_in, pool_w=pool_w, pool_scale=pool_scale, even_w_out=even_w_out, ssm_w_in=ssm_w_in,
             ssm_conv_w=ssm_conv_w, ssm_conv_b=ssm_conv_b, ssm_dt_bias=ssm_dt_bias, ssm_a_log=ssm_a_log, ssm_d=ssm_d,
             ssm_norm=ssm_norm, ssm_w_out=ssm_w_out, final_norm=final_norm)
    m = dict(ada_w=m_ada_w, ada_b=m_ada_b, norm_mix=m_norm_mix, norm_ffn=m_norm_ffn, ffn_w1=m_ffn_w1, ffn_w2=m_ffn_w2,
             even_w_in=m_even_w_in, pool_w=m_pool_w, pool_scale=m_pool_scale, even_w_out=m_even_w_out,
             ssm_w_in=m_ssm_w_in, ssm_conv_w=m_ssm_conv_w, ssm_conv_b=m_ssm_conv_b, ssm_dt_bias=m_ssm_dt_bias,
             ssm_a_log=m_ssm_a_log, ssm_d=m_ssm_d, ssm_norm=m_ssm_norm, ssm_w_out=m_ssm_w_out, final_norm=m_final_norm)
    v = dict(ada_w=v_ada_w, ada_b=v_ada_b, norm_mix=v_norm_mix, norm_ffn=v_norm_ffn, ffn_w1=v_ffn_w1, ffn_w2=v_ffn_w2,
             even_w_in=v_even_w_in, pool_w=v_pool_w, pool_scale=v_pool_scale, even_w_out=v_even_w_out,
             ssm_w_in=v_ssm_w_in, ssm_conv_w=v_ssm_conv_w, ssm_conv_b=v_ssm_conv_b, ssm_dt_bias=v_ssm_dt_bias,
             ssm_a_log=v_ssm_a_log, ssm_d=v_ssm_d, ssm_norm=v_ssm_norm, ssm_w_out=v_ssm_w_out, final_norm=v_final_norm)
    d = D_MODEL
    me = _my_index()

    sharded_shapes = [(SSM_CONV, SSM_CONV_DIM // N_DEV), (SSM_CONV_DIM // N_DEV,), (SSM_D_INNER // N_DEV,)]
    shards = {(name, idx): w[name][idx].astype(BF16) for layer in BIG_LAYERS for name, idx in layer}
    small_in = _pack([c.reshape(-1)] + [w[k].reshape(-1) for k in SMALL_SHARDED], 128)
    got, got_even_in = _exchange("gather_first", [small_in, shards[("even_w_in", 0)]], scatter=False)
    w_even_in = _full_weight("even_w_in", got_even_in)
    c_all, conv_w_sh, conv_b_sh, norm_sh = _unpack(got, [(d,)] + sharded_shapes, lead=(N_DEV,))
    sm = {k: w[k] for k in SMALL_REPLICATED}
    sm["ssm_conv_w"] = conv_w_sh.transpose(1, 0, 2).reshape(SSM_CONV, SSM_CONV_DIM)
    sm["ssm_conv_b"] = conv_b_sh.reshape(SSM_CONV_DIM)
    sm["ssm_norm"] = norm_sh.reshape(SSM_D_INNER)

    ada_cols = 6 * d // N_DEV
    c_pad = jnp.pad(c_all, ((0, 16 - N_DEV), (0, 0)))
    mod_cols = _matmul(
        "ada_fwd", c_pad, ada_w, mode="nn", dims=(16, 2 * ada_cols, d), tiles=(16, ada_cols // 2, d), a_fn=_silu,
        b_spec=pl.BlockSpec((None, d, ada_cols // 2), lambda i, j, k: (j // 2, k, j % 2)),
        outs=[((16, 2 * ada_cols), F32)], epilogue=_epi_plain)[0]
    mod_got = _all_to_all("ada_exchange", mod_cols[:N_DEV].reshape(N_DEV, 2, ada_cols))
    mod = (mod_got.transpose(1, 0, 2).reshape(2, 6 * d) + ada_b).reshape(2, 6, d)

    loss, grad_x, grad_got, dmod, small = _device_step(x[0], loss_target[0], mod, sm, shards, w_even_in)
    loss = lax.psum(loss, ("x", "y", "c"))

    grads, delta, new_m, new_v = {}, {}, {}, {}

    def update(name, parts, shape=None):
        rows, cols = parts.shape[1:]
        res = _adamw("adamw_" + name, parts, w[name].reshape(rows, cols), m[name].reshape(rows, cols), v[name].reshape(rows, cols))
        return [r.reshape(w[name].shape if shape is None else shape) for r in res]

    rep_shapes = [w[k].shape for k in SMALL_REPLICATED]
    pack_rep = lambda tree: _pack([tree[k].reshape(-1) for k in SMALL_REPLICATED], 128)
    dmod_all, rep_got = _exchange("gather_small_grads", [dmod.reshape(2, 6 * d), pack_rep(small)], scatter=False)

    my_cols = lax.dynamic_slice_in_dim(dmod_all, me * ada_cols, ada_cols, axis=2).reshape(N_DEV, 2 * ada_cols)
    g_ada_w = _matmul(
        "ada_dw", c_pad, jnp.pad(my_cols, ((0, 16 - N_DEV), (0, 0))), mode="tn", dims=(d, 2 * ada_cols, 16),
        tiles=(1024, ada_cols // 2, 16), a_fn=_silu, outs=[((2, d, ada_cols), F32)],
        out_specs=[pl.BlockSpec((None, 1024, ada_cols // 2), lambda i, j, k: (j // 2, i, j % 2))], epilogue=_epi_plain)[0]
    grads["ada_w"], delta["ada_w"], new_m["ada_w"], new_v["ada_w"] = update("ada_w", g_ada_w.reshape(1, 2 * d, ada_cols))
    grads["ada_b"], delta["ada_b"], new_m["ada_b"], new_v["ada_b"] = update("ada_b", dmod_all)

    sh_pieces = [small["ssm_conv_w"].reshape(SSM_CONV, N_DEV, -1).transpose(1, 0, 2).reshape(N_DEV, -1),
                 small["ssm_conv_b"].reshape(N_DEV, -1), small["ssm_norm"].reshape(N_DEV, -1)]
    sh_got = _all_to_all("exchange_small_grads", _pack(sh_pieces, 128, lead=(N_DEV,)))

    stacked = {name: [None, None] for name in STACKED}
    for layer in BIG_LAYERS:
        for name, idx in layer:
            res = _adamw(f"adamw_{name}_{idx}", grad_got[(name, idx)], w[name][idx], m[name][idx], v[name][idx])
            if name in stacked:
                stacked[name][idx] = res
            else:
                grads[name], delta[name], new_m[name], new_v[name] = [r[None] for r in res]
    for name, per_layer in stacked.items():
        grads[name], delta[name], new_m[name], new_v[name] = [jnp.stack([per_layer[0][q], per_layer[1][q]]) for q in range(4)]

    rep_res = _adamw("adamw_small_replicated", rep_got, pack_rep(w), pack_rep(m), pack_rep(v))
    for dst, packed in zip((grads, delta, new_m, new_v), rep_res):
        for k, val in zip(SMALL_REPLICATED, _unpack(packed, rep_shapes)):
            dst[k] = val
    pack_sh = lambda tree: _pack([tree[k].reshape(-1) for k in SMALL_SHARDED], 128)
    sh_res = _adamw("adamw_small_sharded", sh_got, pack_sh(w), pack_sh(m), pack_sh(v))
    for dst, packed in zip((grads, delta, new_m, new_v), sh_res):
        for k, val in zip(SMALL_SHARDED, _unpack(packed, [w[k].shape for k in SMALL_SHARDED])):
            dst[k] = val

    out = [loss, grad_x[None]]
    for tree in (grads, delta, new_m, new_v):
        out.extend(tree[k] for k in WEIGHT_ORDER)
    return tuple(out)
```

```python
import functools
import math

import jax
import jax.numpy as jnp
from jax import lax
from jax.experimental import pallas as pl
from jax.experimental.pallas import tpu as pltpu

F32 = jnp.float32
BF16 = jnp.bfloat16

N_DEV = 8
D_MODEL = 2048
NORM_EPS = 1e-6
ATTN_BLOCK = 128
ATTN_GROUPS = 3
ATTN_GROUP_WIDTH = 512
ATTN_QKV_WIDTH = 3 * ATTN_GROUPS * ATTN_GROUP_WIDTH
POOL_GROUPS = 4
POOL_GROUP_WIDTH = 128
POOL_WIDTH = 512
POOL_HALO = 16
EVEN_IN_WIDTH = ATTN_QKV_WIDTH + POOL_WIDTH
EVEN_OUT_WIDTH = 1024
SSM_D_INNER = 4096
SSM_HEADS = 64
SSM_GROUPS = 8
SSM_GROUP_WIDTH = 512
SSM_STATE = 128
SSM_CHUNK = 128
SSM_CONV = 4
SSM_CONV_DIM = 6144
SSM_IN_WIDTH = 10304
SSM_IN_PAD = 10368
FFN_HIDDEN = 8192

ADAM_LR = 0.001
ADAM_B1 = 0.9
ADAM_B2 = 0.999
ADAM_EPS = 1e-08
ADAM_WD = 0.01
ADAM_STEP = 10

VMEM_LIMIT_BYTES = 56 * 1024 * 1024
NEG_BIG = -1e30

MESH_ID = pl.DeviceIdType.MESH
ANY = pl.BlockSpec(memory_space=pl.ANY)


def _cparams(*sem):
    return pltpu.CompilerParams(dimension_semantics=tuple(sem) if sem else None, vmem_limit_bytes=VMEM_LIMIT_BYTES)


def _dot(a, b):
    return lax.dot_general(a, b, (((1,), (0,)), ((), ())), preferred_element_type=F32)


def _dot_nt(a, b):
    return lax.dot_general(a, b, (((1,), (1,)), ((), ())), preferred_element_type=F32)


def _dot_tn(a, b):
    return lax.dot_general(a, b, (((0,), (0,)), ((), ())), preferred_element_type=F32)


def _split3(v):
    hi = v.astype(BF16)
    r1 = v - hi.astype(F32)
    mid = r1.astype(BF16)
    lo = (r1 - mid.astype(F32)).astype(BF16)
    return hi, mid, lo


def _dot3_left(const_bf16, v):
    hi, mid, lo = _split3(v)
    return _dot(const_bf16, hi) + _dot(const_bf16, mid) + _dot(const_bf16, lo)


def _dot3_right(v, const_bf16):
    hi, mid, lo = _split3(v)
    return _dot(hi, const_bf16) + _dot(mid, const_bf16) + _dot(lo, const_bf16)


def _iota(shape, dim):
    return lax.broadcasted_iota(jnp.int32, shape, dim)


def _sigmoid(x):
    return 1.0 / (1.0 + jnp.exp(-x))


def _peer(k):
    x, y, c = lax.axis_index("x"), lax.axis_index("y"), lax.axis_index("c")
    px = 1 - x if k & 4 else x
    py = 1 - y if k & 2 else y
    pc = 1 - c if k & 1 else c
    return (px, py, pc), 4 * px + 2 * py + pc


def _my_index():
    return 4 * lax.axis_index("x") + 2 * lax.axis_index("y") + lax.axis_index("c")


def _exchange(name, arrays, *, scatter):
    n = len(arrays)

    def body(*refs):
        ex = _Exchange(refs[:n], refs[n:2 * n], *refs[2 * n:], scatter)
        ex.start()
        ex.wait()

    return pl.pallas_call(
        body, name=name, out_shape=_exchange_out_shapes(arrays, scatter), in_specs=[ANY] * n, out_specs=[ANY] * n,
        scratch_shapes=_exchange_sems(n),
    )(*arrays)


def _exchange_out_shapes(arrays, scatter):
    return [jax.ShapeDtypeStruct((N_DEV,) + (tuple(a.shape[1:]) if scatter else tuple(a.shape)), a.dtype) for a in arrays]


def _exchange_sems(n):
    return [pltpu.SemaphoreType.DMA((n * (N_DEV - 1),)), pltpu.SemaphoreType.DMA((n * (N_DEV - 1),)),
            pltpu.SemaphoreType.DMA((n,))]


class _Exchange:
    def __init__(self, x_refs, out_refs, send_sems, recv_sems, local_sems, scatter):
        self.x_refs, self.out_refs, self.scatter = x_refs, out_refs, scatter
        self.send_sems, self.recv_sems, self.local_sems = send_sems, recv_sems, local_sems

    def _src(self, a, idx):
        return self.x_refs[a].at[idx] if self.scatter else self.x_refs[a]

    def _local(self, a):
        me = _my_index()
        return pltpu.make_async_copy(self._src(a, me), self.out_refs[a].at[me], self.local_sems.at[a])

    def _remote(self, a, k, landing):
        peer, peer_idx = _peer(k)
        sem = a * (N_DEV - 1) + k - 1
        slot = peer_idx if landing else _my_index()
        return pltpu.make_async_remote_copy(
            src_ref=self._src(a, peer_idx), dst_ref=self.out_refs[a].at[slot], send_sem=self.send_sems.at[sem],
            recv_sem=self.recv_sems.at[sem], device_id=peer, device_id_type=MESH_ID)

    def start(self):
        n = len(self.x_refs)
        for a in range(n):
            self._local(a).start()
        for k in range(1, N_DEV):
            for a in range(n):
                self._remote(a, k, False).start()

    def wait(self):
        n = len(self.x_refs)
        for k in range(1, N_DEV):
            for a in range(n):
                self._remote(a, k, True).wait_recv()
        for k in range(1, N_DEV):
            for a in range(n):
                self._remote(a, k, False).wait_send()
        for a in range(n):
            self._local(a).wait()


def _pcall(name, body, *, grid, in_specs, out_specs, out_shape, scratch_shapes, sem, args, ride=None, aliases=None):
    aliases = aliases or {}
    if ride is None:
        res = pl.pallas_call(body, name=name, grid=grid, in_specs=in_specs, out_specs=out_specs, out_shape=out_shape,
                             scratch_shapes=scratch_shapes, input_output_aliases=aliases, compiler_params=_cparams(*sem))(*args)
        return list(res), []
    arrays, scatter = ride
    n_in, n_out, n_scr, n_ride = len(in_specs), len(out_specs), len(scratch_shapes), len(arrays)

    def wrapped(*refs):
        ins, refs = refs[:n_in], refs[n_in:]
        ride_in, refs = refs[:n_ride], refs[n_ride:]
        outs, refs = refs[:n_out], refs[n_out:]
        ride_out, refs = refs[:n_ride], refs[n_ride:]
        scr, sems = refs[:n_scr], refs[n_scr:]
        exchange = _Exchange(ride_in, ride_out, *sems, scatter)
        ids = [pl.program_id(axis) for axis in range(len(grid))]
        first, last = ids[0] == 0, ids[0] == grid[0] - 1
        for axis in range(1, len(grid)):
            first, last = first & (ids[axis] == 0), last & (ids[axis] == grid[axis] - 1)

        @pl.when(first)
        def _():
            exchange.start()

        body(*ins, *outs, *scr)

        @pl.when(last)
        def _():
            exchange.wait()

    res = pl.pallas_call(
        wrapped, name=name, grid=grid, in_specs=list(in_specs) + [ANY] * n_ride, out_specs=list(out_specs) + [ANY] * n_ride,
        out_shape=list(out_shape) + _exchange_out_shapes(arrays, scatter),
        scratch_shapes=list(scratch_shapes) + _exchange_sems(n_ride), input_output_aliases=aliases,
        compiler_params=_cparams(*(("arbitrary",) * len(grid))))(*args, *arrays)
    return list(res[:n_out]), list(res[n_out:])


def _all_gather(name, x):
    return _exchange(name, [x], scatter=False)[0]


def _all_to_all(name, x):
    return _exchange(name, [x], scatter=True)[0]


def _assemble_cols(name, shards, width):
    _, k_dim, ns = shards.shape
    tr = 256

    def body(s_ref, o_ref):
        for dev in range(N_DEV):
            o_ref[:, ns * dev:ns * (dev + 1)] = s_ref[dev]
        if width > N_DEV * ns:
            o_ref[:, N_DEV * ns:] = jnp.zeros((tr, width - N_DEV * ns), o_ref.dtype)

    return pl.pallas_call(
        body, name=name, grid=(k_dim // tr,), in_specs=[pl.BlockSpec((N_DEV, tr, ns), lambda i: (0, i, 0))],
        out_specs=pl.BlockSpec((tr, width), lambda i: (i, 0)), out_shape=jax.ShapeDtypeStruct((k_dim, width), shards.dtype),
        compiler_params=_cparams("parallel"),
    )(shards)


def _split_cols(name, full, ns, row0=0, n_rows=None):
    k_dim, width = full.shape
    n_rows = k_dim if n_rows is None else n_rows
    tr = 256
    first = row0 // tr

    def body(f_ref, o_ref):
        for dev in range(N_DEV):
            o_ref[dev] = f_ref[:, ns * dev:ns * (dev + 1)]

    return pl.pallas_call(
        body, name=name, grid=(n_rows // tr,), in_specs=[pl.BlockSpec((tr, width), lambda i: (first + i, 0))],
        out_specs=pl.BlockSpec((N_DEV, tr, ns), lambda i: (0, i, 0)),
        out_shape=jax.ShapeDtypeStruct((N_DEV, n_rows, ns), full.dtype), compiler_params=_cparams("parallel"),
    )(full)


_DIMS = {"nn": (((1,), (0,)), ((), ())), "nt": (((1,), (1,)), ((), ())), "tn": (((0,), (0,)), ((), ()))}


def _matmul(name, a, b, *, mode, dims, tiles, outs, epilogue, a_spec=None, b_spec=None, out_specs=None,
            extras=(), extra_specs=(), a_fn=None, ride=None, b_slabs=0):
    m_dim, n_dim, k_dim = dims
    tm, tn, tk = tiles
    assert m_dim % tm == 0 and n_dim % tn == 0 and k_dim % tk == 0, (name, dims, tiles)
    grid = (m_dim // tm, n_dim // tn, k_dim // tk)
    nk = grid[2]
    if a_spec is None:
        a_spec = pl.BlockSpec((tk, tm), lambda i, j, k: (k, i)) if mode == "tn" else pl.BlockSpec((tm, tk), lambda i, j, k: (i, k))
    if b_spec is None:
        b_spec = pl.BlockSpec((tn, tk), lambda i, j, k: (j, k)) if mode == "nt" else pl.BlockSpec((tk, tn), lambda i, j, k: (k, j))
    if out_specs is None:
        out_specs = [pl.BlockSpec((tm, tn), lambda i, j, k: (i, j)) for _ in outs]
    n_ex, n_out = len(extras), len(outs)
    ride_arrays, scatter = ride if ride is not None else ((), False)
    n_ride = len(ride_arrays)
    dn = _DIMS[mode]

    def body(a_ref, b_ref, *rest):
        ex_refs, rest = rest[:n_ex], rest[n_ex:]
        ride_in, rest = rest[:n_ride], rest[n_ride:]
        out_refs, rest = rest[:n_out], rest[n_out:]
        ride_out, rest = rest[:n_ride], rest[n_ride:]
        i, j, k = pl.program_id(0), pl.program_id(1), pl.program_id(2)
        if n_ride:
            exchange = _Exchange(ride_in, ride_out, *rest[-3:], scatter)

            @pl.when((i == 0) & (j == 0) & (k == 0))
            def _():
                exchange.start()

        at = a_ref[...]
        if a_fn is not None:
            at = a_fn(at)
        at = at.astype(BF16)
        if b_slabs:
            width = tk // b_slabs
            part = lax.dot_general(at[:, :width], b_ref[0].astype(BF16), dn, preferred_element_type=F32)
            for slab_i in range(1, b_slabs):
                part += lax.dot_general(at[:, slab_i * width:(slab_i + 1) * width], b_ref[slab_i].astype(BF16), dn,
                                        preferred_element_type=F32)
        else:
            part = lax.dot_general(at, b_ref[...].astype(BF16), dn, preferred_element_type=F32)

        def finish(total):
            res = epilogue(total, *[e[...] for e in ex_refs])
            for r, o in zip(res, out_refs):
                o[...] = r.astype(o.dtype)

        if nk == 1:
            finish(part)
        else:
            acc = rest[0]

            @pl.when(k == 0)
            def _():
                acc[...] = part

            @pl.when(k > 0)
            def _():
                acc[...] += part

            @pl.when(k == nk - 1)
            def _():
                finish(acc[...])

        if n_ride:
            @pl.when((i == grid[0] - 1) & (j == grid[1] - 1) & (k == nk - 1))
            def _():
                exchange.wait()

    scratch = ([pltpu.VMEM((tm, tn), F32)] if nk > 1 else []) + (_exchange_sems(n_ride) if n_ride else [])
    sem = ("arbitrary",) * 3 if n_ride else ("parallel", "parallel", "arbitrary")
    return pl.pallas_call(
        body, name=name, grid=grid,
        in_specs=[a_spec, b_spec, *extra_specs] + [ANY] * n_ride, out_specs=list(out_specs) + [ANY] * n_ride,
        out_shape=[jax.ShapeDtypeStruct(s, d) for s, d in outs] + (_exchange_out_shapes(ride_arrays, scatter) if n_ride else []),
        scratch_shapes=scratch, compiler_params=_cparams(*sem),
    )(a, b, *extras, *ride_arrays)


def _epi_plain(acc):
    return (acc,)


def _epi_relu2(acc):
    return jnp.square(jnp.maximum(acc, 0.0)), acc


def _epi_resgate(acc, res, gate):
    return res + gate * acc, acc


def _epi_drelu2(acc, pre):
    return (acc * (2.0 * jnp.maximum(pre.astype(F32), 0.0)),)


def _silu(v):
    return v * _sigmoid(v)


ROW_TILE = 256


def _row_spec(width, tr=ROW_TILE):
    return pl.BlockSpec((tr, width), lambda i: (i, 0))


def _vec_spec(width):
    return pl.BlockSpec((1, width), lambda i: (0, 0))


def _stat_spec(width):
    return pl.BlockSpec((8, width), lambda i: (0, 0))


def _norm_mod_fwd(name, x, gain, scale, shift):
    s, d = x.shape

    def body(x_ref, g_ref, sc_ref, sh_ref, h_ref):
        xv = x_ref[...]
        r = lax.rsqrt(jnp.mean(xv * xv, axis=-1, keepdims=True) + NORM_EPS)
        h_ref[...] = ((xv * r * g_ref[...]) * (1.0 + sc_ref[...]) + sh_ref[...]).astype(BF16)

    return pl.pallas_call(
        body, name=name, grid=(s // ROW_TILE,),
        in_specs=[_row_spec(d), _vec_spec(d), _vec_spec(d), _vec_spec(d)], out_specs=_row_spec(d),
        out_shape=jax.ShapeDtypeStruct((s, d), BF16), compiler_params=_cparams("parallel"),
    )(x, gain, scale, shift)


GATE_STAT_ROW = 3


def _gate_below(dx, y_ref, gate_ref, dy_ref, st_ref):
    dy_ref[...] = (dx * gate_ref[...]).astype(BF16)
    st_ref[GATE_STAT_ROW:GATE_STAT_ROW + 1, :] += jnp.sum(dx * y_ref[...].astype(F32), axis=0, keepdims=True)


def _norm_mod_bwd(name, x, gain, scale, dh, dres, below=None):
    s, d = x.shape

    def body(x_ref, g_ref, sc_ref, dh_ref, dres_ref, *rest):
        dx_ref, st_ref = rest[-2:] if below is None else (rest[2], rest[3])

        @pl.when(pl.program_id(0) == 0)
        def _():
            st_ref[...] = jnp.zeros_like(st_ref)

        xv = x_ref[...]
        dhv = dh_ref[...].astype(F32)
        r = lax.rsqrt(jnp.mean(xv * xv, axis=-1, keepdims=True) + NORM_EPS)
        xh = xv * r
        n = xh * g_ref[...]
        dn = dhv * (1.0 + sc_ref[...])
        dxh = dn * g_ref[...]
        dx = dres_ref[...] + r * (dxh - xh * jnp.mean(dxh * xh, axis=-1, keepdims=True))
        dx_ref[...] = dx
        st_ref[0:1, :] += jnp.sum(dhv, axis=0, keepdims=True)
        st_ref[1:2, :] += jnp.sum(dhv * n, axis=0, keepdims=True)
        st_ref[2:3, :] += jnp.sum(dn * xh, axis=0, keepdims=True)
        if below is not None:
            _gate_below(dx, rest[0], rest[1], rest[4], st_ref)

    extra_in = [] if below is None else [_row_spec(d), _vec_spec(d)]
    extra_out = [] if below is None else [(_row_spec(d), jax.ShapeDtypeStruct((s, d), BF16))]
    return pl.pallas_call(
        body, name=name, grid=(s // ROW_TILE,),
        in_specs=[_row_spec(d), _vec_spec(d), _vec_spec(d), _row_spec(d), _row_spec(d)] + extra_in,
        out_specs=[_row_spec(d), _stat_spec(d)] + [spec for spec, _ in extra_out],
        out_shape=[jax.ShapeDtypeStruct((s, d), F32), jax.ShapeDtypeStruct((8, d), F32)] + [shp for _, shp in extra_out],
        compiler_params=_cparams("arbitrary"),
    )(x, gain, scale, dh, dres, *(below or ()))


def _loss_head(name, x, gain, target, below):
    s, d = x.shape

    def body(x_ref, g_ref, t_ref, y_ref, gate_ref, dx_ref, st_ref, dy_ref):
        @pl.when(pl.program_id(0) == 0)
        def _():
            st_ref[...] = jnp.zeros_like(st_ref)

        xv = x_ref[...]
        r = lax.rsqrt(jnp.mean(xv * xv, axis=-1, keepdims=True) + NORM_EPS)
        xh = xv * r
        err = xh * g_ref[...] - t_ref[...]
        dyf = err * (1.0 / d)
        dxh = dyf * g_ref[...]
        dx = r * (dxh - xh * jnp.mean(dxh * xh, axis=-1, keepdims=True))
        dx_ref[...] = dx
        st_ref[0:1, :] += jnp.sum(dyf * xh, axis=0, keepdims=True)
        st_ref[1:2, :] += jnp.sum(err * err, axis=0, keepdims=True) * (0.5 / d)
        _gate_below(dx, y_ref, gate_ref, dy_ref, st_ref)

    return pl.pallas_call(
        body, name=name, grid=(s // ROW_TILE,),
        in_specs=[_row_spec(d), _vec_spec(d), _row_spec(d), _row_spec(d), _vec_spec(d)],
        out_specs=[_row_spec(d), _stat_spec(d), _row_spec(d)],
        out_shape=[jax.ShapeDtypeStruct((s, d), F32), jax.ShapeDtypeStruct((8, d), F32), jax.ShapeDtypeStruct((s, d), BF16)],
        compiler_params=_cparams("arbitrary"),
    )(x, gain, target, *below)


def _adamw(name, parts, w, m, v):
    n_parts, rows, cols = parts.shape
    tr = rows
    for cand in (512, 256, 128, 64, 32, 16, 8):
        if rows % cand == 0 and cand * cols * 4 <= 2 * 1024 * 1024:
            tr = cand
            break
    c1 = 1.0 - ADAM_B1 ** ADAM_STEP
    c2 = 1.0 - ADAM_B2 ** ADAM_STEP

    def body(p_ref, w_ref, m_ref, v_ref, g_out, d_out, m_out, v_out):
        g = p_ref[0].astype(F32)
        for i in range(1, n_parts):
            g = g + p_ref[i].astype(F32)
        m_new = ADAM_B1 * m_ref[...] + (1.0 - ADAM_B1) * g
        v_new = ADAM_B2 * v_ref[...] + (1.0 - ADAM_B2) * (g * g)
        g_out[...] = g
        m_out[...] = m_new
        v_out[...] = v_new
        d_out[...] = -ADAM_LR * ((m_new / c1) / (jnp.sqrt(v_new / c2) + ADAM_EPS) + ADAM_WD * w_ref[...])

    spec = pl.BlockSpec((tr, cols), lambda i: (i, 0))
    return pl.pallas_call(
        body, name=name, grid=(rows // tr,),
        in_specs=[pl.BlockSpec((n_parts, tr, cols), lambda i: (0, i, 0)), spec, spec, spec],
        out_specs=[spec, spec, spec, spec],
        out_shape=[jax.ShapeDtypeStruct((rows, cols), F32)] * 4, compiler_params=_cparams("parallel"),
    )(parts, w, m, v)


ATTN_DILATIONS = (1, 4, 16)


def _attn_fwd(proj, g, ride=None):
    s = proj.shape[1]
    dil = ATTN_DILATIONS[g]
    rows, nb = s // dil, s // dil // ATTN_BLOCK

    def body(q_ref, kp_ref, kc_ref, vp_ref, vc_ref, o_ref, l_ref):
        first = pl.program_id(1) == 0
        qi, kj = _iota((128, 128), 0), _iota((128, 128), 1)
        mask_c = kj <= qi
        mask_p = jnp.logical_and(kj >= qi, jnp.logical_not(first))
        low = kj < 64
        for p in range(4):
            sl = slice(128 * p, 128 * p + 128)
            q, kp, kc, vp, vc = q_ref[:, sl], kp_ref[:, sl], kc_ref[:, sl], vp_ref[:, sl], vc_ref[:, sl]
            o_pair = jnp.zeros((128, 128), F32)
            l_pair = jnp.zeros((128, 128), F32)
            for half in range(2):
                hm = low if half == 0 else jnp.logical_not(low)
                qm = jnp.where(hm, q, jnp.zeros_like(q))
                sc = jnp.where(mask_c, _dot_nt(qm, kc) * 0.125, NEG_BIG)
                sp = jnp.where(mask_p, _dot_nt(qm, kp) * 0.125, NEG_BIG)
                m = jnp.maximum(jnp.max(sc, axis=-1, keepdims=True), jnp.max(sp, axis=-1, keepdims=True))
                pc, pp = jnp.exp(sc - m), jnp.exp(sp - m)
                den = jnp.sum(pc, axis=-1, keepdims=True) + jnp.sum(pp, axis=-1, keepdims=True)
                oh = _dot((pc / den).astype(BF16), vc) + _dot((pp / den).astype(BF16), vp)
                o_pair = jnp.where(hm, oh, o_pair)
                l_pair = jnp.where(hm, m + jnp.log(den), l_pair)
            o_ref[:, sl] = o_pair
            l_ref[:, sl] = l_pair

    view = proj[3 * g:3 * g + 3].reshape(3, rows, dil * ATTN_GROUP_WIDTH)
    blk = (None, ATTN_BLOCK, ATTN_GROUP_WIDTH)
    prev = lambda j: jnp.maximum(j - 1, 0)
    out_blk = pl.BlockSpec((ATTN_BLOCK, ATTN_GROUP_WIDTH), lambda r, j: (j, r))
    (o, lse), got = _pcall(
        f"attn_fwd_g{g}", body, grid=(dil, nb),
        in_specs=[pl.BlockSpec(blk, lambda r, j: (0, j, r)),
                  pl.BlockSpec(blk, lambda r, j: (1, prev(j), r)), pl.BlockSpec(blk, lambda r, j: (1, j, r)),
                  pl.BlockSpec(blk, lambda r, j: (2, prev(j), r)), pl.BlockSpec(blk, lambda r, j: (2, j, r))],
        out_specs=[out_blk] * 2, out_shape=[jax.ShapeDtypeStruct((rows, dil * ATTN_GROUP_WIDTH), F32)] * 2,
        scratch_shapes=[], sem=("parallel", "parallel"), args=(view,) * 5, ride=ride)
    return o.reshape(s, ATTN_GROUP_WIDTH), lse.reshape(s, ATTN_GROUP_WIDTH), got


def _attn_bwd(proj, g, do, lse, cc, ride=None):
    s = proj.shape[1]
    dil = ATTN_DILATIONS[g]
    rows, nblk = s // dil, s // dil // ATTN_BLOCK

    def body(q_ref, kp_ref, kc_ref, vp_ref, vc_ref, do_ref, l_ref, c_ref, dq_ref, dk_ref, dv_ref, ck, cv):
        j = pl.program_id(1)
        valid = j < nblk
        first = jnp.minimum(j, nblk - 1) == 0

        @pl.when(j == 0)
        def _():
            ck[...] = jnp.zeros_like(ck)
            cv[...] = jnp.zeros_like(cv)

        qi, kj = _iota((128, 128), 0), _iota((128, 128), 1)
        mask_c = jnp.logical_and(kj <= qi, valid)
        mask_p = jnp.logical_and(jnp.logical_and(kj >= qi, jnp.logical_not(first)), valid)
        low = kj < 64
        for p in range(4):
            sl = slice(128 * p, 128 * p + 128)
            q, kp, kc, vp, vc, dov = q_ref[:, sl], kp_ref[:, sl], kc_ref[:, sl], vp_ref[:, sl], vc_ref[:, sl], do_ref[:, sl]
            lse_pair, c_pair = l_ref[:, sl], c_ref[:, sl]
            dq_pair = jnp.zeros((128, 128), F32)
            dkc = jnp.zeros((128, 128), F32)
            dkp = jnp.zeros((128, 128), F32)
            dvc = jnp.zeros((128, 128), F32)
            dvp = jnp.zeros((128, 128), F32)
            for half in range(2):
                hm = low if half == 0 else jnp.logical_not(low)
                col = slice(64 * half, 64 * half + 1)
                lse_h, c_h = lse_pair[:, col], c_pair[:, col]
                qm = jnp.where(hm, q, jnp.zeros_like(q))
                dom = jnp.where(hm, dov, jnp.zeros_like(dov))
                pc = jnp.exp(jnp.where(mask_c, _dot_nt(qm, kc) * 0.125, NEG_BIG) - lse_h)
                pp = jnp.exp(jnp.where(mask_p, _dot_nt(qm, kp) * 0.125, NEG_BIG) - lse_h)
                dsc = (pc * (_dot_nt(dom, vc) + c_h) * 0.125).astype(BF16)
                dsp = (pp * (_dot_nt(dom, vp) + c_h) * 0.125).astype(BF16)
                dq_pair = jnp.where(hm, _dot(dsc, kc) + _dot(dsp, kp), dq_pair)
                dkc += _dot_tn(dsc, qm)
                dkp += _dot_tn(dsp, qm)
                dvc += _dot_tn(pc.astype(BF16), dom)
                dvp += _dot_tn(pp.astype(BF16), dom)

            @pl.when(valid)
            def _():
                dq_ref[:, sl] = dq_pair.astype(BF16)

            dk_ref[:, sl] = (ck[:, sl] + dkp).astype(BF16)
            dv_ref[:, sl] = (cv[:, sl] + dvp).astype(BF16)
            ck[:, sl] = dkc
            cv[:, sl] = dvc

    wide = dil * ATTN_GROUP_WIDTH
    view = proj[3 * g:3 * g + 3].reshape(3, rows, wide)
    blk = (None, ATTN_BLOCK, ATTN_GROUP_WIDTH)
    flat = (ATTN_BLOCK, ATTN_GROUP_WIDTH)
    cur = lambda j: jnp.minimum(j, nblk - 1)
    prev = lambda j: jnp.maximum(jnp.minimum(j, nblk - 1) - 1, 0)
    out_prev = lambda j: jnp.maximum(j - 1, 0)
    (dq, dk, dv), got = _pcall(
        f"attn_bwd_g{g}", body, grid=(dil, nblk + 1),
        in_specs=[pl.BlockSpec(blk, lambda r, j: (0, cur(j), r)),
                  pl.BlockSpec(blk, lambda r, j: (1, prev(j), r)), pl.BlockSpec(blk, lambda r, j: (1, cur(j), r)),
                  pl.BlockSpec(blk, lambda r, j: (2, prev(j), r)), pl.BlockSpec(blk, lambda r, j: (2, cur(j), r)),
                  pl.BlockSpec(flat, lambda r, j: (cur(j), r)), pl.BlockSpec(flat, lambda r, j: (cur(j), r)),
                  pl.BlockSpec(flat, lambda r, j: (cur(j), r))],
        out_specs=[pl.BlockSpec(flat, lambda r, j: (cur(j), r)),
                   pl.BlockSpec(flat, lambda r, j: (out_prev(j), r)), pl.BlockSpec(flat, lambda r, j: (out_prev(j), r))],
        out_shape=[jax.ShapeDtypeStruct((rows, wide), BF16)] * 3,
        scratch_shapes=[pltpu.VMEM((ATTN_BLOCK, ATTN_GROUP_WIDTH), F32)] * 2, sem=("parallel", "arbitrary"),
        args=(view,) * 5 + (do.reshape(rows, wide), lse.reshape(rows, wide), cc.reshape(rows, wide)), ride=ride)
    return [t.reshape(s, ATTN_GROUP_WIDTH) for t in (dq, dk, dv)], got


MP_TILE = 256


def _merge_weights(l_refs):
    l0, l1, l2 = l_refs[0][...], l_refs[1][...], l_refs[2][...]
    m = jnp.maximum(jnp.maximum(l0, l1), l2)
    e0, e1, e2 = jnp.exp(l0 - m), jnp.exp(l1 - m), jnp.exp(l2 - m)
    den = e0 + e1 + e2
    return e0 / den, e1 / den, e2 / den


def _pool_diff(ucat, gi, tok):
    window = 2 << gi
    ug = ucat[:, 128 * gi:128 * gi + 128]
    acc, shift = ug, 1
    while shift < window:
        acc = acc + pltpu.roll(acc, shift, 0)
        shift *= 2
    cnt = jnp.minimum(tok + 1, window).astype(F32)
    return acc[POOL_HALO:, :] / cnt - ug[POOL_HALO:, :]


def _merge_pool_fwd(o, lse, proj, pool_w, pool_scale):
    s = o[0].shape[0]
    tr = MP_TILE

    def body(o0_ref, o1_ref, o2_ref, l0_ref, l1_ref, l2_ref, u_ref, uh_ref, pw_ref, ps_ref, cat_ref):
        i = pl.program_id(0)
        w0, w1, w2 = _merge_weights((l0_ref, l1_ref, l2_ref))
        cat_ref[:, 0:512] = (w0 * o0_ref[...] + w1 * o1_ref[...] + w2 * o2_ref[...]).astype(BF16)
        halo = jnp.where(i > 0, uh_ref[...].astype(F32), 0.0)
        ucat = jnp.concatenate([halo, u_ref[...].astype(F32)], axis=0)
        tok = i * tr + _iota((tr, 1), 0)
        for gi in range(POOL_GROUPS):
            sl = slice(128 * gi, 128 * gi + 128)
            diff = _pool_diff(ucat, gi, tok)
            yg = _dot(diff.astype(BF16), pw_ref[gi].astype(BF16)) * ps_ref[:, sl]
            cat_ref[:, 512 + 128 * gi:640 + 128 * gi] = yg.astype(BF16)

    return pl.pallas_call(
        body, name="merge_pool_fwd", grid=(s // tr,),
        in_specs=[pl.BlockSpec((tr, 512), lambda i: (i, 0))] * 6 + [
                  pl.BlockSpec((None, tr, 512), lambda i: (9, i, 0)),
                  pl.BlockSpec((None, POOL_HALO, 512), lambda i: (9, jnp.maximum(i * (tr // POOL_HALO) - 1, 0), 0)),
                  pl.BlockSpec((4, 128, 128), lambda i: (0, 0, 0)), pl.BlockSpec((1, 512), lambda i: (0, 0))],
        out_specs=pl.BlockSpec((tr, 1024), lambda i: (i, 0)),
        out_shape=jax.ShapeDtypeStruct((s, EVEN_OUT_WIDTH), BF16), compiler_params=_cparams("parallel"),
    )(*o, *lse, proj, proj, pool_w, pool_scale)


def _merge_pool_bwd(dcat, o, lse, proj, pool_w, pool_scale, head_sum):
    s = o[0].shape[0]
    tr = MP_TILE
    n_tiles = s // tr

    def body(da_ref, dp_ref, dph_ref, o0_ref, o1_ref, o2_ref, l0_ref, l1_ref, l2_ref, u_ref, uh_ref, pw_ref, ps_ref,
             hs_ref, do0_ref, do1_ref, do2_ref, cc0_ref, cc1_ref, cc2_ref, du_ref, dpw_ref, st_ref):
        i = pl.program_id(0)

        @pl.when(i == 0)
        def _():
            dpw_ref[...] = jnp.zeros_like(dpw_ref)
            st_ref[...] = jnp.zeros_like(st_ref)

        ws = _merge_weights((l0_ref, l1_ref, l2_ref))
        da = da_ref[...]
        attn = ws[0] * o0_ref[...] + ws[1] * o1_ref[...] + ws[2] * o2_ref[...]
        per_head = _dot3_right(da * attn, hs_ref[...])
        for wg, do_ref, cc_ref in zip(ws, (do0_ref, do1_ref, do2_ref), (cc0_ref, cc1_ref, cc2_ref)):
            do_ref[...] = (wg * da).astype(BF16)
            cc_ref[...] = -wg * per_head

        halo = jnp.where(i > 0, uh_ref[...].astype(F32), 0.0)
        ucat = jnp.concatenate([halo, u_ref[...].astype(F32)], axis=0)
        tok = i * tr + _iota((tr, 1), 0)
        dyp = dp_ref[...]
        dnext = jnp.where(i < n_tiles - 1, dph_ref[...], 0.0)
        dyp_ext = jnp.concatenate([dyp, dnext], axis=0)
        tok_ext = i * tr + _iota((tr + POOL_HALO, 1), 0)
        for gi in range(POOL_GROUPS):
            sl = slice(128 * gi, 128 * gi + 128)
            window = 2 << gi
            pw16 = pw_ref[gi].astype(BF16)
            d16 = _pool_diff(ucat, gi, tok).astype(BF16)
            st_ref[0:1, sl] += jnp.sum(dyp[:, sl] * _dot(d16, pw16), axis=0, keepdims=True)
            dpw_ref[gi] += _dot_tn(d16, (dyp[:, sl] * ps_ref[:, sl]).astype(BF16))
            dd = _dot_nt((dyp_ext[:, sl] * ps_ref[:, sl]).astype(BF16), pw16)
            acc = dd / jnp.minimum(tok_ext + 1, window).astype(F32)
            shift = 1
            while shift < window:
                acc = acc + pltpu.roll(acc, tr + POOL_HALO - shift, 0)
                shift *= 2
            du_ref[:, sl] = (acc[:tr, :] - dd[:tr, :]).astype(BF16)

    halo_blocks = tr // POOL_HALO
    return pl.pallas_call(
        body, name="merge_pool_bwd", grid=(n_tiles,),
        in_specs=[pl.BlockSpec((tr, 512), lambda i: (i, 0)), pl.BlockSpec((tr, 512), lambda i: (i, 1)),
                  pl.BlockSpec((POOL_HALO, 512), lambda i: (jnp.minimum((i + 1) * halo_blocks, s // POOL_HALO - 1), 1))]
                 + [pl.BlockSpec((tr, 512), lambda i: (i, 0))] * 6 + [
                  pl.BlockSpec((None, tr, 512), lambda i: (9, i, 0)),
                  pl.BlockSpec((None, POOL_HALO, 512), lambda i: (9, jnp.maximum(i * halo_blocks - 1, 0), 0)),
                  pl.BlockSpec((4, 128, 128), lambda i: (0, 0, 0)), pl.BlockSpec((1, 512), lambda i: (0, 0)),
                  pl.BlockSpec((512, 512), lambda i: (0, 0))],
        out_specs=[pl.BlockSpec((tr, 512), lambda i: (i, 0))] * 7 + [
                   pl.BlockSpec((4, 128, 128), lambda i: (0, 0, 0)), pl.BlockSpec((8, 512), lambda i: (0, 0))],
        out_shape=[jax.ShapeDtypeStruct((s, 512), BF16)] * 3 + [jax.ShapeDtypeStruct((s, 512), F32)] * 3 + [
                   jax.ShapeDtypeStruct((s, 512), BF16), jax.ShapeDtypeStruct((4, 128, 128), F32),
                   jax.ShapeDtypeStruct((8, 512), F32)],
        compiler_params=_cparams("arbitrary"),
    )(dcat, dcat, dcat, *o, *lse, proj, proj, pool_w, pool_scale, head_sum)


CONV_TILE = 1024
CONV_HALO = 8
XBC_BLOCK0 = SSM_D_INNER // 512
DT_BLOCK = (SSM_D_INNER + SSM_CONV_DIM) // 128


def _conv_fwd(proj, conv_w, conv_b):
    s = proj.shape[0]
    tr = CONV_TILE

    def body(x_ref, xh_ref, w_ref, b_ref, pre_ref, act_ref):
        i = pl.program_id(0)
        xcat = jnp.concatenate([jnp.where(i > 0, xh_ref[...], 0.0), x_ref[...]], axis=0)
        w = w_ref[...]
        pre = b_ref[...] + w[3:4, :] * xcat[CONV_HALO:, :]
        for back in range(1, SSM_CONV):
            pre = pre + w[3 - back:4 - back, :] * pltpu.roll(xcat, back, 0)[CONV_HALO:, :]
        pre_ref[...] = pre
        act_ref[...] = pre * _sigmoid(pre)

    hb = tr // CONV_HALO
    return pl.pallas_call(
        body, name="conv_fwd", grid=(s // tr, SSM_CONV_DIM // 512),
        in_specs=[pl.BlockSpec((tr, 512), lambda i, j: (i, XBC_BLOCK0 + j)),
                  pl.BlockSpec((CONV_HALO, 512), lambda i, j: (jnp.maximum(i * hb - 1, 0), XBC_BLOCK0 + j)),
                  pl.BlockSpec((SSM_CONV, 512), lambda i, j: (0, j)), pl.BlockSpec((1, 512), lambda i, j: (0, j))],
        out_specs=[pl.BlockSpec((tr, 512), lambda i, j: (i, j))] * 2,
        out_shape=[jax.ShapeDtypeStruct((s, SSM_CONV_DIM), F32)] * 2, compiler_params=_cparams("parallel", "parallel"),
    )(proj, proj, conv_w, conv_b)


def _conv_bwd(name, dact, pre, proj, conv_w, dproj, *, block0):
    s, width = dact.shape
    tr = CONV_TILE
    n_tiles = s // tr
    hb = tr // CONV_HALO

    def body(da_ref, dah_ref, pre_ref, preh_ref, x_ref, xh_ref, w_ref, _, dp_ref, st_ref):
        i = pl.program_id(1)

        @pl.when(i == 0)
        def _():
            st_ref[...] = jnp.zeros_like(st_ref)

        da_ext = jnp.concatenate([da_ref[...], jnp.where(i < n_tiles - 1, dah_ref[...], 0.0)], axis=0)
        pre_ext = jnp.concatenate([pre_ref[...], preh_ref[...]], axis=0)
        sg = _sigmoid(pre_ext)
        dpre_ext = da_ext * (sg * (1.0 + pre_ext * (1.0 - sg)))
        w = w_ref[...]
        draw = w[3:4, :] * dpre_ext[:tr, :]
        for ahead in range(1, SSM_CONV):
            draw = draw + w[3 - ahead:4 - ahead, :] * pltpu.roll(dpre_ext, tr + CONV_HALO - ahead, 0)[:tr, :]
        dp_ref[...] = draw.astype(BF16)
        dpre = dpre_ext[:tr, :]
        xcat = jnp.concatenate([jnp.where(i > 0, xh_ref[...], 0.0), x_ref[...]], axis=0)
        st_ref[3:4, :] += jnp.sum(dpre * xcat[CONV_HALO:, :], axis=0, keepdims=True)
        for back in range(1, SSM_CONV):
            st_ref[3 - back:4 - back, :] += jnp.sum(dpre * pltpu.roll(xcat, back, 0)[CONV_HALO:, :], axis=0, keepdims=True)
        st_ref[4:5, :] += jnp.sum(dpre, axis=0, keepdims=True)

    nxt = lambda i: jnp.minimum((i + 1) * hb, s // CONV_HALO - 1)
    prv = lambda i: jnp.maximum(i * hb - 1, 0)
    return pl.pallas_call(
        body, name=name, grid=(width // 512, n_tiles),
        in_specs=[pl.BlockSpec((tr, 512), lambda j, i: (i, j)), pl.BlockSpec((CONV_HALO, 512), lambda j, i: (nxt(i), j)),
                  pl.BlockSpec((tr, 512), lambda j, i: (i, block0 + j)),
                  pl.BlockSpec((CONV_HALO, 512), lambda j, i: (nxt(i), block0 + j)),
                  pl.BlockSpec((tr, 512), lambda j, i: (i, XBC_BLOCK0 + block0 + j)),
                  pl.BlockSpec((CONV_HALO, 512), lambda j, i: (prv(i), XBC_BLOCK0 + block0 + j)),
                  pl.BlockSpec((SSM_CONV, 512), lambda j, i: (0, block0 + j)), ANY],
        out_specs=[pl.BlockSpec((tr, 512), lambda j, i: (i, XBC_BLOCK0 + block0 + j)),
                   pl.BlockSpec((8, 512), lambda j, i: (0, j))],
        out_shape=[jax.ShapeDtypeStruct(dproj.shape, BF16), jax.ShapeDtypeStruct((8, width), F32)],
        input_output_aliases={7: 0}, compiler_params=_cparams("parallel", "arbitrary"),
    )(dact, dact, pre, pre, proj, proj, conv_w, dproj)


def _dt_fwd(proj, dt_bias):
    s = proj.shape[0]

    def body(x_ref, b_ref, o_ref):
        v = x_ref[...] + b_ref[...]
        o_ref[...] = jnp.maximum(v, 0.0) + jnp.log(1.0 + jnp.exp(-jnp.abs(v)))

    return pl.pallas_call(
        body, name="dt_fwd", grid=(s // CONV_TILE,),
        in_specs=[pl.BlockSpec((CONV_TILE, 128), lambda i: (i, DT_BLOCK)), pl.BlockSpec((1, 128), lambda i: (0, 0))],
        out_specs=pl.BlockSpec((CONV_TILE, 128), lambda i: (i, 0)),
        out_shape=jax.ShapeDtypeStruct((s, 128), F32), compiler_params=_cparams("parallel"),
    )(proj, dt_bias)


def _dt_bwd(ddt, proj, dt_bias, dproj):
    s = proj.shape[0]

    def body(d_ref, x_ref, b_ref, _, dp_ref, st_ref):
        @pl.when(pl.program_id(0) == 0)
        def _():
            st_ref[...] = jnp.zeros_like(st_ref)

        draw = d_ref[...] * _sigmoid(x_ref[...] + b_ref[...])
        dp_ref[...] = draw.astype(BF16)
        st_ref[0:1, :] += jnp.sum(draw, axis=0, keepdims=True)

    return pl.pallas_call(
        body, name="dt_bwd", grid=(s // CONV_TILE,),
        in_specs=[pl.BlockSpec((CONV_TILE, 128), lambda i: (i, 0)), pl.BlockSpec((CONV_TILE, 128), lambda i: (i, DT_BLOCK)),
                  pl.BlockSpec((1, 128), lambda i: (0, 0)), ANY],
        out_specs=[pl.BlockSpec((CONV_TILE, 128), lambda i: (i, DT_BLOCK)), pl.BlockSpec((8, 128), lambda i: (0, 0))],
        out_shape=[jax.ShapeDtypeStruct(dproj.shape, BF16), jax.ShapeDtypeStruct((8, 128), F32)],
        input_output_aliases={3: 0}, compiler_params=_cparams("arbitrary"),
    )(ddt, proj, dt_bias, dproj)


def _ssd_common(x_ref, b_ref, c_ref, dt_ref, al_ref):
    row, col = _iota((128, 128), 0), _iota((128, 128), 1)
    tril = row >= col
    expand = jnp.where((_iota((128, 512), 1) >> 6) == _iota((128, 512), 0), 1.0, 0.0).astype(BF16)
    dt = dt_ref[...]
    a_neg = -jnp.exp(al_ref[...])
    a_col = _dot3_left(jnp.where(tril, 1.0, 0.0).astype(BF16), dt * a_neg)
    a_exp = _dot3_right(a_col, expand)
    dt_exp = _dot3_right(dt, expand)
    x = x_ref[...]
    return dict(tril=tril, col=col, expand=expand, dt=dt, a_neg=a_neg, a_col=a_col, a_row=a_col.T, a_exp=a_exp,
                dt_exp=dt_exp, a_last=a_exp[127:128, :], x=x, xd=x * dt_exp,
                b16=b_ref[...].astype(BF16), c16=c_ref[...].astype(BF16))


def _ssd_specs(nc, order):
    return [pl.BlockSpec((SSM_CHUNK, 512), lambda g, c: (order(c), g)),
            pl.BlockSpec((SSM_CHUNK, 128), lambda g, c: (order(c), SSM_D_INNER // 128 + g)),
            pl.BlockSpec((SSM_CHUNK, 128), lambda g, c: (order(c), SSM_D_INNER // 128 + SSM_GROUPS + g)),
            pl.BlockSpec((SSM_CHUNK, 512), lambda g, c: (order(c), g)),
            pl.BlockSpec((None, SSM_CHUNK, 128), lambda g, c: (g, order(c), 0)),
            pl.BlockSpec((None, 1, 128), lambda g, c: (g, 0, 0)),
            pl.BlockSpec((1, 512), lambda g, c: (0, g)), pl.BlockSpec((1, 512), lambda g, c: (0, g))]


def _ssd_fwd(act, proj, dtc, a_log, d_skip, norm_g, ride=None):
    s = act.shape[0]
    nc = s // SSM_CHUNK

    def body(x_ref, b_ref, c_ref, z_ref, dt_ref, al_ref, dsk_ref, ng_ref, y_ref, yn_ref, hin_ref, h_sc):
        @pl.when(pl.program_id(1) == 0)
        def _():
            h_sc[...] = jnp.zeros_like(h_sc)

        q = _ssd_common(x_ref, b_ref, c_ref, dt_ref, al_ref)
        gmat = _dot_nt(q["c16"], q["b16"])
        h_in = h_sc[...]
        hin_ref[...] = h_in
        zmat = _dot(q["c16"], h_in.astype(BF16))
        xd16 = q["xd"].astype(BF16)
        low = q["col"] < 64
        pieces = []
        for p in range(4):
            xs = xd16[:, 128 * p:128 * p + 128]
            acc = jnp.zeros((128, 128), F32)
            for half in range(2):
                r = 2 * p + half
                decay = jnp.exp(jnp.where(q["tril"], q["a_col"][:, r:r + 1] - q["a_row"][r:r + 1, :], NEG_BIG))
                hm = low if half == 0 else jnp.logical_not(low)
                acc += _dot((gmat * decay).astype(BF16), jnp.where(hm, xs, jnp.zeros_like(xs)))
            pieces.append(acc)
        y = jnp.concatenate(pieces, axis=1) + zmat * jnp.exp(q["a_exp"]) + q["x"] * dsk_ref[...]
        y_ref[...] = y
        w16 = (q["xd"] * jnp.exp(q["a_last"] - q["a_exp"])).astype(BF16)
        h_sc[...] = h_in * jnp.exp(q["a_last"]) + _dot_tn(q["b16"], w16)
        z = z_ref[...]
        yg = y * (z * _sigmoid(z))
        rr = lax.rsqrt(jnp.mean(yg * yg, axis=-1, keepdims=True) + NORM_EPS)
        yn_ref[...] = (yg * rr * ng_ref[...]).astype(BF16)

    blk = pl.BlockSpec((SSM_CHUNK, 512), lambda g, c: (c, g))
    return _pcall(
        "ssd_fwd", body, grid=(SSM_GROUPS, nc), in_specs=_ssd_specs(nc, lambda c: c),
        out_specs=[blk, blk, pl.BlockSpec((None, None, SSM_STATE, 512), lambda g, c: (g, c, 0, 0))],
        out_shape=[jax.ShapeDtypeStruct((s, SSM_D_INNER), F32), jax.ShapeDtypeStruct((s, SSM_D_INNER), BF16),
                   jax.ShapeDtypeStruct((SSM_GROUPS, nc, SSM_STATE, 512), F32)],
        scratch_shapes=[pltpu.VMEM((SSM_STATE, 512), F32)], sem=("parallel", "arbitrary"),
        args=(act, act, act, proj, dtc, a_log, d_skip, norm_g), ride=ride)


def _ssd_bwd(act, proj, dtc, a_log, d_skip, norm_g, y, h_in_all, dyn, ride=None):
    s = act.shape[0]
    nc = s // SSM_CHUNK

    def body(x_ref, b_ref, c_ref, z_ref, dt_ref, al_ref, dsk_ref, ng_ref, y_ref, hin_ref, dyn_ref,
             dz_ref, dx_ref, db_ref, dc_ref, ddt_ref, st_ref, dal_ref, dh_sc):
        @pl.when(pl.program_id(1) == 0)
        def _():
            dh_sc[...] = jnp.zeros_like(dh_sc)
            st_ref[...] = jnp.zeros_like(st_ref)
            dal_ref[...] = jnp.zeros_like(dal_ref)

        q = _ssd_common(x_ref, b_ref, c_ref, dt_ref, al_ref)
        x, xd, b16, c16 = q["x"], q["xd"], q["b16"], q["c16"]
        z, yv, dyn_v = z_ref[...], y_ref[...], dyn_ref[...]
        sig = _sigmoid(z)
        sil = z * sig
        yg = yv * sil
        rr = lax.rsqrt(jnp.mean(yg * yg, axis=-1, keepdims=True) + NORM_EPS)
        st_ref[0:1, :] += jnp.sum(dyn_v * yg * rr, axis=0, keepdims=True)
        t1 = dyn_v * ng_ref[...]
        dyg = rr * (t1 - yg * (rr * rr) * jnp.mean(t1 * yg, axis=-1, keepdims=True))
        dy = dyg * sil
        dz_ref[...] = (dyg * yv * (sig * (1.0 + z * (1.0 - sig)))).astype(BF16)
        st_ref[1:2, :] += jnp.sum(dy * x, axis=0, keepdims=True)
        dx = dsk_ref[...] * dy
        h_in = hin_ref[...]
        h16 = h_in.astype(BF16)
        ea = jnp.exp(q["a_exp"])
        zmat = _dot(c16, h16)
        dz16 = (dy * ea).astype(BF16)
        da_ch = dy * zmat * ea
        dcm = _dot_nt(dz16, h16)
        dh_in = _dot_tn(c16, dz16)
        dh_out = dh_sc[...]
        dho16 = dh_out.astype(BF16)
        eal = jnp.exp(q["a_last"])
        dh_in += dh_out * eal
        dal_ch = jnp.sum(dh_out * h_in, axis=0, keepdims=True) * eal
        to_end = jnp.exp(q["a_last"] - q["a_exp"])
        wmat = xd * to_end
        dbm = _dot_nt(wmat.astype(BF16), dho16)
        dw = _dot(b16, dho16)
        dxd = dw * to_end
        g_end = dw * wmat
        da_ch -= g_end
        dal_ch += jnp.sum(g_end, axis=0, keepdims=True)
        gmat = _dot_nt(c16, b16)
        row = _iota((128, 128), 0)
        triu = row <= q["col"]
        xd16, dy16 = xd.astype(BF16), dy.astype(BF16)
        low = q["col"] < 64
        da_col = jnp.zeros((128, 128), F32)
        da_key = jnp.zeros((128, 128), F32)
        dg = jnp.zeros((128, 128), F32)
        pieces = []
        for p in range(4):
            xs, dys = xd16[:, 128 * p:128 * p + 128], dy16[:, 128 * p:128 * p + 128]
            acc = jnp.zeros((128, 128), F32)
            for half in range(2):
                r = 2 * p + half
                hm = low if half == 0 else jnp.logical_not(low)
                xm = jnp.where(hm, xs, jnp.zeros_like(xs))
                dym = jnp.where(hm, dys, jnp.zeros_like(dys))
                decay = jnp.exp(jnp.where(q["tril"], q["a_col"][:, r:r + 1] - q["a_row"][r:r + 1, :], NEG_BIG))
                acc += _dot_tn((gmat * decay).astype(BF16), dym)
                dgl = _dot_nt(dym, xm) * decay
                dg += dgl
                n_ls = dgl * gmat
                da_col += jnp.where(q["col"] == r, jnp.sum(n_ls, axis=-1, keepdims=True), 0.0)
                da_key += jnp.where(row == r, jnp.sum(n_ls, axis=0, keepdims=True), 0.0)
            pieces.append(acc)
        da_col -= da_key.T
        dxd += jnp.concatenate(pieces, axis=1)
        dg16 = dg.astype(BF16)
        dcm += _dot(dg16, b16)
        dbm += _dot_tn(dg16, c16)
        fold = jnp.where((_iota((512, 128), 0) >> 6) == _iota((512, 128), 1), 1.0, 0.0).astype(BF16)
        da_ch += jnp.where(_iota((128, 1), 0) == 127, dal_ch, 0.0)
        da_col += _dot3_right(da_ch, fold)
        d_dta = _dot3_left(jnp.where(triu, 1.0, 0.0).astype(BF16), da_col)
        ddt_ref[...] = d_dta * q["a_neg"] + _dot3_right(dxd * x, fold)
        dal_ref[0:1, :] += jnp.sum(d_dta * q["dt"] * q["a_neg"], axis=0, keepdims=True)
        dx_ref[...] = dx + dxd * q["dt_exp"]
        db_ref[...] = dbm
        dc_ref[...] = dcm
        dh_sc[...] = dh_in

    rev = lambda c: nc - 1 - c
    blk = pl.BlockSpec((SSM_CHUNK, 512), lambda g, c: (rev(c), g))
    small = pl.BlockSpec((SSM_CHUNK, 128), lambda g, c: (rev(c), g))
    return _pcall(
        "ssd_bwd", body, grid=(SSM_GROUPS, nc), ride=ride,
        args=(act, act, act, proj, dtc, a_log, d_skip, norm_g, y, h_in_all, dyn), sem=("parallel", "arbitrary"),
        in_specs=_ssd_specs(nc, rev) + [blk, pl.BlockSpec((None, None, SSM_STATE, 512), lambda g, c: (g, rev(c), 0, 0)), blk],
        out_specs=[blk, blk, small, small, pl.BlockSpec((None, SSM_CHUNK, 128), lambda g, c: (g, rev(c), 0)),
                   pl.BlockSpec((8, 512), lambda g, c: (0, g)), pl.BlockSpec((None, 8, 128), lambda g, c: (g, 0, 0))],
        out_shape=[jax.ShapeDtypeStruct((s, SSM_IN_PAD), BF16), jax.ShapeDtypeStruct((s, SSM_D_INNER), F32),
                   jax.ShapeDtypeStruct((s, SSM_GROUPS * SSM_STATE), F32), jax.ShapeDtypeStruct((s, SSM_GROUPS * SSM_STATE), F32),
                   jax.ShapeDtypeStruct((SSM_GROUPS, s, 128), F32), jax.ShapeDtypeStruct((8, SSM_D_INNER), F32),
                   jax.ShapeDtypeStruct((SSM_GROUPS, 8, 128), F32)],
        scratch_shapes=[pltpu.VMEM((SSM_STATE, 512), F32)])


MM_TM = 1024
MM_TK = 512
MM_TK_BIG = 2048
FFN_SHARD = FFN_HIDDEN // N_DEV


def _ij(tm, tn):
    return pl.BlockSpec((tm, tn), lambda i, j, k: (i, j))


def _mm_plain(name, a, b, mode, dims, out_dtype, tn=1024, tk=MM_TK, **kw):
    res = _matmul(name, a, b, mode=mode, dims=dims, tiles=(MM_TM, tn, tk), outs=[((dims[0], dims[1]), out_dtype)],
                  epilogue=_epi_plain, **kw)
    return res[0], res[1:]


def _mm_resgate(name, a, b, res, gate, k_dim, tk, ride=None):
    s, d = res.shape
    got = _matmul(name, a, b, mode="nn", dims=(s, d, k_dim), tiles=(MM_TM, 1024, tk),
                  outs=[((s, d), F32), ((s, d), BF16)], epilogue=_epi_resgate, extras=[res, gate],
                  extra_specs=[_ij(MM_TM, 1024), pl.BlockSpec((1, 1024), lambda i, j, k: (0, j))], ride=ride)
    return got[0], got[1], got[2:]


def _ffn_fwd(tag, x_in, gain, scale, shift, gate, w1, w2_of, ride1=None, ride2=None):
    s, d = x_in.shape
    h = _norm_mod_fwd(tag + "_norm2", x_in, gain, scale, shift)
    rr, pre, *got1 = _matmul(tag + "_ffn1", h, w1, mode="nn", dims=(s, FFN_HIDDEN, d), tiles=(MM_TM, FFN_SHARD, MM_TK_BIG),
                             b_spec=pl.BlockSpec((None, MM_TK_BIG, FFN_SHARD), lambda i, j, k: (j, k, 0)),
                             outs=[((s, FFN_HIDDEN), BF16)] * 2, epilogue=_epi_relu2, ride=ride1)
    x_out, f, got2 = _mm_resgate(tag + "_ffn2", rr, w2_of(got1), x_in, gate, FFN_HIDDEN, MM_TK_BIG, ride=ride2)
    return x_out, (h, rr, pre, f), got1, got2


def _ffn_bwd(tag, dx_out, dy, x_in, saved, gain, scale, w1, w2, below=None, ride_dx2=None, ride_dw2=None,
             ride_dx1_of=None):
    s, d = x_in.shape
    h, rr, pre, _ = saved
    da, *got2 = _matmul(tag + "_ffn2_dx", dy, w2, mode="nt", dims=(s, FFN_HIDDEN, d), tiles=(MM_TM, 1024, MM_TK_BIG),
                        outs=[((s, FFN_HIDDEN), BF16)], epilogue=_epi_drelu2, extras=[pre],
                        extra_specs=[_ij(MM_TM, 1024)], ride=ride_dx2)
    dw2, got_dw2 = _mm_plain(tag + "_ffn2_dw", rr, dy, "tn", (FFN_HIDDEN, d, s), BF16, tk=MM_TK_BIG, ride=ride_dw2)
    dh, got1 = _mm_plain(tag + "_ffn1_dx", da, w1, "nt", (s, d, FFN_HIDDEN), F32, tk=2 * FFN_SHARD, b_slabs=2,
                         b_spec=pl.BlockSpec((2, 1024, FFN_SHARD), lambda i, j, k: (k, j, 0)),
                         ride=None if ride_dx1_of is None else ride_dx1_of(dw2))
    dw1 = _matmul(tag + "_ffn1_dw", h, da, mode="tn", dims=(d, FFN_HIDDEN, s), tiles=(MM_TM, FFN_SHARD, MM_TK_BIG),
                  outs=[((N_DEV, d, FFN_SHARD), BF16)], epilogue=_epi_plain,
                  out_specs=[pl.BlockSpec((None, MM_TM, FFN_SHARD), lambda i, j, k: (j, i, 0))])[0]
    dx_in, st_norm, *dy_below = _norm_mod_bwd(tag + "_norm2_bwd", x_in, gain, scale, dh, dx_out, below)
    return dx_in, dw1, dw2, st_norm, (dy_below[0] if dy_below else None), got2, got_dw2, got1


def _slab_of(col_block):
    return jnp.where(col_block < 9, 3 * (col_block % 3) + col_block // 3, 9)


def _full_weight(name, gathered):
    if name in ASSEMBLED:
        return _assemble_cols("assemble_" + name, gathered, ASSEMBLED[name])
    if name == "ffn_w1":
        return gathered
    return gathered.reshape(-1, gathered.shape[2])


def _grad_pieces(name, full, shard_shape, half=None):
    rows = shard_shape[0]
    lo, n = (0, rows) if half is None else (half * (rows // 2), rows // 2)
    if name in ASSEMBLED:
        return _split_cols(f"split_{name}_{lo}", full, shard_shape[1], lo, n)
    return full.reshape(N_DEV, *shard_shape)[:, lo:lo + n]


def _device_step(x, target, mod, sm, shards, w_even_in):
    s, d = x.shape
    mv = [[mod[i, k].reshape(1, d) for k in range(6)] for i in range(2)]
    nm = [sm["norm_mix"][i].reshape(1, d) for i in range(2)]
    nf = [sm["norm_ffn"][i].reshape(1, d) for i in range(2)]
    pool_w, pool_scale = sm["pool_w"].reshape(4, 128, 128), sm["pool_scale"].reshape(1, POOL_WIDTH)
    got = {}

    def gather(*items):
        return [shards[it] for it in items], False

    def gather_half(item, half):
        rows = shards[item].shape[0] // 2
        return [shards[item][half * rows:(half + 1) * rows]], False

    def scatter(*pieces):
        return list(pieces), True

    def pieces(item, full, half=None):
        return _grad_pieces(item[0], full, shards[item].shape, half)

    def joined(top, bottom):
        return jnp.concatenate([top, bottom], axis=1)

    sh1, sc1, g1, sh2, sc2, g2 = mv[0]
    h1 = _norm_mod_fwd("l0_norm1", x, nm[0], sc1, sh1)
    slab = pl.BlockSpec((None, MM_TM, 512), lambda i, j, k: (_slab_of(j), i, 0))
    proj0, g_eout, g_sout = _matmul(
        "l0_in", h1, w_even_in, mode="nn", dims=(s, EVEN_IN_WIDTH, d), tiles=(MM_TM, 512, MM_TK_BIG),
        outs=[((EVEN_IN_WIDTH // 512, s, 512), BF16)], out_specs=[slab], epilogue=_epi_plain,
        ride=gather(("even_w_out", 0), ("ssm_w_out", 0)))
    w_even_out, w_ssm_out = _full_weight("even_w_out", g_eout), _full_weight("ssm_w_out", g_sout)
    o, lse = [None] * 3, [None] * 3
    o[0], lse[0], (g_w1_top,) = _attn_fwd(proj0, 0, ride=gather_half(("ffn_w1", 0), 0))
    o[1], lse[1], (g_w1_bottom,) = _attn_fwd(proj0, 1, ride=gather_half(("ffn_w1", 0), 1))
    o[2], lse[2], (g_sin_top,) = _attn_fwd(proj0, 2, ride=gather_half(("ssm_w_in", 0), 0))
    w1_0 = _full_weight("ffn_w1", joined(g_w1_top, g_w1_bottom))
    cat = _merge_pool_fwd(o, lse, proj0, pool_w, pool_scale)
    x1, y0, _ = _mm_resgate("l0_out", cat, w_even_out, x, g1, EVEN_OUT_WIDTH, EVEN_OUT_WIDTH)
    x2, ffn0, g_w2_0, (g_sin_bottom,) = _ffn_fwd(
        "l0", x1, nf[0], sc2, sh2, g2, w1_0, lambda arrived: _full_weight("ffn_w2", arrived[0]),
        ride1=gather(("ffn_w2", 0)), ride2=gather_half(("ssm_w_in", 0), 1))
    w2_0, w_ssm_in = _full_weight("ffn_w2", g_w2_0[0]), _full_weight("ssm_w_in", joined(g_sin_top, g_sin_bottom))

    th1, tc1, tg1, th2, tc2, tg2 = mv[1]
    h3 = _norm_mod_fwd("l1_norm1", x2, nm[1], tc1, th1)
    proj1, (g_w1_1,) = _mm_plain("l1_in", h3, w_ssm_in, "nn", (s, SSM_IN_PAD, d), F32, tn=1152, tk=MM_TK_BIG,
                                 ride=gather(("ffn_w1", 1)))
    w1_1 = _full_weight("ffn_w1", g_w1_1)
    conv_w = sm["ssm_conv_w"].reshape(SSM_CONV, SSM_CONV_DIM)
    pre, act = _conv_fwd(proj1, conv_w, sm["ssm_conv_b"].reshape(1, SSM_CONV_DIM))
    dt_bias = jnp.pad(sm["ssm_dt_bias"].reshape(1, SSM_HEADS), ((0, 0), (0, 128 - SSM_HEADS)))
    dt_full = _dt_fwd(proj1, dt_bias)
    dtc = jnp.pad(dt_full[:, :SSM_HEADS].reshape(s, SSM_GROUPS, 8).transpose(1, 0, 2), ((0, 0), (0, 0), (0, 120)))
    a_log = jnp.pad(sm["ssm_a_log"].reshape(SSM_GROUPS, 1, 8), ((0, 0), (0, 0), (0, 120)))
    d_skip = jnp.repeat(sm["ssm_d"].reshape(SSM_HEADS), SSM_D_INNER // SSM_HEADS).reshape(1, SSM_D_INNER)
    norm_g = sm["ssm_norm"].reshape(1, SSM_D_INNER)
    (y, yn, h_in_all), (g_w2_1,) = _ssd_fwd(act, proj1, dtc, a_log, d_skip, norm_g, ride=gather(("ffn_w2", 1)))
    w2_1 = _full_weight("ffn_w2", g_w2_1)
    x3, y1, _ = _mm_resgate("l1_out", yn, w_ssm_out, x2, tg1, SSM_D_INNER, MM_TK_BIG)
    x4, ffn1, _, _ = _ffn_fwd("l1", x3, nf[1], tc2, th2, tg2, w1_1, lambda arrived: w2_1)

    dx4, st_loss, dyf1 = _loss_head("loss_head", x4, sm["final_norm"].reshape(1, d), target, (ffn1[3], tg2))
    loss = jnp.sum(st_loss[1])

    dx3, dw1_1, dw2_1, st_n2_1, dy1, _, _, (got[("ffn_w2", 1)],) = _ffn_bwd(
        "l1", dx4, dyf1, x3, ffn1, nf[1], tc2, w1_1, w2_1, below=(y1, tg1),
        ride_dx1_of=lambda dw2: scatter(pieces(("ffn_w2", 1), dw2)))
    dyn, _ = _mm_plain("l1_out_dx", dy1, w_ssm_out, "nt", (s, SSM_D_INNER, d), F32, tk=MM_TK_BIG)
    dw_sout, _ = _mm_plain("l1_out_dw", yn, dy1, "tn", (SSM_D_INNER, d, s), BF16, tk=MM_TK_BIG)
    (dproj1, dxs, dbm, dcm, ddt, st_ssd, d_alog), (got[("ffn_w1", 1)], got[("ssm_w_out", 0)]) = _ssd_bwd(
        act, proj1, dtc, a_log, d_skip, norm_g, y, h_in_all, dyn,
        ride=scatter(dw1_1, pieces(("ssm_w_out", 0), dw_sout)))
    dproj1, st_cx = _conv_bwd("conv_bwd_x", dxs, pre, proj1, conv_w, dproj1, block0=0)
    dproj1, st_cb = _conv_bwd("conv_bwd_b", dbm, pre, proj1, conv_w, dproj1, block0=SSM_D_INNER // 512)
    dproj1, st_cc = _conv_bwd("conv_bwd_c", dcm, pre, proj1, conv_w, dproj1, block0=SSM_D_INNER // 512 + 2)
    ddt_rows = jnp.pad(ddt[:, :, :8].transpose(1, 0, 2).reshape(s, SSM_HEADS), ((0, 0), (0, 128 - SSM_HEADS)))
    dproj1, st_dt = _dt_bwd(ddt_rows, proj1, dt_bias, dproj1)
    dh3, _ = _mm_plain("l1_in_dx", dproj1, w_ssm_in, "nt", (s, d, SSM_IN_PAD), F32, tk=SSM_IN_PAD // 3)
    dw_sin, _ = _mm_plain("l1_in_dw", h3, dproj1, "tn", (d, SSM_IN_PAD, s), BF16, tn=1152, tk=MM_TK_BIG)
    dx2, st_n1_1, dyf0 = _norm_mod_bwd("l1_norm1_bwd", x2, nm[1], tc1, dh3, dx3, (ffn0[3], g2))

    dx1, dw1_0, dw2_0, st_n2_0, dy0, (sin_top,), (sin_bottom,), (got[("ffn_w2", 0)],) = _ffn_bwd(
        "l0", dx2, dyf0, x1, ffn0, nf[0], sc2, w1_0, w2_0, below=(y0, g1),
        ride_dx2=scatter(pieces(("ssm_w_in", 0), dw_sin, 0)), ride_dw2=scatter(pieces(("ssm_w_in", 0), dw_sin, 1)),
        ride_dx1_of=lambda dw2: scatter(pieces(("ffn_w2", 0), dw2)))
    got[("ssm_w_in", 0)] = joined(sin_top, sin_bottom)
    dcat, _ = _mm_plain("l0_out_dx", dy0, w_even_out, "nt", (s, EVEN_OUT_WIDTH, d), F32, tk=MM_TK_BIG)
    dw_eout, _ = _mm_plain("l0_out_dw", cat, dy0, "tn", (EVEN_OUT_WIDTH, d, s), BF16, tk=MM_TK_BIG)
    lane = jnp.arange(512) // 64
    head_sum = (lane[:, None] == lane[None, :]).astype(BF16)
    *do_cc, du, d_pool_w, st_pool = _merge_pool_bwd(dcat, o, lse, proj0, pool_w, pool_scale, head_sum)
    do, cc = do_cc[:3], do_cc[3:]
    dqkv = [None] * 3
    dqkv[0], (w1_top,) = _attn_bwd(proj0, 0, do[0], lse[0], cc[0], ride=scatter(pieces(("ffn_w1", 0), dw1_0, 0)))
    dqkv[1], (w1_bottom,) = _attn_bwd(proj0, 1, do[1], lse[1], cc[1], ride=scatter(pieces(("ffn_w1", 0), dw1_0, 1)))
    dqkv[2], (got[("even_w_out", 0)],) = _attn_bwd(proj0, 2, do[2], lse[2], cc[2],
                                                   ride=scatter(pieces(("even_w_out", 0), dw_eout)))
    got[("ffn_w1", 0)] = joined(w1_top, w1_bottom)
    dproj0 = jnp.stack([dqkv[g][kind] for g in range(3) for kind in range(3)] + [du])
    dw_ein, _ = _mm_plain("l0_in_dw", h1, dproj0, "tn", (d, EVEN_IN_WIDTH, s), BF16, tn=512, tk=MM_TK_BIG,
                          b_spec=pl.BlockSpec((None, MM_TK_BIG, 512), lambda i, j, k: (_slab_of(j), k, 0)))
    dh1, (got[("even_w_in", 0)],) = _mm_plain("l0_in_dx", dproj0, w_even_in, "nt", (s, d, EVEN_IN_WIDTH), F32,
                                              a_spec=pl.BlockSpec((None, MM_TM, 512), lambda i, j, k: (_slab_of(k), i, 0)),
                                              ride=scatter(pieces(("even_w_in", 0), dw_ein)))
    grad_x, st_n1_0 = _norm_mod_bwd("l0_norm1_bwd", x, nm[0], sc1, dh1, dx1)

    dg1_0, dg2_0 = st_n2_0[GATE_STAT_ROW], st_n1_1[GATE_STAT_ROW]
    dg1_1, dg2_1 = st_n2_1[GATE_STAT_ROW], st_loss[GATE_STAT_ROW]
    dmod = jnp.stack([
        jnp.stack([st_n1_0[0], st_n1_0[1], dg1_0, st_n2_0[0], st_n2_0[1], dg2_0]),
        jnp.stack([st_n1_1[0], st_n1_1[1], dg1_1, st_n2_1[0], st_n2_1[1], dg2_1])])
    st_conv = jnp.concatenate([st_cx, st_cb, st_cc], axis=1)
    small = dict(
        norm_mix=jnp.stack([st_n1_0[2], st_n1_1[2]]), norm_ffn=jnp.stack([st_n2_0[2], st_n2_1[2]]),
        pool_w=d_pool_w, pool_scale=st_pool[0], ssm_conv_w=st_conv[:SSM_CONV], ssm_conv_b=st_conv[SSM_CONV],
        ssm_dt_bias=st_dt[0, :SSM_HEADS], ssm_a_log=d_alog[:, 0, :8].reshape(SSM_HEADS),
        ssm_d=jnp.sum(st_ssd[1].reshape(SSM_HEADS, SSM_D_INNER // SSM_HEADS), axis=-1), ssm_norm=st_ssd[0],
        final_norm=st_loss[0])
    return loss, grad_x, got, dmod, small


WEIGHT_ORDER = ("ada_w", "ada_b", "norm_mix", "norm_ffn", "ffn_w1", "ffn_w2", "even_w_in", "pool_w", "pool_scale",
                "even_w_out", "ssm_w_in", "ssm_conv_w", "ssm_conv_b", "ssm_dt_bias", "ssm_a_log", "ssm_d", "ssm_norm",
                "ssm_w_out", "final_norm")
BIG_LAYERS = ((("even_w_in", 0), ("even_w_out", 0), ("ffn_w1", 0), ("ffn_w2", 0)),
              (("ssm_w_in", 0), ("ssm_w_out", 0), ("ffn_w1", 1), ("ffn_w2", 1)))
STACKED = ("ffn_w1", "ffn_w2")
ASSEMBLED = {"even_w_in": EVEN_IN_WIDTH, "even_w_out": D_MODEL, "ssm_w_in": SSM_IN_PAD}
SMALL_REPLICATED = ("norm_mix", "norm_ffn", "pool_w", "pool_scale", "ssm_dt_bias", "ssm_a_log", "ssm_d", "final_norm")
SMALL_SHARDED = ("ssm_conv_w", "ssm_conv_b", "ssm_norm")


def _pack(flat_parts, width, lead=()):
    flat = jnp.concatenate(flat_parts, axis=-1)
    n = flat.shape[-1]
    rows = -(-n // (8 * width)) * 8
    flat = jnp.pad(flat, [(0, 0)] * len(lead) + [(0, rows * width - n)])
    return flat.reshape(*lead, rows, width)


def _unpack(packed, shapes, lead=()):
    flat = packed.reshape(*lead, -1)
    out, off = [], 0
    for shp in shapes:
        n = math.prod(shp)
        out.append(flat[..., off:off + n].reshape(*lead, *shp))
        off += n
    return out


def kernel(x, c, ada_w, ada_b, norm_mix, norm_ffn, ffn_w1, ffn_w2, even_w_in, pool_w, pool_scale, even_w_out, ssm_w_in, ssm_conv_w, ssm_conv_b, ssm_dt_bias, ssm_a_log, ssm_d, ssm_norm, ssm_w_out, final_norm, loss_target, m_ada_w, m_ada_b, m_norm_mix, m_norm_ffn, m_ffn_w1, m_ffn_w2, m_even_w_in, m_pool_w, m_pool_scale, m_even_w_out, m_ssm_w_in, m_ssm_conv_w, m_ssm_conv_b, m_ssm_dt_bias, m_ssm_a_log, m_ssm_d, m_ssm_norm, m_ssm_w_out, m_final_norm, v_ada_w, v_ada_b, v_norm_mix, v_norm_ffn, v_ffn_w1, v_ffn_w2, v_even_w_in, v_pool_w, v_pool_scale, v_even_w_out, v_ssm_w_in, v_ssm_conv_w, v_ssm_conv_b, v_ssm_dt_bias, v_ssm_a_log, v_ssm_d, v_ssm_norm, v_ssm_w_out, v_final_norm):
    w = dict(ada_w=ada_w, ada_b=ada_b, norm_mix=norm_mix, norm_ffn=norm_ffn, ffn_w1=ffn_w1, ffn_w2=ffn_w2,
             even_w_in=even_w_in, pool_w=pool_w, pool_scale=pool_scale, even_w_out=even_w_out, ssm_w_in=ssm_w_in,
             ssm_conv_w=ssm_conv_w, ssm_conv_b=ssm_conv_b, ssm_dt_bias=ssm_dt_bias, ssm_a_log=ssm_a_log, ssm_d=ssm_d,
             ssm_norm=ssm_norm, ssm_w_out=ssm_w_out, final_norm=final_norm)
    m = dict(ada_w=m_ada_w, ada_b=m_ada_b, norm_mix=m_norm_mix, norm_ffn=m_norm_ffn, ffn_w1=m_ffn_w1, ffn_w2=m_ffn_w2,
             even_w_in=m_even_w_in, pool_w=m_pool_w, pool_scale=m_pool_scale, even_w_out=m_even_w_out,
             ssm_w_in=m_ssm_w_in, ssm_conv_w=m_ssm_conv_w, ssm_conv_b=m_ssm_conv_b, ssm_dt_bias=m_ssm_dt_bias,
             ssm_a_log=m_ssm_a_log, ssm_d=m_ssm_d, ssm_norm=m_ssm_norm, ssm_w_out=m_ssm_w_out, final_norm=m_final_norm)
    v = dict(ada_w=v_ada_w, ada_b=v_ada_b, norm_mix=v_norm_mix, norm_ffn=v_norm_ffn, ffn_w1=v_ffn_w1, ffn_w2=v_ffn_w2,
             even_w_in=v_even_w_in, pool_w=v_pool_w, pool_scale=v_pool_scale, even_w_out=v_even_w_out,
             ssm_w_in=v_ssm_w_in, ssm_conv_w=v_ssm_conv_w, ssm_conv_b=v_ssm_conv_b, ssm_dt_bias=v_ssm_dt_bias,
             ssm_a_log=v_ssm_a_log, ssm_d=v_ssm_d, ssm_norm=v_ssm_norm, ssm_w_out=v_ssm_w_out, final_norm=v_final_norm)
    d = D_MODEL
    me = _my_index()

    sharded_shapes = [(SSM_CONV, SSM_CONV_DIM // N_DEV), (SSM_CONV_DIM // N_DEV,), (SSM_D_INNER // N_DEV,)]
    shards = {(name, idx): w[name][idx].astype(BF16) for layer in BIG_LAYERS for name, idx in layer}
    small_in = _pack([c.reshape(-1)] + [w[k].reshape(-1) for k in SMALL_SHARDED], 128)
    got, got_even_in = _exchange("gather_first", [small_in, shards[("even_w_in", 0)]], scatter=False)
    w_even_in = _full_weight("even_w_in", got_even_in)
    c_all, conv_w_sh, conv_b_sh, norm_sh = _unpack(got, [(d,)] + sharded_shapes, lead=(N_DEV,))
    sm = {k: w[k] for k in SMALL_REPLICATED}
    sm["ssm_conv_w"] = conv_w_sh.transpose(1, 0, 2).reshape(SSM_CONV, SSM_CONV_DIM)
    sm["ssm_conv_b"] = conv_b_sh.reshape(SSM_CONV_DIM)
    sm["ssm_norm"] = norm_sh.reshape(SSM_D_INNER)

    ada_cols = 6 * d // N_DEV
    c_pad = jnp.pad(c_all, ((0, 16 - N_DEV), (0, 0)))
    mod_cols = _matmul(
        "ada_fwd", c_pad, ada_w, mode="nn", dims=(16, 2 * ada_cols, d), tiles=(16, ada_cols // 2, d), a_fn=_silu,
        b_spec=pl.BlockSpec((None, d, ada_cols // 2), lambda i, j, k: (j // 2, k, j % 2)),
        outs=[((16, 2 * ada_cols), F32)], epilogue=_epi_plain)[0]
    mod_got = _all_to_all("ada_exchange", mod_cols[:N_DEV].reshape(N_DEV, 2, ada_cols))
    mod = (mod_got.transpose(1, 0, 2).reshape(2, 6 * d) + ada_b).reshape(2, 6, d)

    loss, grad_x, grad_got, dmod, small = _device_step(x[0], loss_target[0], mod, sm, shards, w_even_in)
    loss = lax.psum(loss, ("x", "y", "c"))

    grads, delta, new_m, new_v = {}, {}, {}, {}

    def update(name, parts, shape=None):
        rows, cols = parts.shape[1:]
        res = _adamw("adamw_" + name, parts, w[name].reshape(rows, cols), m[name].reshape(rows, cols), v[name].reshape(rows, cols))
        return [r.reshape(w[name].shape if shape is None else shape) for r in res]

    rep_shapes = [w[k].shape for k in SMALL_REPLICATED]
    pack_rep = lambda tree: _pack([tree[k].reshape(-1) for k in SMALL_REPLICATED], 128)
    dmod_all, rep_got = _exchange("gather_small_grads", [dmod.reshape(2, 6 * d), pack_rep(small)], scatter=False)

    my_cols = lax.dynamic_slice_in_dim(dmod_all, me * ada_cols, ada_cols, axis=2).reshape(N_DEV, 2 * ada_cols)
    g_ada_w = _matmul(
        "ada_dw", c_pad, jnp.pad(my_cols, ((0, 16 - N_DEV), (0, 0))), mode="tn", dims=(d, 2 * ada_cols, 16),
        tiles=(1024, ada_cols // 2, 16), a_fn=_silu, outs=[((2, d, ada_cols), F32)],
        out_specs=[pl.BlockSpec((None, 1024, ada_cols // 2), lambda i, j, k: (j // 2, i, j % 2))], epilogue=_epi_plain)[0]
    grads["ada_w"], delta["ada_w"], new_m["ada_w"], new_v["ada_w"] = update("ada_w", g_ada_w.reshape(1, 2 * d, ada_cols))
    grads["ada_b"], delta["ada_b"], new_m["ada_b"], new_v["ada_b"] = update("ada_b", dmod_all)

    sh_pieces = [small["ssm_conv_w"].reshape(SSM_CONV, N_DEV, -1).transpose(1, 0, 2).reshape(N_DEV, -1),
                 small["ssm_conv_b"].reshape(N_DEV, -1), small["ssm_norm"].reshape(N_DEV, -1)]
    sh_got = _all_to_all("exchange_small_grads", _pack(sh_pieces, 128, lead=(N_DEV,)))

    stacked = {name: [None, None] for name in STACKED}
    for layer in BIG_LAYERS:
        for name, idx in layer:
            res = _adamw(f"adamw_{name}_{idx}", grad_got[(name, idx)], w[name][idx], m[name][idx], v[name][idx])
            if name in stacked:
                stacked[name][idx] = res
            else:
                grads[name], delta[name], new_m[name], new_v[name] = [r[None] for r in res]
    for name, per_layer in stacked.items():
        grads[name], delta[name], new_m[name], new_v[name] = [jnp.stack([per_layer[0][q], per_layer[1][q]]) for q in range(4)]

    rep_res = _adamw("adamw_small_replicated", rep_got, pack_rep(w), pack_rep(m), pack_rep(v))
    for dst, packed in zip((grads, delta, new_m, new_v), rep_res):
        for k, val in zip(SMALL_REPLICATED, _unpack(packed, rep_shapes)):
            dst[k] = val
    pack_sh = lambda tree: _pack([tree[k].reshape(-1) for k in SMALL_SHARDED], 128)
    sh_res = _adamw("adamw_small_sharded", sh_got, pack_sh(w), pack_sh(m), pack_sh(v))
    for dst, packed in zip((grads, delta, new_m, new_v), sh_res):
        for k, val in zip(SMALL_SHARDED, _unpack(packed, [w[k].shape for k in SMALL_SHARDED])):
            dst[k] = val

    out = [loss, grad_x[None]]
    for tree in (grads, delta, new_m, new_v):
        out.extend(tree[k] for k in WEIGHT_ORDER)
    return tuple(out)
```

```python
import functools
import math

import jax
import jax.numpy as jnp
from jax import lax
from jax.experimental import pallas as pl
from jax.experimental.pallas import tpu as pltpu

F32 = jnp.float32
BF16 = jnp.bfloat16

N_DEV = 8
D_MODEL = 2048
NORM_EPS = 1e-6
ATTN_BLOCK = 128
ATTN_GROUPS = 3
ATTN_GROUP_WIDTH = 512
ATTN_QKV_WIDTH = 3 * ATTN_GROUPS * ATTN_GROUP_WIDTH
POOL_GROUPS = 4
POOL_GROUP_WIDTH = 128
POOL_WIDTH = 512
POOL_HALO = 16
EVEN_IN_WIDTH = ATTN_QKV_WIDTH + POOL_WIDTH
EVEN_OUT_WIDTH = 1024
SSM_D_INNER = 4096
SSM_HEADS = 64
SSM_GROUPS = 8
SSM_GROUP_WIDTH = 512
SSM_STATE = 128
SSM_CHUNK = 128
SSM_CONV = 4
SSM_CONV_DIM = 6144
SSM_IN_WIDTH = 10304
SSM_IN_PAD = 10368
FFN_HIDDEN = 8192

ADAM_LR = 0.001
ADAM_B1 = 0.9
ADAM_B2 = 0.999
ADAM_EPS = 1e-08
ADAM_WD = 0.01
ADAM_STEP = 10

VMEM_LIMIT_BYTES = 56 * 1024 * 1024
NEG_BIG = -1e30

MESH_ID = pl.DeviceIdType.MESH
ANY = pl.BlockSpec(memory_space=pl.ANY)


def _cparams(*sem):
    return pltpu.CompilerParams(dimension_semantics=tuple(sem) if sem else None, vmem_limit_bytes=VMEM_LIMIT_BYTES)


def _dot(a, b):
    return lax.dot_general(a, b, (((1,), (0,)), ((), ())), preferred_element_type=F32)


def _dot_nt(a, b):
    return lax.dot_general(a, b, (((1,), (1,)), ((), ())), preferred_element_type=F32)


def _dot_tn(a, b):
    return lax.dot_general(a, b, (((0,), (0,)), ((), ())), preferred_element_type=F32)


def _split3(v):
    hi = v.astype(BF16)
    r1 = v - hi.astype(F32)
    mid = r1.astype(BF16)
    lo = (r1 - mid.astype(F32)).astype(BF16)
    return hi, mid, lo


def _dot3_left(const_bf16, v):
    hi, mid, lo = _split3(v)
    return _dot(const_bf16, hi) + _dot(const_bf16, mid) + _dot(const_bf16, lo)


def _dot3_right(v, const_bf16):
    hi, mid, lo = _split3(v)
    return _dot(hi, const_bf16) + _dot(mid, const_bf16) + _dot(lo, const_bf16)


def _iota(shape, dim):
    return lax.broadcasted_iota(jnp.int32, shape, dim)


def _sigmoid(x):
    return 1.0 / (1.0 + jnp.exp(-x))


def _peer(k):
    x, y, c = lax.axis_index("x"), lax.axis_index("y"), lax.axis_index("c")
    px = 1 - x if k & 4 else x
    py = 1 - y if k & 2 else y
    pc = 1 - c if k & 1 else c
    return (px, py, pc), 4 * px + 2 * py + pc


def _my_index():
    return 4 * lax.axis_index("x") + 2 * lax.axis_index("y") + lax.axis_index("c")


def _exchange(name, arrays, *, scatter):
    n = len(arrays)

    def body(*refs):
        ex = _Exchange(refs[:n], refs[n:2 * n], *refs[2 * n:], scatter)
        ex.start()
        ex.wait()

    return pl.pallas_call(
        body, name=name, out_shape=_exchange_out_shapes(arrays, scatter), in_specs=[ANY] * n, out_specs=[ANY] * n,
        scratch_shapes=_exchange_sems(n),
    )(*arrays)


def _exchange_out_shapes(arrays, scatter):
    return [jax.ShapeDtypeStruct((N_DEV,) + (tuple(a.shape[1:]) if scatter else tuple(a.shape)), a.dtype) for a in arrays]


def _exchange_sems(n):
    return [pltpu.SemaphoreType.DMA((n * (N_DEV - 1),)), pltpu.SemaphoreType.DMA((n * (N_DEV - 1),)),
            pltpu.SemaphoreType.DMA((n,))]


class _Exchange:
    def __init__(self, x_refs, out_refs, send_sems, recv_sems, local_sems, scatter):
        self.x_refs, self.out_refs, self.scatter = x_refs, out_refs, scatter
        self.send_sems, self.recv_sems, self.local_sems = send_sems, recv_sems, local_sems

    def _src(self, a, idx):
        return self.x_refs[a].at[idx] if self.scatter else self.x_refs[a]

    def _local(self, a):
        me = _my_index()
        return pltpu.make_async_copy(self._src(a, me), self.out_refs[a].at[me], self.local_sems.at[a])

    def _remote(self, a, k, landing):
        peer, peer_idx = _peer(k)
        sem = a * (N_DEV - 1) + k - 1
        slot = peer_idx if landing else _my_index()
        return pltpu.make_async_remote_copy(
            src_ref=self._src(a, peer_idx), dst_ref=self.out_refs[a].at[slot], send_sem=self.send_sems.at[sem],
            recv_sem=self.recv_sems.at[sem], device_id=peer, device_id_type=MESH_ID)

    def start(self):
        n = len(self.x_refs)
        for a in range(n):
            self._local(a).start()
        for k in range(1, N_DEV):
            for a in range(n):
                self._remote(a, k, False).start()

    def wait(self):
        n = len(self.x_refs)
        for k in range(1, N_DEV):
            for a in range(n):
                self._remote(a, k, True).wait_recv()
        for k in range(1, N_DEV):
            for a in range(n):
                self._remote(a, k, False).wait_send()
        for a in range(n):
            self._local(a).wait()


def _pcall(name, body, *, grid, in_specs, out_specs, out_shape, scratch_shapes, sem, args, ride=None, aliases=None):
    aliases = aliases or {}
    if ride is None:
        res = pl.pallas_call(body, name=name, grid=grid, in_specs=in_specs, out_specs=out_specs, out_shape=out_shape,
                             scratch_shapes=scratch_shapes, input_output_aliases=aliases, compiler_params=_cparams(*sem))(*args)
        return list(res), []
    arrays, scatter = ride
    n_in, n_out, n_scr, n_ride = len(in_specs), len(out_specs), len(scratch_shapes), len(arrays)

    def wrapped(*refs):
        ins, refs = refs[:n_in], refs[n_in:]
        ride_in, refs = refs[:n_ride], refs[n_ride:]
        outs, refs = refs[:n_out], refs[n_out:]
        ride_out, refs = refs[:n_ride], refs[n_ride:]
        scr, sems = refs[:n_scr], refs[n_scr:]
        exchange = _Exchange(ride_in, ride_out, *sems, scatter)
        ids = [pl.program_id(axis) for axis in range(len(grid))]
        first, last = ids[0] == 0, ids[0] == grid[0] - 1
        for axis in range(1, len(grid)):
            first, last = first & (ids[axis] == 0), last & (ids[axis] == grid[axis] - 1)

        @pl.when(first)
        def _():
            exchange.start()

        body(*ins, *outs, *scr)

        @pl.when(last)
        def _():
            exchange.wait()

    res = pl.pallas_call(
        wrapped, name=name, grid=grid, in_specs=list(in_specs) + [ANY] * n_ride, out_specs=list(out_specs) + [ANY] * n_ride,
        out_shape=list(out_shape) + _exchange_out_shapes(arrays, scatter),
        scratch_shapes=list(scratch_shapes) + _exchange_sems(n_ride), input_output_aliases=aliases,
        compiler_params=_cparams(*(("arbitrary",) * len(grid))))(*args, *arrays)
    return list(res[:n_out]), list(res[n_out:])


def _all_gather(name, x):
    return _exchange(name, [x], scatter=False)[0]


def _all_to_all(name, x):
    return _exchange(name, [x], scatter=True)[0]


def _assemble_cols(name, shards, width):
    _, k_dim, ns = shards.shape
    tr = 256

    def body(s_ref, o_ref):
        for dev in range(N_DEV):
            o_ref[:, ns * dev:ns * (dev + 1)] = s_ref[dev]
        if width > N_DEV * ns:
            o_ref[:, N_DEV * ns:] = jnp.zeros((tr, width - N_DEV * ns), o_ref.dtype)

    return pl.pallas_call(
        body, name=name, grid=(k_dim // tr,), in_specs=[pl.BlockSpec((N_DEV, tr, ns), lambda i: (0, i, 0))],
        out_specs=pl.BlockSpec((tr, width), lambda i: (i, 0)), out_shape=jax.ShapeDtypeStruct((k_dim, width), shards.dtype),
        compiler_params=_cparams("parallel"),
    )(shards)


def _split_cols(name, full, ns, row0=0, n_rows=None):
    k_dim, width = full.shape
    n_rows = k_dim if n_rows is None else n_rows
    tr = 256
    first = row0 // tr

    def body(f_ref, o_ref):
        for dev in range(N_DEV):
            o_ref[dev] = f_ref[:, ns * dev:ns * (dev + 1)]

    return pl.pallas_call(
        body, name=name, grid=(n_rows // tr,), in_specs=[pl.BlockSpec((tr, width), lambda i: (first + i, 0))],
        out_specs=pl.BlockSpec((N_DEV, tr, ns), lambda i: (0, i, 0)),
        out_shape=jax.ShapeDtypeStruct((N_DEV, n_rows, ns), full.dtype), compiler_params=_cparams("parallel"),
    )(full)


_DIMS = {"nn": (((1,), (0,)), ((), ())), "nt": (((1,), (1,)), ((), ())), "tn": (((0,), (0,)), ((), ()))}


def _matmul(name, a, b, *, mode, dims, tiles, outs, epilogue, a_spec=None, b_spec=None, out_specs=None,
            extras=(), extra_specs=(), a_fn=None, ride=None, b_slabs=0):
    m_dim, n_dim, k_dim = dims
    tm, tn, tk = tiles
    assert m_dim % tm == 0 and n_dim % tn == 0 and k_dim % tk == 0, (name, dims, tiles)
    grid = (m_dim // tm, n_dim // tn, k_dim // tk)
    nk = grid[2]
    if a_spec is None:
        a_spec = pl.BlockSpec((tk, tm), lambda i, j, k: (k, i)) if mode == "tn" else pl.BlockSpec((tm, tk), lambda i, j, k: (i, k))
    if b_spec is None:
        b_spec = pl.BlockSpec((tn, tk), lambda i, j, k: (j, k)) if mode == "nt" else pl.BlockSpec((tk, tn), lambda i, j, k: (k, j))
    if out_specs is None:
        out_specs = [pl.BlockSpec((tm, tn), lambda i, j, k: (i, j)) for _ in outs]
    n_ex, n_out = len(extras), len(outs)
    ride_arrays, scatter = ride if ride is not None else ((), False)
    n_ride = len(ride_arrays)
    dn = _DIMS[mode]

    def body(a_ref, b_ref, *rest):
        ex_refs, rest = rest[:n_ex], rest[n_ex:]
        ride_in, rest = rest[:n_ride], rest[n_ride:]
        out_refs, rest = rest[:n_out], rest[n_out:]
        ride_out, rest = rest[:n_ride], rest[n_ride:]
        i, j, k = pl.program_id(0), pl.program_id(1), pl.program_id(2)
        if n_ride:
            exchange = _Exchange(ride_in, ride_out, *rest[-3:], scatter)

            @pl.when((i == 0) & (j == 0) & (k == 0))
            def _():
                exchange.start()

        at = a_ref[...]
        if a_fn is not None:
            at = a_fn(at)
        at = at.astype(BF16)
        if b_slabs:
            width = tk // b_slabs
            part = lax.dot_general(at[:, :width], b_ref[0].astype(BF16), dn, preferred_element_type=F32)
            for slab_i in range(1, b_slabs):
                part += lax.dot_general(at[:, slab_i * width:(slab_i + 1) * width], b_ref[slab_i].astype(BF16), dn,
                                        preferred_element_type=F32)
        else:
            part = lax.dot_general(at, b_ref[...].astype(BF16), dn, preferred_element_type=F32)

        def finish(total):
            res = epilogue(total, *[e[...] for e in ex_refs])
            for r, o in zip(res, out_refs):
                o[...] = r.astype(o.dtype)

        if nk == 1:
            finish(part)
        else:
            acc = rest[0]

            @pl.when(k == 0)
            def _():
                acc[...] = part

            @pl.when(k > 0)
            def _():
                acc[...] += part

            @pl.when(k == nk - 1)
            def _():
                finish(acc[...])

        if n_ride:
            @pl.when((i == grid[0] - 1) & (j == grid[1] - 1) & (k == nk - 1))
            def _():
                exchange.wait()

    scratch = ([pltpu.VMEM((tm, tn), F32)] if nk > 1 else []) + (_exchange_sems(n_ride) if n_ride else [])
    sem = ("arbitrary",) * 3 if n_ride else ("parallel", "parallel", "arbitrary")
    return pl.pallas_call(
        body, name=name, grid=grid,
        in_specs=[a_spec, b_spec, *extra_specs] + [ANY] * n_ride, out_specs=list(out_specs) + [ANY] * n_ride,
        out_shape=[jax.ShapeDtypeStruct(s, d) for s, d in outs] + (_exchange_out_shapes(ride_arrays, scatter) if n_ride else []),
        scratch_shapes=scratch, compiler_params=_cparams(*sem),
    )(a, b, *extras, *ride_arrays)


def _epi_plain(acc):
    return (acc,)


def _epi_relu2(acc):
    return jnp.square(jnp.maximum(acc, 0.0)), acc


def _epi_resgate(acc, res, gate):
    return res + gate * acc, acc


def _epi_drelu2(acc, pre):
    return (acc * (2.0 * jnp.maximum(pre.astype(F32), 0.0)),)


def _silu(v):
    return v * _sigmoid(v)


ROW_TILE = 256


def _row_spec(width, tr=ROW_TILE):
    return pl.BlockSpec((tr, width), lambda i: (i, 0))


def _vec_spec(width):
    return pl.BlockSpec((1, width), lambda i: (0, 0))


def _stat_spec(width):
    return pl.BlockSpec((8, width), lambda i: (0, 0))


def _norm_mod_fwd(name, x, gain, scale, shift):
    s, d = x.shape

    def body(x_ref, g_ref, sc_ref, sh_ref, h_ref):
        xv = x_ref[...]
        r = lax.rsqrt(jnp.mean(xv * xv, axis=-1, keepdims=True) + NORM_EPS)
        h_ref[...] = ((xv * r * g_ref[...]) * (1.0 + sc_ref[...]) + sh_ref[...]).astype(BF16)

    return pl.pallas_call(
        body, name=name, grid=(s // ROW_TILE,),
        in_specs=[_row_spec(d), _vec_spec(d), _vec_spec(d), _vec_spec(d)], out_specs=_row_spec(d),
        out_shape=jax.ShapeDtypeStruct((s, d), BF16), compiler_params=_cparams("parallel"),
    )(x, gain, scale, shift)


GATE_STAT_ROW = 3


def _gate_below(dx, y_ref, gate_ref, dy_ref, st_ref):
    dy_ref[...] = (dx * gate_ref[...]).astype(BF16)
    st_ref[GATE_STAT_ROW:GATE_STAT_ROW + 1, :] += jnp.sum(dx * y_ref[...].astype(F32), axis=0, keepdims=True)


def _norm_mod_bwd(name, x, gain, scale, dh, dres, below=None, ride=None):
    s, d = x.shape

    def body(x_ref, g_ref, sc_ref, dh_ref, dres_ref, *rest):
        dx_ref, st_ref = rest[-2:] if below is None else (rest[2], rest[3])

        @pl.when(pl.program_id(0) == 0)
        def _():
            st_ref[...] = jnp.zeros_like(st_ref)

        xv = x_ref[...]
        dhv = dh_ref[...].astype(F32)
        r = lax.rsqrt(jnp.mean(xv * xv, axis=-1, keepdims=True) + NORM_EPS)
        xh = xv * r
        n = xh * g_ref[...]
        dn = dhv * (1.0 + sc_ref[...])
        dxh = dn * g_ref[...]
        dx = dres_ref[...] + r * (dxh - xh * jnp.mean(dxh * xh, axis=-1, keepdims=True))
        dx_ref[...] = dx
        st_ref[0:1, :] += jnp.sum(dhv, axis=0, keepdims=True)
        st_ref[1:2, :] += jnp.sum(dhv * n, axis=0, keepdims=True)
        st_ref[2:3, :] += jnp.sum(dn * xh, axis=0, keepdims=True)
        if below is not None:
            _gate_below(dx, rest[0], rest[1], rest[4], st_ref)

    extra_in = [] if below is None else [_row_spec(d), _vec_spec(d)]
    extra_out = [] if below is None else [(_row_spec(d), jax.ShapeDtypeStruct((s, d), BF16))]
    res, got = _pcall(
        name, body, grid=(s // ROW_TILE,),
        in_specs=[_row_spec(d), _vec_spec(d), _vec_spec(d), _row_spec(d), _row_spec(d)] + extra_in,
        out_specs=[_row_spec(d), _stat_spec(d)] + [spec for spec, _ in extra_out],
        out_shape=[jax.ShapeDtypeStruct((s, d), F32), jax.ShapeDtypeStruct((8, d), F32)] + [shp for _, shp in extra_out],
        scratch_shapes=[], sem=("arbitrary",), args=(x, gain, scale, dh, dres, *(below or ())), ride=ride)
    return res if ride is None else (res, got)


def _loss_head(name, x, gain, target, below):
    s, d = x.shape

    def body(x_ref, g_ref, t_ref, y_ref, gate_ref, dx_ref, st_ref, dy_ref):
        @pl.when(pl.program_id(0) == 0)
        def _():
            st_ref[...] = jnp.zeros_like(st_ref)

        xv = x_ref[...]
        r = lax.rsqrt(jnp.mean(xv * xv, axis=-1, keepdims=True) + NORM_EPS)
        xh = xv * r
        err = xh * g_ref[...] - t_ref[...]
        dyf = err * (1.0 / d)
        dxh = dyf * g_ref[...]
        dx = r * (dxh - xh * jnp.mean(dxh * xh, axis=-1, keepdims=True))
        dx_ref[...] = dx
        st_ref[0:1, :] += jnp.sum(dyf * xh, axis=0, keepdims=True)
        st_ref[1:2, :] += jnp.sum(err * err, axis=0, keepdims=True) * (0.5 / d)
        _gate_below(dx, y_ref, gate_ref, dy_ref, st_ref)

    return pl.pallas_call(
        body, name=name, grid=(s // ROW_TILE,),
        in_specs=[_row_spec(d), _vec_spec(d), _row_spec(d), _row_spec(d), _vec_spec(d)],
        out_specs=[_row_spec(d), _stat_spec(d), _row_spec(d)],
        out_shape=[jax.ShapeDtypeStruct((s, d), F32), jax.ShapeDtypeStruct((8, d), F32), jax.ShapeDtypeStruct((s, d), BF16)],
        compiler_params=_cparams("arbitrary"),
    )(x, gain, target, *below)


def _adamw(name, parts, w, m, v):
    n_parts, rows, cols = parts.shape
    tr = rows
    for cand in (512, 256, 128, 64, 32, 16, 8):
        if rows % cand == 0 and cand * cols * 4 <= 2 * 1024 * 1024:
            tr = cand
            break
    c1 = 1.0 - ADAM_B1 ** ADAM_STEP
    c2 = 1.0 - ADAM_B2 ** ADAM_STEP

    def body(p_ref, w_ref, m_ref, v_ref, g_out, d_out, m_out, v_out):
        g = p_ref[0].astype(F32)
        for i in range(1, n_parts):
            g = g + p_ref[i].astype(F32)
        m_new = ADAM_B1 * m_ref[...] + (1.0 - ADAM_B1) * g
        v_new = ADAM_B2 * v_ref[...] + (1.0 - ADAM_B2) * (g * g)
        g_out[...] = g
        m_out[...] = m_new
        v_out[...] = v_new
        d_out[...] = -ADAM_LR * ((m_new / c1) / (jnp.sqrt(v_new / c2) + ADAM_EPS) + ADAM_WD * w_ref[...])

    spec = pl.BlockSpec((tr, cols), lambda i: (i, 0))
    return pl.pallas_call(
        body, name=name, grid=(rows // tr,),
        in_specs=[pl.BlockSpec((n_parts, tr, cols), lambda i: (0, i, 0)), spec, spec, spec],
        out_specs=[spec, spec, spec, spec],
        out_shape=[jax.ShapeDtypeStruct((rows, cols), F32)] * 4, compiler_params=_cparams("parallel"),
    )(parts, w, m, v)


ATTN_DILATIONS = (1, 4, 16)


def _attn_fwd(proj, g, ride=None):
    s = proj.shape[1]
    dil = ATTN_DILATIONS[g]
    rows, nb = s // dil, s // dil // ATTN_BLOCK

    def body(q_ref, kp_ref, kc_ref, vp_ref, vc_ref, o_ref, l_ref):
        first = pl.program_id(1) == 0
        qi, kj = _iota((128, 128), 0), _iota((128, 128), 1)
        mask_c = kj <= qi
        mask_p = jnp.logical_and(kj >= qi, jnp.logical_not(first))
        low = kj < 64
        for p in range(4):
            sl = slice(128 * p, 128 * p + 128)
            q, kp, kc, vp, vc = q_ref[:, sl], kp_ref[:, sl], kc_ref[:, sl], vp_ref[:, sl], vc_ref[:, sl]
            o_pair = jnp.zeros((128, 128), F32)
            l_pair = jnp.zeros((128, 128), F32)
            for half in range(2):
                hm = low if half == 0 else jnp.logical_not(low)
                qm = jnp.where(hm, q, jnp.zeros_like(q))
                sc = jnp.where(mask_c, _dot_nt(qm, kc) * 0.125, NEG_BIG)
                sp = jnp.where(mask_p, _dot_nt(qm, kp) * 0.125, NEG_BIG)
                m = jnp.maximum(jnp.max(sc, axis=-1, keepdims=True), jnp.max(sp, axis=-1, keepdims=True))
                pc, pp = jnp.exp(sc - m), jnp.exp(sp - m)
                den = jnp.sum(pc, axis=-1, keepdims=True) + jnp.sum(pp, axis=-1, keepdims=True)
                oh = _dot((pc / den).astype(BF16), vc) + _dot((pp / den).astype(BF16), vp)
                o_pair = jnp.where(hm, oh, o_pair)
                l_pair = jnp.where(hm, m + jnp.log(den), l_pair)
            o_ref[:, sl] = o_pair
            l_ref[:, sl] = l_pair

    view = proj[3 * g:3 * g + 3].reshape(3, rows, dil * ATTN_GROUP_WIDTH)
    blk = (None, ATTN_BLOCK, ATTN_GROUP_WIDTH)
    prev = lambda j: jnp.maximum(j - 1, 0)
    out_blk = pl.BlockSpec((ATTN_BLOCK, ATTN_GROUP_WIDTH), lambda r, j: (j, r))
    (o, lse), got = _pcall(
        f"attn_fwd_g{g}", body, grid=(dil, nb),
        in_specs=[pl.BlockSpec(blk, lambda r, j: (0, j, r)),
                  pl.BlockSpec(blk, lambda r, j: (1, prev(j), r)), pl.BlockSpec(blk, lambda r, j: (1, j, r)),
                  pl.BlockSpec(blk, lambda r, j: (2, prev(j), r)), pl.BlockSpec(blk, lambda r, j: (2, j, r))],
        out_specs=[out_blk] * 2, out_shape=[jax.ShapeDtypeStruct((rows, dil * ATTN_GROUP_WIDTH), F32)] * 2,
        scratch_shapes=[], sem=("parallel", "parallel"), args=(view,) * 5, ride=ride)
    return o.reshape(s, ATTN_GROUP_WIDTH), lse.reshape(s, ATTN_GROUP_WIDTH), got


def _attn_bwd(proj, g, do, lse, cc, ride=None):
    s = proj.shape[1]
    dil = ATTN_DILATIONS[g]
    rows, nblk = s // dil, s // dil // ATTN_BLOCK

    def body(q_ref, kp_ref, kc_ref, vp_ref, vc_ref, do_ref, l_ref, c_ref, dq_ref, dk_ref, dv_ref, ck, cv):
        j = pl.program_id(1)
        valid = j < nblk
        first = jnp.minimum(j, nblk - 1) == 0

        @pl.when(j == 0)
        def _():
            ck[...] = jnp.zeros_like(ck)
            cv[...] = jnp.zeros_like(cv)

        qi, kj = _iota((128, 128), 0), _iota((128, 128), 1)
        mask_c = jnp.logical_and(kj <= qi, valid)
        mask_p = jnp.logical_and(jnp.logical_and(kj >= qi, jnp.logical_not(first)), valid)
        low = kj < 64
        for p in range(4):
            sl = slice(128 * p, 128 * p + 128)
            q, kp, kc, vp, vc, dov = q_ref[:, sl], kp_ref[:, sl], kc_ref[:, sl], vp_ref[:, sl], vc_ref[:, sl], do_ref[:, sl]
            lse_pair, c_pair = l_ref[:, sl], c_ref[:, sl]
            dq_pair = jnp.zeros((128, 128), F32)
            dkc = jnp.zeros((128, 128), F32)
            dkp = jnp.zeros((128, 128), F32)
            dvc = jnp.zeros((128, 128), F32)
            dvp = jnp.zeros((128, 128), F32)
            for half in range(2):
                hm = low if half == 0 else jnp.logical_not(low)
                col = slice(64 * half, 64 * half + 1)
                lse_h, c_h = lse_pair[:, col], c_pair[:, col]
                qm = jnp.where(hm, q, jnp.zeros_like(q))
                dom = jnp.where(hm, dov, jnp.zeros_like(dov))
                pc = jnp.exp(jnp.where(mask_c, _dot_nt(qm, kc) * 0.125, NEG_BIG) - lse_h)
                pp = jnp.exp(jnp.where(mask_p, _dot_nt(qm, kp) * 0.125, NEG_BIG) - lse_h)
                dsc = (pc * (_dot_nt(dom, vc) + c_h) * 0.125).astype(BF16)
                dsp = (pp * (_dot_nt(dom, vp) + c_h) * 0.125).astype(BF16)
                dq_pair = jnp.where(hm, _dot(dsc, kc) + _dot(dsp, kp), dq_pair)
                dkc += _dot_tn(dsc, qm)
                dkp += _dot_tn(dsp, qm)
                dvc += _dot_tn(pc.astype(BF16), dom)
                dvp += _dot_tn(pp.astype(BF16), dom)

            @pl.when(valid)
            def _():
                dq_ref[:, sl] = dq_pair.astype(BF16)

            dk_ref[:, sl] = (ck[:, sl] + dkp).astype(BF16)
            dv_ref[:, sl] = (cv[:, sl] + dvp).astype(BF16)
            ck[:, sl] = dkc
            cv[:, sl] = dvc

    wide = dil * ATTN_GROUP_WIDTH
    view = proj[3 * g:3 * g + 3].reshape(3, rows, wide)
    blk = (None, ATTN_BLOCK, ATTN_GROUP_WIDTH)
    flat = (ATTN_BLOCK, ATTN_GROUP_WIDTH)
    cur = lambda j: jnp.minimum(j, nblk - 1)
    prev = lambda j: jnp.maximum(jnp.minimum(j, nblk - 1) - 1, 0)
    out_prev = lambda j: jnp.maximum(j - 1, 0)
    (dq, dk, dv), got = _pcall(
        f"attn_bwd_g{g}", body, grid=(dil, nblk + 1),
        in_specs=[pl.BlockSpec(blk, lambda r, j: (0, cur(j), r)),
                  pl.BlockSpec(blk, lambda r, j: (1, prev(j), r)), pl.BlockSpec(blk, lambda r, j: (1, cur(j), r)),
                  pl.BlockSpec(blk, lambda r, j: (2, prev(j), r)), pl.BlockSpec(blk, lambda r, j: (2, cur(j), r)),
                  pl.BlockSpec(flat, lambda r, j: (cur(j), r)), pl.BlockSpec(flat, lambda r, j: (cur(j), r)),
                  pl.BlockSpec(flat, lambda r, j: (cur(j), r))],
        out_specs=[pl.BlockSpec(flat, lambda r, j: (cur(j), r)),
                   pl.BlockSpec(flat, lambda r, j: (out_prev(j), r)), pl.BlockSpec(flat, lambda r, j: (out_prev(j), r))],
        out_shape=[jax.ShapeDtypeStruct((rows, wide), BF16)] * 3,
        scratch_shapes=[pltpu.VMEM((ATTN_BLOCK, ATTN_GROUP_WIDTH), F32)] * 2, sem=("parallel", "arbitrary"),
        args=(view,) * 5 + (do.reshape(rows, wide), lse.reshape(rows, wide), cc.reshape(rows, wide)), ride=ride)
    return [t.reshape(s, ATTN_GROUP_WIDTH) for t in (dq, dk, dv)], got


MP_TILE = 256


def _merge_weights(l_refs):
    l0, l1, l2 = l_refs[0][...], l_refs[1][...], l_refs[2][...]
    m = jnp.maximum(jnp.maximum(l0, l1), l2)
    e0, e1, e2 = jnp.exp(l0 - m), jnp.exp(l1 - m), jnp.exp(l2 - m)
    den = e0 + e1 + e2
    return e0 / den, e1 / den, e2 / den


def _pool_diff(ucat, gi, tok):
    window = 2 << gi
    ug = ucat[:, 128 * gi:128 * gi + 128]
    acc, shift = ug, 1
    while shift < window:
        acc = acc + pltpu.roll(acc, shift, 0)
        shift *= 2
    cnt = jnp.minimum(tok + 1, window).astype(F32)
    return acc[POOL_HALO:, :] / cnt - ug[POOL_HALO:, :]


def _merge_pool_fwd(o, lse, proj, pool_w, pool_scale):
    s = o[0].shape[0]
    tr = MP_TILE

    def body(o0_ref, o1_ref, o2_ref, l0_ref, l1_ref, l2_ref, u_ref, uh_ref, pw_ref, ps_ref, cat_ref):
        i = pl.program_id(0)
        w0, w1, w2 = _merge_weights((l0_ref, l1_ref, l2_ref))
        cat_ref[:, 0:512] = (w0 * o0_ref[...] + w1 * o1_ref[...] + w2 * o2_ref[...]).astype(BF16)
        halo = jnp.where(i > 0, uh_ref[...].astype(F32), 0.0)
        ucat = jnp.concatenate([halo, u_ref[...].astype(F32)], axis=0)
        tok = i * tr + _iota((tr, 1), 0)
        for gi in range(POOL_GROUPS):
            sl = slice(128 * gi, 128 * gi + 128)
            diff = _pool_diff(ucat, gi, tok)
            yg = _dot(diff.astype(BF16), pw_ref[gi].astype(BF16)) * ps_ref[:, sl]
            cat_ref[:, 512 + 128 * gi:640 + 128 * gi] = yg.astype(BF16)

    return pl.pallas_call(
        body, name="merge_pool_fwd", grid=(s // tr,),
        in_specs=[pl.BlockSpec((tr, 512), lambda i: (i, 0))] * 6 + [
                  pl.BlockSpec((None, tr, 512), lambda i: (9, i, 0)),
                  pl.BlockSpec((None, POOL_HALO, 512), lambda i: (9, jnp.maximum(i * (tr // POOL_HALO) - 1, 0), 0)),
                  pl.BlockSpec((4, 128, 128), lambda i: (0, 0, 0)), pl.BlockSpec((1, 512), lambda i: (0, 0))],
        out_specs=pl.BlockSpec((tr, 1024), lambda i: (i, 0)),
        out_shape=jax.ShapeDtypeStruct((s, EVEN_OUT_WIDTH), BF16), compiler_params=_cparams("parallel"),
    )(*o, *lse, proj, proj, pool_w, pool_scale)


def _merge_pool_bwd(dcat, o, lse, proj, pool_w, pool_scale, head_sum):
    s = o[0].shape[0]
    tr = MP_TILE
    n_tiles = s // tr

    def body(da_ref, dp_ref, dph_ref, o0_ref, o1_ref, o2_ref, l0_ref, l1_ref, l2_ref, u_ref, uh_ref, pw_ref, ps_ref,
             hs_ref, do0_ref, do1_ref, do2_ref, cc0_ref, cc1_ref, cc2_ref, du_ref, dpw_ref, st_ref):
        i = pl.program_id(0)

        @pl.when(i == 0)
        def _():
            dpw_ref[...] = jnp.zeros_like(dpw_ref)
            st_ref[...] = jnp.zeros_like(st_ref)

        ws = _merge_weights((l0_ref, l1_ref, l2_ref))
        da = da_ref[...]
        attn = ws[0] * o0_ref[...] + ws[1] * o1_ref[...] + ws[2] * o2_ref[...]
        per_head = _dot3_right(da * attn, hs_ref[...])
        for wg, do_ref, cc_ref in zip(ws, (do0_ref, do1_ref, do2_ref), (cc0_ref, cc1_ref, cc2_ref)):
            do_ref[...] = (wg * da).astype(BF16)
            cc_ref[...] = -wg * per_head

        halo = jnp.where(i > 0, uh_ref[...].astype(F32), 0.0)
        ucat = jnp.concatenate([halo, u_ref[...].astype(F32)], axis=0)
        tok = i * tr + _iota((tr, 1), 0)
        dyp = dp_ref[...]
        dnext = jnp.where(i < n_tiles - 1, dph_ref[...], 0.0)
        dyp_ext = jnp.concatenate([dyp, dnext], axis=0)
        tok_ext = i * tr + _iota((tr + POOL_HALO, 1), 0)
        for gi in range(POOL_GROUPS):
            sl = slice(128 * gi, 128 * gi + 128)
            window = 2 << gi
            pw16 = pw_ref[gi].astype(BF16)
            d16 = _pool_diff(ucat, gi, tok).astype(BF16)
            st_ref[0:1, sl] += jnp.sum(dyp[:, sl] * _dot(d16, pw16), axis=0, keepdims=True)
            dpw_ref[gi] += _dot_tn(d16, (dyp[:, sl] * ps_ref[:, sl]).astype(BF16))
            dd = _dot_nt((dyp_ext[:, sl] * ps_ref[:, sl]).astype(BF16), pw16)
            acc = dd / jnp.minimum(tok_ext + 1, window).astype(F32)
            shift = 1
            while shift < window:
                acc = acc + pltpu.roll(acc, tr + POOL_HALO - shift, 0)
                shift *= 2
            du_ref[:, sl] = (acc[:tr, :] - dd[:tr, :]).astype(BF16)

    halo_blocks = tr // POOL_HALO
    return pl.pallas_call(
        body, name="merge_pool_bwd", grid=(n_tiles,),
        in_specs=[pl.BlockSpec((tr, 512), lambda i: (i, 0)), pl.BlockSpec((tr, 512), lambda i: (i, 1)),
                  pl.BlockSpec((POOL_HALO, 512), lambda i: (jnp.minimum((i + 1) * halo_blocks, s // POOL_HALO - 1), 1))]
                 + [pl.BlockSpec((tr, 512), lambda i: (i, 0))] * 6 + [
                  pl.BlockSpec((None, tr, 512), lambda i: (9, i, 0)),
                  pl.BlockSpec((None, POOL_HALO, 512), lambda i: (9, jnp.maximum(i * halo_blocks - 1, 0), 0)),
                  pl.BlockSpec((4, 128, 128), lambda i: (0, 0, 0)), pl.BlockSpec((1, 512), lambda i: (0, 0)),
                  pl.BlockSpec((512, 512), lambda i: (0, 0))],
        out_specs=[pl.BlockSpec((tr, 512), lambda i: (i, 0))] * 7 + [
                   pl.BlockSpec((4, 128, 128), lambda i: (0, 0, 0)), pl.BlockSpec((8, 512), lambda i: (0, 0))],
        out_shape=[jax.ShapeDtypeStruct((s, 512), BF16)] * 3 + [jax.ShapeDtypeStruct((s, 512), F32)] * 3 + [
                   jax.ShapeDtypeStruct((s, 512), BF16), jax.ShapeDtypeStruct((4, 128, 128), F32),
                   jax.ShapeDtypeStruct((8, 512), F32)],
        compiler_params=_cparams("arbitrary"),
    )(dcat, dcat, dcat, *o, *lse, proj, proj, pool_w, pool_scale, head_sum)


CONV_TILE = 1024
CONV_HALO = 8
XBC_BLOCK0 = SSM_D_INNER // 512
DT_BLOCK = (SSM_D_INNER + SSM_CONV_DIM) // 128


def _conv_taps(xcat, w, bias):
    pre = bias + w[3:4, :] * xcat[CONV_HALO:, :]
    for back in range(1, SSM_CONV):
        pre = pre + w[3 - back:4 - back, :] * pltpu.roll(xcat, back, 0)[CONV_HALO:, :]
    return pre


def _conv_fwd(proj, conv_w, conv_b):
    s = proj.shape[0]
    tr = CONV_TILE

    def body(x_ref, xh_ref, w_ref, b_ref, act_ref):
        i = pl.program_id(0)
        xcat = jnp.concatenate([jnp.where(i > 0, xh_ref[...], 0.0), x_ref[...]], axis=0)
        pre = _conv_taps(xcat, w_ref[...], b_ref[...])
        act_ref[...] = pre * _sigmoid(pre)

    hb = tr // CONV_HALO
    return pl.pallas_call(
        body, name="conv_fwd", grid=(s // tr, SSM_CONV_DIM // 512),
        in_specs=[pl.BlockSpec((tr, 512), lambda i, j: (i, XBC_BLOCK0 + j)),
                  pl.BlockSpec((CONV_HALO, 512), lambda i, j: (jnp.maximum(i * hb - 1, 0), XBC_BLOCK0 + j)),
                  pl.BlockSpec((SSM_CONV, 512), lambda i, j: (0, j)), pl.BlockSpec((1, 512), lambda i, j: (0, j))],
        out_specs=pl.BlockSpec((tr, 512), lambda i, j: (i, j)),
        out_shape=jax.ShapeDtypeStruct((s, SSM_CONV_DIM), F32), compiler_params=_cparams("parallel", "parallel"),
    )(proj, proj, conv_w, conv_b)


def _conv_bwd(name, dact, proj, conv_w, conv_b, dproj, *, block0):
    s, width = dact.shape
    tr = CONV_TILE
    n_tiles = s // tr
    hb = tr // CONV_HALO

    def body(da_ref, dah_ref, x_ref, xh_ref, xn_ref, w_ref, b_ref, _, dp_ref, st_ref):
        i = pl.program_id(1)

        @pl.when(i == 0)
        def _():
            st_ref[...] = jnp.zeros_like(st_ref)

        da_ext = jnp.concatenate([da_ref[...], jnp.where(i < n_tiles - 1, dah_ref[...], 0.0)], axis=0)
        xcat = jnp.concatenate([jnp.where(i > 0, xh_ref[...], 0.0), x_ref[...]], axis=0)
        pre_ext = _conv_taps(jnp.concatenate([xcat, xn_ref[...]], axis=0), w_ref[...], b_ref[...])
        sg = _sigmoid(pre_ext)
        dpre_ext = da_ext * (sg * (1.0 + pre_ext * (1.0 - sg)))
        w = w_ref[...]
        draw = w[3:4, :] * dpre_ext[:tr, :]
        for ahead in range(1, SSM_CONV):
            draw = draw + w[3 - ahead:4 - ahead, :] * pltpu.roll(dpre_ext, tr + CONV_HALO - ahead, 0)[:tr, :]
        dp_ref[...] = draw.astype(BF16)
        dpre = dpre_ext[:tr, :]
        st_ref[3:4, :] += jnp.sum(dpre * xcat[CONV_HALO:, :], axis=0, keepdims=True)
        for back in range(1, SSM_CONV):
            st_ref[3 - back:4 - back, :] += jnp.sum(dpre * pltpu.roll(xcat, back, 0)[CONV_HALO:, :], axis=0, keepdims=True)
        st_ref[4:5, :] += jnp.sum(dpre, axis=0, keepdims=True)

    nxt = lambda i: jnp.minimum((i + 1) * hb, s // CONV_HALO - 1)
    prv = lambda i: jnp.maximum(i * hb - 1, 0)
    return pl.pallas_call(
        body, name=name, grid=(width // 512, n_tiles),
        in_specs=[pl.BlockSpec((tr, 512), lambda j, i: (i, j)), pl.BlockSpec((CONV_HALO, 512), lambda j, i: (nxt(i), j)),
                  pl.BlockSpec((tr, 512), lambda j, i: (i, XBC_BLOCK0 + block0 + j)),
                  pl.BlockSpec((CONV_HALO, 512), lambda j, i: (prv(i), XBC_BLOCK0 + block0 + j)),
                  pl.BlockSpec((CONV_HALO, 512), lambda j, i: (nxt(i), XBC_BLOCK0 + block0 + j)),
                  pl.BlockSpec((SSM_CONV, 512), lambda j, i: (0, block0 + j)),
                  pl.BlockSpec((1, 512), lambda j, i: (0, block0 + j)), ANY],
        out_specs=[pl.BlockSpec((tr, 512), lambda j, i: (i, XBC_BLOCK0 + block0 + j)),
                   pl.BlockSpec((8, 512), lambda j, i: (0, j))],
        out_shape=[jax.ShapeDtypeStruct(dproj.shape, BF16), jax.ShapeDtypeStruct((8, width), F32)],
        input_output_aliases={7: 0}, compiler_params=_cparams("parallel", "arbitrary"),
    )(dact, dact, proj, proj, proj, conv_w, conv_b, dproj)


def _dt_fwd(proj, dt_bias):
    s = proj.shape[0]

    def body(x_ref, b_ref, o_ref):
        v = x_ref[...] + b_ref[...]
        o_ref[...] = jnp.maximum(v, 0.0) + jnp.log(1.0 + jnp.exp(-jnp.abs(v)))

    return pl.pallas_call(
        body, name="dt_fwd", grid=(s // CONV_TILE,),
        in_specs=[pl.BlockSpec((CONV_TILE, 128), lambda i: (i, DT_BLOCK)), pl.BlockSpec((1, 128), lambda i: (0, 0))],
        out_specs=pl.BlockSpec((CONV_TILE, 128), lambda i: (i, 0)),
        out_shape=jax.ShapeDtypeStruct((s, 128), F32), compiler_params=_cparams("parallel"),
    )(proj, dt_bias)


def _dt_bwd(ddt, proj, dt_bias, dproj):
    s = proj.shape[0]

    def body(d_ref, x_ref, b_ref, _, dp_ref, st_ref):
        @pl.when(pl.program_id(0) == 0)
        def _():
            st_ref[...] = jnp.zeros_like(st_ref)

        draw = d_ref[...] * _sigmoid(x_ref[...] + b_ref[...])
        dp_ref[...] = draw.astype(BF16)
        st_ref[0:1, :] += jnp.sum(draw, axis=0, keepdims=True)

    return pl.pallas_call(
        body, name="dt_bwd", grid=(s // CONV_TILE,),
        in_specs=[pl.BlockSpec((CONV_TILE, 128), lambda i: (i, 0)), pl.BlockSpec((CONV_TILE, 128), lambda i: (i, DT_BLOCK)),
                  pl.BlockSpec((1, 128), lambda i: (0, 0)), ANY],
        out_specs=[pl.BlockSpec((CONV_TILE, 128), lambda i: (i, DT_BLOCK)), pl.BlockSpec((8, 128), lambda i: (0, 0))],
        out_shape=[jax.ShapeDtypeStruct(dproj.shape, BF16), jax.ShapeDtypeStruct((8, 128), F32)],
        input_output_aliases={3: 0}, compiler_params=_cparams("arbitrary"),
    )(ddt, proj, dt_bias, dproj)


def _ssd_common(x_ref, b_ref, c_ref, dt_ref, al_ref):
    row, col = _iota((128, 128), 0), _iota((128, 128), 1)
    tril = row >= col
    expand = jnp.where((_iota((128, 512), 1) >> 6) == _iota((128, 512), 0), 1.0, 0.0).astype(BF16)
    dt = dt_ref[...]
    a_neg = -jnp.exp(al_ref[...])
    a_col = _dot3_left(jnp.where(tril, 1.0, 0.0).astype(BF16), dt * a_neg)
    a_exp = _dot3_right(a_col, expand)
    dt_exp = _dot3_right(dt, expand)
    x = x_ref[...]
    return dict(tril=tril, col=col, expand=expand, dt=dt, a_neg=a_neg, a_col=a_col, a_row=a_col.T, a_exp=a_exp,
                dt_exp=dt_exp, a_last=a_exp[127:128, :], x=x, xd=x * dt_exp,
                b16=b_ref[...].astype(BF16), c16=c_ref[...].astype(BF16))


def _ssd_specs(nc, order):
    return [pl.BlockSpec((SSM_CHUNK, 512), lambda g, c: (order(c), g)),
            pl.BlockSpec((SSM_CHUNK, 128), lambda g, c: (order(c), SSM_D_INNER // 128 + g)),
            pl.BlockSpec((SSM_CHUNK, 128), lambda g, c: (order(c), SSM_D_INNER // 128 + SSM_GROUPS + g)),
            pl.BlockSpec((SSM_CHUNK, 512), lambda g, c: (order(c), g)),
            pl.BlockSpec((None, SSM_CHUNK, 128), lambda g, c: (g, order(c), 0)),
            pl.BlockSpec((None, 1, 128), lambda g, c: (g, 0, 0)),
            pl.BlockSpec((1, 512), lambda g, c: (0, g)), pl.BlockSpec((1, 512), lambda g, c: (0, g))]


def _ssd_fwd(act, proj, dtc, a_log, d_skip, norm_g, ride=None):
    s = act.shape[0]
    nc = s // SSM_CHUNK

    def body(x_ref, b_ref, c_ref, z_ref, dt_ref, al_ref, dsk_ref, ng_ref, y_ref, yn_ref, hin_ref, h_sc):
        @pl.when(pl.program_id(1) == 0)
        def _():
            h_sc[...] = jnp.zeros_like(h_sc)

        q = _ssd_common(x_ref, b_ref, c_ref, dt_ref, al_ref)
        gmat = _dot_nt(q["c16"], q["b16"])
        h_in = h_sc[...]
        hin_ref[...] = h_in
        zmat = _dot(q["c16"], h_in.astype(BF16))
        xd16 = q["xd"].astype(BF16)
        low = q["col"] < 64
        pieces = []
        for p in range(4):
            xs = xd16[:, 128 * p:128 * p + 128]
            acc = jnp.zeros((128, 128), F32)
            for half in range(2):
                r = 2 * p + half
                decay = jnp.exp(jnp.where(q["tril"], q["a_col"][:, r:r + 1] - q["a_row"][r:r + 1, :], NEG_BIG))
                hm = low if half == 0 else jnp.logical_not(low)
                acc += _dot((gmat * decay).astype(BF16), jnp.where(hm, xs, jnp.zeros_like(xs)))
            pieces.append(acc)
        y = jnp.concatenate(pieces, axis=1) + zmat * jnp.exp(q["a_exp"]) + q["x"] * dsk_ref[...]
        y_ref[...] = y
        w16 = (q["xd"] * jnp.exp(q["a_last"] - q["a_exp"])).astype(BF16)
        h_sc[...] = h_in * jnp.exp(q["a_last"]) + _dot_tn(q["b16"], w16)
        z = z_ref[...]
        yg = y * (z * _sigmoid(z))
        rr = lax.rsqrt(jnp.mean(yg * yg, axis=-1, keepdims=True) + NORM_EPS)
        yn_ref[...] = (yg * rr * ng_ref[...]).astype(BF16)

    blk = pl.BlockSpec((SSM_CHUNK, 512), lambda g, c: (c, g))
    return _pcall(
        "ssd_fwd", body, grid=(SSM_GROUPS, nc), in_specs=_ssd_specs(nc, lambda c: c),
        out_specs=[blk, blk, pl.BlockSpec((None, None, SSM_STATE, 512), lambda g, c: (g, c, 0, 0))],
        out_shape=[jax.ShapeDtypeStruct((s, SSM_D_INNER), F32), jax.ShapeDtypeStruct((s, SSM_D_INNER), BF16),
                   jax.ShapeDtypeStruct((SSM_GROUPS, nc, SSM_STATE, 512), F32)],
        scratch_shapes=[pltpu.VMEM((SSM_STATE, 512), F32)], sem=("parallel", "arbitrary"),
        args=(act, act, act, proj, dtc, a_log, d_skip, norm_g), ride=ride)


def _ssd_bwd(act, proj, dtc, a_log, d_skip, norm_g, y, h_in_all, dyn, ride=None):
    s = act.shape[0]
    nc = s // SSM_CHUNK

    def body(x_ref, b_ref, c_ref, z_ref, dt_ref, al_ref, dsk_ref, ng_ref, y_ref, hin_ref, dyn_ref,
             dz_ref, dx_ref, db_ref, dc_ref, ddt_ref, st_ref, dal_ref, dh_sc):
        @pl.when(pl.program_id(1) == 0)
        def _():
            dh_sc[...] = jnp.zeros_like(dh_sc)
            st_ref[...] = jnp.zeros_like(st_ref)
            dal_ref[...] = jnp.zeros_like(dal_ref)

        q = _ssd_common(x_ref, b_ref, c_ref, dt_ref, al_ref)
        x, xd, b16, c16 = q["x"], q["xd"], q["b16"], q["c16"]
        z, yv, dyn_v = z_ref[...], y_ref[...], dyn_ref[...]
        sig = _sigmoid(z)
        sil = z * sig
        yg = yv * sil
        rr = lax.rsqrt(jnp.mean(yg * yg, axis=-1, keepdims=True) + NORM_EPS)
        st_ref[0:1, :] += jnp.sum(dyn_v * yg * rr, axis=0, keepdims=True)
        t1 = dyn_v * ng_ref[...]
        dyg = rr * (t1 - yg * (rr * rr) * jnp.mean(t1 * yg, axis=-1, keepdims=True))
        dy = dyg * sil
        dz_ref[...] = (dyg * yv * (sig * (1.0 + z * (1.0 - sig)))).astype(BF16)
        st_ref[1:2, :] += jnp.sum(dy * x, axis=0, keepdims=True)
        dx = dsk_ref[...] * dy
        h_in = hin_ref[...]
        h16 = h_in.astype(BF16)
        ea = jnp.exp(q["a_exp"])
        zmat = _dot(c16, h16)
        dz16 = (dy * ea).astype(BF16)
        da_ch = dy * zmat * ea
        dcm = _dot_nt(dz16, h16)
        dh_in = _dot_tn(c16, dz16)
        dh_out = dh_sc[...]
        dho16 = dh_out.astype(BF16)
        eal = jnp.exp(q["a_last"])
        dh_in += dh_out * eal
        dal_ch = jnp.sum(dh_out * h_in, axis=0, keepdims=True) * eal
        to_end = jnp.exp(q["a_last"] - q["a_exp"])
        wmat = xd * to_end
        dbm = _dot_nt(wmat.astype(BF16), dho16)
        dw = _dot(b16, dho16)
        dxd = dw * to_end
        g_end = dw * wmat
        da_ch -= g_end
        dal_ch += jnp.sum(g_end, axis=0, keepdims=True)
        gmat = _dot_nt(c16, b16)
        row = _iota((128, 128), 0)
        triu = row <= q["col"]
        xd16, dy16 = xd.astype(BF16), dy.astype(BF16)
        low = q["col"] < 64
        da_col = jnp.zeros((128, 128), F32)
        da_key = jnp.zeros((128, 128), F32)
        dg = jnp.zeros((128, 128), F32)
        pieces = []
        for p in range(4):
            xs, dys = xd16[:, 128 * p:128 * p + 128], dy16[:, 128 * p:128 * p + 128]
            acc = jnp.zeros((128, 128), F32)
            for half in range(2):
                r = 2 * p + half
                hm = low if half == 0 else jnp.logical_not(low)
                xm = jnp.where(hm, xs, jnp.zeros_like(xs))
                dym = jnp.where(hm, dys, jnp.zeros_like(dys))
                decay = jnp.exp(jnp.where(q["tril"], q["a_col"][:, r:r + 1] - q["a_row"][r:r + 1, :], NEG_BIG))
                acc += _dot_tn((gmat * decay).astype(BF16), dym)
                dgl = _dot_nt(dym, xm) * decay
                dg += dgl
                n_ls = dgl * gmat
                da_col += jnp.where(q["col"] == r, jnp.sum(n_ls, axis=-1, keepdims=True), 0.0)
                da_key += jnp.where(row == r, jnp.sum(n_ls, axis=0, keepdims=True), 0.0)
            pieces.append(acc)
        da_col -= da_key.T
        dxd += jnp.concatenate(pieces, axis=1)
        dg16 = dg.astype(BF16)
        dcm += _dot(dg16, b16)
        dbm += _dot_tn(dg16, c16)
        fold = jnp.where((_iota((512, 128), 0) >> 6) == _iota((512, 128), 1), 1.0, 0.0).astype(BF16)
        da_ch += jnp.where(_iota((128, 1), 0) == 127, dal_ch, 0.0)
        da_col += _dot3_right(da_ch, fold)
        d_dta = _dot3_left(jnp.where(triu, 1.0, 0.0).astype(BF16), da_col)
        ddt_ref[...] = d_dta * q["a_neg"] + _dot3_right(dxd * x, fold)
        dal_ref[0:1, :] += jnp.sum(d_dta * q["dt"] * q["a_neg"], axis=0, keepdims=True)
        dx_ref[...] = dx + dxd * q["dt_exp"]
        db_ref[...] = dbm
        dc_ref[...] = dcm
        dh_sc[...] = dh_in

    rev = lambda c: nc - 1 - c
    blk = pl.BlockSpec((SSM_CHUNK, 512), lambda g, c: (rev(c), g))
    small = pl.BlockSpec((SSM_CHUNK, 128), lambda g, c: (rev(c), g))
    return _pcall(
        "ssd_bwd", body, grid=(SSM_GROUPS, nc), ride=ride,
        args=(act, act, act, proj, dtc, a_log, d_skip, norm_g, y, h_in_all, dyn), sem=("parallel", "arbitrary"),
        in_specs=_ssd_specs(nc, rev) + [blk, pl.BlockSpec((None, None, SSM_STATE, 512), lambda g, c: (g, rev(c), 0, 0)), blk],
        out_specs=[blk, blk, small, small, pl.BlockSpec((None, SSM_CHUNK, 128), lambda g, c: (g, rev(c), 0)),
                   pl.BlockSpec((8, 512), lambda g, c: (0, g)), pl.BlockSpec((None, 8, 128), lambda g, c: (g, 0, 0))],
        out_shape=[jax.ShapeDtypeStruct((s, SSM_IN_PAD), BF16), jax.ShapeDtypeStruct((s, SSM_D_INNER), F32),
                   jax.ShapeDtypeStruct((s, SSM_GROUPS * SSM_STATE), F32), jax.ShapeDtypeStruct((s, SSM_GROUPS * SSM_STATE), F32),
                   jax.ShapeDtypeStruct((SSM_GROUPS, s, 128), F32), jax.ShapeDtypeStruct((8, SSM_D_INNER), F32),
                   jax.ShapeDtypeStruct((SSM_GROUPS, 8, 128), F32)],
        scratch_shapes=[pltpu.VMEM((SSM_STATE, 512), F32)])


MM_TM = 1024
MM_TK = 512
MM_TK_BIG = 2048
FFN_SHARD = FFN_HIDDEN // N_DEV


def _ij(tm, tn):
    return pl.BlockSpec((tm, tn), lambda i, j, k: (i, j))


def _mm_plain(name, a, b, mode, dims, out_dtype, tn=1024, tk=MM_TK, **kw):
    res = _matmul(name, a, b, mode=mode, dims=dims, tiles=(MM_TM, tn, tk), outs=[((dims[0], dims[1]), out_dtype)],
                  epilogue=_epi_plain, **kw)
    return res[0], res[1:]


def _mm_resgate(name, a, b, res, gate, k_dim, tk, ride=None):
    s, d = res.shape
    got = _matmul(name, a, b, mode="nn", dims=(s, d, k_dim), tiles=(MM_TM, 1024, tk),
                  outs=[((s, d), F32), ((s, d), BF16)], epilogue=_epi_resgate, extras=[res, gate],
                  extra_specs=[_ij(MM_TM, 1024), pl.BlockSpec((1, 1024), lambda i, j, k: (0, j))], ride=ride)
    return got[0], got[1], got[2:]


def _ffn_fwd(tag, x_in, gain, scale, shift, gate, w1, w2_of, ride1=None, ride2=None):
    s, d = x_in.shape
    h = _norm_mod_fwd(tag + "_norm2", x_in, gain, scale, shift)
    rr, pre, *got1 = _matmul(tag + "_ffn1", h, w1, mode="nn", dims=(s, FFN_HIDDEN, d), tiles=(MM_TM, FFN_SHARD, MM_TK_BIG),
                             b_spec=pl.BlockSpec((None, MM_TK_BIG, FFN_SHARD), lambda i, j, k: (j, k, 0)),
                             outs=[((s, FFN_HIDDEN), BF16)] * 2, epilogue=_epi_relu2, ride=ride1)
    x_out, f, got2 = _mm_resgate(tag + "_ffn2", rr, w2_of(got1), x_in, gate, FFN_HIDDEN, MM_TK_BIG, ride=ride2)
    return x_out, (h, rr, pre, f), got1, got2


def _ffn_bwd(tag, dx_out, dy, x_in, saved, gain, scale, w1, w2, below=None, ride_dx2=None, ride_dw2=None,
             ride_dx1_of=None):
    s, d = x_in.shape
    h, rr, pre, _ = saved
    da, *got2 = _matmul(tag + "_ffn2_dx", dy, w2, mode="nt", dims=(s, FFN_HIDDEN, d), tiles=(MM_TM, 1024, MM_TK_BIG),
                        outs=[((s, FFN_HIDDEN), BF16)], epilogue=_epi_drelu2, extras=[pre],
                        extra_specs=[_ij(MM_TM, 1024)], ride=ride_dx2)
    dw2, got_dw2 = _mm_plain(tag + "_ffn2_dw", rr, dy, "tn", (FFN_HIDDEN, d, s), BF16, tk=MM_TK_BIG, ride=ride_dw2)
    dh, got1 = _mm_plain(tag + "_ffn1_dx", da, w1, "nt", (s, d, FFN_HIDDEN), F32, tk=2 * FFN_SHARD, b_slabs=2,
                         b_spec=pl.BlockSpec((2, 1024, FFN_SHARD), lambda i, j, k: (k, j, 0)),
                         ride=None if ride_dx1_of is None else ride_dx1_of(dw2))
    dw1 = _matmul(tag + "_ffn1_dw", h, da, mode="tn", dims=(d, FFN_HIDDEN, s), tiles=(MM_TM, FFN_SHARD, MM_TK_BIG),
                  outs=[((N_DEV, d, FFN_SHARD), BF16)], epilogue=_epi_plain,
                  out_specs=[pl.BlockSpec((None, MM_TM, FFN_SHARD), lambda i, j, k: (j, i, 0))])[0]
    dx_in, st_norm, *dy_below = _norm_mod_bwd(tag + "_norm2_bwd", x_in, gain, scale, dh, dx_out, below)
    return dx_in, dw1, dw2, st_norm, (dy_below[0] if dy_below else None), got2, got_dw2, got1


def _slab_of(col_block):
    return jnp.where(col_block < 9, 3 * (col_block % 3) + col_block // 3, 9)


def _full_weight(name, gathered):
    if name in ASSEMBLED:
        return _assemble_cols("assemble_" + name, gathered, ASSEMBLED[name])
    if name == "ffn_w1":
        return gathered
    return gathered.reshape(-1, gathered.shape[2])


def _grad_pieces(name, full, shard_shape, half=None):
    rows = shard_shape[0]
    lo, n = (0, rows) if half is None else (half * (rows // 2), rows // 2)
    if name in ASSEMBLED:
        return _split_cols(f"split_{name}_{lo}", full, shard_shape[1], lo, n)
    return full.reshape(N_DEV, *shard_shape)[:, lo:lo + n]


def _device_step(x, target, mod, sm, shards, w_even_in):
    s, d = x.shape
    mv = [[mod[i, k].reshape(1, d) for k in range(6)] for i in range(2)]
    nm = [sm["norm_mix"][i].reshape(1, d) for i in range(2)]
    nf = [sm["norm_ffn"][i].reshape(1, d) for i in range(2)]
    pool_w, pool_scale = sm["pool_w"].reshape(4, 128, 128), sm["pool_scale"].reshape(1, POOL_WIDTH)
    got = {}

    def gather(*items):
        return [shards[it] for it in items], False

    def gather_half(item, half):
        rows = shards[item].shape[0] // 2
        return [shards[item][half * rows:(half + 1) * rows]], False

    def scatter(*pieces):
        return list(pieces), True

    def pieces(item, full, half=None):
        return _grad_pieces(item[0], full, shards[item].shape, half)

    def joined(top, bottom):
        return jnp.concatenate([top, bottom], axis=1)

    sh1, sc1, g1, sh2, sc2, g2 = mv[0]
    h1 = _norm_mod_fwd("l0_norm1", x, nm[0], sc1, sh1)
    slab = pl.BlockSpec((None, MM_TM, 512), lambda i, j, k: (_slab_of(j), i, 0))
    proj0, g_eout, g_sout = _matmul(
        "l0_in", h1, w_even_in, mode="nn", dims=(s, EVEN_IN_WIDTH, d), tiles=(MM_TM, 512, MM_TK_BIG),
        outs=[((EVEN_IN_WIDTH // 512, s, 512), BF16)], out_specs=[slab], epilogue=_epi_plain,
        ride=gather(("even_w_out", 0), ("ssm_w_out", 0)))
    w_even_out, w_ssm_out = _full_weight("even_w_out", g_eout), _full_weight("ssm_w_out", g_sout)
    o, lse = [None] * 3, [None] * 3
    o[0], lse[0], (g_w1_top,) = _attn_fwd(proj0, 0, ride=gather_half(("ffn_w1", 0), 0))
    o[1], lse[1], (g_w1_bottom,) = _attn_fwd(proj0, 1, ride=gather_half(("ffn_w1", 0), 1))
    o[2], lse[2], (g_sin_top,) = _attn_fwd(proj0, 2, ride=gather_half(("ssm_w_in", 0), 0))
    w1_0 = _full_weight("ffn_w1", joined(g_w1_top, g_w1_bottom))
    cat = _merge_pool_fwd(o, lse, proj0, pool_w, pool_scale)
    x1, y0, _ = _mm_resgate("l0_out", cat, w_even_out, x, g1, EVEN_OUT_WIDTH, EVEN_OUT_WIDTH)
    x2, ffn0, g_w2_0, (g_sin_bottom,) = _ffn_fwd(
        "l0", x1, nf[0], sc2, sh2, g2, w1_0, lambda arrived: _full_weight("ffn_w2", arrived[0]),
        ride1=gather(("ffn_w2", 0)), ride2=gather_half(("ssm_w_in", 0), 1))
    w2_0, w_ssm_in = _full_weight("ffn_w2", g_w2_0[0]), _full_weight("ssm_w_in", joined(g_sin_top, g_sin_bottom))

    th1, tc1, tg1, th2, tc2, tg2 = mv[1]
    h3 = _norm_mod_fwd("l1_norm1", x2, nm[1], tc1, th1)
    proj1, (g_w1_1,) = _mm_plain("l1_in", h3, w_ssm_in, "nn", (s, SSM_IN_PAD, d), F32, tn=1152, tk=MM_TK_BIG,
                                 ride=gather(("ffn_w1", 1)))
    w1_1 = _full_weight("ffn_w1", g_w1_1)
    conv_w = sm["ssm_conv_w"].reshape(SSM_CONV, SSM_CONV_DIM)
    conv_b = sm["ssm_conv_b"].reshape(1, SSM_CONV_DIM)
    act = _conv_fwd(proj1, conv_w, conv_b)
    dt_bias = jnp.pad(sm["ssm_dt_bias"].reshape(1, SSM_HEADS), ((0, 0), (0, 128 - SSM_HEADS)))
    dt_full = _dt_fwd(proj1, dt_bias)
    dtc = jnp.pad(dt_full[:, :SSM_HEADS].reshape(s, SSM_GROUPS, 8).transpose(1, 0, 2), ((0, 0), (0, 0), (0, 120)))
    a_log = jnp.pad(sm["ssm_a_log"].reshape(SSM_GROUPS, 1, 8), ((0, 0), (0, 0), (0, 120)))
    d_skip = jnp.repeat(sm["ssm_d"].reshape(SSM_HEADS), SSM_D_INNER // SSM_HEADS).reshape(1, SSM_D_INNER)
    norm_g = sm["ssm_norm"].reshape(1, SSM_D_INNER)
    (y, yn, h_in_all), (g_w2_1,) = _ssd_fwd(act, proj1, dtc, a_log, d_skip, norm_g, ride=gather(("ffn_w2", 1)))
    w2_1 = _full_weight("ffn_w2", g_w2_1)
    x3, y1, _ = _mm_resgate("l1_out", yn, w_ssm_out, x2, tg1, SSM_D_INNER, MM_TK_BIG)
    x4, ffn1, _, _ = _ffn_fwd("l1", x3, nf[1], tc2, th2, tg2, w1_1, lambda arrived: w2_1)

    dx4, st_loss, dyf1 = _loss_head("loss_head", x4, sm["final_norm"].reshape(1, d), target, (ffn1[3], tg2))
    loss = jnp.sum(st_loss[1])

    dx3, dw1_1, dw2_1, st_n2_1, dy1, _, _, (got[("ffn_w2", 1)],) = _ffn_bwd(
        "l1", dx4, dyf1, x3, ffn1, nf[1], tc2, w1_1, w2_1, below=(y1, tg1),
        ride_dx1_of=lambda dw2: scatter(pieces(("ffn_w2", 1), dw2)))
    dyn, _ = _mm_plain("l1_out_dx", dy1, w_ssm_out, "nt", (s, SSM_D_INNER, d), F32, tk=MM_TK_BIG)
    dw_sout, _ = _mm_plain("l1_out_dw", yn, dy1, "tn", (SSM_D_INNER, d, s), BF16, tk=MM_TK_BIG)
    (dproj1, dxs, dbm, dcm, ddt, st_ssd, d_alog), (got[("ffn_w1", 1)], got[("ssm_w_out", 0)]) = _ssd_bwd(
        act, proj1, dtc, a_log, d_skip, norm_g, y, h_in_all, dyn,
        ride=scatter(dw1_1, pieces(("ssm_w_out", 0), dw_sout)))
    dproj1, st_cx = _conv_bwd("conv_bwd_x", dxs, proj1, conv_w, conv_b, dproj1, block0=0)
    dproj1, st_cb = _conv_bwd("conv_bwd_b", dbm, proj1, conv_w, conv_b, dproj1, block0=SSM_D_INNER // 512)
    dproj1, st_cc = _conv_bwd("conv_bwd_c", dcm, proj1, conv_w, conv_b, dproj1, block0=SSM_D_INNER // 512 + 2)
    ddt_rows = jnp.pad(ddt[:, :, :8].transpose(1, 0, 2).reshape(s, SSM_HEADS), ((0, 0), (0, 128 - SSM_HEADS)))
    dproj1, st_dt = _dt_bwd(ddt_rows, proj1, dt_bias, dproj1)
    dh3, _ = _mm_plain("l1_in_dx", dproj1, w_ssm_in, "nt", (s, d, SSM_IN_PAD), F32, tk=SSM_IN_PAD // 3)
    dw_sin, _ = _mm_plain("l1_in_dw", h3, dproj1, "tn", (d, SSM_IN_PAD, s), BF16, tn=1152, tk=MM_TK_BIG)
    dx2, st_n1_1, dyf0 = _norm_mod_bwd("l1_norm1_bwd", x2, nm[1], tc1, dh3, dx3, (ffn0[3], g2))

    dx1, dw1_0, dw2_0, st_n2_0, dy0, (sin_top,), (sin_bottom,), (got[("ffn_w2", 0)],) = _ffn_bwd(
        "l0", dx2, dyf0, x1, ffn0, nf[0], sc2, w1_0, w2_0, below=(y0, g1),
        ride_dx2=scatter(pieces(("ssm_w_in", 0), dw_sin, 0)), ride_dw2=scatter(pieces(("ssm_w_in", 0), dw_sin, 1)),
        ride_dx1_of=lambda dw2: scatter(pieces(("ffn_w2", 0), dw2)))
    got[("ssm_w_in", 0)] = joined(sin_top, sin_bottom)
    dcat, _ = _mm_plain("l0_out_dx", dy0, w_even_out, "nt", (s, EVEN_OUT_WIDTH, d), F32, tk=MM_TK_BIG)
    dw_eout, _ = _mm_plain("l0_out_dw", cat, dy0, "tn", (EVEN_OUT_WIDTH, d, s), BF16, tk=MM_TK_BIG)
    lane = jnp.arange(512) // 64
    head_sum = (lane[:, None] == lane[None, :]).astype(BF16)
    *do_cc, du, d_pool_w, st_pool = _merge_pool_bwd(dcat, o, lse, proj0, pool_w, pool_scale, head_sum)
    do, cc = do_cc[:3], do_cc[3:]
    dqkv = [None] * 3
    dqkv[0], (w1_top,) = _attn_bwd(proj0, 0, do[0], lse[0], cc[0], ride=scatter(pieces(("ffn_w1", 0), dw1_0, 0)))
    dqkv[1], (w1_bottom,) = _attn_bwd(proj0, 1, do[1], lse[1], cc[1], ride=scatter(pieces(("ffn_w1", 0), dw1_0, 1)))
    dqkv[2], (got[("even_w_out", 0)],) = _attn_bwd(proj0, 2, do[2], lse[2], cc[2],
                                                   ride=scatter(pieces(("even_w_out", 0), dw_eout)))
    got[("ffn_w1", 0)] = joined(w1_top, w1_bottom)
    dproj0 = jnp.stack([dqkv[g][kind] for g in range(3) for kind in range(3)] + [du])
    dw_ein, _ = _mm_plain("l0_in_dw", h1, dproj0, "tn", (d, EVEN_IN_WIDTH, s), BF16, tn=512, tk=MM_TK_BIG,
                          b_spec=pl.BlockSpec((None, MM_TK_BIG, 512), lambda i, j, k: (_slab_of(j), k, 0)))
    dh1, (ein_top,) = _mm_plain("l0_in_dx", dproj0, w_even_in, "nt", (s, d, EVEN_IN_WIDTH), F32,
                                a_spec=pl.BlockSpec((None, MM_TM, 512), lambda i, j, k: (_slab_of(k), i, 0)),
                                ride=scatter(pieces(("even_w_in", 0), dw_ein, 0)))
    (grad_x, st_n1_0), (ein_bottom,) = _norm_mod_bwd("l0_norm1_bwd", x, nm[0], sc1, dh1, dx1,
                                                     ride=scatter(pieces(("even_w_in", 0), dw_ein, 1)))
    got[("even_w_in", 0)] = joined(ein_top, ein_bottom)

    dg1_0, dg2_0 = st_n2_0[GATE_STAT_ROW], st_n1_1[GATE_STAT_ROW]
    dg1_1, dg2_1 = st_n2_1[GATE_STAT_ROW], st_loss[GATE_STAT_ROW]
    dmod = jnp.stack([
        jnp.stack([st_n1_0[0], st_n1_0[1], dg1_0, st_n2_0[0], st_n2_0[1], dg2_0]),
        jnp.stack([st_n1_1[0], st_n1_1[1], dg1_1, st_n2_1[0], st_n2_1[1], dg2_1])])
    st_conv = jnp.concatenate([st_cx, st_cb, st_cc], axis=1)
    small = dict(
        norm_mix=jnp.stack([st_n1_0[2], st_n1_1[2]]), norm_ffn=jnp.stack([st_n2_0[2], st_n2_1[2]]),
        pool_w=d_pool_w, pool_scale=st_pool[0], ssm_conv_w=st_conv[:SSM_CONV], ssm_conv_b=st_conv[SSM_CONV],
        ssm_dt_bias=st_dt[0, :SSM_HEADS], ssm_a_log=d_alog[:, 0, :8].reshape(SSM_HEADS),
        ssm_d=jnp.sum(st_ssd[1].reshape(SSM_HEADS, SSM_D_INNER // SSM_HEADS), axis=-1), ssm_norm=st_ssd[0],
        final_norm=st_loss[0])
    return loss, grad_x, got, dmod, small


WEIGHT_ORDER = ("ada_w", "ada_b", "norm_mix", "norm_ffn", "ffn_w1", "ffn_w2", "even_w_in", "pool_w", "pool_scale",
                "even_w_out", "ssm_w_in", "ssm_conv_w", "ssm_conv_b", "ssm_dt_bias", "ssm_a_log", "ssm_d", "ssm_norm",
                "ssm_w_out", "final_norm")
BIG_LAYERS = ((("even_w_in", 0), ("even_w_out", 0), ("ffn_w1", 0), ("ffn_w2", 0)),
              (("ssm_w_in", 0), ("ssm_w_out", 0), ("ffn_w1", 1), ("ffn_w2", 1)))
STACKED = ("ffn_w1", "ffn_w2")
ASSEMBLED = {"even_w_in": EVEN_IN_WIDTH, "even_w_out": D_MODEL, "ssm_w_in": SSM_IN_PAD}
SMALL_REPLICATED = ("norm_mix", "norm_ffn", "pool_w", "pool_scale", "ssm_dt_bias", "ssm_a_log", "ssm_d", "final_norm")
SMALL_SHARDED = ("ssm_conv_w", "ssm_conv_b", "ssm_norm")


def _pack(flat_parts, width, lead=()):
    flat = jnp.concatenate(flat_parts, axis=-1)
    n = flat.shape[-1]
    rows = -(-n // (8 * width)) * 8
    flat = jnp.pad(flat, [(0, 0)] * len(lead) + [(0, rows * width - n)])
    return flat.reshape(*lead, rows, width)


def _unpack(packed, shapes, lead=()):
    flat = packed.reshape(*lead, -1)
    out, off = [], 0
    for shp in shapes:
        n = math.prod(shp)
        out.append(flat[..., off:off + n].reshape(*lead, *shp))
        off += n
    return out


def kernel(x, c, ada_w, ada_b, norm_mix, norm_ffn, ffn_w1, ffn_w2, even_w_in, pool_w, pool_scale, even_w_out, ssm_w_in, ssm_conv_w, ssm_conv_b, ssm_dt_bias, ssm_a_log, ssm_d, ssm_norm, ssm_w_out, final_norm, loss_target, m_ada_w, m_ada_b, m_norm_mix, m_norm_ffn, m_ffn_w1, m_ffn_w2, m_even_w_in, m_pool_w, m_pool_scale, m_even_w_out, m_ssm_w_in, m_ssm_conv_w, m_ssm_conv_b, m_ssm_dt_bias, m_ssm_a_log, m_ssm_d, m_ssm_norm, m_ssm_w_out, m_final_norm, v_ada_w, v_ada_b, v_norm_mix, v_norm_ffn, v_ffn_w1, v_ffn_w2, v_even_w_in, v_pool_w, v_pool_scale, v_even_w_out, v_ssm_w_in, v_ssm_conv_w, v_ssm_conv_b, v_ssm_dt_bias, v_ssm_a_log, v_ssm_d, v_ssm_norm, v_ssm_w_out, v_final_norm):
    w = dict(ada_w=ada_w, ada_b=ada_b, norm_mix=norm_mix, norm_ffn=norm_ffn, ffn_w1=ffn_w1, ffn_w2=ffn_w2,
             even_w_in=even_w_in, pool_w=pool_w, pool_scale=pool_scale, even_w_out=even_w_out, ssm_w_in=ssm_w_in,
             ssm_conv_w=ssm_conv_w, ssm_conv_b=ssm_conv_b, ssm_dt_bias=ssm_dt_bias, ssm_a_log=ssm_a_log, ssm_d=ssm_d,
             ssm_norm=ssm_norm, ssm_w_out=ssm_w_out, final_norm=final_norm)
    m = dict(ada_w=m_ada_w, ada_b=m_ada_b, norm_mix=m_norm_mix, norm_ffn=m_norm_ffn, ffn_w1=m_ffn_w1, ffn_w2=m_ffn_w2,
             even_w_in=m_even_w_in, pool_w=m_pool_w, pool_scale=m_pool_scale, even_w_out=m_even_w_out,
             ssm_w_in=m_ssm_w_in, ssm_conv_w=m_ssm_conv_w, ssm_conv_b=m_ssm_conv_b, ssm_dt_bias=m_ssm_dt_bias,
             ssm_a_log=m_ssm_a_log, ssm_d=m_ssm_d, ssm_norm=m_ssm_norm, ssm_w_out=m_ssm_w_out, final_norm=m_final_norm)
    v = dict(ada_w=v_ada_w, ada_b=v_ada_b, norm_mix=v_norm_mix, norm_ffn=v_norm_ffn, ffn_w1=v_ffn_w1, ffn_w2=v_ffn_w2,
             even_w_in=v_even_w_in, pool_w=v_pool_w, pool_scale=v_pool_scale, even_w_out=v_even_w_out,
             ssm_w_in=v_ssm_w_in, ssm_conv_w=v_ssm_conv_w, ssm_conv_b=v_ssm_conv_b, ssm_dt_bias=v_ssm_dt_bias,
             ssm_a_log=v_ssm_a_log, ssm_d=v_ssm_d, ssm_norm=v_ssm_norm, ssm_w_out=v_ssm_w_out, final_norm=v_final_norm)
    d = D_MODEL
    me = _my_index()

    sharded_shapes = [(SSM_CONV, SSM_CONV_DIM // N_DEV), (SSM_CONV_DIM // N_DEV,), (SSM_D_INNER // N_DEV,)]
    shards = {(name, idx): w[name][idx].astype(BF16) for layer in BIG_LAYERS for name, idx in layer}
    small_in = _pack([c.reshape(-1)] + [w[k].reshape(-1) for k in SMALL_SHARDED], 128)
    got, got_even_in = _exchange("gather_first", [small_in, shards[("even_w_in", 0)]], scatter=False)
    w_even_in = _full_weight("even_w_in", got_even_in)
    c_all, conv_w_sh, conv_b_sh, norm_sh = _unpack(got, [(d,)] + sharded_shapes, lead=(N_DEV,))
    sm = {k: w[k] for k in SMALL_REPLICATED}
    sm["ssm_conv_w"] = conv_w_sh.transpose(1, 0, 2).reshape(SSM_CONV, SSM_CONV_DIM)
    sm["ssm_conv_b"] = conv_b_sh.reshape(SSM_CONV_DIM)
    sm["ssm_norm"] = norm_sh.reshape(SSM_D_INNER)

    ada_cols = 6 * d // N_DEV
    c_pad = jnp.pad(c_all, ((0, 16 - N_DEV), (0, 0)))
    mod_cols = _matmul(
        "ada_fwd", c_pad, ada_w, mode="nn", dims=(16, 2 * ada_cols, d), tiles=(16, ada_cols // 2, d), a_fn=_silu,
        b_spec=pl.BlockSpec((None, d, ada_cols // 2), lambda i, j, k: (j // 2, k, j % 2)),
        outs=[((16, 2 * ada_cols), F32)], epilogue=_epi_plain)[0]
    mod_got = _all_to_all("ada_exchange", mod_cols[:N_DEV].reshape(N_DEV, 2, ada_cols))
    mod = (mod_got.transpose(1, 0, 2).reshape(2, 6 * d) + ada_b).reshape(2, 6, d)

    loss, grad_x, grad_got, dmod, small = _device_step(x[0], loss_target[0], mod, sm, shards, w_even_in)
    loss = lax.psum(loss, ("x", "y", "c"))

    grads, delta, new_m, new_v = {}, {}, {}, {}

    def update(name, parts, shape=None):
        rows, cols = parts.shape[1:]
        res = _adamw("adamw_" + name, parts, w[name].reshape(rows, cols), m[name].reshape(rows, cols), v[name].reshape(rows, cols))
        return [r.reshape(w[name].shape if shape is None else shape) for r in res]

    rep_shapes = [w[k].shape for k in SMALL_REPLICATED]
    pack_rep = lambda tree: _pack([tree[k].reshape(-1) for k in SMALL_REPLICATED], 128)
    dmod_all, rep_got = _exchange("gather_small_grads", [dmod.reshape(2, 6 * d), pack_rep(small)], scatter=False)

    my_cols = lax.dynamic_slice_in_dim(dmod_all, me * ada_cols, ada_cols, axis=2).reshape(N_DEV, 2 * ada_cols)
    g_ada_w = _matmul(
        "ada_dw", c_pad, jnp.pad(my_cols, ((0, 16 - N_DEV), (0, 0))), mode="tn", dims=(d, 2 * ada_cols, 16),
        tiles=(1024, ada_cols // 2, 16), a_fn=_silu, outs=[((2, d, ada_cols), F32)],
        out_specs=[pl.BlockSpec((None, 1024, ada_cols // 2), lambda i, j, k: (j // 2, i, j % 2))], epilogue=_epi_plain)[0]
    grads["ada_w"], delta["ada_w"], new_m["ada_w"], new_v["ada_w"] = update("ada_w", g_ada_w.reshape(1, 2 * d, ada_cols))
    grads["ada_b"], delta["ada_b"], new_m["ada_b"], new_v["ada_b"] = update("ada_b", dmod_all)

    sh_pieces = [small["ssm_conv_w"].reshape(SSM_CONV, N_DEV, -1).transpose(1, 0, 2).reshape(N_DEV, -1),
                 small["ssm_conv_b"].reshape(N_DEV, -1), small["ssm_norm"].reshape(N_DEV, -1)]
    sh_got = _all_to_all("exchange_small_grads", _pack(sh_pieces, 128, lead=(N_DEV,)))

    stacked = {name: [None, None] for name in STACKED}
    for layer in BIG_LAYERS:
        for name, idx in layer:
            res = _adamw(f"adamw_{name}_{idx}", grad_got[(name, idx)], w[name][idx], m[name][idx], v[name][idx])
            if name in stacked:
                stacked[name][idx] = res
            else:
                grads[name], delta[name], new_m[name], new_v[name] = [r[None] for r in res]
    for name, per_layer in stacked.items():
        grads[name], delta[name], new_m[name], new_v[name] = [jnp.stack([per_layer[0][q], per_layer[1][q]]) for q in range(4)]

    rep_res = _adamw("adamw_small_replicated", rep_got, pack_rep(w), pack_rep(m), pack_rep(v))
    for dst, packed in zip((grads, delta, new_m, new_v), rep_res):
        for k, val in zip(SMALL_REPLICATED, _unpack(packed, rep_shapes)):
            dst[k] = val
    pack_sh = lambda tree: _pack([tree[k].reshape(-1) for k in SMALL_SHARDED], 128)
    sh_res = _adamw("adamw_small_sharded", sh_got, pack_sh(w), pack_sh(m), pack_sh(v))
    for dst, packed in zip((grads, delta, new_m, new_v), sh_res):
        for k, val in zip(SMALL_SHARDED, _unpack(packed, [w[k].shape for k in SMALL_SHARDED])):
            dst[k] = val

    out = [loss, grad_x[None]]
    for tree in (grads, delta, new_m, new_v):
        out.extend(tree[k] for k in WEIGHT_ORDER)
    return tuple(out)
```

```python
import functools
import math

import jax
import jax.numpy as jnp
from jax import lax
from jax.experimental import pallas as pl
from jax.experimental.pallas import tpu as pltpu

F32 = jnp.float32
BF16 = jnp.bfloat16

N_DEV = 8
D_MODEL = 2048
NORM_EPS = 1e-6
ATTN_BLOCK = 128
ATTN_GROUPS = 3
ATTN_GROUP_WIDTH = 512
ATTN_QKV_WIDTH = 3 * ATTN_GROUPS * ATTN_GROUP_WIDTH
POOL_GROUPS = 4
POOL_GROUP_WIDTH = 128
POOL_WIDTH = 512
POOL_HALO = 16
EVEN_IN_WIDTH = ATTN_QKV_WIDTH + POOL_WIDTH
EVEN_OUT_WIDTH = 1024
SSM_D_INNER = 4096
SSM_HEADS = 64
SSM_GROUPS = 8
SSM_GROUP_WIDTH = 512
SSM_STATE = 128
SSM_CHUNK = 128
SSM_CONV = 4
SSM_CONV_DIM = 6144
SSM_IN_WIDTH = 10304
SSM_IN_PAD = 10368
FFN_HIDDEN = 8192

ADAM_LR = 0.001
ADAM_B1 = 0.9
ADAM_B2 = 0.999
ADAM_EPS = 1e-08
ADAM_WD = 0.01
ADAM_STEP = 10

VMEM_LIMIT_BYTES = 56 * 1024 * 1024
NEG_BIG = -1e30

MESH_ID = pl.DeviceIdType.MESH
ANY = pl.BlockSpec(memory_space=pl.ANY)


def _cparams(*sem):
    return pltpu.CompilerParams(dimension_semantics=tuple(sem) if sem else None, vmem_limit_bytes=VMEM_LIMIT_BYTES)


def _dot(a, b):
    return lax.dot_general(a, b, (((1,), (0,)), ((), ())), preferred_element_type=F32)


def _dot_nt(a, b):
    return lax.dot_general(a, b, (((1,), (1,)), ((), ())), preferred_element_type=F32)


def _dot_tn(a, b):
    return lax.dot_general(a, b, (((0,), (0,)), ((), ())), preferred_element_type=F32)


def _split3(v):
    hi = v.astype(BF16)
    r1 = v - hi.astype(F32)
    mid = r1.astype(BF16)
    lo = (r1 - mid.astype(F32)).astype(BF16)
    return hi, mid, lo


def _dot3_left(const_bf16, v):
    hi, mid, lo = _split3(v)
    return _dot(const_bf16, hi) + _dot(const_bf16, mid) + _dot(const_bf16, lo)


def _dot3_right(v, const_bf16):
    hi, mid, lo = _split3(v)
    return _dot(hi, const_bf16) + _dot(mid, const_bf16) + _dot(lo, const_bf16)


def _iota(shape, dim):
    return lax.broadcasted_iota(jnp.int32, shape, dim)


def _sigmoid(x):
    return 1.0 / (1.0 + jnp.exp(-x))


def _peer(k):
    x, y, c = lax.axis_index("x"), lax.axis_index("y"), lax.axis_index("c")
    px = 1 - x if k & 4 else x
    py = 1 - y if k & 2 else y
    pc = 1 - c if k & 1 else c
    return (px, py, pc), 4 * px + 2 * py + pc


def _my_index():
    return 4 * lax.axis_index("x") + 2 * lax.axis_index("y") + lax.axis_index("c")


def _exchange(name, arrays, *, scatter):
    n = len(arrays)

    def body(*refs):
        ex = _Exchange(refs[:n], refs[n:2 * n], *refs[2 * n:], scatter)
        ex.start()
        ex.wait()

    return pl.pallas_call(
        body, name=name, out_shape=_exchange_out_shapes(arrays, scatter), in_specs=[ANY] * n, out_specs=[ANY] * n,
        scratch_shapes=_exchange_sems(n),
    )(*arrays)


def _exchange_out_shapes(arrays, scatter):
    return [jax.ShapeDtypeStruct((N_DEV,) + (tuple(a.shape[1:]) if scatter else tuple(a.shape)), a.dtype) for a in arrays]


def _exchange_sems(n):
    return [pltpu.SemaphoreType.DMA((n * (N_DEV - 1),)), pltpu.SemaphoreType.DMA((n * (N_DEV - 1),)),
            pltpu.SemaphoreType.DMA((n,))]


class _Exchange:
    def __init__(self, x_refs, out_refs, send_sems, recv_sems, local_sems, scatter):
        self.x_refs, self.out_refs, self.scatter = x_refs, out_refs, scatter
        self.send_sems, self.recv_sems, self.local_sems = send_sems, recv_sems, local_sems

    def _src(self, a, idx):
        return self.x_refs[a].at[idx] if self.scatter else self.x_refs[a]

    def _local(self, a):
        me = _my_index()
        return pltpu.make_async_copy(self._src(a, me), self.out_refs[a].at[me], self.local_sems.at[a])

    def _remote(self, a, k, landing):
        peer, peer_idx = _peer(k)
        sem = a * (N_DEV - 1) + k - 1
        slot = peer_idx if landing else _my_index()
        return pltpu.make_async_remote_copy(
            src_ref=self._src(a, peer_idx), dst_ref=self.out_refs[a].at[slot], send_sem=self.send_sems.at[sem],
            recv_sem=self.recv_sems.at[sem], device_id=peer, device_id_type=MESH_ID)

    def start(self):
        n = len(self.x_refs)
        for a in range(n):
            self._local(a).start()
        for k in range(1, N_DEV):
            for a in range(n):
                self._remote(a, k, False).start()

    def wait(self):
        n = len(self.x_refs)
        for k in range(1, N_DEV):
            for a in range(n):
                self._remote(a, k, True).wait_recv()
        for k in range(1, N_DEV):
            for a in range(n):
                self._remote(a, k, False).wait_send()
        for a in range(n):
            self._local(a).wait()


def _pcall(name, body, *, grid, in_specs, out_specs, out_shape, scratch_shapes, sem, args, ride=None, aliases=None):
    aliases = aliases or {}
    if ride is None:
        res = pl.pallas_call(body, name=name, grid=grid, in_specs=in_specs, out_specs=out_specs, out_shape=out_shape,
                             scratch_shapes=scratch_shapes, input_output_aliases=aliases, compiler_params=_cparams(*sem))(*args)
        return list(res), []
    arrays, scatter = ride
    n_in, n_out, n_scr, n_ride = len(in_specs), len(out_specs), len(scratch_shapes), len(arrays)

    def wrapped(*refs):
        ins, refs = refs[:n_in], refs[n_in:]
        ride_in, refs = refs[:n_ride], refs[n_ride:]
        outs, refs = refs[:n_out], refs[n_out:]
        ride_out, refs = refs[:n_ride], refs[n_ride:]
        scr, sems = refs[:n_scr], refs[n_scr:]
        exchange = _Exchange(ride_in, ride_out, *sems, scatter)
        ids = [pl.program_id(axis) for axis in range(len(grid))]
        first, last = ids[0] == 0, ids[0] == grid[0] - 1
        for axis in range(1, len(grid)):
            first, last = first & (ids[axis] == 0), last & (ids[axis] == grid[axis] - 1)

        @pl.when(first)
        def _():
            exchange.start()

        body(*ins, *outs, *scr)

        @pl.when(last)
        def _():
            exchange.wait()

    res = pl.pallas_call(
        wrapped, name=name, grid=grid, in_specs=list(in_specs) + [ANY] * n_ride, out_specs=list(out_specs) + [ANY] * n_ride,
        out_shape=list(out_shape) + _exchange_out_shapes(arrays, scatter),
        scratch_shapes=list(scratch_shapes) + _exchange_sems(n_ride), input_output_aliases=aliases,
        compiler_params=_cparams(*(("arbitrary",) * len(grid))))(*args, *arrays)
    return list(res[:n_out]), list(res[n_out:])


def _all_gather(name, x):
    return _exchange(name, [x], scatter=False)[0]


def _all_to_all(name, x):
    return _exchange(name, [x], scatter=True)[0]


def _assemble_cols(name, shards, width):
    _, k_dim, ns = shards.shape
    tr = 256

    def body(s_ref, o_ref):
        for dev in range(N_DEV):
            o_ref[:, ns * dev:ns * (dev + 1)] = s_ref[dev]
        if width > N_DEV * ns:
            o_ref[:, N_DEV * ns:] = jnp.zeros((tr, width - N_DEV * ns), o_ref.dtype)

    return pl.pallas_call(
        body, name=name, grid=(k_dim // tr,), in_specs=[pl.BlockSpec((N_DEV, tr, ns), lambda i: (0, i, 0))],
        out_specs=pl.BlockSpec((tr, width), lambda i: (i, 0)), out_shape=jax.ShapeDtypeStruct((k_dim, width), shards.dtype),
        compiler_params=_cparams("parallel"),
    )(shards)


def _split_cols(name, full, ns, row0=0, n_rows=None):
    k_dim, width = full.shape
    n_rows = k_dim if n_rows is None else n_rows
    tr = 256
    first = row0 // tr

    def body(f_ref, o_ref):
        for dev in range(N_DEV):
            o_ref[dev] = f_ref[:, ns * dev:ns * (dev + 1)]

    return pl.pallas_call(
        body, name=name, grid=(n_rows // tr,), in_specs=[pl.BlockSpec((tr, width), lambda i: (first + i, 0))],
        out_specs=pl.BlockSpec((N_DEV, tr, ns), lambda i: (0, i, 0)),
        out_shape=jax.ShapeDtypeStruct((N_DEV, n_rows, ns), full.dtype), compiler_params=_cparams("parallel"),
    )(full)


_DIMS = {"nn": (((1,), (0,)), ((), ())), "nt": (((1,), (1,)), ((), ())), "tn": (((0,), (0,)), ((), ()))}


def _matmul(name, a, b, *, mode, dims, tiles, outs, epilogue, a_spec=None, b_spec=None, out_specs=None,
            extras=(), extra_specs=(), a_fn=None, ride=None, a_slabs=0, b_slabs=0):
    m_dim, n_dim, k_dim = dims
    tm, tn, tk = tiles
    assert m_dim % tm == 0 and n_dim % tn == 0 and k_dim % tk == 0, (name, dims, tiles)
    grid = (m_dim // tm, n_dim // tn, k_dim // tk)
    nk = grid[2]
    if a_spec is None:
        a_spec = pl.BlockSpec((tk, tm), lambda i, j, k: (k, i)) if mode == "tn" else pl.BlockSpec((tm, tk), lambda i, j, k: (i, k))
    if b_spec is None:
        b_spec = pl.BlockSpec((tn, tk), lambda i, j, k: (j, k)) if mode == "nt" else pl.BlockSpec((tk, tn), lambda i, j, k: (k, j))
    if out_specs is None:
        out_specs = [pl.BlockSpec((tm, tn), lambda i, j, k: (i, j)) for _ in outs]
    n_ex, n_out = len(extras), len(outs)
    ride_arrays, scatter = ride if ride is not None else ((), False)
    n_ride = len(ride_arrays)
    dn = _DIMS[mode]

    def body(a_ref, b_ref, *rest):
        ex_refs, rest = rest[:n_ex], rest[n_ex:]
        ride_in, rest = rest[:n_ride], rest[n_ride:]
        out_refs, rest = rest[:n_out], rest[n_out:]
        ride_out, rest = rest[:n_ride], rest[n_ride:]
        i, j, k = pl.program_id(0), pl.program_id(1), pl.program_id(2)
        if n_ride:
            exchange = _Exchange(ride_in, ride_out, *rest[-3:], scatter)

            @pl.when((i == 0) & (j == 0) & (k == 0))
            def _():
                exchange.start()

        at = a_ref[...]
        if a_fn is not None:
            at = a_fn(at)
        at = at.astype(BF16)
        if a_slabs:
            width = tk // a_slabs
            bt = b_ref[...].astype(BF16)
            part = lax.dot_general(at[0], bt[:, :width], dn, preferred_element_type=F32)
            for slab_i in range(1, a_slabs):
                part += lax.dot_general(at[slab_i], bt[:, slab_i * width:(slab_i + 1) * width], dn,
                                        preferred_element_type=F32)
        elif b_slabs:
            width = tk // b_slabs
            part = lax.dot_general(at[:, :width], b_ref[0].astype(BF16), dn, preferred_element_type=F32)
            for slab_i in range(1, b_slabs):
                part += lax.dot_general(at[:, slab_i * width:(slab_i + 1) * width], b_ref[slab_i].astype(BF16), dn,
                                        preferred_element_type=F32)
        else:
            part = lax.dot_general(at, b_ref[...].astype(BF16), dn, preferred_element_type=F32)

        def finish(total):
            res = epilogue(total, *[e[...] for e in ex_refs])
            for r, o in zip(res, out_refs):
                o[...] = r.astype(o.dtype)

        if nk == 1:
            finish(part)
        else:
            acc = rest[0]

            @pl.when(k == 0)
            def _():
                acc[...] = part

            @pl.when(k > 0)
            def _():
                acc[...] += part

            @pl.when(k == nk - 1)
            def _():
                finish(acc[...])

        if n_ride:
            @pl.when((i == grid[0] - 1) & (j == grid[1] - 1) & (k == nk - 1))
            def _():
                exchange.wait()

    scratch = ([pltpu.VMEM((tm, tn), F32)] if nk > 1 else []) + (_exchange_sems(n_ride) if n_ride else [])
    sem = ("arbitrary",) * 3 if n_ride else ("parallel", "parallel", "arbitrary")
    return pl.pallas_call(
        body, name=name, grid=grid,
        in_specs=[a_spec, b_spec, *extra_specs] + [ANY] * n_ride, out_specs=list(out_specs) + [ANY] * n_ride,
        out_shape=[jax.ShapeDtypeStruct(s, d) for s, d in outs] + (_exchange_out_shapes(ride_arrays, scatter) if n_ride else []),
        scratch_shapes=scratch, compiler_params=_cparams(*sem),
    )(a, b, *extras, *ride_arrays)


def _epi_plain(acc):
    return (acc,)


def _epi_relu2(acc):
    return jnp.square(jnp.maximum(acc, 0.0)), acc


def _epi_resgate(acc, res, gate):
    return res + gate * acc, acc


def _epi_drelu2(acc, pre):
    return (acc * (2.0 * jnp.maximum(pre.astype(F32), 0.0)),)


def _silu(v):
    return v * _sigmoid(v)


ROW_TILE = 256


def _row_spec(width, tr=ROW_TILE):
    return pl.BlockSpec((tr, width), lambda i: (i, 0))


def _vec_spec(width):
    return pl.BlockSpec((1, width), lambda i: (0, 0))


def _stat_spec(width):
    return pl.BlockSpec((8, width), lambda i: (0, 0))


def _norm_mod_fwd(name, x, gain, scale, shift):
    s, d = x.shape

    def body(x_ref, g_ref, sc_ref, sh_ref, h_ref):
        xv = x_ref[...]
        r = lax.rsqrt(jnp.mean(xv * xv, axis=-1, keepdims=True) + NORM_EPS)
        h_ref[...] = ((xv * r * g_ref[...]) * (1.0 + sc_ref[...]) + sh_ref[...]).astype(BF16)

    return pl.pallas_call(
        body, name=name, grid=(s // ROW_TILE,),
        in_specs=[_row_spec(d), _vec_spec(d), _vec_spec(d), _vec_spec(d)], out_specs=_row_spec(d),
        out_shape=jax.ShapeDtypeStruct((s, d), BF16), compiler_params=_cparams("parallel"),
    )(x, gain, scale, shift)


GATE_STAT_ROW = 3


def _gate_below(dx, y_ref, gate_ref, dy_ref, st_ref):
    dy_ref[...] = (dx * gate_ref[...]).astype(BF16)
    st_ref[GATE_STAT_ROW:GATE_STAT_ROW + 1, :] += jnp.sum(dx * y_ref[...].astype(F32), axis=0, keepdims=True)


def _norm_mod_bwd(name, x, gain, scale, dh, dres, below=None, ride=None):
    s, d = x.shape

    def body(x_ref, g_ref, sc_ref, dh_ref, dres_ref, *rest):
        dx_ref, st_ref = rest[-2:] if below is None else (rest[2], rest[3])

        @pl.when(pl.program_id(0) == 0)
        def _():
            st_ref[...] = jnp.zeros_like(st_ref)

        xv = x_ref[...]
        dhv = dh_ref[...].astype(F32)
        r = lax.rsqrt(jnp.mean(xv * xv, axis=-1, keepdims=True) + NORM_EPS)
        xh = xv * r
        n = xh * g_ref[...]
        dn = dhv * (1.0 + sc_ref[...])
        dxh = dn * g_ref[...]
        dx = dres_ref[...] + r * (dxh - xh * jnp.mean(dxh * xh, axis=-1, keepdims=True))
        dx_ref[...] = dx
        st_ref[0:1, :] += jnp.sum(dhv, axis=0, keepdims=True)
        st_ref[1:2, :] += jnp.sum(dhv * n, axis=0, keepdims=True)
        st_ref[2:3, :] += jnp.sum(dn * xh, axis=0, keepdims=True)
        if below is not None:
            _gate_below(dx, rest[0], rest[1], rest[4], st_ref)

    extra_in = [] if below is None else [_row_spec(d), _vec_spec(d)]
    extra_out = [] if below is None else [(_row_spec(d), jax.ShapeDtypeStruct((s, d), BF16))]
    res, got = _pcall(
        name, body, grid=(s // ROW_TILE,),
        in_specs=[_row_spec(d), _vec_spec(d), _vec_spec(d), _row_spec(d), _row_spec(d)] + extra_in,
        out_specs=[_row_spec(d), _stat_spec(d)] + [spec for spec, _ in extra_out],
        out_shape=[jax.ShapeDtypeStruct((s, d), F32), jax.ShapeDtypeStruct((8, d), F32)] + [shp for _, shp in extra_out],
        scratch_shapes=[], sem=("arbitrary",), args=(x, gain, scale, dh, dres, *(below or ())), ride=ride)
    return res if ride is None else (res, got)


def _loss_head(name, x, gain, target, below):
    s, d = x.shape

    def body(x_ref, g_ref, t_ref, y_ref, gate_ref, dx_ref, st_ref, dy_ref):
        @pl.when(pl.program_id(0) == 0)
        def _():
            st_ref[...] = jnp.zeros_like(st_ref)

        xv = x_ref[...]
        r = lax.rsqrt(jnp.mean(xv * xv, axis=-1, keepdims=True) + NORM_EPS)
        xh = xv * r
        err = xh * g_ref[...] - t_ref[...]
        dyf = err * (1.0 / d)
        dxh = dyf * g_ref[...]
        dx = r * (dxh - xh * jnp.mean(dxh * xh, axis=-1, keepdims=True))
        dx_ref[...] = dx
        st_ref[0:1, :] += jnp.sum(dyf * xh, axis=0, keepdims=True)
        st_ref[1:2, :] += jnp.sum(err * err, axis=0, keepdims=True) * (0.5 / d)
        _gate_below(dx, y_ref, gate_ref, dy_ref, st_ref)

    return pl.pallas_call(
        body, name=name, grid=(s // ROW_TILE,),
        in_specs=[_row_spec(d), _vec_spec(d), _row_spec(d), _row_spec(d), _vec_spec(d)],
        out_specs=[_row_spec(d), _stat_spec(d), _row_spec(d)],
        out_shape=[jax.ShapeDtypeStruct((s, d), F32), jax.ShapeDtypeStruct((8, d), F32), jax.ShapeDtypeStruct((s, d), BF16)],
        compiler_params=_cparams("arbitrary"),
    )(x, gain, target, *below)


def _adamw(name, parts, w, m, v):
    n_parts, rows, cols = parts.shape
    tr = rows
    for cand in (512, 256, 128, 64, 32, 16, 8):
        if rows % cand == 0 and cand * cols * 4 <= 2 * 1024 * 1024:
            tr = cand
            break
    c1 = 1.0 - ADAM_B1 ** ADAM_STEP
    c2 = 1.0 - ADAM_B2 ** ADAM_STEP

    def body(p_ref, w_ref, m_ref, v_ref, g_out, d_out, m_out, v_out):
        g = p_ref[0].astype(F32)
        for i in range(1, n_parts):
            g = g + p_ref[i].astype(F32)
        m_new = ADAM_B1 * m_ref[...] + (1.0 - ADAM_B1) * g
        v_new = ADAM_B2 * v_ref[...] + (1.0 - ADAM_B2) * (g * g)
        g_out[...] = g
        m_out[...] = m_new
        v_out[...] = v_new
        d_out[...] = -ADAM_LR * ((m_new / c1) / (jnp.sqrt(v_new / c2) + ADAM_EPS) + ADAM_WD * w_ref[...])

    spec = pl.BlockSpec((tr, cols), lambda i: (i, 0))
    return pl.pallas_call(
        body, name=name, grid=(rows // tr,),
        in_specs=[pl.BlockSpec((n_parts, tr, cols), lambda i: (0, i, 0)), spec, spec, spec],
        out_specs=[spec, spec, spec, spec],
        out_shape=[jax.ShapeDtypeStruct((rows, cols), F32)] * 4, compiler_params=_cparams("parallel"),
    )(parts, w, m, v)


ATTN_DILATIONS = (1, 4, 16)


def _attn_fwd(proj, g, ride=None):
    s = proj.shape[1]
    dil = ATTN_DILATIONS[g]
    rows, nb = s // dil, s // dil // ATTN_BLOCK

    def body(q_ref, kp_ref, kc_ref, vp_ref, vc_ref, o_ref, l_ref):
        first = pl.program_id(1) == 0
        qi, kj = _iota((128, 128), 0), _iota((128, 128), 1)
        mask_c = kj <= qi
        mask_p = jnp.logical_and(kj >= qi, jnp.logical_not(first))
        low = kj < 64
        for p in range(4):
            sl = slice(128 * p, 128 * p + 128)
            q, kp, kc, vp, vc = q_ref[:, sl], kp_ref[:, sl], kc_ref[:, sl], vp_ref[:, sl], vc_ref[:, sl]
            o_pair = jnp.zeros((128, 128), F32)
            l_pair = jnp.zeros((128, 128), F32)
            for half in range(2):
                hm = low if half == 0 else jnp.logical_not(low)
                qm = jnp.where(hm, q, jnp.zeros_like(q))
                sc = jnp.where(mask_c, _dot_nt(qm, kc) * 0.125, NEG_BIG)
                sp = jnp.where(mask_p, _dot_nt(qm, kp) * 0.125, NEG_BIG)
                m = jnp.maximum(jnp.max(sc, axis=-1, keepdims=True), jnp.max(sp, axis=-1, keepdims=True))
                pc, pp = jnp.exp(sc - m), jnp.exp(sp - m)
                den = jnp.sum(pc, axis=-1, keepdims=True) + jnp.sum(pp, axis=-1, keepdims=True)
                oh = _dot((pc / den).astype(BF16), vc) + _dot((pp / den).astype(BF16), vp)
                o_pair = jnp.where(hm, oh, o_pair)
                l_pair = jnp.where(hm, m + jnp.log(den), l_pair)
            o_ref[:, sl] = o_pair
            l_ref[:, sl] = l_pair

    view = proj[3 * g:3 * g + 3].reshape(3, rows, dil * ATTN_GROUP_WIDTH)
    blk = (None, ATTN_BLOCK, ATTN_GROUP_WIDTH)
    prev = lambda j: jnp.maximum(j - 1, 0)
    out_blk = pl.BlockSpec((ATTN_BLOCK, ATTN_GROUP_WIDTH), lambda r, j: (j, r))
    (o, lse), got = _pcall(
        f"attn_fwd_g{g}", body, grid=(dil, nb),
        in_specs=[pl.BlockSpec(blk, lambda r, j: (0, j, r)),
                  pl.BlockSpec(blk, lambda r, j: (1, prev(j), r)), pl.BlockSpec(blk, lambda r, j: (1, j, r)),
                  pl.BlockSpec(blk, lambda r, j: (2, prev(j), r)), pl.BlockSpec(blk, lambda r, j: (2, j, r))],
        out_specs=[out_blk] * 2, out_shape=[jax.ShapeDtypeStruct((rows, dil * ATTN_GROUP_WIDTH), F32)] * 2,
        scratch_shapes=[], sem=("parallel", "parallel"), args=(view,) * 5, ride=ride)
    return o.reshape(s, ATTN_GROUP_WIDTH), lse.reshape(s, ATTN_GROUP_WIDTH), got


def _attn_bwd(proj, g, do, lse, cc, ride=None):
    s = proj.shape[1]
    dil = ATTN_DILATIONS[g]
    rows, nblk = s // dil, s // dil // ATTN_BLOCK

    def body(q_ref, kp_ref, kc_ref, vp_ref, vc_ref, do_ref, l_ref, c_ref, dq_ref, dk_ref, dv_ref, ck, cv):
        j = pl.program_id(1)
        valid = j < nblk
        first = jnp.minimum(j, nblk - 1) == 0

        @pl.when(j == 0)
        def _():
            ck[...] = jnp.zeros_like(ck)
            cv[...] = jnp.zeros_like(cv)

        qi, kj = _iota((128, 128), 0), _iota((128, 128), 1)
        mask_c = jnp.logical_and(kj <= qi, valid)
        mask_p = jnp.logical_and(jnp.logical_and(kj >= qi, jnp.logical_not(first)), valid)
        low = kj < 64
        for p in range(4):
            sl = slice(128 * p, 128 * p + 128)
            q, kp, kc, vp, vc, dov = q_ref[:, sl], kp_ref[:, sl], kc_ref[:, sl], vp_ref[:, sl], vc_ref[:, sl], do_ref[:, sl]
            lse_pair, c_pair = l_ref[:, sl], c_ref[:, sl]
            dq_pair = jnp.zeros((128, 128), F32)
            dkc = jnp.zeros((128, 128), F32)
            dkp = jnp.zeros((128, 128), F32)
            dvc = jnp.zeros((128, 128), F32)
            dvp = jnp.zeros((128, 128), F32)
            for half in range(2):
                hm = low if half == 0 else jnp.logical_not(low)
                col = slice(64 * half, 64 * half + 1)
                lse_h, c_h = lse_pair[:, col], c_pair[:, col]
                qm = jnp.where(hm, q, jnp.zeros_like(q))
                dom = jnp.where(hm, dov, jnp.zeros_like(dov))
                pc = jnp.exp(jnp.where(mask_c, _dot_nt(qm, kc) * 0.125, NEG_BIG) - lse_h)
                pp = jnp.exp(jnp.where(mask_p, _dot_nt(qm, kp) * 0.125, NEG_BIG) - lse_h)
                dsc = (pc * (_dot_nt(dom, vc) + c_h) * 0.125).astype(BF16)
                dsp = (pp * (_dot_nt(dom, vp) + c_h) * 0.125).astype(BF16)
                dq_pair = jnp.where(hm, _dot(dsc, kc) + _dot(dsp, kp), dq_pair)
                dkc += _dot_tn(dsc, qm)
                dkp += _dot_tn(dsp, qm)
                dvc += _dot_tn(pc.astype(BF16), dom)
                dvp += _dot_tn(pp.astype(BF16), dom)

            @pl.when(valid)
            def _():
                dq_ref[:, sl] = dq_pair.astype(BF16)

            dk_ref[:, sl] = (ck[:, sl] + dkp).astype(BF16)
            dv_ref[:, sl] = (cv[:, sl] + dvp).astype(BF16)
            ck[:, sl] = dkc
            cv[:, sl] = dvc

    wide = dil * ATTN_GROUP_WIDTH
    view = proj[3 * g:3 * g + 3].reshape(3, rows, wide)
    blk = (None, ATTN_BLOCK, ATTN_GROUP_WIDTH)
    flat = (ATTN_BLOCK, ATTN_GROUP_WIDTH)
    cur = lambda j: jnp.minimum(j, nblk - 1)
    prev = lambda j: jnp.maximum(jnp.minimum(j, nblk - 1) - 1, 0)
    out_prev = lambda j: jnp.maximum(j - 1, 0)
    (dq, dk, dv), got = _pcall(
        f"attn_bwd_g{g}", body, grid=(dil, nblk + 1),
        in_specs=[pl.BlockSpec(blk, lambda r, j: (0, cur(j), r)),
                  pl.BlockSpec(blk, lambda r, j: (1, prev(j), r)), pl.BlockSpec(blk, lambda r, j: (1, cur(j), r)),
                  pl.BlockSpec(blk, lambda r, j: (2, prev(j), r)), pl.BlockSpec(blk, lambda r, j: (2, cur(j), r)),
                  pl.BlockSpec(flat, lambda r, j: (cur(j), r)), pl.BlockSpec(flat, lambda r, j: (cur(j), r)),
                  pl.BlockSpec(flat, lambda r, j: (cur(j), r))],
        out_specs=[pl.BlockSpec(flat, lambda r, j: (cur(j), r)),
                   pl.BlockSpec(flat, lambda r, j: (out_prev(j), r)), pl.BlockSpec(flat, lambda r, j: (out_prev(j), r))],
        out_shape=[jax.ShapeDtypeStruct((rows, wide), BF16)] * 3,
        scratch_shapes=[pltpu.VMEM((ATTN_BLOCK, ATTN_GROUP_WIDTH), F32)] * 2, sem=("parallel", "arbitrary"),
        args=(view,) * 5 + (do.reshape(rows, wide), lse.reshape(rows, wide), cc.reshape(rows, wide)), ride=ride)
    return [t.reshape(s, ATTN_GROUP_WIDTH) for t in (dq, dk, dv)], got


MP_TILE = 256


def _merge_weights(l_refs):
    l0, l1, l2 = l_refs[0][...], l_refs[1][...], l_refs[2][...]
    m = jnp.maximum(jnp.maximum(l0, l1), l2)
    e0, e1, e2 = jnp.exp(l0 - m), jnp.exp(l1 - m), jnp.exp(l2 - m)
    den = e0 + e1 + e2
    return e0 / den, e1 / den, e2 / den


def _pool_diff(ucat, gi, tok):
    window = 2 << gi
    ug = ucat[:, 128 * gi:128 * gi + 128]
    acc, shift = ug, 1
    while shift < window:
        acc = acc + pltpu.roll(acc, shift, 0)
        shift *= 2
    cnt = jnp.minimum(tok + 1, window).astype(F32)
    return acc[POOL_HALO:, :] / cnt - ug[POOL_HALO:, :]


def _merge_pool_fwd(o, lse, proj, pool_w, pool_scale):
    s = o[0].shape[0]
    tr = MP_TILE

    def body(o0_ref, o1_ref, o2_ref, l0_ref, l1_ref, l2_ref, u_ref, uh_ref, pw_ref, ps_ref, cat_ref):
        i = pl.program_id(0)
        w0, w1, w2 = _merge_weights((l0_ref, l1_ref, l2_ref))
        cat_ref[:, 0:512] = (w0 * o0_ref[...] + w1 * o1_ref[...] + w2 * o2_ref[...]).astype(BF16)
        halo = jnp.where(i > 0, uh_ref[...].astype(F32), 0.0)
        ucat = jnp.concatenate([halo, u_ref[...].astype(F32)], axis=0)
        tok = i * tr + _iota((tr, 1), 0)
        for gi in range(POOL_GROUPS):
            sl = slice(128 * gi, 128 * gi + 128)
            diff = _pool_diff(ucat, gi, tok)
            yg = _dot(diff.astype(BF16), pw_ref[gi].astype(BF16)) * ps_ref[:, sl]
            cat_ref[:, 512 + 128 * gi:640 + 128 * gi] = yg.astype(BF16)

    return pl.pallas_call(
        body, name="merge_pool_fwd", grid=(s // tr,),
        in_specs=[pl.BlockSpec((tr, 512), lambda i: (i, 0))] * 6 + [
                  pl.BlockSpec((None, tr, 512), lambda i: (9, i, 0)),
                  pl.BlockSpec((None, POOL_HALO, 512), lambda i: (9, jnp.maximum(i * (tr // POOL_HALO) - 1, 0), 0)),
                  pl.BlockSpec((4, 128, 128), lambda i: (0, 0, 0)), pl.BlockSpec((1, 512), lambda i: (0, 0))],
        out_specs=pl.BlockSpec((tr, 1024), lambda i: (i, 0)),
        out_shape=jax.ShapeDtypeStruct((s, EVEN_OUT_WIDTH), BF16), compiler_params=_cparams("parallel"),
    )(*o, *lse, proj, proj, pool_w, pool_scale)


def _merge_pool_bwd(dcat, o, lse, proj, pool_w, pool_scale, head_sum):
    s = o[0].shape[0]
    tr = MP_TILE
    n_tiles = s // tr

    def body(da_ref, dp_ref, dph_ref, o0_ref, o1_ref, o2_ref, l0_ref, l1_ref, l2_ref, u_ref, uh_ref, pw_ref, ps_ref,
             hs_ref, do0_ref, do1_ref, do2_ref, cc0_ref, cc1_ref, cc2_ref, du_ref, dpw_ref, st_ref):
        i = pl.program_id(0)

        @pl.when(i == 0)
        def _():
            dpw_ref[...] = jnp.zeros_like(dpw_ref)
            st_ref[...] = jnp.zeros_like(st_ref)

        ws = _merge_weights((l0_ref, l1_ref, l2_ref))
        da = da_ref[...]
        attn = ws[0] * o0_ref[...] + ws[1] * o1_ref[...] + ws[2] * o2_ref[...]
        per_head = _dot3_right(da * attn, hs_ref[...])
        for wg, do_ref, cc_ref in zip(ws, (do0_ref, do1_ref, do2_ref), (cc0_ref, cc1_ref, cc2_ref)):
            do_ref[...] = (wg * da).astype(BF16)
            cc_ref[...] = -wg * per_head

        halo = jnp.where(i > 0, uh_ref[...].astype(F32), 0.0)
        ucat = jnp.concatenate([halo, u_ref[...].astype(F32)], axis=0)
        tok = i * tr + _iota((tr, 1), 0)
        dyp = dp_ref[...]
        dnext = jnp.where(i < n_tiles - 1, dph_ref[...], 0.0)
        dyp_ext = jnp.concatenate([dyp, dnext], axis=0)
        tok_ext = i * tr + _iota((tr + POOL_HALO, 1), 0)
        for gi in range(POOL_GROUPS):
            sl = slice(128 * gi, 128 * gi + 128)
            window = 2 << gi
            pw16 = pw_ref[gi].astype(BF16)
            d16 = _pool_diff(ucat, gi, tok).astype(BF16)
            st_ref[0:1, sl] += jnp.sum(dyp[:, sl] * _dot(d16, pw16), axis=0, keepdims=True)
            dpw_ref[gi] += _dot_tn(d16, (dyp[:, sl] * ps_ref[:, sl]).astype(BF16))
            dd = _dot_nt((dyp_ext[:, sl] * ps_ref[:, sl]).astype(BF16), pw16)
            acc = dd / jnp.minimum(tok_ext + 1, window).astype(F32)
            shift = 1
            while shift < window:
                acc = acc + pltpu.roll(acc, tr + POOL_HALO - shift, 0)
                shift *= 2
            du_ref[:, sl] = (acc[:tr, :] - dd[:tr, :]).astype(BF16)

    halo_blocks = tr // POOL_HALO
    return pl.pallas_call(
        body, name="merge_pool_bwd", grid=(n_tiles,),
        in_specs=[pl.BlockSpec((tr, 512), lambda i: (i, 0)), pl.BlockSpec((tr, 512), lambda i: (i, 1)),
                  pl.BlockSpec((POOL_HALO, 512), lambda i: (jnp.minimum((i + 1) * halo_blocks, s // POOL_HALO - 1), 1))]
                 + [pl.BlockSpec((tr, 512), lambda i: (i, 0))] * 6 + [
                  pl.BlockSpec((None, tr, 512), lambda i: (9, i, 0)),
                  pl.BlockSpec((None, POOL_HALO, 512), lambda i: (9, jnp.maximum(i * halo_blocks - 1, 0), 0)),
                  pl.BlockSpec((4, 128, 128), lambda i: (0, 0, 0)), pl.BlockSpec((1, 512), lambda i: (0, 0)),
                  pl.BlockSpec((512, 512), lambda i: (0, 0))],
        out_specs=[pl.BlockSpec((tr, 512), lambda i: (i, 0))] * 7 + [
                   pl.BlockSpec((4, 128, 128), lambda i: (0, 0, 0)), pl.BlockSpec((8, 512), lambda i: (0, 0))],
        out_shape=[jax.ShapeDtypeStruct((s, 512), BF16)] * 3 + [jax.ShapeDtypeStruct((s, 512), F32)] * 3 + [
                   jax.ShapeDtypeStruct((s, 512), BF16), jax.ShapeDtypeStruct((4, 128, 128), F32),
                   jax.ShapeDtypeStruct((8, 512), F32)],
        compiler_params=_cparams("arbitrary"),
    )(dcat, dcat, dcat, *o, *lse, proj, proj, pool_w, pool_scale, head_sum)


CONV_TILE = 1024
CONV_HALO = 8
XBC_BLOCK0 = SSM_D_INNER // 512
DT_BLOCK = (SSM_D_INNER + SSM_CONV_DIM) // 128


def _conv_taps(xcat, w, bias):
    pre = bias + w[3:4, :] * xcat[CONV_HALO:, :]
    for back in range(1, SSM_CONV):
        pre = pre + w[3 - back:4 - back, :] * pltpu.roll(xcat, back, 0)[CONV_HALO:, :]
    return pre


def _conv_fwd(proj, conv_w, conv_b):
    s = proj.shape[0]
    tr = CONV_TILE

    def body(x_ref, xh_ref, w_ref, b_ref, act_ref):
        i = pl.program_id(0)
        xcat = jnp.concatenate([jnp.where(i > 0, xh_ref[...], 0.0), x_ref[...]], axis=0)
        pre = _conv_taps(xcat, w_ref[...], b_ref[...])
        act_ref[...] = pre * _sigmoid(pre)

    hb = tr // CONV_HALO
    return pl.pallas_call(
        body, name="conv_fwd", grid=(s // tr, SSM_CONV_DIM // 512),
        in_specs=[pl.BlockSpec((tr, 512), lambda i, j: (i, XBC_BLOCK0 + j)),
                  pl.BlockSpec((CONV_HALO, 512), lambda i, j: (jnp.maximum(i * hb - 1, 0), XBC_BLOCK0 + j)),
                  pl.BlockSpec((SSM_CONV, 512), lambda i, j: (0, j)), pl.BlockSpec((1, 512), lambda i, j: (0, j))],
        out_specs=pl.BlockSpec((tr, 512), lambda i, j: (i, j)),
        out_shape=jax.ShapeDtypeStruct((s, SSM_CONV_DIM), F32), compiler_params=_cparams("parallel", "parallel"),
    )(proj, proj, conv_w, conv_b)


def _conv_bwd(name, dact, proj, conv_w, conv_b, dproj, *, block0):
    s, width = dact.shape
    tr = CONV_TILE
    n_tiles = s // tr
    hb = tr // CONV_HALO

    def body(da_ref, dah_ref, x_ref, xh_ref, xn_ref, w_ref, b_ref, _, dp_ref, st_ref):
        i = pl.program_id(1)

        @pl.when(i == 0)
        def _():
            st_ref[...] = jnp.zeros_like(st_ref)

        da_ext = jnp.concatenate([da_ref[...], jnp.where(i < n_tiles - 1, dah_ref[...], 0.0)], axis=0)
        xcat = jnp.concatenate([jnp.where(i > 0, xh_ref[...], 0.0), x_ref[...]], axis=0)
        pre_ext = _conv_taps(jnp.concatenate([xcat, xn_ref[...]], axis=0), w_ref[...], b_ref[...])
        sg = _sigmoid(pre_ext)
        dpre_ext = da_ext * (sg * (1.0 + pre_ext * (1.0 - sg)))
        w = w_ref[...]
        draw = w[3:4, :] * dpre_ext[:tr, :]
        for ahead in range(1, SSM_CONV):
            draw = draw + w[3 - ahead:4 - ahead, :] * pltpu.roll(dpre_ext, tr + CONV_HALO - ahead, 0)[:tr, :]
        dp_ref[...] = draw.astype(BF16)
        dpre = dpre_ext[:tr, :]
        st_ref[3:4, :] += jnp.sum(dpre * xcat[CONV_HALO:, :], axis=0, keepdims=True)
        for back in range(1, SSM_CONV):
            st_ref[3 - back:4 - back, :] += jnp.sum(dpre * pltpu.roll(xcat, back, 0)[CONV_HALO:, :], axis=0, keepdims=True)
        st_ref[4:5, :] += jnp.sum(dpre, axis=0, keepdims=True)

    nxt = lambda i: jnp.minimum((i + 1) * hb, s // CONV_HALO - 1)
    prv = lambda i: jnp.maximum(i * hb - 1, 0)
    return pl.pallas_call(
        body, name=name, grid=(width // 512, n_tiles),
        in_specs=[pl.BlockSpec((tr, 512), lambda j, i: (i, j)), pl.BlockSpec((CONV_HALO, 512), lambda j, i: (nxt(i), j)),
                  pl.BlockSpec((tr, 512), lambda j, i: (i, XBC_BLOCK0 + block0 + j)),
                  pl.BlockSpec((CONV_HALO, 512), lambda j, i: (prv(i), XBC_BLOCK0 + block0 + j)),
                  pl.BlockSpec((CONV_HALO, 512), lambda j, i: (nxt(i), XBC_BLOCK0 + block0 + j)),
                  pl.BlockSpec((SSM_CONV, 512), lambda j, i: (0, block0 + j)),
                  pl.BlockSpec((1, 512), lambda j, i: (0, block0 + j)), ANY],
        out_specs=[pl.BlockSpec((tr, 512), lambda j, i: (i, XBC_BLOCK0 + block0 + j)),
                   pl.BlockSpec((8, 512), lambda j, i: (0, j))],
        out_shape=[jax.ShapeDtypeStruct(dproj.shape, BF16), jax.ShapeDtypeStruct((8, width), F32)],
        input_output_aliases={7: 0}, compiler_params=_cparams("parallel", "arbitrary"),
    )(dact, dact, proj, proj, proj, conv_w, conv_b, dproj)


def _dt_fwd(proj, dt_bias):
    s = proj.shape[0]

    def body(x_ref, b_ref, o_ref):
        v = x_ref[...] + b_ref[...]
        o_ref[...] = jnp.maximum(v, 0.0) + jnp.log(1.0 + jnp.exp(-jnp.abs(v)))

    return pl.pallas_call(
        body, name="dt_fwd", grid=(s // CONV_TILE,),
        in_specs=[pl.BlockSpec((CONV_TILE, 128), lambda i: (i, DT_BLOCK)), pl.BlockSpec((1, 128), lambda i: (0, 0))],
        out_specs=pl.BlockSpec((CONV_TILE, 128), lambda i: (i, 0)),
        out_shape=jax.ShapeDtypeStruct((s, 128), F32), compiler_params=_cparams("parallel"),
    )(proj, dt_bias)


def _dt_bwd(ddt, proj, dt_bias, dproj):
    s = proj.shape[0]

    def body(d_ref, x_ref, b_ref, _, dp_ref, st_ref):
        @pl.when(pl.program_id(0) == 0)
        def _():
            st_ref[...] = jnp.zeros_like(st_ref)

        draw = d_ref[...] * _sigmoid(x_ref[...] + b_ref[...])
        dp_ref[...] = draw.astype(BF16)
        st_ref[0:1, :] += jnp.sum(draw, axis=0, keepdims=True)

    return pl.pallas_call(
        body, name="dt_bwd", grid=(s // CONV_TILE,),
        in_specs=[pl.BlockSpec((CONV_TILE, 128), lambda i: (i, 0)), pl.BlockSpec((CONV_TILE, 128), lambda i: (i, DT_BLOCK)),
                  pl.BlockSpec((1, 128), lambda i: (0, 0)), ANY],
        out_specs=[pl.BlockSpec((CONV_TILE, 128), lambda i: (i, DT_BLOCK)), pl.BlockSpec((8, 128), lambda i: (0, 0))],
        out_shape=[jax.ShapeDtypeStruct(dproj.shape, BF16), jax.ShapeDtypeStruct((8, 128), F32)],
        input_output_aliases={3: 0}, compiler_params=_cparams("arbitrary"),
    )(ddt, proj, dt_bias, dproj)


def _ssd_common(x_ref, b_ref, c_ref, dt_ref, al_ref):
    row, col = _iota((128, 128), 0), _iota((128, 128), 1)
    tril = row >= col
    expand = jnp.where((_iota((128, 512), 1) >> 6) == _iota((128, 512), 0), 1.0, 0.0).astype(BF16)
    dt = dt_ref[...]
    a_neg = -jnp.exp(al_ref[...])
    a_col = _dot3_left(jnp.where(tril, 1.0, 0.0).astype(BF16), dt * a_neg)
    a_exp = _dot3_right(a_col, expand)
    dt_exp = _dot3_right(dt, expand)
    x = x_ref[...]
    return dict(tril=tril, col=col, expand=expand, dt=dt, a_neg=a_neg, a_col=a_col, a_row=a_col.T, a_exp=a_exp,
                dt_exp=dt_exp, a_last=a_exp[127:128, :], x=x, xd=x * dt_exp,
                b16=b_ref[...].astype(BF16), c16=c_ref[...].astype(BF16))


def _ssd_specs(nc, order):
    return [pl.BlockSpec((SSM_CHUNK, 512), lambda g, c: (order(c), g)),
            pl.BlockSpec((SSM_CHUNK, 128), lambda g, c: (order(c), SSM_D_INNER // 128 + g)),
            pl.BlockSpec((SSM_CHUNK, 128), lambda g, c: (order(c), SSM_D_INNER // 128 + SSM_GROUPS + g)),
            pl.BlockSpec((SSM_CHUNK, 512), lambda g, c: (order(c), g)),
            pl.BlockSpec((None, SSM_CHUNK, 128), lambda g, c: (g, order(c), 0)),
            pl.BlockSpec((None, 1, 128), lambda g, c: (g, 0, 0)),
            pl.BlockSpec((1, 512), lambda g, c: (0, g)), pl.BlockSpec((1, 512), lambda g, c: (0, g))]


def _ssd_fwd(act, proj, dtc, a_log, d_skip, norm_g, ride=None):
    s = act.shape[0]
    nc = s // SSM_CHUNK

    def body(x_ref, b_ref, c_ref, z_ref, dt_ref, al_ref, dsk_ref, ng_ref, y_ref, yn_ref, hin_ref, h_sc):
        @pl.when(pl.program_id(1) == 0)
        def _():
            h_sc[...] = jnp.zeros_like(h_sc)

        q = _ssd_common(x_ref, b_ref, c_ref, dt_ref, al_ref)
        gmat = _dot_nt(q["c16"], q["b16"])
        h_in = h_sc[...]
        hin_ref[...] = h_in
        zmat = _dot(q["c16"], h_in.astype(BF16))
        xd16 = q["xd"].astype(BF16)
        low = q["col"] < 64
        pieces = []
        for p in range(4):
            xs = xd16[:, 128 * p:128 * p + 128]
            acc = jnp.zeros((128, 128), F32)
            for half in range(2):
                r = 2 * p + half
                decay = jnp.exp(jnp.where(q["tril"], q["a_col"][:, r:r + 1] - q["a_row"][r:r + 1, :], NEG_BIG))
                hm = low if half == 0 else jnp.logical_not(low)
                acc += _dot((gmat * decay).astype(BF16), jnp.where(hm, xs, jnp.zeros_like(xs)))
            pieces.append(acc)
        y = jnp.concatenate(pieces, axis=1) + zmat * jnp.exp(q["a_exp"]) + q["x"] * dsk_ref[...]
        y_ref[...] = y
        w16 = (q["xd"] * jnp.exp(q["a_last"] - q["a_exp"])).astype(BF16)
        h_sc[...] = h_in * jnp.exp(q["a_last"]) + _dot_tn(q["b16"], w16)
        z = z_ref[...]
        yg = y * (z * _sigmoid(z))
        rr = lax.rsqrt(jnp.mean(yg * yg, axis=-1, keepdims=True) + NORM_EPS)
        yn_ref[...] = (yg * rr * ng_ref[...]).astype(BF16)

    blk = pl.BlockSpec((SSM_CHUNK, 512), lambda g, c: (c, g))
    return _pcall(
        "ssd_fwd", body, grid=(SSM_GROUPS, nc), in_specs=_ssd_specs(nc, lambda c: c),
        out_specs=[blk, blk, pl.BlockSpec((None, None, SSM_STATE, 512), lambda g, c: (g, c, 0, 0))],
        out_shape=[jax.ShapeDtypeStruct((s, SSM_D_INNER), F32), jax.ShapeDtypeStruct((s, SSM_D_INNER), BF16),
                   jax.ShapeDtypeStruct((SSM_GROUPS, nc, SSM_STATE, 512), F32)],
        scratch_shapes=[pltpu.VMEM((SSM_STATE, 512), F32)], sem=("parallel", "arbitrary"),
        args=(act, act, act, proj, dtc, a_log, d_skip, norm_g), ride=ride)


def _ssd_bwd(act, proj, dtc, a_log, d_skip, norm_g, y, h_in_all, dyn, ride=None):
    s = act.shape[0]
    nc = s // SSM_CHUNK

    def body(x_ref, b_ref, c_ref, z_ref, dt_ref, al_ref, dsk_ref, ng_ref, y_ref, hin_ref, dyn_ref,
             dz_ref, dx_ref, db_ref, dc_ref, ddt_ref, st_ref, dal_ref, dh_sc):
        @pl.when(pl.program_id(1) == 0)
        def _():
            dh_sc[...] = jnp.zeros_like(dh_sc)
            st_ref[...] = jnp.zeros_like(st_ref)
            dal_ref[...] = jnp.zeros_like(dal_ref)

        q = _ssd_common(x_ref, b_ref, c_ref, dt_ref, al_ref)
        x, xd, b16, c16 = q["x"], q["xd"], q["b16"], q["c16"]
        z, yv, dyn_v = z_ref[...], y_ref[...], dyn_ref[...]
        sig = _sigmoid(z)
        sil = z * sig
        yg = yv * sil
        rr = lax.rsqrt(jnp.mean(yg * yg, axis=-1, keepdims=True) + NORM_EPS)
        st_ref[0:1, :] += jnp.sum(dyn_v * yg * rr, axis=0, keepdims=True)
        t1 = dyn_v * ng_ref[...]
        dyg = rr * (t1 - yg * (rr * rr) * jnp.mean(t1 * yg, axis=-1, keepdims=True))
        dy = dyg * sil
        dz_ref[...] = (dyg * yv * (sig * (1.0 + z * (1.0 - sig)))).astype(BF16)
        st_ref[1:2, :] += jnp.sum(dy * x, axis=0, keepdims=True)
        dx = dsk_ref[...] * dy
        h_in = hin_ref[...]
        h16 = h_in.astype(BF16)
        ea = jnp.exp(q["a_exp"])
        zmat = _dot(c16, h16)
        dz16 = (dy * ea).astype(BF16)
        da_ch = dy * zmat * ea
        dcm = _dot_nt(dz16, h16)
        dh_in = _dot_tn(c16, dz16)
        dh_out = dh_sc[...]
        dho16 = dh_out.astype(BF16)
        eal = jnp.exp(q["a_last"])
        dh_in += dh_out * eal
        dal_ch = jnp.sum(dh_out * h_in, axis=0, keepdims=True) * eal
        to_end = jnp.exp(q["a_last"] - q["a_exp"])
        wmat = xd * to_end
        dbm = _dot_nt(wmat.astype(BF16), dho16)
        dw = _dot(b16, dho16)
        dxd = dw * to_end
        g_end = dw * wmat
        da_ch -= g_end
        dal_ch += jnp.sum(g_end, axis=0, keepdims=True)
        gmat = _dot_nt(c16, b16)
        row = _iota((128, 128), 0)
        triu = row <= q["col"]
        xd16, dy16 = xd.astype(BF16), dy.astype(BF16)
        low = q["col"] < 64
        da_col = jnp.zeros((128, 128), F32)
        da_key = jnp.zeros((128, 128), F32)
        dg = jnp.zeros((128, 128), F32)
        pieces = []
        for p in range(4):
            xs, dys = xd16[:, 128 * p:128 * p + 128], dy16[:, 128 * p:128 * p + 128]
            acc = jnp.zeros((128, 128), F32)
            for half in range(2):
                r = 2 * p + half
                hm = low if half == 0 else jnp.logical_not(low)
                xm = jnp.where(hm, xs, jnp.zeros_like(xs))
                dym = jnp.where(hm, dys, jnp.zeros_like(dys))
                decay = jnp.exp(jnp.where(q["tril"], q["a_col"][:, r:r + 1] - q["a_row"][r:r + 1, :], NEG_BIG))
                acc += _dot_tn((gmat * decay).astype(BF16), dym)
                dgl = _dot_nt(dym, xm) * decay
                dg += dgl
                n_ls = dgl * gmat
                da_col += jnp.where(q["col"] == r, jnp.sum(n_ls, axis=-1, keepdims=True), 0.0)
                da_key += jnp.where(row == r, jnp.sum(n_ls, axis=0, keepdims=True), 0.0)
            pieces.append(acc)
        da_col -= da_key.T
        dxd += jnp.concatenate(pieces, axis=1)
        dg16 = dg.astype(BF16)
        dcm += _dot(dg16, b16)
        dbm += _dot_tn(dg16, c16)
        fold = jnp.where((_iota((512, 128), 0) >> 6) == _iota((512, 128), 1), 1.0, 0.0).astype(BF16)
        da_ch += jnp.where(_iota((128, 1), 0) == 127, dal_ch, 0.0)
        da_col += _dot3_right(da_ch, fold)
        d_dta = _dot3_left(jnp.where(triu, 1.0, 0.0).astype(BF16), da_col)
        ddt_ref[...] = d_dta * q["a_neg"] + _dot3_right(dxd * x, fold)
        dal_ref[0:1, :] += jnp.sum(d_dta * q["dt"] * q["a_neg"], axis=0, keepdims=True)
        dx_ref[...] = dx + dxd * q["dt_exp"]
        db_ref[...] = dbm
        dc_ref[...] = dcm
        dh_sc[...] = dh_in

    rev = lambda c: nc - 1 - c
    blk = pl.BlockSpec((SSM_CHUNK, 512), lambda g, c: (rev(c), g))
    small = pl.BlockSpec((SSM_CHUNK, 128), lambda g, c: (rev(c), g))
    return _pcall(
        "ssd_bwd", body, grid=(SSM_GROUPS, nc), ride=ride,
        args=(act, act, act, proj, dtc, a_log, d_skip, norm_g, y, h_in_all, dyn), sem=("parallel", "arbitrary"),
        in_specs=_ssd_specs(nc, rev) + [blk, pl.BlockSpec((None, None, SSM_STATE, 512), lambda g, c: (g, rev(c), 0, 0)), blk],
        out_specs=[blk, blk, small, small, pl.BlockSpec((None, SSM_CHUNK, 128), lambda g, c: (g, rev(c), 0)),
                   pl.BlockSpec((8, 512), lambda g, c: (0, g)), pl.BlockSpec((None, 8, 128), lambda g, c: (g, 0, 0))],
        out_shape=[jax.ShapeDtypeStruct((s, SSM_IN_PAD), BF16), jax.ShapeDtypeStruct((s, SSM_D_INNER), F32),
                   jax.ShapeDtypeStruct((s, SSM_GROUPS * SSM_STATE), F32), jax.ShapeDtypeStruct((s, SSM_GROUPS * SSM_STATE), F32),
                   jax.ShapeDtypeStruct((SSM_GROUPS, s, 128), F32), jax.ShapeDtypeStruct((8, SSM_D_INNER), F32),
                   jax.ShapeDtypeStruct((SSM_GROUPS, 8, 128), F32)],
        scratch_shapes=[pltpu.VMEM((SSM_STATE, 512), F32)])


MM_TM = 1024
MM_TK = 512
MM_TK_BIG = 2048
FFN_SHARD = FFN_HIDDEN // N_DEV


def _ij(tm, tn):
    return pl.BlockSpec((tm, tn), lambda i, j, k: (i, j))


def _mm_plain(name, a, b, mode, dims, out_dtype, tn=1024, tk=MM_TK, **kw):
    res = _matmul(name, a, b, mode=mode, dims=dims, tiles=(MM_TM, tn, tk), outs=[((dims[0], dims[1]), out_dtype)],
                  epilogue=_epi_plain, **kw)
    return res[0], res[1:]


def _mm_resgate(name, a, b, res, gate, k_dim, tk, ride=None):
    s, d = res.shape
    got = _matmul(name, a, b, mode="nn", dims=(s, d, k_dim), tiles=(MM_TM, 1024, tk),
                  outs=[((s, d), F32), ((s, d), BF16)], epilogue=_epi_resgate, extras=[res, gate],
                  extra_specs=[_ij(MM_TM, 1024), pl.BlockSpec((1, 1024), lambda i, j, k: (0, j))], ride=ride)
    return got[0], got[1], got[2:]


def _ffn_fwd(tag, x_in, gain, scale, shift, gate, w1, w2_of, ride1=None, ride2=None):
    s, d = x_in.shape
    h = _norm_mod_fwd(tag + "_norm2", x_in, gain, scale, shift)
    rr, pre, *got1 = _matmul(tag + "_ffn1", h, w1, mode="nn", dims=(s, FFN_HIDDEN, d), tiles=(MM_TM, FFN_SHARD, MM_TK_BIG),
                             b_spec=pl.BlockSpec((None, MM_TK_BIG, FFN_SHARD), lambda i, j, k: (j, k, 0)),
                             outs=[((s, FFN_HIDDEN), BF16)] * 2, epilogue=_epi_relu2, ride=ride1)
    x_out, f, got2 = _mm_resgate(tag + "_ffn2", rr, w2_of(got1), x_in, gate, FFN_HIDDEN, MM_TK_BIG, ride=ride2)
    return x_out, (h, rr, pre, f), got1, got2


def _ffn_bwd(tag, dx_out, dy, x_in, saved, gain, scale, w1, w2, below=None, ride_dx2=None, ride_dw2=None,
             ride_dx1_of=None):
    s, d = x_in.shape
    h, rr, pre, _ = saved
    da, *got2 = _matmul(tag + "_ffn2_dx", dy, w2, mode="nt", dims=(s, FFN_HIDDEN, d), tiles=(MM_TM, 1024, MM_TK_BIG),
                        outs=[((s, FFN_HIDDEN), BF16)], epilogue=_epi_drelu2, extras=[pre],
                        extra_specs=[_ij(MM_TM, 1024)], ride=ride_dx2)
    dw2, got_dw2 = _mm_plain(tag + "_ffn2_dw", rr, dy, "tn", (FFN_HIDDEN, d, s), BF16, tk=MM_TK_BIG, ride=ride_dw2)
    dh, got1 = _mm_plain(tag + "_ffn1_dx", da, w1, "nt", (s, d, FFN_HIDDEN), F32, tk=2 * FFN_SHARD, b_slabs=2,
                         b_spec=pl.BlockSpec((2, 1024, FFN_SHARD), lambda i, j, k: (k, j, 0)),
                         ride=None if ride_dx1_of is None else ride_dx1_of(dw2))
    dw1 = _matmul(tag + "_ffn1_dw", h, da, mode="tn", dims=(d, FFN_HIDDEN, s), tiles=(MM_TM, FFN_SHARD, MM_TK_BIG),
                  outs=[((N_DEV, d, FFN_SHARD), BF16)], epilogue=_epi_plain,
                  out_specs=[pl.BlockSpec((None, MM_TM, FFN_SHARD), lambda i, j, k: (j, i, 0))])[0]
    dx_in, st_norm, *dy_below = _norm_mod_bwd(tag + "_norm2_bwd", x_in, gain, scale, dh, dx_out, below)
    return dx_in, dw1, dw2, st_norm, (dy_below[0] if dy_below else None), got2, got_dw2, got1


def _slab_of(col_block):
    return jnp.where(col_block < 9, 3 * (col_block % 3) + col_block // 3, 9)


def _slab_ordered_cols(name, w):
    k_dim, width = w.shape
    tr = 512

    def body(w_ref, o_ref):
        o_ref[...] = w_ref[...]

    return pl.pallas_call(
        body, name=name, grid=(k_dim // tr, width // 512),
        in_specs=[pl.BlockSpec((tr, 512), lambda i, j: (i, _slab_of(j)))], out_specs=pl.BlockSpec((tr, 512), lambda i, j: (i, j)),
        out_shape=jax.ShapeDtypeStruct(w.shape, w.dtype), compiler_params=_cparams("parallel", "parallel"),
    )(w)


def _full_weight(name, gathered):
    if name in ASSEMBLED:
        return _assemble_cols("assemble_" + name, gathered, ASSEMBLED[name])
    if name == "ffn_w1":
        return gathered
    return gathered.reshape(-1, gathered.shape[2])


def _grad_pieces(name, full, shard_shape, half=None):
    rows = shard_shape[0]
    lo, n = (0, rows) if half is None else (half * (rows // 2), rows // 2)
    if name in ASSEMBLED:
        return _split_cols(f"split_{name}_{lo}", full, shard_shape[1], lo, n)
    return full.reshape(N_DEV, *shard_shape)[:, lo:lo + n]


def _device_step(x, target, mod, sm, shards, w_even_in):
    s, d = x.shape
    mv = [[mod[i, k].reshape(1, d) for k in range(6)] for i in range(2)]
    nm = [sm["norm_mix"][i].reshape(1, d) for i in range(2)]
    nf = [sm["norm_ffn"][i].reshape(1, d) for i in range(2)]
    pool_w, pool_scale = sm["pool_w"].reshape(4, 128, 128), sm["pool_scale"].reshape(1, POOL_WIDTH)
    got = {}

    def gather(*items):
        return [shards[it] for it in items], False

    def gather_half(item, half):
        rows = shards[item].shape[0] // 2
        return [shards[item][half * rows:(half + 1) * rows]], False

    def scatter(*pieces):
        return list(pieces), True

    def pieces(item, full, half=None):
        return _grad_pieces(item[0], full, shards[item].shape, half)

    def joined(top, bottom):
        return jnp.concatenate([top, bottom], axis=1)

    sh1, sc1, g1, sh2, sc2, g2 = mv[0]
    h1 = _norm_mod_fwd("l0_norm1", x, nm[0], sc1, sh1)
    slab = pl.BlockSpec((None, MM_TM, 512), lambda i, j, k: (_slab_of(j), i, 0))
    proj0, g_eout, g_sout = _matmul(
        "l0_in", h1, w_even_in, mode="nn", dims=(s, EVEN_IN_WIDTH, d), tiles=(MM_TM, 512, MM_TK_BIG),
        outs=[((EVEN_IN_WIDTH // 512, s, 512), BF16)], out_specs=[slab], epilogue=_epi_plain,
        ride=gather(("even_w_out", 0), ("ssm_w_out", 0)))
    w_even_out, w_ssm_out = _full_weight("even_w_out", g_eout), _full_weight("ssm_w_out", g_sout)
    o, lse = [None] * 3, [None] * 3
    o[0], lse[0], (g_w1_top,) = _attn_fwd(proj0, 0, ride=gather_half(("ffn_w1", 0), 0))
    o[1], lse[1], (g_w1_bottom,) = _attn_fwd(proj0, 1, ride=gather_half(("ffn_w1", 0), 1))
    o[2], lse[2], (g_sin_top,) = _attn_fwd(proj0, 2, ride=gather_half(("ssm_w_in", 0), 0))
    w1_0 = _full_weight("ffn_w1", joined(g_w1_top, g_w1_bottom))
    cat = _merge_pool_fwd(o, lse, proj0, pool_w, pool_scale)
    x1, y0, _ = _mm_resgate("l0_out", cat, w_even_out, x, g1, EVEN_OUT_WIDTH, EVEN_OUT_WIDTH)
    x2, ffn0, g_w2_0, (g_sin_bottom,) = _ffn_fwd(
        "l0", x1, nf[0], sc2, sh2, g2, w1_0, lambda arrived: _full_weight("ffn_w2", arrived[0]),
        ride1=gather(("ffn_w2", 0)), ride2=gather_half(("ssm_w_in", 0), 1))
    w2_0, w_ssm_in = _full_weight("ffn_w2", g_w2_0[0]), _full_weight("ssm_w_in", joined(g_sin_top, g_sin_bottom))

    th1, tc1, tg1, th2, tc2, tg2 = mv[1]
    h3 = _norm_mod_fwd("l1_norm1", x2, nm[1], tc1, th1)
    proj1, (g_w1_1,) = _mm_plain("l1_in", h3, w_ssm_in, "nn", (s, SSM_IN_PAD, d), F32, tn=1152, tk=MM_TK_BIG,
                                 ride=gather(("ffn_w1", 1)))
    w1_1 = _full_weight("ffn_w1", g_w1_1)
    conv_w = sm["ssm_conv_w"].reshape(SSM_CONV, SSM_CONV_DIM)
    conv_b = sm["ssm_conv_b"].reshape(1, SSM_CONV_DIM)
    act = _conv_fwd(proj1, conv_w, conv_b)
    dt_bias = jnp.pad(sm["ssm_dt_bias"].reshape(1, SSM_HEADS), ((0, 0), (0, 128 - SSM_HEADS)))
    dt_full = _dt_fwd(proj1, dt_bias)
    dtc = jnp.pad(dt_full[:, :SSM_HEADS].reshape(s, SSM_GROUPS, 8).transpose(1, 0, 2), ((0, 0), (0, 0), (0, 120)))
    a_log = jnp.pad(sm["ssm_a_log"].reshape(SSM_GROUPS, 1, 8), ((0, 0), (0, 0), (0, 120)))
    d_skip = jnp.repeat(sm["ssm_d"].reshape(SSM_HEADS), SSM_D_INNER // SSM_HEADS).reshape(1, SSM_D_INNER)
    norm_g = sm["ssm_norm"].reshape(1, SSM_D_INNER)
    (y, yn, h_in_all), (g_w2_1,) = _ssd_fwd(act, proj1, dtc, a_log, d_skip, norm_g, ride=gather(("ffn_w2", 1)))
    w2_1 = _full_weight("ffn_w2", g_w2_1)
    x3, y1, _ = _mm_resgate("l1_out", yn, w_ssm_out, x2, tg1, SSM_D_INNER, MM_TK_BIG)
    x4, ffn1, _, _ = _ffn_fwd("l1", x3, nf[1], tc2, th2, tg2, w1_1, lambda arrived: w2_1)

    dx4, st_loss, dyf1 = _loss_head("loss_head", x4, sm["final_norm"].reshape(1, d), target, (ffn1[3], tg2))
    loss = jnp.sum(st_loss[1])

    dx3, dw1_1, dw2_1, st_n2_1, dy1, _, _, (got[("ffn_w2", 1)],) = _ffn_bwd(
        "l1", dx4, dyf1, x3, ffn1, nf[1], tc2, w1_1, w2_1, below=(y1, tg1),
        ride_dx1_of=lambda dw2: scatter(pieces(("ffn_w2", 1), dw2)))
    dyn, _ = _mm_plain("l1_out_dx", dy1, w_ssm_out, "nt", (s, SSM_D_INNER, d), F32, tk=MM_TK_BIG)
    dw_sout, _ = _mm_plain("l1_out_dw", yn, dy1, "tn", (SSM_D_INNER, d, s), BF16, tk=MM_TK_BIG)
    (dproj1, dxs, dbm, dcm, ddt, st_ssd, d_alog), (got[("ffn_w1", 1)], got[("ssm_w_out", 0)]) = _ssd_bwd(
        act, proj1, dtc, a_log, d_skip, norm_g, y, h_in_all, dyn,
        ride=scatter(dw1_1, pieces(("ssm_w_out", 0), dw_sout)))
    dproj1, st_cx = _conv_bwd("conv_bwd_x", dxs, proj1, conv_w, conv_b, dproj1, block0=0)
    dproj1, st_cb = _conv_bwd("conv_bwd_b", dbm, proj1, conv_w, conv_b, dproj1, block0=SSM_D_INNER // 512)
    dproj1, st_cc = _conv_bwd("conv_bwd_c", dcm, proj1, conv_w, conv_b, dproj1, block0=SSM_D_INNER // 512 + 2)
    ddt_rows = jnp.pad(ddt[:, :, :8].transpose(1, 0, 2).reshape(s, SSM_HEADS), ((0, 0), (0, 128 - SSM_HEADS)))
    dproj1, st_dt = _dt_bwd(ddt_rows, proj1, dt_bias, dproj1)
    dh3, _ = _mm_plain("l1_in_dx", dproj1, w_ssm_in, "nt", (s, d, SSM_IN_PAD), F32, tk=SSM_IN_PAD // 3)
    dw_sin, _ = _mm_plain("l1_in_dw", h3, dproj1, "tn", (d, SSM_IN_PAD, s), BF16, tn=1152, tk=MM_TK_BIG)
    dx2, st_n1_1, dyf0 = _norm_mod_bwd("l1_norm1_bwd", x2, nm[1], tc1, dh3, dx3, (ffn0[3], g2))

    dx1, dw1_0, dw2_0, st_n2_0, dy0, (sin_top,), (sin_bottom,), (got[("ffn_w2", 0)],) = _ffn_bwd(
        "l0", dx2, dyf0, x1, ffn0, nf[0], sc2, w1_0, w2_0, below=(y0, g1),
        ride_dx2=scatter(pieces(("ssm_w_in", 0), dw_sin, 0)), ride_dw2=scatter(pieces(("ssm_w_in", 0), dw_sin, 1)),
        ride_dx1_of=lambda dw2: scatter(pieces(("ffn_w2", 0), dw2)))
    got[("ssm_w_in", 0)] = joined(sin_top, sin_bottom)
    dcat, _ = _mm_plain("l0_out_dx", dy0, w_even_out, "nt", (s, EVEN_OUT_WIDTH, d), F32, tk=MM_TK_BIG)
    dw_eout, _ = _mm_plain("l0_out_dw", cat, dy0, "tn", (EVEN_OUT_WIDTH, d, s), BF16, tk=MM_TK_BIG)
    lane = jnp.arange(512) // 64
    head_sum = (lane[:, None] == lane[None, :]).astype(BF16)
    *do_cc, du, d_pool_w, st_pool = _merge_pool_bwd(dcat, o, lse, proj0, pool_w, pool_scale, head_sum)
    do, cc = do_cc[:3], do_cc[3:]
    dqkv = [None] * 3
    dqkv[0], (w1_top,) = _attn_bwd(proj0, 0, do[0], lse[0], cc[0], ride=scatter(pieces(("ffn_w1", 0), dw1_0, 0)))
    dqkv[1], (w1_bottom,) = _attn_bwd(proj0, 1, do[1], lse[1], cc[1], ride=scatter(pieces(("ffn_w1", 0), dw1_0, 1)))
    dqkv[2], (got[("even_w_out", 0)],) = _attn_bwd(proj0, 2, do[2], lse[2], cc[2],
                                                   ride=scatter(pieces(("even_w_out", 0), dw_eout)))
    got[("ffn_w1", 0)] = joined(w1_top, w1_bottom)
    dproj0 = jnp.stack([dqkv[g][kind] for g in range(3) for kind in range(3)] + [du])
    dw_ein, _ = _mm_plain("l0_in_dw", h1, dproj0, "tn", (d, EVEN_IN_WIDTH, s), BF16, tn=512, tk=MM_TK_BIG,
                          b_spec=pl.BlockSpec((None, MM_TK_BIG, 512), lambda i, j, k: (_slab_of(j), k, 0)))
    half_slabs = EVEN_IN_WIDTH // 512 // 2
    dh1, (got[("even_w_in", 0)],) = _mm_plain(
        "l0_in_dx", dproj0, _slab_ordered_cols("even_w_in_by_slab", w_even_in), "nt", (s, d, EVEN_IN_WIDTH), F32,
        tk=half_slabs * 512, a_slabs=half_slabs,
        a_spec=pl.BlockSpec((half_slabs, MM_TM, 512), lambda i, j, k: (k, i, 0)),
        ride=scatter(pieces(("even_w_in", 0), dw_ein)))
    grad_x, st_n1_0 = _norm_mod_bwd("l0_norm1_bwd", x, nm[0], sc1, dh1, dx1)

    dg1_0, dg2_0 = st_n2_0[GATE_STAT_ROW], st_n1_1[GATE_STAT_ROW]
    dg1_1, dg2_1 = st_n2_1[GATE_STAT_ROW], st_loss[GATE_STAT_ROW]
    dmod = jnp.stack([
        jnp.stack([st_n1_0[0], st_n1_0[1], dg1_0, st_n2_0[0], st_n2_0[1], dg2_0]),
        jnp.stack([st_n1_1[0], st_n1_1[1], dg1_1, st_n2_1[0], st_n2_1[1], dg2_1])])
    st_conv = jnp.concatenate([st_cx, st_cb, st_cc], axis=1)
    small = dict(
        norm_mix=jnp.stack([st_n1_0[2], st_n1_1[2]]), norm_ffn=jnp.stack([st_n2_0[2], st_n2_1[2]]),
        pool_w=d_pool_w, pool_scale=st_pool[0], ssm_conv_w=st_conv[:SSM_CONV], ssm_conv_b=st_conv[SSM_CONV],
        ssm_dt_bias=st_dt[0, :SSM_HEADS], ssm_a_log=d_alog[:, 0, :8].reshape(SSM_HEADS),
        ssm_d=jnp.sum(st_ssd[1].reshape(SSM_HEADS, SSM_D_INNER // SSM_HEADS), axis=-1), ssm_norm=st_ssd[0],
        final_norm=st_loss[0])
    return loss, grad_x, got, dmod, small


WEIGHT_ORDER = ("ada_w", "ada_b", "norm_mix", "norm_ffn", "ffn_w1", "ffn_w2", "even_w_in", "pool_w", "pool_scale",
                "even_w_out", "ssm_w_in", "ssm_conv_w", "ssm_conv_b", "ssm_dt_bias", "ssm_a_log", "ssm_d", "ssm_norm",
                "ssm_w_out", "final_norm")
BIG_LAYERS = ((("even_w_in", 0), ("even_w_out", 0), ("ffn_w1", 0), ("ffn_w2", 0)),
              (("ssm_w_in", 0), ("ssm_w_out", 0), ("ffn_w1", 1), ("ffn_w2", 1)))
STACKED = ("ffn_w1", "ffn_w2")
ASSEMBLED = {"even_w_in": EVEN_IN_WIDTH, "even_w_out": D_MODEL, "ssm_w_in": SSM_IN_PAD}
SMALL_REPLICATED = ("norm_mix", "norm_ffn", "pool_w", "pool_scale", "ssm_dt_bias", "ssm_a_log", "ssm_d", "final_norm")
SMALL_SHARDED = ("ssm_conv_w", "ssm_conv_b", "ssm_norm")


def _pack(flat_parts, width, lead=()):
    flat = jnp.concatenate(flat_parts, axis=-1)
    n = flat.shape[-1]
    rows = -(-n // (8 * width)) * 8
    flat = jnp.pad(flat, [(0, 0)] * len(lead) + [(0, rows * width - n)])
    return flat.reshape(*lead, rows, width)


def _unpack(packed, shapes, lead=()):
    flat = packed.reshape(*lead, -1)
    out, off = [], 0
    for shp in shapes:
        n = math.prod(shp)
        out.append(flat[..., off:off + n].reshape(*lead, *shp))
        off += n
    return out


def kernel(x, c, ada_w, ada_b, norm_mix, norm_ffn, ffn_w1, ffn_w2, even_w_in, pool_w, pool_scale, even_w_out, ssm_w_in, ssm_conv_w, ssm_conv_b, ssm_dt_bias, ssm_a_log, ssm_d, ssm_norm, ssm_w_out, final_norm, loss_target, m_ada_w, m_ada_b, m_norm_mix, m_norm_ffn, m_ffn_w1, m_ffn_w2, m_even_w_in, m_pool_w, m_pool_scale, m_even_w_out, m_ssm_w_in, m_ssm_conv_w, m_ssm_conv_b, m_ssm_dt_bias, m_ssm_a_log, m_ssm_d, m_ssm_norm, m_ssm_w_out, m_final_norm, v_ada_w, v_ada_b, v_norm_mix, v_norm_ffn, v_ffn_w1, v_ffn_w2, v_even_w_in, v_pool_w, v_pool_scale, v_even_w_out, v_ssm_w_in, v_ssm_conv_w, v_ssm_conv_b, v_ssm_dt_bias, v_ssm_a_log, v_ssm_d, v_ssm_norm, v_ssm_w_out, v_final_norm):
    w = dict(ada_w=ada_w, ada_b=ada_b, norm_mix=norm_mix, norm_ffn=norm_ffn, ffn_w1=ffn_w1, ffn_w2=ffn_w2,
             even_w_in=even_w_in, pool_w=pool_w, pool_scale=pool_scale, even_w_out=even_w_out, ssm_w_in=ssm_w_in,
             ssm_conv_w=ssm_conv_w, ssm_conv_b=ssm_conv_b, ssm_dt_bias=ssm_dt_bias, ssm_a_log=ssm_a_log, ssm_d=ssm_d,
             ssm_norm=ssm_norm, ssm_w_out=ssm_w_out, final_norm=final_norm)
    m = dict(ada_w=m_ada_w, ada_b=m_ada_b, norm_mix=m_norm_mix, norm_ffn=m_norm_ffn, ffn_w1=m_ffn_w1, ffn_w2=m_ffn_w2,
             even_w_in=m_even_w_in, pool_w=m_pool_w, pool_scale=m_pool_scale, even_w_out=m_even_w_out,
             ssm_w_in=m_ssm_w_in, ssm_conv_w=m_ssm_conv_w, ssm_conv_b=m_ssm_conv_b, ssm_dt_bias=m_ssm_dt_bias,
             ssm_a_log=m_ssm_a_log, ssm_d=m_ssm_d, ssm_norm=m_ssm_norm, ssm_w_out=m_ssm_w_out, final_norm=m_final_norm)
    v = dict(ada_w=v_ada_w, ada_b=v_ada_b, norm_mix=v_norm_mix, norm_ffn=v_norm_ffn, ffn_w1=v_ffn_w1, ffn_w2=v_ffn_w2,
             even_w_in=v_even_w_in, pool_w=v_pool_w, pool_scale=v_pool_scale, even_w_out=v_even_w_out,
             ssm_w_in=v_ssm_w_in, ssm_conv_w=v_ssm_conv_w, ssm_conv_b=v_ssm_conv_b, ssm_dt_bias=v_ssm_dt_bias,
             ssm_a_log=v_ssm_a_log, ssm_d=v_ssm_d, ssm_norm=v_ssm_norm, ssm_w_out=v_ssm_w_out, final_norm=v_final_norm)
    d = D_MODEL
    me = _my_index()

    sharded_shapes = [(SSM_CONV, SSM_CONV_DIM // N_DEV), (SSM_CONV_DIM // N_DEV,), (SSM_D_INNER // N_DEV,)]
    shards = {(name, idx): w[name][idx].astype(BF16) for layer in BIG_LAYERS for name, idx in layer}
    small_in = _pack([c.reshape(-1)] + [w[k].reshape(-1) for k in SMALL_SHARDED], 128)
    got, got_even_in = _exchange("gather_first", [small_in, shards[("even_w_in", 0)]], scatter=False)
    w_even_in = _full_weight("even_w_in", got_even_in)
    c_all, conv_w_sh, conv_b_sh, norm_sh = _unpack(got, [(d,)] + sharded_shapes, lead=(N_DEV,))
    sm = {k: w[k] for k in SMALL_REPLICATED}
    sm["ssm_conv_w"] = conv_w_sh.transpose(1, 0, 2).reshape(SSM_CONV, SSM_CONV_DIM)
    sm["ssm_conv_b"] = conv_b_sh.reshape(SSM_CONV_DIM)
    sm["ssm_norm"] = norm_sh.reshape(SSM_D_INNER)

    ada_cols = 6 * d // N_DEV
    c_pad = jnp.pad(c_all, ((0, 16 - N_DEV), (0, 0)))
    mod_cols = _matmul(
        "ada_fwd", c_pad, ada_w, mode="nn", dims=(16, 2 * ada_cols, d), tiles=(16, ada_cols // 2, d), a_fn=_silu,
        b_spec=pl.BlockSpec((None, d, ada_cols // 2), lambda i, j, k: (j // 2, k, j % 2)),
        outs=[((16, 2 * ada_cols), F32)], epilogue=_epi_plain)[0]
    mod_got = _all_to_all("ada_exchange", mod_cols[:N_DEV].reshape(N_DEV, 2, ada_cols))
    mod = (mod_got.transpose(1, 0, 2).reshape(2, 6 * d) + ada_b).reshape(2, 6, d)

    loss, grad_x, grad_got, dmod, small = _device_step(x[0], loss_target[0], mod, sm, shards, w_even_in)
    loss = lax.psum(loss, ("x", "y", "c"))

    grads, delta, new_m, new_v = {}, {}, {}, {}

    def update(name, parts, shape=None):
        rows, cols = parts.shape[1:]
        res = _adamw("adamw_" + name, parts, w[name].reshape(rows, cols), m[name].reshape(rows, cols), v[name].reshape(rows, cols))
        return [r.reshape(w[name].shape if shape is None else shape) for r in res]

    rep_shapes = [w[k].shape for k in SMALL_REPLICATED]
    pack_rep = lambda tree: _pack([tree[k].reshape(-1) for k in SMALL_REPLICATED], 128)
    dmod_all, rep_got = _exchange("gather_small_grads", [dmod.reshape(2, 6 * d), pack_rep(small)], scatter=False)

    my_cols = lax.dynamic_slice_in_dim(dmod_all, me * ada_cols, ada_cols, axis=2).reshape(N_DEV, 2 * ada_cols)
    g_ada_w = _matmul(
        "ada_dw", c_pad, jnp.pad(my_cols, ((0, 16 - N_DEV), (0, 0))), mode="tn", dims=(d, 2 * ada_cols, 16),
        tiles=(1024, ada_cols // 2, 16), a_fn=_silu, outs=[((2, d, ada_cols), F32)],
        out_specs=[pl.BlockSpec((None, 1024, ada_cols // 2), lambda i, j, k: (j // 2, i, j % 2))], epilogue=_epi_plain)[0]
    grads["ada_w"], delta["ada_w"], new_m["ada_w"], new_v["ada_w"] = update("ada_w", g_ada_w.reshape(1, 2 * d, ada_cols))
    grads["ada_b"], delta["ada_b"], new_m["ada_b"], new_v["ada_b"] = update("ada_b", dmod_all)

    sh_pieces = [small["ssm_conv_w"].reshape(SSM_CONV, N_DEV, -1).transpose(1, 0, 2).reshape(N_DEV, -1),
                 small["ssm_conv_b"].reshape(N_DEV, -1), small["ssm_norm"].reshape(N_DEV, -1)]
    sh_got = _all_to_all("exchange_small_grads", _pack(sh_pieces, 128, lead=(N_DEV,)))

    stacked = {name: [None, None] for name in STACKED}
    for layer in BIG_LAYERS:
        for name, idx in layer:
            res = _adamw(f"adamw_{name}_{idx}", grad_got[(name, idx)], w[name][idx], m[name][idx], v[name][idx])
            if name in stacked:
                stacked[name][idx] = res
            else:
                grads[name], delta[name], new_m[name], new_v[name] = [r[None] for r in res]
    for name, per_layer in stacked.items():
        grads[name], delta[name], new_m[name], new_v[name] = [jnp.stack([per_layer[0][q], per_layer[1][q]]) for q in range(4)]

    rep_res = _adamw("adamw_small_replicated", rep_got, pack_rep(w), pack_rep(m), pack_rep(v))
    for dst, packed in zip((grads, delta, new_m, new_v), rep_res):
        for k, val in zip(SMALL_REPLICATED, _unpack(packed, rep_shapes)):
            dst[k] = val
    pack_sh = lambda tree: _pack([tree[k].reshape(-1) for k in SMALL_SHARDED], 128)
    sh_res = _adamw("adamw_small_sharded", sh_got, pack_sh(w), pack_sh(m), pack_sh(v))
    for dst, packed in zip((grads, delta, new_m, new_v), sh_res):
        for k, val in zip(SMALL_SHARDED, _unpack(packed, [w[k].shape for k in SMALL_SHARDED])):
            dst[k] = val

    out = [loss, grad_x[None]]
    for tree in (grads, delta, new_m, new_v):
        out.extend(tree[k] for k in WEIGHT_ORDER)
    return tuple(out)
```

```python
import functools
import math

import jax
import jax.numpy as jnp
from jax import lax
from jax.experimental import pallas as pl
from jax.experimental.pallas import tpu as pltpu

F32 = jnp.float32
BF16 = jnp.bfloat16

N_DEV = 8
D_MODEL = 2048
NORM_EPS = 1e-6
ATTN_BLOCK = 128
ATTN_GROUPS = 3
ATTN_GROUP_WIDTH = 512
ATTN_QKV_WIDTH = 3 * ATTN_GROUPS * ATTN_GROUP_WIDTH
POOL_GROUPS = 4
POOL_GROUP_WIDTH = 128
POOL_WIDTH = 512
POOL_HALO = 16
EVEN_IN_WIDTH = ATTN_QKV_WIDTH + POOL_WIDTH
EVEN_OUT_WIDTH = 1024
SSM_D_INNER = 4096
SSM_HEADS = 64
SSM_GROUPS = 8
SSM_GROUP_WIDTH = 512
SSM_STATE = 128
SSM_CHUNK = 128
SSM_CONV = 4
SSM_CONV_DIM = 6144
SSM_IN_WIDTH = 10304
SSM_IN_PAD = 10368
FFN_HIDDEN = 8192

ADAM_LR = 0.001
ADAM_B1 = 0.9
ADAM_B2 = 0.999
ADAM_EPS = 1e-08
ADAM_WD = 0.01
ADAM_STEP = 10

VMEM_LIMIT_BYTES = 56 * 1024 * 1024
NEG_BIG = -1e30

MESH_ID = pl.DeviceIdType.MESH
ANY = pl.BlockSpec(memory_space=pl.ANY)


def _cparams(*sem):
    return pltpu.CompilerParams(dimension_semantics=tuple(sem) if sem else None, vmem_limit_bytes=VMEM_LIMIT_BYTES)


def _dot(a, b):
    return lax.dot_general(a, b, (((1,), (0,)), ((), ())), preferred_element_type=F32)


def _dot_nt(a, b):
    return lax.dot_general(a, b, (((1,), (1,)), ((), ())), preferred_element_type=F32)


def _dot_tn(a, b):
    return lax.dot_general(a, b, (((0,), (0,)), ((), ())), preferred_element_type=F32)


def _split3(v):
    hi = v.astype(BF16)
    r1 = v - hi.astype(F32)
    mid = r1.astype(BF16)
    lo = (r1 - mid.astype(F32)).astype(BF16)
    return hi, mid, lo


def _dot3_left(const_bf16, v):
    hi, mid, lo = _split3(v)
    return _dot(const_bf16, hi) + _dot(const_bf16, mid) + _dot(const_bf16, lo)


def _dot3_right(v, const_bf16):
    hi, mid, lo = _split3(v)
    return _dot(hi, const_bf16) + _dot(mid, const_bf16) + _dot(lo, const_bf16)


def _iota(shape, dim):
    return lax.broadcasted_iota(jnp.int32, shape, dim)


def _sigmoid(x):
    return 1.0 / (1.0 + jnp.exp(-x))


def _peer(k):
    x, y, c = lax.axis_index("x"), lax.axis_index("y"), lax.axis_index("c")
    px = 1 - x if k & 4 else x
    py = 1 - y if k & 2 else y
    pc = 1 - c if k & 1 else c
    return (px, py, pc), 4 * px + 2 * py + pc


def _my_index():
    return 4 * lax.axis_index("x") + 2 * lax.axis_index("y") + lax.axis_index("c")


def _exchange(name, arrays, *, scatter):
    n = len(arrays)

    def body(*refs):
        ex = _Exchange(refs[:n], refs[n:2 * n], *refs[2 * n:], scatter)
        ex.start()
        ex.wait()

    return pl.pallas_call(
        body, name=name, out_shape=_exchange_out_shapes(arrays, scatter), in_specs=[ANY] * n, out_specs=[ANY] * n,
        scratch_shapes=_exchange_sems(n),
    )(*arrays)


def _exchange_out_shapes(arrays, scatter):
    return [jax.ShapeDtypeStruct((N_DEV,) + (tuple(a.shape[1:]) if scatter else tuple(a.shape)), a.dtype) for a in arrays]


def _exchange_sems(n):
    return [pltpu.SemaphoreType.DMA((n * (N_DEV - 1),)), pltpu.SemaphoreType.DMA((n * (N_DEV - 1),)),
            pltpu.SemaphoreType.DMA((n,))]


class _Exchange:
    def __init__(self, x_refs, out_refs, send_sems, recv_sems, local_sems, scatter):
        self.x_refs, self.out_refs, self.scatter = x_refs, out_refs, scatter
        self.send_sems, self.recv_sems, self.local_sems = send_sems, recv_sems, local_sems

    def _src(self, a, idx):
        return self.x_refs[a].at[idx] if self.scatter else self.x_refs[a]

    def _local(self, a):
        me = _my_index()
        return pltpu.make_async_copy(self._src(a, me), self.out_refs[a].at[me], self.local_sems.at[a])

    def _remote(self, a, k, landing):
        peer, peer_idx = _peer(k)
        sem = a * (N_DEV - 1) + k - 1
        slot = peer_idx if landing else _my_index()
        return pltpu.make_async_remote_copy(
            src_ref=self._src(a, peer_idx), dst_ref=self.out_refs[a].at[slot], send_sem=self.send_sems.at[sem],
            recv_sem=self.recv_sems.at[sem], device_id=peer, device_id_type=MESH_ID)

    def start(self):
        n = len(self.x_refs)
        for a in range(n):
            self._local(a).start()
        for k in range(1, N_DEV):
            for a in range(n):
                self._remote(a, k, False).start()

    def wait(self):
        n = len(self.x_refs)
        for k in range(1, N_DEV):
            for a in range(n):
                self._remote(a, k, True).wait_recv()
        for k in range(1, N_DEV):
            for a in range(n):
                self._remote(a, k, False).wait_send()
        for a in range(n):
            self._local(a).wait()


def _pcall(name, body, *, grid, in_specs, out_specs, out_shape, scratch_shapes, sem, args, ride=None, aliases=None):
    aliases = aliases or {}
    if ride is None:
        res = pl.pallas_call(body, name=name, grid=grid, in_specs=in_specs, out_specs=out_specs, out_shape=out_shape,
                             scratch_shapes=scratch_shapes, input_output_aliases=aliases, compiler_params=_cparams(*sem))(*args)
        return list(res), []
    arrays, scatter = ride
    n_in, n_out, n_scr, n_ride = len(in_specs), len(out_specs), len(scratch_shapes), len(arrays)

    def wrapped(*refs):
        ins, refs = refs[:n_in], refs[n_in:]
        ride_in, refs = refs[:n_ride], refs[n_ride:]
        outs, refs = refs[:n_out], refs[n_out:]
        ride_out, refs = refs[:n_ride], refs[n_ride:]
        scr, sems = refs[:n_scr], refs[n_scr:]
        exchange = _Exchange(ride_in, ride_out, *sems, scatter)
        ids = [pl.program_id(axis) for axis in range(len(grid))]
        first, last = ids[0] == 0, ids[0] == grid[0] - 1
        for axis in range(1, len(grid)):
            first, last = first & (ids[axis] == 0), last & (ids[axis] == grid[axis] - 1)

        @pl.when(first)
        def _():
            exchange.start()

        body(*ins, *outs, *scr)

        @pl.when(last)
        def _():
            exchange.wait()

    res = pl.pallas_call(
        wrapped, name=name, grid=grid, in_specs=list(in_specs) + [ANY] * n_ride, out_specs=list(out_specs) + [ANY] * n_ride,
        out_shape=list(out_shape) + _exchange_out_shapes(arrays, scatter),
        scratch_shapes=list(scratch_shapes) + _exchange_sems(n_ride), input_output_aliases=aliases,
        compiler_params=_cparams(*(("arbitrary",) * len(grid))))(*args, *arrays)
    return list(res[:n_out]), list(res[n_out:])


def _all_gather(name, x):
    return _exchange(name, [x], scatter=False)[0]


def _all_to_all(name, x):
    return _exchange(name, [x], scatter=True)[0]


def _assemble_cols(name, shards, width):
    _, k_dim, ns = shards.shape
    tr = 256

    def body(s_ref, o_ref):
        for dev in range(N_DEV):
            o_ref[:, ns * dev:ns * (dev + 1)] = s_ref[dev]
        if width > N_DEV * ns:
            o_ref[:, N_DEV * ns:] = jnp.zeros((tr, width - N_DEV * ns), o_ref.dtype)

    return pl.pallas_call(
        body, name=name, grid=(k_dim // tr,), in_specs=[pl.BlockSpec((N_DEV, tr, ns), lambda i: (0, i, 0))],
        out_specs=pl.BlockSpec((tr, width), lambda i: (i, 0)), out_shape=jax.ShapeDtypeStruct((k_dim, width), shards.dtype),
        compiler_params=_cparams("parallel"),
    )(shards)


def _split_cols(name, full, ns, row0=0, n_rows=None):
    k_dim, width = full.shape
    n_rows = k_dim if n_rows is None else n_rows
    tr = 256
    first = row0 // tr

    def body(f_ref, o_ref):
        for dev in range(N_DEV):
            o_ref[dev] = f_ref[:, ns * dev:ns * (dev + 1)]

    return pl.pallas_call(
        body, name=name, grid=(n_rows // tr,), in_specs=[pl.BlockSpec((tr, width), lambda i: (first + i, 0))],
        out_specs=pl.BlockSpec((N_DEV, tr, ns), lambda i: (0, i, 0)),
        out_shape=jax.ShapeDtypeStruct((N_DEV, n_rows, ns), full.dtype), compiler_params=_cparams("parallel"),
    )(full)


_DIMS = {"nn": (((1,), (0,)), ((), ())), "nt": (((1,), (1,)), ((), ())), "tn": (((0,), (0,)), ((), ()))}


def _matmul(name, a, b, *, mode, dims, tiles, outs, epilogue, a_spec=None, b_spec=None, out_specs=None,
            extras=(), extra_specs=(), a_fn=None, ride=None, a_slabs=0, b_slabs=0):
    m_dim, n_dim, k_dim = dims
    tm, tn, tk = tiles
    assert m_dim % tm == 0 and n_dim % tn == 0 and k_dim % tk == 0, (name, dims, tiles)
    grid = (m_dim // tm, n_dim // tn, k_dim // tk)
    nk = grid[2]
    if a_spec is None:
        a_spec = pl.BlockSpec((tk, tm), lambda i, j, k: (k, i)) if mode == "tn" else pl.BlockSpec((tm, tk), lambda i, j, k: (i, k))
    if b_spec is None:
        b_spec = pl.BlockSpec((tn, tk), lambda i, j, k: (j, k)) if mode == "nt" else pl.BlockSpec((tk, tn), lambda i, j, k: (k, j))
    if out_specs is None:
        out_specs = [pl.BlockSpec((tm, tn), lambda i, j, k: (i, j)) for _ in outs]
    n_ex, n_out = len(extras), len(outs)
    ride_arrays, scatter = ride if ride is not None else ((), False)
    n_ride = len(ride_arrays)
    dn = _DIMS[mode]

    def body(a_ref, b_ref, *rest):
        ex_refs, rest = rest[:n_ex], rest[n_ex:]
        ride_in, rest = rest[:n_ride], rest[n_ride:]
        out_refs, rest = rest[:n_out], rest[n_out:]
        ride_out, rest = rest[:n_ride], rest[n_ride:]
        i, j, k = pl.program_id(0), pl.program_id(1), pl.program_id(2)
        if n_ride:
            exchange = _Exchange(ride_in, ride_out, *rest[-3:], scatter)

            @pl.when((i == 0) & (j == 0) & (k == 0))
            def _():
                exchange.start()

        at = a_ref[...]
        if a_fn is not None:
            at = a_fn(at)
        at = at.astype(BF16)
        if a_slabs:
            width = tk // a_slabs
            bt = b_ref[...].astype(BF16)
            part = lax.dot_general(at[0], bt[:, :width], dn, preferred_element_type=F32)
            for slab_i in range(1, a_slabs):
                part += lax.dot_general(at[slab_i], bt[:, slab_i * width:(slab_i + 1) * width], dn,
                                        preferred_element_type=F32)
        elif b_slabs:
            width = tk // b_slabs
            part = lax.dot_general(at[:, :width], b_ref[0].astype(BF16), dn, preferred_element_type=F32)
            for slab_i in range(1, b_slabs):
                part += lax.dot_general(at[:, slab_i * width:(slab_i + 1) * width], b_ref[slab_i].astype(BF16), dn,
                                        preferred_element_type=F32)
        else:
            part = lax.dot_general(at, b_ref[...].astype(BF16), dn, preferred_element_type=F32)

        def finish(total):
            res = epilogue(total, *[e[...] for e in ex_refs])
            for r, o in zip(res, out_refs):
                o[...] = r.astype(o.dtype)

        if nk == 1:
            finish(part)
        else:
            acc = rest[0]

            @pl.when(k == 0)
            def _():
                acc[...] = part

            @pl.when(k > 0)
            def _():
                acc[...] += part

            @pl.when(k == nk - 1)
            def _():
                finish(acc[...])

        if n_ride:
            @pl.when((i == grid[0] - 1) & (j == grid[1] - 1) & (k == nk - 1))
            def _():
                exchange.wait()

    scratch = ([pltpu.VMEM((tm, tn), F32)] if nk > 1 else []) + (_exchange_sems(n_ride) if n_ride else [])
    sem = ("arbitrary",) * 3 if n_ride else ("parallel", "parallel", "arbitrary")
    return pl.pallas_call(
        body, name=name, grid=grid,
        in_specs=[a_spec, b_spec, *extra_specs] + [ANY] * n_ride, out_specs=list(out_specs) + [ANY] * n_ride,
        out_shape=[jax.ShapeDtypeStruct(s, d) for s, d in outs] + (_exchange_out_shapes(ride_arrays, scatter) if n_ride else []),
        scratch_shapes=scratch, compiler_params=_cparams(*sem),
    )(a, b, *extras, *ride_arrays)


def _epi_plain(acc):
    return (acc,)


def _epi_relu2(acc):
    return jnp.square(jnp.maximum(acc, 0.0)), acc


def _epi_resgate(acc, res, gate):
    return res + gate * acc, acc


def _epi_drelu2(acc, pre):
    return (acc * (2.0 * jnp.maximum(pre.astype(F32), 0.0)),)


def _silu(v):
    return v * _sigmoid(v)


ROW_TILE = 256


def _row_spec(width, tr=ROW_TILE):
    return pl.BlockSpec((tr, width), lambda i: (i, 0))


def _vec_spec(width):
    return pl.BlockSpec((1, width), lambda i: (0, 0))


def _stat_spec(width):
    return pl.BlockSpec((8, width), lambda i: (0, 0))


def _norm_mod_fwd(name, x, gain, scale, shift):
    s, d = x.shape

    def body(x_ref, g_ref, sc_ref, sh_ref, h_ref):
        xv = x_ref[...]
        r = lax.rsqrt(jnp.mean(xv * xv, axis=-1, keepdims=True) + NORM_EPS)
        h_ref[...] = ((xv * r * g_ref[...]) * (1.0 + sc_ref[...]) + sh_ref[...]).astype(BF16)

    return pl.pallas_call(
        body, name=name, grid=(s // ROW_TILE,),
        in_specs=[_row_spec(d), _vec_spec(d), _vec_spec(d), _vec_spec(d)], out_specs=_row_spec(d),
        out_shape=jax.ShapeDtypeStruct((s, d), BF16), compiler_params=_cparams("parallel"),
    )(x, gain, scale, shift)


GATE_STAT_ROW = 3


def _gate_below(dx, y_ref, gate_ref, dy_ref, st_ref):
    dy_ref[...] = (dx * gate_ref[...]).astype(BF16)
    st_ref[GATE_STAT_ROW:GATE_STAT_ROW + 1, :] += jnp.sum(dx * y_ref[...].astype(F32), axis=0, keepdims=True)


def _norm_mod_bwd(name, x, gain, scale, dh, dres, below=None, ride=None):
    s, d = x.shape

    def body(x_ref, g_ref, sc_ref, dh_ref, dres_ref, *rest):
        dx_ref, st_ref = rest[-2:] if below is None else (rest[2], rest[3])

        @pl.when(pl.program_id(0) == 0)
        def _():
            st_ref[...] = jnp.zeros_like(st_ref)

        xv = x_ref[...]
        dhv = dh_ref[...].astype(F32)
        r = lax.rsqrt(jnp.mean(xv * xv, axis=-1, keepdims=True) + NORM_EPS)
        xh = xv * r
        n = xh * g_ref[...]
        dn = dhv * (1.0 + sc_ref[...])
        dxh = dn * g_ref[...]
        dx = dres_ref[...] + r * (dxh - xh * jnp.mean(dxh * xh, axis=-1, keepdims=True))
        dx_ref[...] = dx
        st_ref[0:1, :] += jnp.sum(dhv, axis=0, keepdims=True)
        st_ref[1:2, :] += jnp.sum(dhv * n, axis=0, keepdims=True)
        st_ref[2:3, :] += jnp.sum(dn * xh, axis=0, keepdims=True)
        if below is not None:
            _gate_below(dx, rest[0], rest[1], rest[4], st_ref)

    extra_in = [] if below is None else [_row_spec(d), _vec_spec(d)]
    extra_out = [] if below is None else [(_row_spec(d), jax.ShapeDtypeStruct((s, d), BF16))]
    res, got = _pcall(
        name, body, grid=(s // ROW_TILE,),
        in_specs=[_row_spec(d), _vec_spec(d), _vec_spec(d), _row_spec(d), _row_spec(d)] + extra_in,
        out_specs=[_row_spec(d), _stat_spec(d)] + [spec for spec, _ in extra_out],
        out_shape=[jax.ShapeDtypeStruct((s, d), F32), jax.ShapeDtypeStruct((8, d), F32)] + [shp for _, shp in extra_out],
        scratch_shapes=[], sem=("arbitrary",), args=(x, gain, scale, dh, dres, *(below or ())), ride=ride)
    return res if ride is None else (res, got)


def _loss_head(name, x, gain, target, below):
    s, d = x.shape

    def body(x_ref, g_ref, t_ref, y_ref, gate_ref, dx_ref, st_ref, dy_ref):
        @pl.when(pl.program_id(0) == 0)
        def _():
            st_ref[...] = jnp.zeros_like(st_ref)

        xv = x_ref[...]
        r = lax.rsqrt(jnp.mean(xv * xv, axis=-1, keepdims=True) + NORM_EPS)
        xh = xv * r
        err = xh * g_ref[...] - t_ref[...]
        dyf = err * (1.0 / d)
        dxh = dyf * g_ref[...]
        dx = r * (dxh - xh * jnp.mean(dxh * xh, axis=-1, keepdims=True))
        dx_ref[...] = dx
        st_ref[0:1, :] += jnp.sum(dyf * xh, axis=0, keepdims=True)
        st_ref[1:2, :] += jnp.sum(err * err, axis=0, keepdims=True) * (0.5 / d)
        _gate_below(dx, y_ref, gate_ref, dy_ref, st_ref)

    return pl.pallas_call(
        body, name=name, grid=(s // ROW_TILE,),
        in_specs=[_row_spec(d), _vec_spec(d), _row_spec(d), _row_spec(d), _vec_spec(d)],
        out_specs=[_row_spec(d), _stat_spec(d), _row_spec(d)],
        out_shape=[jax.ShapeDtypeStruct((s, d), F32), jax.ShapeDtypeStruct((8, d), F32), jax.ShapeDtypeStruct((s, d), BF16)],
        compiler_params=_cparams("arbitrary"),
    )(x, gain, target, *below)


def _adamw(name, parts, w, m, v):
    n_parts, rows, cols = parts.shape
    tr = rows
    for cand in (512, 256, 128, 64, 32, 16, 8):
        if rows % cand == 0 and cand * cols * 4 <= 2 * 1024 * 1024:
            tr = cand
            break
    c1 = 1.0 - ADAM_B1 ** ADAM_STEP
    c2 = 1.0 - ADAM_B2 ** ADAM_STEP

    def body(p_ref, w_ref, m_ref, v_ref, g_out, d_out, m_out, v_out):
        g = p_ref[0].astype(F32)
        for i in range(1, n_parts):
            g = g + p_ref[i].astype(F32)
        m_new = ADAM_B1 * m_ref[...] + (1.0 - ADAM_B1) * g
        v_new = ADAM_B2 * v_ref[...] + (1.0 - ADAM_B2) * (g * g)
        g_out[...] = g
        m_out[...] = m_new
        v_out[...] = v_new
        d_out[...] = -ADAM_LR * ((m_new / c1) / (jnp.sqrt(v_new / c2) + ADAM_EPS) + ADAM_WD * w_ref[...])

    spec = pl.BlockSpec((tr, cols), lambda i: (i, 0))
    return pl.pallas_call(
        body, name=name, grid=(rows // tr,),
        in_specs=[pl.BlockSpec((n_parts, tr, cols), lambda i: (0, i, 0)), spec, spec, spec],
        out_specs=[spec, spec, spec, spec],
        out_shape=[jax.ShapeDtypeStruct((rows, cols), F32)] * 4, compiler_params=_cparams("parallel"),
    )(parts, w, m, v)


ATTN_DILATIONS = (1, 4, 16)


def _attn_fwd(proj, g, ride=None):
    s = proj.shape[1]
    dil = ATTN_DILATIONS[g]
    rows, nb = s // dil, s // dil // ATTN_BLOCK

    def body(q_ref, kp_ref, kc_ref, vp_ref, vc_ref, o_ref, l_ref):
        first = pl.program_id(1) == 0
        qi, kj = _iota((128, 128), 0), _iota((128, 128), 1)
        mask_c = kj <= qi
        mask_p = jnp.logical_and(kj >= qi, jnp.logical_not(first))
        low = kj < 64
        for p in range(4):
            sl = slice(128 * p, 128 * p + 128)
            q, kp, kc, vp, vc = q_ref[:, sl], kp_ref[:, sl], kc_ref[:, sl], vp_ref[:, sl], vc_ref[:, sl]
            o_pair = jnp.zeros((128, 128), F32)
            l_pair = jnp.zeros((128, 128), F32)
            for half in range(2):
                hm = low if half == 0 else jnp.logical_not(low)
                qm = jnp.where(hm, q, jnp.zeros_like(q))
                sc = jnp.where(mask_c, _dot_nt(qm, kc) * 0.125, NEG_BIG)
                sp = jnp.where(mask_p, _dot_nt(qm, kp) * 0.125, NEG_BIG)
                m = jnp.maximum(jnp.max(sc, axis=-1, keepdims=True), jnp.max(sp, axis=-1, keepdims=True))
                pc, pp = jnp.exp(sc - m), jnp.exp(sp - m)
                den = jnp.sum(pc, axis=-1, keepdims=True) + jnp.sum(pp, axis=-1, keepdims=True)
                oh = _dot((pc / den).astype(BF16), vc) + _dot((pp / den).astype(BF16), vp)
                o_pair = jnp.where(hm, oh, o_pair)
                l_pair = jnp.where(hm, m + jnp.log(den), l_pair)
            o_ref[:, sl] = o_pair
            l_ref[:, sl] = l_pair

    view = proj[3 * g:3 * g + 3].reshape(3, rows, dil * ATTN_GROUP_WIDTH)
    blk = (None, ATTN_BLOCK, ATTN_GROUP_WIDTH)
    prev = lambda j: jnp.maximum(j - 1, 0)
    out_blk = pl.BlockSpec((ATTN_BLOCK, ATTN_GROUP_WIDTH), lambda r, j: (j, r))
    (o, lse), got = _pcall(
        f"attn_fwd_g{g}", body, grid=(dil, nb),
        in_specs=[pl.BlockSpec(blk, lambda r, j: (0, j, r)),
                  pl.BlockSpec(blk, lambda r, j: (1, prev(j), r)), pl.BlockSpec(blk, lambda r, j: (1, j, r)),
                  pl.BlockSpec(blk, lambda r, j: (2, prev(j), r)), pl.BlockSpec(blk, lambda r, j: (2, j, r))],
        out_specs=[out_blk] * 2, out_shape=[jax.ShapeDtypeStruct((rows, dil * ATTN_GROUP_WIDTH), F32)] * 2,
        scratch_shapes=[], sem=("parallel", "parallel"), args=(view,) * 5, ride=ride)
    return o.reshape(s, ATTN_GROUP_WIDTH), lse.reshape(s, ATTN_GROUP_WIDTH), got


def _attn_bwd(proj, g, do, lse, cc, ride=None):
    s = proj.shape[1]
    dil = ATTN_DILATIONS[g]
    rows, nblk = s // dil, s // dil // ATTN_BLOCK

    def body(q_ref, kp_ref, kc_ref, vp_ref, vc_ref, do_ref, l_ref, c_ref, dq_ref, dk_ref, dv_ref, ck, cv):
        j = pl.program_id(1)
        valid = j < nblk
        first = jnp.minimum(j, nblk - 1) == 0

        @pl.when(j == 0)
        def _():
            ck[...] = jnp.zeros_like(ck)
            cv[...] = jnp.zeros_like(cv)

        qi, kj = _iota((128, 128), 0), _iota((128, 128), 1)
        mask_c = jnp.logical_and(kj <= qi, valid)
        mask_p = jnp.logical_and(jnp.logical_and(kj >= qi, jnp.logical_not(first)), valid)
        low = kj < 64
        for p in range(4):
            sl = slice(128 * p, 128 * p + 128)
            q, kp, kc, vp, vc, dov = q_ref[:, sl], kp_ref[:, sl], kc_ref[:, sl], vp_ref[:, sl], vc_ref[:, sl], do_ref[:, sl]
            lse_pair, c_pair = l_ref[:, sl], c_ref[:, sl]
            dq_pair = jnp.zeros((128, 128), F32)
            dkc = jnp.zeros((128, 128), F32)
            dkp = jnp.zeros((128, 128), F32)
            dvc = jnp.zeros((128, 128), F32)
            dvp = jnp.zeros((128, 128), F32)
            for half in range(2):
                hm = low if half == 0 else jnp.logical_not(low)
                col = slice(64 * half, 64 * half + 1)
                lse_h, c_h = lse_pair[:, col], c_pair[:, col]
                qm = jnp.where(hm, q, jnp.zeros_like(q))
                dom = jnp.where(hm, dov, jnp.zeros_like(dov))
                pc = jnp.exp(jnp.where(mask_c, _dot_nt(qm, kc) * 0.125, NEG_BIG) - lse_h)
                pp = jnp.exp(jnp.where(mask_p, _dot_nt(qm, kp) * 0.125, NEG_BIG) - lse_h)
                dsc = (pc * (_dot_nt(dom, vc) + c_h) * 0.125).astype(BF16)
                dsp = (pp * (_dot_nt(dom, vp) + c_h) * 0.125).astype(BF16)
                dq_pair = jnp.where(hm, _dot(dsc, kc) + _dot(dsp, kp), dq_pair)
                dkc += _dot_tn(dsc, qm)
                dkp += _dot_tn(dsp, qm)
                dvc += _dot_tn(pc.astype(BF16), dom)
                dvp += _dot_tn(pp.astype(BF16), dom)

            @pl.when(valid)
            def _():
                dq_ref[:, sl] = dq_pair.astype(BF16)

            dk_ref[:, sl] = (ck[:, sl] + dkp).astype(BF16)
            dv_ref[:, sl] = (cv[:, sl] + dvp).astype(BF16)
            ck[:, sl] = dkc
            cv[:, sl] = dvc

    wide = dil * ATTN_GROUP_WIDTH
    view = proj[3 * g:3 * g + 3].reshape(3, rows, wide)
    blk = (None, ATTN_BLOCK, ATTN_GROUP_WIDTH)
    flat = (ATTN_BLOCK, ATTN_GROUP_WIDTH)
    cur = lambda j: jnp.minimum(j, nblk - 1)
    prev = lambda j: jnp.maximum(jnp.minimum(j, nblk - 1) - 1, 0)
    out_prev = lambda j: jnp.maximum(j - 1, 0)
    (dq, dk, dv), got = _pcall(
        f"attn_bwd_g{g}", body, grid=(dil, nblk + 1),
        in_specs=[pl.BlockSpec(blk, lambda r, j: (0, cur(j), r)),
                  pl.BlockSpec(blk, lambda r, j: (1, prev(j), r)), pl.BlockSpec(blk, lambda r, j: (1, cur(j), r)),
                  pl.BlockSpec(blk, lambda r, j: (2, prev(j), r)), pl.BlockSpec(blk, lambda r, j: (2, cur(j), r)),
                  pl.BlockSpec(flat, lambda r, j: (cur(j), r)), pl.BlockSpec(flat, lambda r, j: (cur(j), r)),
                  pl.BlockSpec(flat, lambda r, j: (cur(j), r))],
        out_specs=[pl.BlockSpec(flat, lambda r, j: (cur(j), r)),
                   pl.BlockSpec(flat, lambda r, j: (out_prev(j), r)), pl.BlockSpec(flat, lambda r, j: (out_prev(j), r))],
        out_shape=[jax.ShapeDtypeStruct((rows, wide), BF16)] * 3,
        scratch_shapes=[pltpu.VMEM((ATTN_BLOCK, ATTN_GROUP_WIDTH), F32)] * 2, sem=("parallel", "arbitrary"),
        args=(view,) * 5 + (do.reshape(rows, wide), lse.reshape(rows, wide), cc.reshape(rows, wide)), ride=ride)
    return [t.reshape(s, ATTN_GROUP_WIDTH) for t in (dq, dk, dv)], got


MP_TILE = 512


def _merge_weights(l_refs):
    l0, l1, l2 = l_refs[0][...], l_refs[1][...], l_refs[2][...]
    m = jnp.maximum(jnp.maximum(l0, l1), l2)
    e0, e1, e2 = jnp.exp(l0 - m), jnp.exp(l1 - m), jnp.exp(l2 - m)
    den = e0 + e1 + e2
    return e0 / den, e1 / den, e2 / den


def _pool_diff(ucat, gi, tok):
    window = 2 << gi
    ug = ucat[:, 128 * gi:128 * gi + 128]
    acc, shift = ug, 1
    while shift < window:
        acc = acc + pltpu.roll(acc, shift, 0)
        shift *= 2
    cnt = jnp.minimum(tok + 1, window).astype(F32)
    return acc[POOL_HALO:, :] / cnt - ug[POOL_HALO:, :]


def _merge_pool_fwd(o, lse, proj, pool_w, pool_scale):
    s = o[0].shape[0]
    tr = MP_TILE

    def body(o0_ref, o1_ref, o2_ref, l0_ref, l1_ref, l2_ref, u_ref, uh_ref, pw_ref, ps_ref, cat_ref):
        i = pl.program_id(0)
        w0, w1, w2 = _merge_weights((l0_ref, l1_ref, l2_ref))
        cat_ref[:, 0:512] = (w0 * o0_ref[...] + w1 * o1_ref[...] + w2 * o2_ref[...]).astype(BF16)
        halo = jnp.where(i > 0, uh_ref[...].astype(F32), 0.0)
        ucat = jnp.concatenate([halo, u_ref[...].astype(F32)], axis=0)
        tok = i * tr + _iota((tr, 1), 0)
        for gi in range(POOL_GROUPS):
            sl = slice(128 * gi, 128 * gi + 128)
            diff = _pool_diff(ucat, gi, tok)
            yg = _dot(diff.astype(BF16), pw_ref[gi].astype(BF16)) * ps_ref[:, sl]
            cat_ref[:, 512 + 128 * gi:640 + 128 * gi] = yg.astype(BF16)

    return pl.pallas_call(
        body, name="merge_pool_fwd", grid=(s // tr,),
        in_specs=[pl.BlockSpec((tr, 512), lambda i: (i, 0))] * 6 + [
                  pl.BlockSpec((None, tr, 512), lambda i: (9, i, 0)),
                  pl.BlockSpec((None, POOL_HALO, 512), lambda i: (9, jnp.maximum(i * (tr // POOL_HALO) - 1, 0), 0)),
                  pl.BlockSpec((4, 128, 128), lambda i: (0, 0, 0)), pl.BlockSpec((1, 512), lambda i: (0, 0))],
        out_specs=pl.BlockSpec((tr, 1024), lambda i: (i, 0)),
        out_shape=jax.ShapeDtypeStruct((s, EVEN_OUT_WIDTH), BF16), compiler_params=_cparams("parallel"),
    )(*o, *lse, proj, proj, pool_w, pool_scale)


def _merge_pool_bwd(dcat, o, lse, proj, pool_w, pool_scale, head_sum):
    s = o[0].shape[0]
    tr = MP_TILE
    n_tiles = s // tr

    def body(da_ref, dp_ref, dph_ref, o0_ref, o1_ref, o2_ref, l0_ref, l1_ref, l2_ref, u_ref, uh_ref, pw_ref, ps_ref,
             hs_ref, do0_ref, do1_ref, do2_ref, cc0_ref, cc1_ref, cc2_ref, du_ref, dpw_ref, st_ref):
        i = pl.program_id(0)

        @pl.when(i == 0)
        def _():
            dpw_ref[...] = jnp.zeros_like(dpw_ref)
            st_ref[...] = jnp.zeros_like(st_ref)

        ws = _merge_weights((l0_ref, l1_ref, l2_ref))
        da = da_ref[...]
        attn = ws[0] * o0_ref[...] + ws[1] * o1_ref[...] + ws[2] * o2_ref[...]
        per_head = _dot3_right(da * attn, hs_ref[...])
        for wg, do_ref, cc_ref in zip(ws, (do0_ref, do1_ref, do2_ref), (cc0_ref, cc1_ref, cc2_ref)):
            do_ref[...] = (wg * da).astype(BF16)
            cc_ref[...] = -wg * per_head

        halo = jnp.where(i > 0, uh_ref[...].astype(F32), 0.0)
        ucat = jnp.concatenate([halo, u_ref[...].astype(F32)], axis=0)
        tok = i * tr + _iota((tr, 1), 0)
        dyp = dp_ref[...]
        dnext = jnp.where(i < n_tiles - 1, dph_ref[...], 0.0)
        dyp_ext = jnp.concatenate([dyp, dnext], axis=0)
        tok_ext = i * tr + _iota((tr + POOL_HALO, 1), 0)
        for gi in range(POOL_GROUPS):
            sl = slice(128 * gi, 128 * gi + 128)
            window = 2 << gi
            pw16 = pw_ref[gi].astype(BF16)
            d16 = _pool_diff(ucat, gi, tok).astype(BF16)
            st_ref[0:1, sl] += jnp.sum(dyp[:, sl] * _dot(d16, pw16), axis=0, keepdims=True)
            dpw_ref[gi] += _dot_tn(d16, (dyp[:, sl] * ps_ref[:, sl]).astype(BF16))
            dd = _dot_nt((dyp_ext[:, sl] * ps_ref[:, sl]).astype(BF16), pw16)
            acc = dd / jnp.minimum(tok_ext + 1, window).astype(F32)
            shift = 1
            while shift < window:
                acc = acc + pltpu.roll(acc, tr + POOL_HALO - shift, 0)
                shift *= 2
            du_ref[:, sl] = (acc[:tr, :] - dd[:tr, :]).astype(BF16)

    halo_blocks = tr // POOL_HALO
    return pl.pallas_call(
        body, name="merge_pool_bwd", grid=(n_tiles,),
        in_specs=[pl.BlockSpec((tr, 512), lambda i: (i, 0)), pl.BlockSpec((tr, 512), lambda i: (i, 1)),
                  pl.BlockSpec((POOL_HALO, 512), lambda i: (jnp.minimum((i + 1) * halo_blocks, s // POOL_HALO - 1), 1))]
                 + [pl.BlockSpec((tr, 512), lambda i: (i, 0))] * 6 + [
                  pl.BlockSpec((None, tr, 512), lambda i: (9, i, 0)),
                  pl.BlockSpec((None, POOL_HALO, 512), lambda i: (9, jnp.maximum(i * halo_blocks - 1, 0), 0)),
                  pl.BlockSpec((4, 128, 128), lambda i: (0, 0, 0)), pl.BlockSpec((1, 512), lambda i: (0, 0)),
                  pl.BlockSpec((512, 512), lambda i: (0, 0))],
        out_specs=[pl.BlockSpec((tr, 512), lambda i: (i, 0))] * 7 + [
                   pl.BlockSpec((4, 128, 128), lambda i: (0, 0, 0)), pl.BlockSpec((8, 512), lambda i: (0, 0))],
        out_shape=[jax.ShapeDtypeStruct((s, 512), BF16)] * 3 + [jax.ShapeDtypeStruct((s, 512), F32)] * 3 + [
                   jax.ShapeDtypeStruct((s, 512), BF16), jax.ShapeDtypeStruct((4, 128, 128), F32),
                   jax.ShapeDtypeStruct((8, 512), F32)],
        compiler_params=_cparams("arbitrary"),
    )(dcat, dcat, dcat, *o, *lse, proj, proj, pool_w, pool_scale, head_sum)


CONV_TILE = 1024
CONV_HALO = 8
XBC_BLOCK0 = SSM_D_INNER // 512
DT_BLOCK = (SSM_D_INNER + SSM_CONV_DIM) // 128


def _conv_taps(xcat, w, bias):
    pre = bias + w[3:4, :] * xcat[CONV_HALO:, :]
    for back in range(1, SSM_CONV):
        pre = pre + w[3 - back:4 - back, :] * pltpu.roll(xcat, back, 0)[CONV_HALO:, :]
    return pre


def _conv_fwd(proj, conv_w, conv_b):
    s = proj.shape[0]
    tr = CONV_TILE

    def body(x_ref, xh_ref, w_ref, b_ref, act_ref):
        i = pl.program_id(0)
        xcat = jnp.concatenate([jnp.where(i > 0, xh_ref[...], 0.0), x_ref[...]], axis=0)
        pre = _conv_taps(xcat, w_ref[...], b_ref[...])
        act_ref[...] = pre * _sigmoid(pre)

    hb = tr // CONV_HALO
    return pl.pallas_call(
        body, name="conv_fwd", grid=(s // tr, SSM_CONV_DIM // 512),
        in_specs=[pl.BlockSpec((tr, 512), lambda i, j: (i, XBC_BLOCK0 + j)),
                  pl.BlockSpec((CONV_HALO, 512), lambda i, j: (jnp.maximum(i * hb - 1, 0), XBC_BLOCK0 + j)),
                  pl.BlockSpec((SSM_CONV, 512), lambda i, j: (0, j)), pl.BlockSpec((1, 512), lambda i, j: (0, j))],
        out_specs=pl.BlockSpec((tr, 512), lambda i, j: (i, j)),
        out_shape=jax.ShapeDtypeStruct((s, SSM_CONV_DIM), F32), compiler_params=_cparams("parallel", "parallel"),
    )(proj, proj, conv_w, conv_b)


def _conv_bwd(name, dact, proj, conv_w, conv_b, dproj, *, block0):
    s, width = dact.shape
    tr = CONV_TILE
    n_tiles = s // tr
    hb = tr // CONV_HALO

    def body(da_ref, dah_ref, x_ref, xh_ref, xn_ref, w_ref, b_ref, _, dp_ref, st_ref):
        i = pl.program_id(1)

        @pl.when(i == 0)
        def _():
            st_ref[...] = jnp.zeros_like(st_ref)

        da_ext = jnp.concatenate([da_ref[...], jnp.where(i < n_tiles - 1, dah_ref[...], 0.0)], axis=0)
        xcat = jnp.concatenate([jnp.where(i > 0, xh_ref[...], 0.0), x_ref[...]], axis=0)
        pre_ext = _conv_taps(jnp.concatenate([xcat, xn_ref[...]], axis=0), w_ref[...], b_ref[...])
        sg = _sigmoid(pre_ext)
        dpre_ext = da_ext * (sg * (1.0 + pre_ext * (1.0 - sg)))
        w = w_ref[...]
        draw = w[3:4, :] * dpre_ext[:tr, :]
        for ahead in range(1, SSM_CONV):
            draw = draw + w[3 - ahead:4 - ahead, :] * pltpu.roll(dpre_ext, tr + CONV_HALO - ahead, 0)[:tr, :]
        dp_ref[...] = draw.astype(BF16)
        dpre = dpre_ext[:tr, :]
        st_ref[3:4, :] += jnp.sum(dpre * xcat[CONV_HALO:, :], axis=0, keepdims=True)
        for back in range(1, SSM_CONV):
            st_ref[3 - back:4 - back, :] += jnp.sum(dpre * pltpu.roll(xcat, back, 0)[CONV_HALO:, :], axis=0, keepdims=True)
        st_ref[4:5, :] += jnp.sum(dpre, axis=0, keepdims=True)

    nxt = lambda i: jnp.minimum((i + 1) * hb, s // CONV_HALO - 1)
    prv = lambda i: jnp.maximum(i * hb - 1, 0)
    return pl.pallas_call(
        body, name=name, grid=(width // 512, n_tiles),
        in_specs=[pl.BlockSpec((tr, 512), lambda j, i: (i, j)), pl.BlockSpec((CONV_HALO, 512), lambda j, i: (nxt(i), j)),
                  pl.BlockSpec((tr, 512), lambda j, i: (i, XBC_BLOCK0 + block0 + j)),
                  pl.BlockSpec((CONV_HALO, 512), lambda j, i: (prv(i), XBC_BLOCK0 + block0 + j)),
                  pl.BlockSpec((CONV_HALO, 512), lambda j, i: (nxt(i), XBC_BLOCK0 + block0 + j)),
                  pl.BlockSpec((SSM_CONV, 512), lambda j, i: (0, block0 + j)),
                  pl.BlockSpec((1, 512), lambda j, i: (0, block0 + j)), ANY],
        out_specs=[pl.BlockSpec((tr, 512), lambda j, i: (i, XBC_BLOCK0 + block0 + j)),
                   pl.BlockSpec((8, 512), lambda j, i: (0, j))],
        out_shape=[jax.ShapeDtypeStruct(dproj.shape, BF16), jax.ShapeDtypeStruct((8, width), F32)],
        input_output_aliases={7: 0}, compiler_params=_cparams("parallel", "arbitrary"),
    )(dact, dact, proj, proj, proj, conv_w, conv_b, dproj)


def _dt_fwd(proj, dt_bias):
    s = proj.shape[0]

    def body(x_ref, b_ref, o_ref):
        v = x_ref[...] + b_ref[...]
        o_ref[...] = jnp.maximum(v, 0.0) + jnp.log(1.0 + jnp.exp(-jnp.abs(v)))

    return pl.pallas_call(
        body, name="dt_fwd", grid=(s // CONV_TILE,),
        in_specs=[pl.BlockSpec((CONV_TILE, 128), lambda i: (i, DT_BLOCK)), pl.BlockSpec((1, 128), lambda i: (0, 0))],
        out_specs=pl.BlockSpec((CONV_TILE, 128), lambda i: (i, 0)),
        out_shape=jax.ShapeDtypeStruct((s, 128), F32), compiler_params=_cparams("parallel"),
    )(proj, dt_bias)


def _dt_bwd(ddt, proj, dt_bias, dproj):
    s = proj.shape[0]

    def body(d_ref, x_ref, b_ref, _, dp_ref, st_ref):
        @pl.when(pl.program_id(0) == 0)
        def _():
            st_ref[...] = jnp.zeros_like(st_ref)

        draw = d_ref[...] * _sigmoid(x_ref[...] + b_ref[...])
        dp_ref[...] = draw.astype(BF16)
        st_ref[0:1, :] += jnp.sum(draw, axis=0, keepdims=True)

    return pl.pallas_call(
        body, name="dt_bwd", grid=(s // CONV_TILE,),
        in_specs=[pl.BlockSpec((CONV_TILE, 128), lambda i: (i, 0)), pl.BlockSpec((CONV_TILE, 128), lambda i: (i, DT_BLOCK)),
                  pl.BlockSpec((1, 128), lambda i: (0, 0)), ANY],
        out_specs=[pl.BlockSpec((CONV_TILE, 128), lambda i: (i, DT_BLOCK)), pl.BlockSpec((8, 128), lambda i: (0, 0))],
        out_shape=[jax.ShapeDtypeStruct(dproj.shape, BF16), jax.ShapeDtypeStruct((8, 128), F32)],
        input_output_aliases={3: 0}, compiler_params=_cparams("arbitrary"),
    )(ddt, proj, dt_bias, dproj)


def _ssd_common(x_ref, b_ref, c_ref, dt_ref, al_ref):
    row, col = _iota((128, 128), 0), _iota((128, 128), 1)
    tril = row >= col
    expand = jnp.where((_iota((128, 512), 1) >> 6) == _iota((128, 512), 0), 1.0, 0.0).astype(BF16)
    dt = dt_ref[...]
    a_neg = -jnp.exp(al_ref[...])
    a_col = _dot3_left(jnp.where(tril, 1.0, 0.0).astype(BF16), dt * a_neg)
    a_exp = _dot3_right(a_col, expand)
    dt_exp = _dot3_right(dt, expand)
    x = x_ref[...]
    return dict(tril=tril, col=col, expand=expand, dt=dt, a_neg=a_neg, a_col=a_col, a_row=a_col.T, a_exp=a_exp,
                dt_exp=dt_exp, a_last=a_exp[127:128, :], x=x, xd=x * dt_exp,
                b16=b_ref[...].astype(BF16), c16=c_ref[...].astype(BF16))


def _ssd_specs(nc, order):
    return [pl.BlockSpec((SSM_CHUNK, 512), lambda g, c: (order(c), g)),
            pl.BlockSpec((SSM_CHUNK, 128), lambda g, c: (order(c), SSM_D_INNER // 128 + g)),
            pl.BlockSpec((SSM_CHUNK, 128), lambda g, c: (order(c), SSM_D_INNER // 128 + SSM_GROUPS + g)),
            pl.BlockSpec((SSM_CHUNK, 512), lambda g, c: (order(c), g)),
            pl.BlockSpec((None, SSM_CHUNK, 128), lambda g, c: (g, order(c), 0)),
            pl.BlockSpec((None, 1, 128), lambda g, c: (g, 0, 0)),
            pl.BlockSpec((1, 512), lambda g, c: (0, g)), pl.BlockSpec((1, 512), lambda g, c: (0, g))]


def _ssd_fwd(act, proj, dtc, a_log, d_skip, norm_g, ride=None):
    s = act.shape[0]
    nc = s // SSM_CHUNK

    def body(x_ref, b_ref, c_ref, z_ref, dt_ref, al_ref, dsk_ref, ng_ref, y_ref, yn_ref, hin_ref, h_sc):
        @pl.when(pl.program_id(1) == 0)
        def _():
            h_sc[...] = jnp.zeros_like(h_sc)

        q = _ssd_common(x_ref, b_ref, c_ref, dt_ref, al_ref)
        gmat = _dot_nt(q["c16"], q["b16"])
        h_in = h_sc[...]
        hin_ref[...] = h_in
        zmat = _dot(q["c16"], h_in.astype(BF16))
        xd16 = q["xd"].astype(BF16)
        low = q["col"] < 64
        pieces = []
        for p in range(4):
            xs = xd16[:, 128 * p:128 * p + 128]
            acc = jnp.zeros((128, 128), F32)
            for half in range(2):
                r = 2 * p + half
                decay = jnp.exp(jnp.where(q["tril"], q["a_col"][:, r:r + 1] - q["a_row"][r:r + 1, :], NEG_BIG))
                hm = low if half == 0 else jnp.logical_not(low)
                acc += _dot((gmat * decay).astype(BF16), jnp.where(hm, xs, jnp.zeros_like(xs)))
            pieces.append(acc)
        y = jnp.concatenate(pieces, axis=1) + zmat * jnp.exp(q["a_exp"]) + q["x"] * dsk_ref[...]
        y_ref[...] = y
        w16 = (q["xd"] * jnp.exp(q["a_last"] - q["a_exp"])).astype(BF16)
        h_sc[...] = h_in * jnp.exp(q["a_last"]) + _dot_tn(q["b16"], w16)
        z = z_ref[...]
        yg = y * (z * _sigmoid(z))
        rr = lax.rsqrt(jnp.mean(yg * yg, axis=-1, keepdims=True) + NORM_EPS)
        yn_ref[...] = (yg * rr * ng_ref[...]).astype(BF16)

    blk = pl.BlockSpec((SSM_CHUNK, 512), lambda g, c: (c, g))
    return _pcall(
        "ssd_fwd", body, grid=(SSM_GROUPS, nc), in_specs=_ssd_specs(nc, lambda c: c),
        out_specs=[blk, blk, pl.BlockSpec((None, None, SSM_STATE, 512), lambda g, c: (g, c, 0, 0))],
        out_shape=[jax.ShapeDtypeStruct((s, SSM_D_INNER), F32), jax.ShapeDtypeStruct((s, SSM_D_INNER), BF16),
                   jax.ShapeDtypeStruct((SSM_GROUPS, nc, SSM_STATE, 512), F32)],
        scratch_shapes=[pltpu.VMEM((SSM_STATE, 512), F32)], sem=("parallel", "arbitrary"),
        args=(act, act, act, proj, dtc, a_log, d_skip, norm_g), ride=ride)


def _ssd_bwd(act, proj, dtc, a_log, d_skip, norm_g, y, h_in_all, dyn, ride=None):
    s = act.shape[0]
    nc = s // SSM_CHUNK

    def body(x_ref, b_ref, c_ref, z_ref, dt_ref, al_ref, dsk_ref, ng_ref, y_ref, hin_ref, dyn_ref,
             dz_ref, dx_ref, db_ref, dc_ref, ddt_ref, st_ref, dal_ref, dh_sc):
        @pl.when(pl.program_id(1) == 0)
        def _():
            dh_sc[...] = jnp.zeros_like(dh_sc)
            st_ref[...] = jnp.zeros_like(st_ref)
            dal_ref[...] = jnp.zeros_like(dal_ref)

        q = _ssd_common(x_ref, b_ref, c_ref, dt_ref, al_ref)
        x, xd, b16, c16 = q["x"], q["xd"], q["b16"], q["c16"]
        z, yv, dyn_v = z_ref[...], y_ref[...], dyn_ref[...]
        sig = _sigmoid(z)
        sil = z * sig
        yg = yv * sil
        rr = lax.rsqrt(jnp.mean(yg * yg, axis=-1, keepdims=True) + NORM_EPS)
        st_ref[0:1, :] += jnp.sum(dyn_v * yg * rr, axis=0, keepdims=True)
        t1 = dyn_v * ng_ref[...]
        dyg = rr * (t1 - yg * (rr * rr) * jnp.mean(t1 * yg, axis=-1, keepdims=True))
        dy = dyg * sil
        dz_ref[...] = (dyg * yv * (sig * (1.0 + z * (1.0 - sig)))).astype(BF16)
        st_ref[1:2, :] += jnp.sum(dy * x, axis=0, keepdims=True)
        dx = dsk_ref[...] * dy
        h_in = hin_ref[...]
        h16 = h_in.astype(BF16)
        ea = jnp.exp(q["a_exp"])
        zmat = _dot(c16, h16)
        dz16 = (dy * ea).astype(BF16)
        da_ch = dy * zmat * ea
        dcm = _dot_nt(dz16, h16)
        dh_in = _dot_tn(c16, dz16)
        dh_out = dh_sc[...]
        dho16 = dh_out.astype(BF16)
        eal = jnp.exp(q["a_last"])
        dh_in += dh_out * eal
        dal_ch = jnp.sum(dh_out * h_in, axis=0, keepdims=True) * eal
        to_end = jnp.exp(q["a_last"] - q["a_exp"])
        wmat = xd * to_end
        dbm = _dot_nt(wmat.astype(BF16), dho16)
        dw = _dot(b16, dho16)
        dxd = dw * to_end
        g_end = dw * wmat
        da_ch -= g_end
        dal_ch += jnp.sum(g_end, axis=0, keepdims=True)
        gmat = _dot_nt(c16, b16)
        row = _iota((128, 128), 0)
        triu = row <= q["col"]
        xd16, dy16 = xd.astype(BF16), dy.astype(BF16)
        low = q["col"] < 64
        da_col = jnp.zeros((128, 128), F32)
        da_key = jnp.zeros((128, 128), F32)
        dg = jnp.zeros((128, 128), F32)
        pieces = []
        for p in range(4):
            xs, dys = xd16[:, 128 * p:128 * p + 128], dy16[:, 128 * p:128 * p + 128]
            acc = jnp.zeros((128, 128), F32)
            for half in range(2):
                r = 2 * p + half
                hm = low if half == 0 else jnp.logical_not(low)
                xm = jnp.where(hm, xs, jnp.zeros_like(xs))
                dym = jnp.where(hm, dys, jnp.zeros_like(dys))
                decay = jnp.exp(jnp.where(q["tril"], q["a_col"][:, r:r + 1] - q["a_row"][r:r + 1, :], NEG_BIG))
                acc += _dot_tn((gmat * decay).astype(BF16), dym)
                dgl = _dot_nt(dym, xm) * decay
                dg += dgl
                n_ls = dgl * gmat
                da_col += jnp.where(q["col"] == r, jnp.sum(n_ls, axis=-1, keepdims=True), 0.0)
                da_key += jnp.where(row == r, jnp.sum(n_ls, axis=0, keepdims=True), 0.0)
            pieces.append(acc)
        da_col -= da_key.T
        dxd += jnp.concatenate(pieces, axis=1)
        dg16 = dg.astype(BF16)
        dcm += _dot(dg16, b16)
        dbm += _dot_tn(dg16, c16)
        fold = jnp.where((_iota((512, 128), 0) >> 6) == _iota((512, 128), 1), 1.0, 0.0).astype(BF16)
        da_ch += jnp.where(_iota((128, 1), 0) == 127, dal_ch, 0.0)
        da_col += _dot3_right(da_ch, fold)
        d_dta = _dot3_left(jnp.where(triu, 1.0, 0.0).astype(BF16), da_col)
        ddt_ref[...] = d_dta * q["a_neg"] + _dot3_right(dxd * x, fold)
        dal_ref[0:1, :] += jnp.sum(d_dta * q["dt"] * q["a_neg"], axis=0, keepdims=True)
        dx_ref[...] = dx + dxd * q["dt_exp"]
        db_ref[...] = dbm
        dc_ref[...] = dcm
        dh_sc[...] = dh_in

    rev = lambda c: nc - 1 - c
    blk = pl.BlockSpec((SSM_CHUNK, 512), lambda g, c: (rev(c), g))
    small = pl.BlockSpec((SSM_CHUNK, 128), lambda g, c: (rev(c), g))
    return _pcall(
        "ssd_bwd", body, grid=(SSM_GROUPS, nc), ride=ride,
        args=(act, act, act, proj, dtc, a_log, d_skip, norm_g, y, h_in_all, dyn), sem=("parallel", "arbitrary"),
        in_specs=_ssd_specs(nc, rev) + [blk, pl.BlockSpec((None, None, SSM_STATE, 512), lambda g, c: (g, rev(c), 0, 0)), blk],
        out_specs=[blk, blk, small, small, pl.BlockSpec((None, SSM_CHUNK, 128), lambda g, c: (g, rev(c), 0)),
                   pl.BlockSpec((8, 512), lambda g, c: (0, g)), pl.BlockSpec((None, 8, 128), lambda g, c: (g, 0, 0))],
        out_shape=[jax.ShapeDtypeStruct((s, SSM_IN_PAD), BF16), jax.ShapeDtypeStruct((s, SSM_D_INNER), F32),
                   jax.ShapeDtypeStruct((s, SSM_GROUPS * SSM_STATE), F32), jax.ShapeDtypeStruct((s, SSM_GROUPS * SSM_STATE), F32),
                   jax.ShapeDtypeStruct((SSM_GROUPS, s, 128), F32), jax.ShapeDtypeStruct((8, SSM_D_INNER), F32),
                   jax.ShapeDtypeStruct((SSM_GROUPS, 8, 128), F32)],
        scratch_shapes=[pltpu.VMEM((SSM_STATE, 512), F32)])


MM_TM = 1024
MM_TK = 512
MM_TK_BIG = 2048
FFN_SHARD = FFN_HIDDEN // N_DEV


def _ij(tm, tn):
    return pl.BlockSpec((tm, tn), lambda i, j, k: (i, j))


def _mm_plain(name, a, b, mode, dims, out_dtype, tn=1024, tk=MM_TK, **kw):
    res = _matmul(name, a, b, mode=mode, dims=dims, tiles=(MM_TM, tn, tk), outs=[((dims[0], dims[1]), out_dtype)],
                  epilogue=_epi_plain, **kw)
    return res[0], res[1:]


def _mm_resgate(name, a, b, res, gate, k_dim, tk, ride=None):
    s, d = res.shape
    got = _matmul(name, a, b, mode="nn", dims=(s, d, k_dim), tiles=(MM_TM, 1024, tk),
                  outs=[((s, d), F32), ((s, d), BF16)], epilogue=_epi_resgate, extras=[res, gate],
                  extra_specs=[_ij(MM_TM, 1024), pl.BlockSpec((1, 1024), lambda i, j, k: (0, j))], ride=ride)
    return got[0], got[1], got[2:]


def _ffn_fwd(tag, x_in, gain, scale, shift, gate, w1, w2_of, ride1=None, ride2=None):
    s, d = x_in.shape
    h = _norm_mod_fwd(tag + "_norm2", x_in, gain, scale, shift)
    rr, pre, *got1 = _matmul(tag + "_ffn1", h, w1, mode="nn", dims=(s, FFN_HIDDEN, d), tiles=(MM_TM, FFN_SHARD, MM_TK_BIG),
                             b_spec=pl.BlockSpec((None, MM_TK_BIG, FFN_SHARD), lambda i, j, k: (j, k, 0)),
                             outs=[((s, FFN_HIDDEN), BF16)] * 2, epilogue=_epi_relu2, ride=ride1)
    x_out, f, got2 = _mm_resgate(tag + "_ffn2", rr, w2_of(got1), x_in, gate, FFN_HIDDEN, MM_TK_BIG, ride=ride2)
    return x_out, (h, rr, pre, f), got1, got2


def _ffn_bwd(tag, dx_out, dy, x_in, saved, gain, scale, w1, w2, below=None, ride_dx2=None, ride_dw2=None,
             ride_dx1_of=None):
    s, d = x_in.shape
    h, rr, pre, _ = saved
    da, *got2 = _matmul(tag + "_ffn2_dx", dy, w2, mode="nt", dims=(s, FFN_HIDDEN, d), tiles=(MM_TM, 1024, MM_TK_BIG),
                        outs=[((s, FFN_HIDDEN), BF16)], epilogue=_epi_drelu2, extras=[pre],
                        extra_specs=[_ij(MM_TM, 1024)], ride=ride_dx2)
    dw2, got_dw2 = _mm_plain(tag + "_ffn2_dw", rr, dy, "tn", (FFN_HIDDEN, d, s), BF16, tk=MM_TK_BIG, ride=ride_dw2)
    dh, got1 = _mm_plain(tag + "_ffn1_dx", da, w1, "nt", (s, d, FFN_HIDDEN), F32, tk=2 * FFN_SHARD, b_slabs=2,
                         b_spec=pl.BlockSpec((2, 1024, FFN_SHARD), lambda i, j, k: (k, j, 0)),
                         ride=None if ride_dx1_of is None else ride_dx1_of(dw2))
    dw1 = _matmul(tag + "_ffn1_dw", h, da, mode="tn", dims=(d, FFN_HIDDEN, s), tiles=(MM_TM, FFN_SHARD, MM_TK_BIG),
                  outs=[((N_DEV, d, FFN_SHARD), BF16)], epilogue=_epi_plain,
                  out_specs=[pl.BlockSpec((None, MM_TM, FFN_SHARD), lambda i, j, k: (j, i, 0))])[0]
    dx_in, st_norm, *dy_below = _norm_mod_bwd(tag + "_norm2_bwd", x_in, gain, scale, dh, dx_out, below)
    return dx_in, dw1, dw2, st_norm, (dy_below[0] if dy_below else None), got2, got_dw2, got1


def _slab_of(col_block):
    return jnp.where(col_block < 9, 3 * (col_block % 3) + col_block // 3, 9)


def _slab_ordered_cols(name, w):
    k_dim, width = w.shape
    tr = k_dim

    def body(w_ref, o_ref):
        o_ref[...] = w_ref[...]

    return pl.pallas_call(
        body, name=name, grid=(k_dim // tr, width // 512),
        in_specs=[pl.BlockSpec((tr, 512), lambda i, j: (i, _slab_of(j)))], out_specs=pl.BlockSpec((tr, 512), lambda i, j: (i, j)),
        out_shape=jax.ShapeDtypeStruct(w.shape, w.dtype), compiler_params=_cparams("parallel", "parallel"),
    )(w)


def _full_weight(name, gathered):
    if name in ASSEMBLED:
        return _assemble_cols("assemble_" + name, gathered, ASSEMBLED[name])
    if name == "ffn_w1":
        return gathered
    return gathered.reshape(-1, gathered.shape[2])


def _grad_pieces(name, full, shard_shape, half=None):
    rows = shard_shape[0]
    lo, n = (0, rows) if half is None else (half * (rows // 2), rows // 2)
    if name in ASSEMBLED:
        return _split_cols(f"split_{name}_{lo}", full, shard_shape[1], lo, n)
    return full.reshape(N_DEV, *shard_shape)[:, lo:lo + n]


def _device_step(x, target, mod, sm, shards, w_even_in):
    s, d = x.shape
    mv = [[mod[i, k].reshape(1, d) for k in range(6)] for i in range(2)]
    nm = [sm["norm_mix"][i].reshape(1, d) for i in range(2)]
    nf = [sm["norm_ffn"][i].reshape(1, d) for i in range(2)]
    pool_w, pool_scale = sm["pool_w"].reshape(4, 128, 128), sm["pool_scale"].reshape(1, POOL_WIDTH)
    got = {}

    def gather(*items):
        return [shards[it] for it in items], False

    def gather_half(item, half):
        rows = shards[item].shape[0] // 2
        return [shards[item][half * rows:(half + 1) * rows]], False

    def scatter(*pieces):
        return list(pieces), True

    def pieces(item, full, half=None):
        return _grad_pieces(item[0], full, shards[item].shape, half)

    def joined(top, bottom):
        return jnp.concatenate([top, bottom], axis=1)

    sh1, sc1, g1, sh2, sc2, g2 = mv[0]
    h1 = _norm_mod_fwd("l0_norm1", x, nm[0], sc1, sh1)
    slab = pl.BlockSpec((None, MM_TM, 512), lambda i, j, k: (_slab_of(j), i, 0))
    proj0, g_eout, g_sout = _matmul(
        "l0_in", h1, w_even_in, mode="nn", dims=(s, EVEN_IN_WIDTH, d), tiles=(MM_TM, 512, MM_TK_BIG),
        outs=[((EVEN_IN_WIDTH // 512, s, 512), BF16)], out_specs=[slab], epilogue=_epi_plain,
        ride=gather(("even_w_out", 0), ("ssm_w_out", 0)))
    w_even_out, w_ssm_out = _full_weight("even_w_out", g_eout), _full_weight("ssm_w_out", g_sout)
    o, lse = [None] * 3, [None] * 3
    o[0], lse[0], (g_w1_top,) = _attn_fwd(proj0, 0, ride=gather_half(("ffn_w1", 0), 0))
    o[1], lse[1], (g_w1_bottom,) = _attn_fwd(proj0, 1, ride=gather_half(("ffn_w1", 0), 1))
    o[2], lse[2], (g_sin_top,) = _attn_fwd(proj0, 2, ride=gather_half(("ssm_w_in", 0), 0))
    w1_0 = _full_weight("ffn_w1", joined(g_w1_top, g_w1_bottom))
    cat = _merge_pool_fwd(o, lse, proj0, pool_w, pool_scale)
    x1, y0, _ = _mm_resgate("l0_out", cat, w_even_out, x, g1, EVEN_OUT_WIDTH, EVEN_OUT_WIDTH)
    x2, ffn0, g_w2_0, (g_sin_bottom,) = _ffn_fwd(
        "l0", x1, nf[0], sc2, sh2, g2, w1_0, lambda arrived: _full_weight("ffn_w2", arrived[0]),
        ride1=gather(("ffn_w2", 0)), ride2=gather_half(("ssm_w_in", 0), 1))
    w2_0, w_ssm_in = _full_weight("ffn_w2", g_w2_0[0]), _full_weight("ssm_w_in", joined(g_sin_top, g_sin_bottom))

    th1, tc1, tg1, th2, tc2, tg2 = mv[1]
    h3 = _norm_mod_fwd("l1_norm1", x2, nm[1], tc1, th1)
    proj1, (g_w1_1,) = _mm_plain("l1_in", h3, w_ssm_in, "nn", (s, SSM_IN_PAD, d), F32, tn=1152, tk=MM_TK_BIG,
                                 ride=gather(("ffn_w1", 1)))
    w1_1 = _full_weight("ffn_w1", g_w1_1)
    conv_w = sm["ssm_conv_w"].reshape(SSM_CONV, SSM_CONV_DIM)
    conv_b = sm["ssm_conv_b"].reshape(1, SSM_CONV_DIM)
    act = _conv_fwd(proj1, conv_w, conv_b)
    dt_bias = jnp.pad(sm["ssm_dt_bias"].reshape(1, SSM_HEADS), ((0, 0), (0, 128 - SSM_HEADS)))
    dt_full = _dt_fwd(proj1, dt_bias)
    dtc = jnp.pad(dt_full[:, :SSM_HEADS].reshape(s, SSM_GROUPS, 8).transpose(1, 0, 2), ((0, 0), (0, 0), (0, 120)))
    a_log = jnp.pad(sm["ssm_a_log"].reshape(SSM_GROUPS, 1, 8), ((0, 0), (0, 0), (0, 120)))
    d_skip = jnp.repeat(sm["ssm_d"].reshape(SSM_HEADS), SSM_D_INNER // SSM_HEADS).reshape(1, SSM_D_INNER)
    norm_g = sm["ssm_norm"].reshape(1, SSM_D_INNER)
    (y, yn, h_in_all), (g_w2_1,) = _ssd_fwd(act, proj1, dtc, a_log, d_skip, norm_g, ride=gather(("ffn_w2", 1)))
    w2_1 = _full_weight("ffn_w2", g_w2_1)
    x3, y1, _ = _mm_resgate("l1_out", yn, w_ssm_out, x2, tg1, SSM_D_INNER, MM_TK_BIG)
    x4, ffn1, _, _ = _ffn_fwd("l1", x3, nf[1], tc2, th2, tg2, w1_1, lambda arrived: w2_1)

    dx4, st_loss, dyf1 = _loss_head("loss_head", x4, sm["final_norm"].reshape(1, d), target, (ffn1[3], tg2))
    loss = jnp.sum(st_loss[1])

    dx3, dw1_1, dw2_1, st_n2_1, dy1, _, _, (got[("ffn_w2", 1)],) = _ffn_bwd(
        "l1", dx4, dyf1, x3, ffn1, nf[1], tc2, w1_1, w2_1, below=(y1, tg1),
        ride_dx1_of=lambda dw2: scatter(pieces(("ffn_w2", 1), dw2)))
    dyn, _ = _mm_plain("l1_out_dx", dy1, w_ssm_out, "nt", (s, SSM_D_INNER, d), F32, tk=MM_TK_BIG)
    dw_sout, _ = _mm_plain("l1_out_dw", yn, dy1, "tn", (SSM_D_INNER, d, s), BF16, tk=MM_TK_BIG)
    (dproj1, dxs, dbm, dcm, ddt, st_ssd, d_alog), (got[("ffn_w1", 1)], got[("ssm_w_out", 0)]) = _ssd_bwd(
        act, proj1, dtc, a_log, d_skip, norm_g, y, h_in_all, dyn,
        ride=scatter(dw1_1, pieces(("ssm_w_out", 0), dw_sout)))
    dproj1, st_cx = _conv_bwd("conv_bwd_x", dxs, proj1, conv_w, conv_b, dproj1, block0=0)
    dproj1, st_cb = _conv_bwd("conv_bwd_b", dbm, proj1, conv_w, conv_b, dproj1, block0=SSM_D_INNER // 512)
    dproj1, st_cc = _conv_bwd("conv_bwd_c", dcm, proj1, conv_w, conv_b, dproj1, block0=SSM_D_INNER // 512 + 2)
    ddt_rows = jnp.pad(ddt[:, :, :8].transpose(1, 0, 2).reshape(s, SSM_HEADS), ((0, 0), (0, 128 - SSM_HEADS)))
    dproj1, st_dt = _dt_bwd(ddt_rows, proj1, dt_bias, dproj1)
    dh3, _ = _mm_plain("l1_in_dx", dproj1, w_ssm_in, "nt", (s, d, SSM_IN_PAD), F32, tk=SSM_IN_PAD // 3)
    dw_sin, _ = _mm_plain("l1_in_dw", h3, dproj1, "tn", (d, SSM_IN_PAD, s), BF16, tn=1152, tk=MM_TK_BIG)
    dx2, st_n1_1, dyf0 = _norm_mod_bwd("l1_norm1_bwd", x2, nm[1], tc1, dh3, dx3, (ffn0[3], g2))

    dx1, dw1_0, dw2_0, st_n2_0, dy0, (sin_top,), (sin_bottom,), (got[("ffn_w2", 0)],) = _ffn_bwd(
        "l0", dx2, dyf0, x1, ffn0, nf[0], sc2, w1_0, w2_0, below=(y0, g1),
        ride_dx2=scatter(pieces(("ssm_w_in", 0), dw_sin, 0)), ride_dw2=scatter(pieces(("ssm_w_in", 0), dw_sin, 1)),
        ride_dx1_of=lambda dw2: scatter(pieces(("ffn_w2", 0), dw2)))
    got[("ssm_w_in", 0)] = joined(sin_top, sin_bottom)
    dcat, _ = _mm_plain("l0_out_dx", dy0, w_even_out, "nt", (s, EVEN_OUT_WIDTH, d), F32, tk=MM_TK_BIG)
    dw_eout, _ = _mm_plain("l0_out_dw", cat, dy0, "tn", (EVEN_OUT_WIDTH, d, s), BF16, tk=MM_TK_BIG)
    lane = jnp.arange(512) // 64
    head_sum = (lane[:, None] == lane[None, :]).astype(BF16)
    *do_cc, du, d_pool_w, st_pool = _merge_pool_bwd(dcat, o, lse, proj0, pool_w, pool_scale, head_sum)
    do, cc = do_cc[:3], do_cc[3:]
    dqkv = [None] * 3
    dqkv[0], (w1_top,) = _attn_bwd(proj0, 0, do[0], lse[0], cc[0], ride=scatter(pieces(("ffn_w1", 0), dw1_0, 0)))
    dqkv[1], (w1_bottom,) = _attn_bwd(proj0, 1, do[1], lse[1], cc[1], ride=scatter(pieces(("ffn_w1", 0), dw1_0, 1)))
    dqkv[2], (got[("even_w_out", 0)],) = _attn_bwd(proj0, 2, do[2], lse[2], cc[2],
                                                   ride=scatter(pieces(("even_w_out", 0), dw_eout)))
    got[("ffn_w1", 0)] = joined(w1_top, w1_bottom)
    dproj0 = jnp.stack([dqkv[g][kind] for g in range(3) for kind in range(3)] + [du])
    dw_ein, _ = _mm_plain("l0_in_dw", h1, dproj0, "tn", (d, EVEN_IN_WIDTH, s), BF16, tn=512, tk=MM_TK_BIG,
                          b_spec=pl.BlockSpec((None, MM_TK_BIG, 512), lambda i, j, k: (_slab_of(j), k, 0)))
    half_slabs = EVEN_IN_WIDTH // 512 // 2
    dh1, (got[("even_w_in", 0)],) = _mm_plain(
        "l0_in_dx", dproj0, _slab_ordered_cols("even_w_in_by_slab", w_even_in), "nt", (s, d, EVEN_IN_WIDTH), F32,
        tk=half_slabs * 512, a_slabs=half_slabs,
        a_spec=pl.BlockSpec((half_slabs, MM_TM, 512), lambda i, j, k: (k, i, 0)),
        ride=scatter(pieces(("even_w_in", 0), dw_ein)))
    grad_x, st_n1_0 = _norm_mod_bwd("l0_norm1_bwd", x, nm[0], sc1, dh1, dx1)

    dg1_0, dg2_0 = st_n2_0[GATE_STAT_ROW], st_n1_1[GATE_STAT_ROW]
    dg1_1, dg2_1 = st_n2_1[GATE_STAT_ROW], st_loss[GATE_STAT_ROW]
    dmod = jnp.stack([
        jnp.stack([st_n1_0[0], st_n1_0[1], dg1_0, st_n2_0[0], st_n2_0[1], dg2_0]),
        jnp.stack([st_n1_1[0], st_n1_1[1], dg1_1, st_n2_1[0], st_n2_1[1], dg2_1])])
    st_conv = jnp.concatenate([st_cx, st_cb, st_cc], axis=1)
    small = dict(
        norm_mix=jnp.stack([st_n1_0[2], st_n1_1[2]]), norm_ffn=jnp.stack([st_n2_0[2], st_n2_1[2]]),
        pool_w=d_pool_w, pool_scale=st_pool[0], ssm_conv_w=st_conv[:SSM_CONV], ssm_conv_b=st_conv[SSM_CONV],
        ssm_dt_bias=st_dt[0, :SSM_HEADS], ssm_a_log=d_alog[:, 0, :8].reshape(SSM_HEADS),
        ssm_d=jnp.sum(st_ssd[1].reshape(SSM_HEADS, SSM_D_INNER // SSM_HEADS), axis=-1), ssm_norm=st_ssd[0],
        final_norm=st_loss[0])
    return loss, grad_x, got, dmod, small


WEIGHT_ORDER = ("ada_w", "ada_b", "norm_mix", "norm_ffn", "ffn_w1", "ffn_w2", "even_w_in", "pool_w", "pool_scale",
                "even_w_out", "ssm_w_in", "ssm_conv_w", "ssm_conv_b", "ssm_dt_bias", "ssm_a_log", "ssm_d", "ssm_norm",
                "ssm_w_out", "final_norm")
BIG_LAYERS = ((("even_w_in", 0), ("even_w_out", 0), ("ffn_w1", 0), ("ffn_w2", 0)),
              (("ssm_w_in", 0), ("ssm_w_out", 0), ("ffn_w1", 1), ("ffn_w2", 1)))
STACKED = ("ffn_w1", "ffn_w2")
ASSEMBLED = {"even_w_in": EVEN_IN_WIDTH, "even_w_out": D_MODEL, "ssm_w_in": SSM_IN_PAD}
SMALL_REPLICATED = ("norm_mix", "norm_ffn", "pool_w", "pool_scale", "ssm_dt_bias", "ssm_a_log", "ssm_d", "final_norm")
SMALL_SHARDED = ("ssm_conv_w", "ssm_conv_b", "ssm_norm")


def _pack(flat_parts, width, lead=()):
    flat = jnp.concatenate(flat_parts, axis=-1)
    n = flat.shape[-1]
    rows = -(-n // (8 * width)) * 8
    flat = jnp.pad(flat, [(0, 0)] * len(lead) + [(0, rows * width - n)])
    return flat.reshape(*lead, rows, width)


def _unpack(packed, shapes, lead=()):
    flat = packed.reshape(*lead, -1)
    out, off = [], 0
    for shp in shapes:
        n = math.prod(shp)
        out.append(flat[..., off:off + n].reshape(*lead, *shp))
        off += n
    return out


def kernel(x, c, ada_w, ada_b, norm_mix, norm_ffn, ffn_w1, ffn_w2, even_w_in, pool_w, pool_scale, even_w_out, ssm_w_in, ssm_conv_w, ssm_conv_b, ssm_dt_bias, ssm_a_log, ssm_d, ssm_norm, ssm_w_out, final_norm, loss_target, m_ada_w, m_ada_b, m_norm_mix, m_norm_ffn, m_ffn_w1, m_ffn_w2, m_even_w_in, m_pool_w, m_pool_scale, m_even_w_out, m_ssm_w_in, m_ssm_conv_w, m_ssm_conv_b, m_ssm_dt_bias, m_ssm_a_log, m_ssm_d, m_ssm_norm, m_ssm_w_out, m_final_norm, v_ada_w, v_ada_b, v_norm_mix, v_norm_ffn, v_ffn_w1, v_ffn_w2, v_even_w_in, v_pool_w, v_pool_scale, v_even_w_out, v_ssm_w_in, v_ssm_conv_w, v_ssm_conv_b, v_ssm_dt_bias, v_ssm_a_log, v_ssm_d, v_ssm_norm, v_ssm_w_out, v_final_norm):
    w = dict(ada_w=ada_w, ada_b=ada_b, norm_mix=norm_mix, norm_ffn=norm_ffn, ffn_w1=ffn_w1, ffn_w2=ffn_w2,
             even_w_in=even_w_in, pool_w=pool_w, pool_scale=pool_scale, even_w_out=even_w_out, ssm_w_in=ssm_w_in,
             ssm_conv_w=ssm_conv_w, ssm_conv_b=ssm_conv_b, ssm_dt_bias=ssm_dt_bias, ssm_a_log=ssm_a_log, ssm_d=ssm_d,
             ssm_norm=ssm_norm, ssm_w_out=ssm_w_out, final_norm=final_norm)
    m = dict(ada_w=m_ada_w, ada_b=m_ada_b, norm_mix=m_norm_mix, norm_ffn=m_norm_ffn, ffn_w1=m_ffn_w1, ffn_w2=m_ffn_w2,
             even_w_in=m_even_w_in, pool_w=m_pool_w, pool_scale=m_pool_scale, even_w_out=m_even_w_out,
             ssm_w_in=m_ssm_w_in, ssm_conv_w=m_ssm_conv_w, ssm_conv_b=m_ssm_conv_b, ssm_dt_bias=m_ssm_dt_bias,
             ssm_a_log=m_ssm_a_log, ssm_d=m_ssm_d, ssm_norm=m_ssm_norm, ssm_w_out=m_ssm_w_out, final_norm=m_final_norm)
    v = dict(ada_w=v_ada_w, ada_b=v_ada_b, norm_mix=v_norm_mix, norm_ffn=v_norm_ffn, ffn_w1=v_ffn_w1, ffn_w2=v_ffn_w2,
             even_w_in=v_even_w_in, pool_w=v_pool_w, pool_scale=v_pool_scale, even_w_out=v_even_w_out,
             ssm_w_in=v_ssm_w_in, ssm_conv_w=v_ssm_conv_w, ssm_conv_b=v_ssm_conv_b, ssm_dt_bias=v_ssm_dt_bias,
             ssm_a_log=v_ssm_a_log, ssm_d=v_ssm_d, ssm_norm=v_ssm_norm, ssm_w_out=v_ssm_w_out, final_norm=v_final_norm)
    d = D_MODEL
    me = _my_index()

    sharded_shapes = [(SSM_CONV, SSM_CONV_DIM // N_DEV), (SSM_CONV_DIM // N_DEV,), (SSM_D_INNER // N_DEV,)]
    shards = {(name, idx): w[name][idx].astype(BF16) for layer in BIG_LAYERS for name, idx in layer}
    small_in = _pack([c.reshape(-1)] + [w[k].reshape(-1) for k in SMALL_SHARDED], 128)
    got, got_even_in = _exchange("gather_first", [small_in, shards[("even_w_in", 0)]], scatter=False)
    w_even_in = _full_weight("even_w_in", got_even_in)
    c_all, conv_w_sh, conv_b_sh, norm_sh = _unpack(got, [(d,)] + sharded_shapes, lead=(N_DEV,))
    sm = {k: w[k] for k in SMALL_REPLICATED}
    sm["ssm_conv_w"] = conv_w_sh.transpose(1, 0, 2).reshape(SSM_CONV, SSM_CONV_DIM)
    sm["ssm_conv_b"] = conv_b_sh.reshape(SSM_CONV_DIM)
    sm["ssm_norm"] = norm_sh.reshape(SSM_D_INNER)

    ada_cols = 6 * d // N_DEV
    c_pad = jnp.pad(c_all, ((0, 16 - N_DEV), (0, 0)))
    mod_cols = _matmul(
        "ada_fwd", c_pad, ada_w, mode="nn", dims=(16, 2 * ada_cols, d), tiles=(16, ada_cols // 2, d), a_fn=_silu,
        b_spec=pl.BlockSpec((None, d, ada_cols // 2), lambda i, j, k: (j // 2, k, j % 2)),
        outs=[((16, 2 * ada_cols), F32)], epilogue=_epi_plain)[0]
    mod_got = _all_to_all("ada_exchange", mod_cols[:N_DEV].reshape(N_DEV, 2, ada_cols))
    mod = (mod_got.transpose(1, 0, 2).reshape(2, 6 * d) + ada_b).reshape(2, 6, d)

    loss, grad_x, grad_got, dmod, small = _device_step(x[0], loss_target[0], mod, sm, shards, w_even_in)
    loss = lax.psum(loss, ("x", "y", "c"))

    grads, delta, new_m, new_v = {}, {}, {}, {}

    def update(name, parts, shape=None):
        rows, cols = parts.shape[1:]
        res = _adamw("adamw_" + name, parts, w[name].reshape(rows, cols), m[name].reshape(rows, cols), v[name].reshape(rows, cols))
        return [r.reshape(w[name].shape if shape is None else shape) for r in res]

    rep_shapes = [w[k].shape for k in SMALL_REPLICATED]
    pack_rep = lambda tree: _pack([tree[k].reshape(-1) for k in SMALL_REPLICATED], 128)
    dmod_all, rep_got = _exchange("gather_small_grads", [dmod.reshape(2, 6 * d), pack_rep(small)], scatter=False)

    my_cols = lax.dynamic_slice_in_dim(dmod_all, me * ada_cols, ada_cols, axis=2).reshape(N_DEV, 2 * ada_cols)
    g_ada_w = _matmul(
        "ada_dw", c_pad, jnp.pad(my_cols, ((0, 16 - N_DEV), (0, 0))), mode="tn", dims=(d, 2 * ada_cols, 16),
        tiles=(1024, ada_cols // 2, 16), a_fn=_silu, outs=[((2, d, ada_cols), F32)],
        out_specs=[pl.BlockSpec((None, 1024, ada_cols // 2), lambda i, j, k: (j // 2, i, j % 2))], epilogue=_epi_plain)[0]
    grads["ada_w"], delta["ada_w"], new_m["ada_w"], new_v["ada_w"] = update("ada_w", g_ada_w.reshape(1, 2 * d, ada_cols))
    grads["ada_b"], delta["ada_b"], new_m["ada_b"], new_v["ada_b"] = update("ada_b", dmod_all)

    sh_pieces = [small["ssm_conv_w"].reshape(SSM_CONV, N_DEV, -1).transpose(1, 0, 2).reshape(N_DEV, -1),
                 small["ssm_conv_b"].reshape(N_DEV, -1), small["ssm_norm"].reshape(N_DEV, -1)]
    sh_got = _all_to_all("exchange_small_grads", _pack(sh_pieces, 128, lead=(N_DEV,)))

    stacked = {name: [None, None] for name in STACKED}
    for layer in BIG_LAYERS:
        for name, idx in layer:
            res = _adamw(f"adamw_{name}_{idx}", grad_got[(name, idx)], w[name][idx], m[name][idx], v[name][idx])
            if name in stacked:
                stacked[name][idx] = res
            else:
                grads[name], delta[name], new_m[name], new_v[name] = [r[None] for r in res]
    for name, per_layer in stacked.items():
        grads[name], delta[name], new_m[name], new_v[name] = [jnp.stack([per_layer[0][q], per_layer[1][q]]) for q in range(4)]

    rep_res = _adamw("adamw_small_replicated", rep_got, pack_rep(w), pack_rep(m), pack_rep(v))
    for dst, packed in zip((grads, delta, new_m, new_v), rep_res):
        for k, val in zip(SMALL_REPLICATED, _unpack(packed, rep_shapes)):
            dst[k] = val
    pack_sh = lambda tree: _pack([tree[k].reshape(-1) for k in SMALL_SHARDED], 128)
    sh_res = _adamw("adamw_small_sharded", sh_got, pack_sh(w), pack_sh(m), pack_sh(v))
    for dst, packed in zip((grads, delta, new_m, new_v), sh_res):
        for k, val in zip(SMALL_SHARDED, _unpack(packed, [w[k].shape for k in SMALL_SHARDED])):
            dst[k] = val

    out = [loss, grad_x[None]]
    for tree in (grads, delta, new_m, new_v):
        out.extend(tree[k] for k in WEIGHT_ORDER)
    return tuple(out)
```
